```python
import math, functools
import jax, jax.numpy as jnp
from jax import lax
import numpy as np

D_MODEL = 1024
BATCH = 4
SEQ = 4096
DEPTH = 1
DEC_BATCH = 128
DEC_SEQ = 4
PAST_LEN = 16384
PAGE_SIZE = 128

GDN_HEADS = 8
GDN_DK = 64
GDN_DV = 64
GDN_CONV = 4
GDN_CHUNK = 64
GDN_QK_DIM = GDN_HEADS * GDN_DK
GDN_V_DIM = GDN_HEADS * GDN_DV
GDN_CONV_DIM = 2 * GDN_QK_DIM + GDN_V_DIM
MLA_HEADS = 8
MLA_NOPE = 64
MLA_ROPE = 32
MLA_V = 64
MLA_KV_RANK = 128
MLA_Q_DIM = MLA_HEADS * (MLA_NOPE + MLA_ROPE)
ROPE_THETA = 10000.0
ATTN_Q_BLOCK = 128
MIX_WIDTH = GDN_V_DIM + MLA_HEADS * MLA_V
IN_SPLITS = (GDN_CONV_DIM, GDN_V_DIM, GDN_HEADS, GDN_HEADS, MLA_Q_DIM, MLA_KV_RANK, MLA_ROPE)
IN_DIM = sum(IN_SPLITS)
D_FF = 2816
N_MOD = 9
NORM_EPS = 1e-6

kernel_name = "hybrid_gdn_mla_macaron_adaln_step"


def split_cols(y, sizes):
    out, start = [], 0
    for s in sizes:
        out.append(y[..., start:start + s])
        start += s
    return out


def rms_norm(x, g):
    xf = x.astype(jnp.float32)
    y = xf * lax.rsqrt(jnp.mean(xf * xf, axis=-1, keepdims=True) + NORM_EPS)
    return (y * g.astype(jnp.float32)).astype(x.dtype)


def l2_norm(x):
    xf = x.astype(jnp.float32)
    return xf * lax.rsqrt(jnp.sum(xf * xf, axis=-1, keepdims=True) + NORM_EPS)


def rope_angles(pos):
    half = MLA_ROPE // 2
    inv = ROPE_THETA ** (-jnp.arange(half, dtype=jnp.float32) / half)
    ang = pos[:, None] * inv[None, :]
    return jnp.cos(ang), jnp.sin(ang)


def apply_rope(x, cos, sin):
    half = MLA_ROPE // 2
    xf = x.astype(jnp.float32)
    x1, x2 = xf[..., :half], xf[..., half:]
    return jnp.concatenate([x1 * cos - x2 * sin, x1 * sin + x2 * cos], axis=-1).astype(x.dtype)


def swiglu_ffn(h, wi, wo):
    gate, up = jnp.split(h @ wi, 2, axis=-1)
    return (jax.nn.silu(gate) * up) @ wo


def causal_conv(prev, x, w):
    T = x.shape[1]
    xin = jnp.concatenate([prev.astype(x.dtype), x], axis=1)
    y = sum(xin[:, j:j + T] * w[j] for j in range(GDN_CONV))
    return jax.nn.silu(y), xin[:, xin.shape[1] - (GDN_CONV - 1):]


def gated_delta_chunked(q, k, v, beta, g, s0):
    f32 = jnp.float32
    B, T, H, DK = q.shape
    DV = v.shape[-1]
    C = min(GDN_CHUNK, T)
    pad = (-T) % C

    def prep(a):
        a = a.astype(f32)
        if pad:
            a = jnp.pad(a, [(0, 0), (0, pad)] + [(0, 0)] * (a.ndim - 2))
        n = a.shape[1] // C
        a = a.reshape((B, n, C) + a.shape[2:])
        return a.transpose((1, 0, 3, 2) + tuple(range(4, a.ndim)))

    q, k, v, beta, g = prep(q), prep(k), prep(v), prep(beta), prep(g)
    gc = jnp.cumsum(g, axis=-1)
    causal = jnp.tril(jnp.ones((C, C), dtype=bool))
    strict = jnp.tril(jnp.ones((C, C), dtype=bool), k=-1)
    decay = jnp.exp(jnp.where(causal, gc[..., :, None] - gc[..., None, :], -jnp.inf))
    kb = k * beta[..., None]
    m = jnp.where(strict, jnp.einsum('nbhid,nbhjd->nbhij', kb, k) * decay, 0.0)
    eye = jnp.eye(C, dtype=f32)
    t_inv = lax.linalg.triangular_solve(eye + m, jnp.broadcast_to(eye, m.shape),
                                        left_side=True, lower=True, unit_diagonal=True)
    u = t_inv @ (v * beta[..., None])
    w = t_inv @ (kb * jnp.exp(gc)[..., None])
    qk = jnp.einsum('nbhid,nbhjd->nbhij', q, k) * decay
    q_dec = q * jnp.exp(gc)[..., None]
    k_dec = k * jnp.exp(gc[..., -1:] - gc)[..., None]
    g_last = jnp.exp(gc[..., -1])

    def step(s, xs):
        q_i, k_i, u_i, w_i, qk_i, gl = xs
        v_new = u_i - w_i @ s
        o = q_i @ s + qk_i @ v_new
        s = s * gl[..., None, None] + jnp.einsum('bhck,bhcv->bhkv', k_i, v_new)
        return s, o

    s, o = lax.scan(step, s0.astype(f32), (q_dec, k_dec, u, w, qk, g_last))
    o = o.transpose(1, 0, 3, 2, 4).reshape(B, -1, H, DV)[:, :T]
    return o, s


def gdn_mixer(qkv, z, b_raw, a_raw, conv_prev, s0, conv_w, a_log, dt_bias, norm_g):
    B, T, _ = qkv.shape
    y, new_conv = causal_conv(conv_prev, qkv, conv_w)
    q, k, v = split_cols(y, (GDN_QK_DIM, GDN_QK_DIM, GDN_V_DIM))
    q = l2_norm(q.reshape(B, T, GDN_HEADS, GDN_DK)) * (GDN_DK ** -0.5)
    k = l2_norm(k.reshape(B, T, GDN_HEADS, GDN_DK))
    v = v.reshape(B, T, GDN_HEADS, GDN_DV)
    beta = jax.nn.sigmoid(b_raw.astype(jnp.float32))
    g = -jnp.exp(a_log.astype(jnp.float32)) * jax.nn.softplus(a_raw.astype(jnp.float32) + dt_bias.astype(jnp.float32))
    o, s = gated_delta_chunked(q, k, v, beta, g, s0)
    o = rms_norm(o, norm_g) * jax.nn.silu(z.reshape(B, T, GDN_HEADS, GDN_DV).astype(jnp.float32))
    return o.reshape(B, T, GDN_V_DIM).astype(qkv.dtype), new_conv, s


def mla_project(q_raw, ckv_raw, kr_raw, pos, qn_g, qr_g, ckv_g, kr_g):
    B, T, _ = q_raw.shape
    q = q_raw.reshape(B, T, MLA_HEADS, MLA_NOPE + MLA_ROPE)
    cos, sin = rope_angles(pos)
    qn = rms_norm(q[..., :MLA_NOPE], qn_g)
    qr = apply_rope(rms_norm(q[..., MLA_NOPE:], qr_g), cos[:, None, :], sin[:, None, :])
    c = rms_norm(ckv_raw, ckv_g)
    kr = apply_rope(rms_norm(kr_raw, kr_g), cos, sin)
    return qn, qr, c, kr


def mla_keys(c, w_uk, kn_g):
    return rms_norm(jnp.einsum('...tr,rhd->...thd', c, w_uk), kn_g)


def mla_core(qn, qr, kn, kr, c, q_pos, k_pos):
    f32 = jnp.float32
    scale = (MLA_NOPE + MLA_ROPE) ** -0.5
    s = (jnp.einsum('qhd,khd->hqk', qn.astype(f32), kn.astype(f32))
         + jnp.einsum('qhd,kd->hqk', qr.astype(f32), kr.astype(f32))) * scale
    s = jnp.where(k_pos[None, None, :] <= q_pos[None, :, None], s, -1e30)
    p = jax.nn.softmax(s, axis=-1)
    return jnp.einsum('hqk,kr->qhr', p.astype(c.dtype), c)


def prompt_mla(qn, qr, c, kr, w_uk, kn_g):
    B, T = c.shape[:2]
    qb = min(ATTN_Q_BLOCK, T)
    nb = T // qb
    pos = jnp.arange(T, dtype=jnp.int32)
    kn = mla_keys(c, w_uk, kn_g)
    core = jax.vmap(mla_core, in_axes=(0, 0, 0, 0, 0, None, None))

    def block(xs):
        qn_b, qr_b, qpos_b = xs
        return core(qn_b, qr_b, kn, kr, c, qpos_b, pos)

    qn_blocks = qn.reshape(B, nb, qb, MLA_HEADS, MLA_NOPE).swapaxes(0, 1)
    qr_blocks = qr.reshape(B, nb, qb, MLA_HEADS, MLA_ROPE).swapaxes(0, 1)
    ctx = lax.map(block, (qn_blocks, qr_blocks, pos.reshape(nb, qb)))
    return ctx.swapaxes(0, 1).reshape(B, T, MLA_HEADS, MLA_KV_RANK)


def sample_mla(qn, qr, c, kr, w_uk, kn_g, cache_c, cache_kr, page_table):
    S = c.shape[1]
    past = page_table.shape[1] * PAGE_SIZE
    k_pos = jnp.arange(past + S, dtype=jnp.int32)
    q_pos = past + jnp.arange(S, dtype=jnp.int32)

    def one(xs):
        pt, qn_s, qr_s, c_s, kr_s = xs
        c_all = jnp.concatenate([cache_c[pt].reshape(past, MLA_KV_RANK).astype(c_s.dtype), c_s], axis=0)
        kr_all = jnp.concatenate([cache_kr[pt].reshape(past, MLA_ROPE).astype(kr_s.dtype), kr_s], axis=0)
        kn = mla_keys(c_all, w_uk, kn_g)
        return mla_core(qn_s, qr_s, kn, kr_all, c_all, q_pos, k_pos)

    return lax.map(one, (page_table, qn, qr, c, kr))


def decoder_layer(x, cond, pos, conv_prev, s0, mla_attend, lp):
    B, T, _ = x.shape
    mods = (jax.nn.silu(cond) @ lp['ada_w'] + lp['ada_b']).reshape(B, N_MOD, 1, D_MODEL)
    h = rms_norm(x, lp['norm_ffn1']) * (1 + mods[:, 1]) + mods[:, 0]
    x = x + 0.5 * mods[:, 2] * swiglu_ffn(h, lp['ffn1_wi'], lp['ffn1_wo'])
    h = rms_norm(x, lp['norm_mix']) * (1 + mods[:, 4]) + mods[:, 3]
    qkv, z, b_raw, a_raw, q_raw, ckv_raw, kr_raw = split_cols(h @ lp['w_in'], IN_SPLITS)
    gdn_out, new_conv, new_s = gdn_mixer(qkv, z, b_raw, a_raw, conv_prev, s0, lp['gdn_conv_w'],
                                         lp['gdn_a_log'], lp['gdn_dt_bias'], lp['gdn_norm'])
    qn, qr, c, kr = mla_project(q_raw, ckv_raw, kr_raw, pos, lp['mla_qn_norm'], lp['mla_qr_norm'],
                                lp['mla_ckv_norm'], lp['mla_kr_norm'])
    ctx = mla_attend(qn, qr, c, kr, lp['mla_w_uk'], lp['mla_kn_norm'])
    mla_out = jnp.einsum('bthr,rhv->bthv', ctx, lp['mla_w_uv']).reshape(B, T, MLA_HEADS * MLA_V)
    mix = jnp.concatenate([gdn_out, mla_out.astype(gdn_out.dtype)], axis=-1) @ lp['w_out']
    x = x + mods[:, 5] * mix
    h = rms_norm(x, lp['norm_ffn2']) * (1 + mods[:, 7]) + mods[:, 6]
    x = x + 0.5 * mods[:, 8] * swiglu_ffn(h, lp['ffn2_wi'], lp['ffn2_wo'])
    return x, c, kr, new_conv, new_s


def setup_inputs(seed: int = 0) -> dict:
    key = jax.random.key(seed)
    ks = iter(jax.random.split(key, 48))
    f32 = jnp.float32
    n_pages = PAST_LEN // PAGE_SIZE
    n_pool = (DEC_BATCH * n_pages * 5) // 4

    def normal(shape, scale=1.0):
        return jax.random.normal(next(ks), shape, f32) * scale

    def gain(shape):
        return 1.0 + 0.02 * normal(shape)

    page_table = jax.random.permutation(next(ks), n_pool)[:DEC_BATCH * n_pages]
    page_table = page_table.reshape(DEC_BATCH, n_pages).astype(jnp.int32)
    dt = jnp.exp(jax.random.uniform(next(ks), (DEPTH, GDN_HEADS), f32, math.log(1e-3), math.log(1e-1)))
    a_log = jnp.log(jax.random.uniform(next(ks), (DEPTH, GDN_HEADS), f32, 1.0, 16.0))
    return {
        'x_prompt': normal((BATCH, SEQ, D_MODEL)),
        'x_sample': normal((DEC_BATCH, DEC_SEQ, D_MODEL)),
        'cache_ckv': normal((DEPTH, n_pool, PAGE_SIZE, MLA_KV_RANK)),
        'cache_krope': normal((DEPTH, n_pool, PAGE_SIZE, MLA_ROPE)),
        'state_conv': normal((DEPTH, DEC_BATCH, GDN_CONV - 1, GDN_CONV_DIM)),
        'state_gdn': normal((DEPTH, DEC_BATCH, GDN_HEADS, GDN_DK, GDN_DV), 0.1),
        'page_table': page_table,
        'c_prompt': normal((BATCH, D_MODEL)),
        'c_sample': normal((DEC_BATCH, D_MODEL)),
        'ada_w': normal((DEPTH, D_MODEL, N_MOD * D_MODEL), D_MODEL ** -0.5),
        'ada_b': normal((DEPTH, N_MOD * D_MODEL), 0.02),
        'norm_ffn1': gain((DEPTH, D_MODEL)),
        'ffn1_wi': normal((DEPTH, D_MODEL, 2 * D_FF), D_MODEL ** -0.5),
        'ffn1_wo': normal((DEPTH, D_FF, D_MODEL), D_FF ** -0.5),
        'norm_mix': gain((DEPTH, D_MODEL)),
        'w_in': normal((DEPTH, D_MODEL, IN_DIM), D_MODEL ** -0.5),
        'gdn_conv_w': normal((DEPTH, GDN_CONV, GDN_CONV_DIM), GDN_CONV ** -0.5),
        'gdn_a_log': a_log,
        'gdn_dt_bias': dt + jnp.log(-jnp.expm1(-dt)),
        'gdn_norm': gain((DEPTH, GDN_DV)),
        'mla_qn_norm': gain((DEPTH, MLA_NOPE)),
        'mla_qr_norm': gain((DEPTH, MLA_ROPE)),
        'mla_ckv_norm': gain((DEPTH, MLA_KV_RANK)),
        'mla_kr_norm': gain((DEPTH, MLA_ROPE)),
        'mla_kn_norm': gain((DEPTH, MLA_NOPE)),
        'mla_w_uk': normal((DEPTH, MLA_KV_RANK, MLA_HEADS, MLA_NOPE), MLA_KV_RANK ** -0.5),
        'mla_w_uv': normal((DEPTH, MLA_KV_RANK, MLA_HEADS, MLA_V), MLA_KV_RANK ** -0.5),
        'w_out': normal((DEPTH, MIX_WIDTH, D_MODEL), MIX_WIDTH ** -0.5),
        'norm_ffn2': gain((DEPTH, D_MODEL)),
        'ffn2_wi': normal((DEPTH, D_MODEL, 2 * D_FF), D_MODEL ** -0.5),
        'ffn2_wo': normal((DEPTH, D_FF, D_MODEL), D_FF ** -0.5),
    }


def reference(x_prompt, x_sample, cache_ckv, cache_krope, state_conv, state_gdn, page_table,
              c_prompt, c_sample, ada_w, ada_b, norm_ffn1, ffn1_wi, ffn1_wo, norm_mix, w_in,
              gdn_conv_w, gdn_a_log, gdn_dt_bias, gdn_norm, mla_qn_norm, mla_qr_norm, mla_ckv_norm,
              mla_kr_norm, mla_kn_norm, mla_w_uk, mla_w_uv, w_out, norm_ffn2, ffn2_wi, ffn2_wo):
    bp, tp = x_prompt.shape[:2]
    ts = x_sample.shape[1]
    pos_p = jnp.arange(tp, dtype=jnp.float32)
    pos_s = PAST_LEN + jnp.arange(ts, dtype=jnp.float32)
    yp, ys = x_prompt, x_sample
    ckv_p, kr_p, conv_p, gdn_p = [], [], [], []
    ckv_s, kr_s, conv_s, gdn_s = [], [], [], []
    for l in range(DEPTH):
        lp = {
            'ada_w': ada_w[l], 'ada_b': ada_b[l],
            'norm_ffn1': norm_ffn1[l], 'ffn1_wi': ffn1_wi[l], 'ffn1_wo': ffn1_wo[l],
            'norm_mix': norm_mix[l], 'w_in': w_in[l],
            'gdn_conv_w': gdn_conv_w[l], 'gdn_a_log': gdn_a_log[l], 'gdn_dt_bias': gdn_dt_bias[l],
            'gdn_norm': gdn_norm[l],
            'mla_qn_norm': mla_qn_norm[l], 'mla_qr_norm': mla_qr_norm[l], 'mla_ckv_norm': mla_ckv_norm[l],
            'mla_kr_norm': mla_kr_norm[l], 'mla_kn_norm': mla_kn_norm[l],
            'mla_w_uk': mla_w_uk[l], 'mla_w_uv': mla_w_uv[l], 'w_out': w_out[l],
            'norm_ffn2': norm_ffn2[l], 'ffn2_wi': ffn2_wi[l], 'ffn2_wo': ffn2_wo[l],
        }
        conv0 = jnp.zeros((bp, GDN_CONV - 1, GDN_CONV_DIM), dtype=yp.dtype)
        s0 = jnp.zeros((bp, GDN_HEADS, GDN_DK, GDN_DV), dtype=jnp.float32)
        yp, c_new, kr_new, cv_new, s_new = decoder_layer(yp, c_prompt, pos_p, conv0, s0, prompt_mla, lp)
        ckv_p.append(c_new); kr_p.append(kr_new); conv_p.append(cv_new); gdn_p.append(s_new)
        attend_s = functools.partial(sample_mla, cache_c=cache_ckv[l], cache_kr=cache_krope[l],
                                     page_table=page_table)
        ys, c_new, kr_new, cv_new, s_new = decoder_layer(ys, c_sample, pos_s, state_conv[l], state_gdn[l],
                                                         attend_s, lp)
        ckv_s.append(c_new); kr_s.append(kr_new); conv_s.append(cv_new); gdn_s.append(s_new)
    new_ckv_prompt = jnp.stack(ckv_p)
    new_krope_prompt = jnp.stack(kr_p)
    new_conv_prompt = jnp.stack(conv_p)
    new_gdn_prompt = jnp.stack(gdn_p)
    new_ckv_sample = jnp.stack(ckv_s)
    new_krope_sample = jnp.stack(kr_s)
    new_conv_sample = jnp.stack(conv_s)
    new_gdn_sample = jnp.stack(gdn_s)
    return (yp, ys, new_ckv_prompt, new_krope_prompt, new_conv_prompt, new_gdn_prompt,
            new_ckv_sample, new_krope_sample, new_conv_sample, new_gdn_sample)
```

```python
import functools
import math

import jax
import jax.numpy as jnp
from jax import lax
from jax.experimental import pallas as pl
from jax.experimental.pallas import tpu as pltpu

F32 = jnp.float32
BF16 = jnp.bfloat16

NORM_EPS = 1e-6
ROPE_THETA = 10000.0
PAGE_SIZE = 128
GDN_CONV = 4
GDN_CHUNK = 64
N_MOD = 9
LANES = 128
SUBLANES = 8
NEG_BIG = -1e30
VMEM_LIMIT = 56 * 1024 * 1024

TOKEN_TILE = 512
ATTN_TILE = 512
GDN_TILE = 256
PAGES_PER_STEP = 16


def _dot(a, b):
    return jnp.dot(a, b, preferred_element_type=F32)


def _dot_t(a, b):
    return lax.dot_general(a, b, (((1,), (1,)), ((), ())), preferred_element_type=F32)


def _dot_ta(a, b):
    return lax.dot_general(a, b, (((0,), (0,)), ((), ())), preferred_element_type=F32)


def _sigmoid(x):
    return 1.0 / (1.0 + jnp.exp(-x))


def _silu(x):
    return x * _sigmoid(x)


def _params(*sem):
    return pltpu.CompilerParams(dimension_semantics=sem, vmem_limit_bytes=VMEM_LIMIT)


def _resident(shape):
    nd = len(shape)
    return pl.BlockSpec(shape, lambda *_: (0,) * nd, pipeline_mode=pl.Buffered(1))


MODS_PER_SUBLAYER = 3


def _mod_row(m_ref, k, per_token):
    return m_ref[k % MODS_PER_SUBLAYER] if per_token else m_ref[k:k + 1, :]


def _mod_norm(x, gain, shift, scale):
    y = x * lax.rsqrt(jnp.mean(x * x, axis=-1, keepdims=True) + NORM_EPS)
    return (y * gain) * (1.0 + scale) + shift


def _mods_spec(per_token, tm, d, sublayer):
    if per_token:
        return pl.BlockSpec((MODS_PER_SUBLAYER, tm, d), lambda b, i: (sublayer, i, 0))
    return pl.BlockSpec((None, N_MOD, d), lambda b, i: (b, 0, 0))


def _mods_body(c_ref, w_ref, b_ref, o_ref):
    c = c_ref[...]
    o_ref[...] = _dot(_silu(c).astype(BF16), w_ref[...].astype(BF16)) + b_ref[...]


def _mods(cond, ada_w, ada_b):
    rows, d = cond.shape
    n = ada_w.shape[1]
    tn = d
    return pl.pallas_call(
        _mods_body,
        grid=(n // tn,),
        in_specs=[pl.BlockSpec((rows, d), lambda j: (0, 0)),
                  pl.BlockSpec((d, tn), lambda j: (0, j)),
                  pl.BlockSpec((1, tn), lambda j: (0, j))],
        out_specs=pl.BlockSpec((rows, tn), lambda j: (0, j)),
        out_shape=jax.ShapeDtypeStruct((rows, n), F32),
        compiler_params=_params("arbitrary"),
        name="mods",
    )(cond, ada_w, ada_b.reshape(1, n))


def _ffn_body(x_ref, m_ref, g_ref, wg_ref, wu_ref, wo_ref, o_ref, *, mod0, per_token, fc):
    x = x_ref[...]
    shift, scale, gate = (_mod_row(m_ref, mod0 + k, per_token) for k in range(3))
    h = _mod_norm(x, g_ref[...], shift, scale).astype(BF16)
    acc = jnp.zeros(x.shape, F32)
    for c in range(wg_ref.shape[1] // fc):
        lo, hi = c * fc, (c + 1) * fc
        a = _silu(_dot(h, wg_ref[:, lo:hi])) * _dot(h, wu_ref[:, lo:hi])
        acc = acc + _dot(a.astype(BF16), wo_ref[lo:hi, :])
    o_ref[...] = x + 0.5 * gate * acc


def _ffn(x, mods, gain, wg, wu, wo, *, mod0, per_token):
    nb, t, d = x.shape
    tm = min(TOKEN_TILE, t)
    dff = wg.shape[1]
    fc = 2 * LANES
    assert t % tm == 0 and dff % fc == 0
    body = functools.partial(_ffn_body, mod0=mod0, per_token=per_token, fc=fc)
    return pl.pallas_call(
        body,
        grid=(nb, t // tm),
        in_specs=[pl.BlockSpec((None, tm, d), lambda b, i: (b, i, 0)),
                  _mods_spec(per_token, tm, d, mod0 // MODS_PER_SUBLAYER),
                  _resident((1, d)), _resident((d, dff)), _resident((d, dff)), _resident((dff, d))],
        out_specs=pl.BlockSpec((None, tm, d), lambda b, i: (b, i, 0)),
        out_shape=jax.ShapeDtypeStruct(x.shape, F32),
        compiler_params=_params("arbitrary", "arbitrary"),
        name="ffn",
    )(x, mods, gain.reshape(1, d), wg, wu, wo)


class _InLayout:
    def __init__(self, conv_dim, v_dim, heads):
        self.qkv = (0, conv_dim)
        self.z = (conv_dim, conv_dim + v_dim)
        self.ba = (self.z[1], self.z[1] + LANES)
        self.q = (self.ba[1], self.ba[1] + heads * LANES)
        self.ckv = (self.q[1], self.q[1] + LANES)
        self.kr = (self.ckv[1], self.ckv[1] + LANES)
        self.krs = (self.kr[1], self.kr[1] + LANES)
        self.total = self.krs[1]


def _inproj_body(x_ref, m_ref, g_ref, w_ref, tab_ref, gq_ref, mq_ref, gc_ref, wuk_ref, mk_ref, gk_ref, gkr_ref,
                 *rest, lay, heads, per_token, sample):
    if sample:
        wqa_ref, rest = rest[0], rest[1:]
    qkv_ref, z_ref, ba_ref, q128_ref, k128_ref, c_ref, cbf_ref, krot_ref = rest[:8]
    x = x_ref[...]
    shift, scale = _mod_row(m_ref, 3, per_token), _mod_row(m_ref, 4, per_token)
    h = _mod_norm(x, g_ref[...], shift, scale).astype(BF16)
    p = _dot(h, w_ref[...])
    qkv_ref[...] = p[:, lay.qkv[0]:lay.qkv[1]]
    z_ref[...] = p[:, lay.z[0]:lay.z[1]]
    ba_ref[...] = p[:, lay.ba[0]:lay.ba[1]]

    tab = tab_ref[...]
    tab_q, tab_c, tab_s = tab[:, :LANES], tab[:, LANES:2 * LANES], tab[:, 2 * LANES:]
    for hh in range(heads):
        qh = p[:, lay.q[0] + hh * LANES: lay.q[0] + (hh + 1) * LANES]
        msq = _dot((qh * qh).astype(BF16), mq_ref[...])
        qn = qh * lax.rsqrt(msq + NORM_EPS) * gq_ref[...] * tab_q
        q128_ref[hh] = qn.astype(BF16)
        if sample:
            qabs_ref, qrope_ref = rest[8], rest[9]
            qa = _dot(qn.astype(BF16), wqa_ref[hh])
            qabs_ref[hh] = qa[:, :LANES].astype(BF16)
            qrope_ref[hh] = qa[:, LANES:2 * LANES].astype(BF16)

    ckv = p[:, lay.ckv[0]:lay.ckv[1]]
    c = ckv * lax.rsqrt(jnp.mean(ckv * ckv, axis=-1, keepdims=True) + NORM_EPS) * gc_ref[...]
    c_ref[...] = c
    cb = c.astype(BF16)
    cbf_ref[...] = cb
    kr = p[:, lay.kr[0]:lay.kr[1]]
    krs = p[:, lay.krs[0]:lay.krs[1]]
    inv = lax.rsqrt(jnp.sum(kr * kr, axis=-1, keepdims=True) * (2.0 / LANES) + NORM_EPS)
    krot = kr * inv * gkr_ref[0:1, :] * tab_c + krs * inv * gkr_ref[1:2, :] * tab_s
    krot_ref[...] = krot
    knr = _dot(cb, wuk_ref[...])
    for hh in range(heads):
        kh = knr[:, hh * LANES:(hh + 1) * LANES]
        msq = _dot((kh * kh).astype(BF16), mk_ref[...])
        k128_ref[hh] = (kh * lax.rsqrt(msq + NORM_EPS) * gk_ref[...] + krot).astype(BF16)


def _inproj(x, mods, gain, w_all, tab, gq, mq, gc, wuk, mk, gk, gkr, wqa, *, lay, heads, per_token):
    nb, t, d = x.shape
    tm = min(TOKEN_TILE, t)
    assert t % tm == 0
    sample = wqa is not None
    body = functools.partial(_inproj_body, lay=lay, heads=heads, per_token=per_token, sample=sample)
    tok = lambda w: pl.BlockSpec((None, tm, w), lambda b, i: (b, i, 0))
    hd = lambda w: pl.BlockSpec((None, heads, tm, w), lambda b, i: (b, 0, i, 0))
    conv_dim, v_dim = lay.qkv[1], lay.z[1] - lay.z[0]
    in_specs = [tok(d), _mods_spec(per_token, tm, d, 1), _resident((1, d)), _resident(w_all.shape),
                pl.BlockSpec((tm, 3 * LANES), lambda b, i: (i, 0)),
                _resident((1, LANES)), _resident((LANES, LANES)), _resident((1, LANES)),
                _resident(wuk.shape), _resident((LANES, LANES)), _resident((1, LANES)), _resident((2, LANES))]
    args = [x, mods, gain.reshape(1, d), w_all, tab, gq, mq, gc, wuk, mk, gk, gkr]
    out_specs = [tok(conv_dim), tok(v_dim), tok(LANES), hd(LANES), hd(LANES), tok(LANES), tok(LANES), tok(LANES)]
    out_shape = [jax.ShapeDtypeStruct((nb, t, conv_dim), F32), jax.ShapeDtypeStruct((nb, t, v_dim), F32),
                 jax.ShapeDtypeStruct((nb, t, LANES), F32),
                 jax.ShapeDtypeStruct((nb, heads, t, LANES), BF16), jax.ShapeDtypeStruct((nb, heads, t, LANES), BF16),
                 jax.ShapeDtypeStruct((nb, t, LANES), F32), jax.ShapeDtypeStruct((nb, t, LANES), BF16),
                 jax.ShapeDtypeStruct((nb, t, LANES), F32)]
    if sample:
        in_specs.append(_resident(wqa.shape))
        args.append(wqa)
        out_specs += [hd(LANES), hd(LANES)]
        out_shape += [jax.ShapeDtypeStruct((nb, heads, t, LANES), BF16)] * 2
    return pl.pallas_call(
        body, grid=(nb, t // tm), in_specs=in_specs, out_specs=out_specs, out_shape=out_shape,
        compiler_params=_params("arbitrary", "arbitrary"), name="inproj",
    )(*args)


def _unit_lower_inverse(a):
    n = a.shape[0]
    assert n & (n - 1) == 0
    row = lax.broadcasted_iota(jnp.int32, (n, n), 0)
    col = lax.broadcasted_iota(jnp.int32, (n, n), 1)
    x = jnp.where(row == col, 1.0, 0.0)
    b = 1
    while b < n:
        lo_mask = (jnp.bitwise_xor(row, col) < 2 * b) & (jnp.bitwise_and(row, b) != 0) & (jnp.bitwise_and(col, b) == 0)
        lo = jnp.where(lo_mask, a, 0.0)
        if b == 1:
            x = x - lo
        else:
            xb = x.astype(BF16)
            x = x - _dot(xb, _dot(lo.astype(BF16), xb).astype(BF16))
        b *= 2
    return x


def _gdn_body(qkv_ref, z_ref, ba_ref, cprev_ref, s0_ref, cw_ref, ap_ref, ng_ref, ms_ref,
              o_ref, nconv_ref, sout_ref, xbuf, qs, ks, vs, gs, bs, s_scr,
              *, tt, t_valid, chunk, heads, dk, dv):
    j = pl.program_id(1)
    hist = SUBLANES

    @pl.when(j == 0)
    def _():
        xbuf[0:hist, :] = cprev_ref[...]
        s_scr[...] = s0_ref[...]

    xbuf[hist:hist + tt, :] = qkv_ref[...]
    conv = xbuf[hist - 3:hist - 3 + tt, :] * cw_ref[0:1, :]
    for jj in range(1, GDN_CONV):
        conv = conv + xbuf[hist - 3 + jj:hist - 3 + jj + tt, :] * cw_ref[jj:jj + 1, :]
    y = _silu(conv)
    tail = xbuf[t_valid:t_valid + hist, :]
    nconv_ref[...] = tail
    xbuf[0:hist, :] = tail

    qk_dim = heads * dk

    def l2(v):
        parts = [_dot((v[:, i * LANES:(i + 1) * LANES] ** 2).astype(BF16), ms_ref[...])
                 for i in range(v.shape[1] // LANES)]
        return v * lax.rsqrt(jnp.concatenate(parts, axis=1) + NORM_EPS)

    q = l2(y[:, :qk_dim]) * (dk ** -0.5)
    k = l2(y[:, qk_dim:2 * qk_dim])
    v = y[:, 2 * qk_dim:]
    ba = ba_ref[...]
    beta = _sigmoid(ba)
    xg = ba + ap_ref[1:2, :]
    g = -ap_ref[0:1, :] * (jnp.maximum(xg, 0.0) + jnp.log1p(jnp.exp(-jnp.abs(xg))))
    if t_valid < tt:
        keep = lax.broadcasted_iota(jnp.int32, (tt, 1), 0) < t_valid
        q, k, v = (jnp.where(keep, a, 0.0) for a in (q, k, v))
        beta, g = jnp.where(keep, beta, 0.0), jnp.where(keep, g, 0.0)
    row = lax.broadcasted_iota(jnp.int32, (tt, tt), 0)
    col = lax.broadcasted_iota(jnp.int32, (tt, tt), 1)
    assert chunk & (chunk - 1) == 0
    tri = jnp.where((row >= col) & (jnp.bitwise_xor(row, col) < chunk), 1.0, 0.0).astype(BF16)
    g_hi = g.astype(BF16)
    g_lo = (g - g_hi.astype(F32)).astype(BF16)
    qs[...] = q
    ks[...] = k
    vs[...] = v
    gs[...] = _dot(tri, g_hi) + _dot(tri, g_lo)
    bs[...] = beta

    crow = lax.broadcasted_iota(jnp.int32, (chunk, chunk), 0)
    ccol = lax.broadcasted_iota(jnp.int32, (chunk, chunk), 1)
    causal = crow >= ccol
    strict = crow > ccol

    def one_chunk(ci, carry):
        r0 = pl.multiple_of(ci * chunk, chunk)
        rows = pl.ds(r0, chunk)
        gcc = gs[rows, :]
        gct = gcc.T
        bc = bs[rows, :]
        for h in range(heads):
            qh = qs[rows, h * dk:(h + 1) * dk]
            kh = ks[rows, h * dk:(h + 1) * dk]
            vh = vs[rows, h * dv:(h + 1) * dv]
            gcol = gcc[:, heads + h:heads + h + 1]
            grow = gct[heads + h:heads + h + 1, :]
            bh = bc[:, h:h + 1]
            diff = gcol - grow
            decay = jnp.exp(jnp.where(causal, diff, NEG_BIG))
            eg = jnp.exp(gcol)
            glast = grow[:, chunk - 1:chunk]
            kb = kh * bh
            khb = kh.astype(BF16)
            m = jnp.where(strict, _dot_t(kb.astype(BF16), khb) * decay, 0.0)
            tinv = _unit_lower_inverse(m).astype(BF16)
            u = _dot(tinv, (vh * bh).astype(BF16))
            w = _dot(tinv, (kb * eg).astype(BF16))
            qkm = (_dot_t(qh.astype(BF16), khb) * decay).astype(BF16)
            q_dec = (qh * eg).astype(BF16)
            k_dec = (kh * jnp.exp(glast - gcol)).astype(BF16)
            s = s_scr[h]
            sb = s.astype(BF16)
            v_new = u - _dot(w.astype(BF16), sb)
            vnb = v_new.astype(BF16)
            o = _dot(q_dec, sb) + _dot(qkm, vnb)
            s_scr[h] = s * jnp.exp(glast) + _dot_ta(k_dec, vnb)
            zh = z_ref[rows, h * dv:(h + 1) * dv]
            on = o * lax.rsqrt(jnp.mean(o * o, axis=-1, keepdims=True) + NORM_EPS) * ng_ref[...]
            o_ref[rows, h * dv:(h + 1) * dv] = on * _silu(zh)
        return carry

    lax.fori_loop(0, tt // chunk, one_chunk, 0)
    sout_ref[...] = s_scr[...]


def _gdn(qkv, z, ba, conv_prev, s0, conv_w, aparams, norm_g, mseg, *, t_valid, chunk):
    nb, t, conv_dim = qkv.shape
    heads, dk, dv = s0.shape[1:]
    tt = min(GDN_TILE, t)
    assert t % tt == 0 and tt % chunk == 0 and (t_valid == tt or t == tt)
    body = functools.partial(_gdn_body, tt=tt, t_valid=t_valid, chunk=chunk, heads=heads, dk=dk, dv=dv)
    v_dim = heads * dv
    tok = lambda w: pl.BlockSpec((None, tt, w), lambda b, i: (b, i, 0))
    return pl.pallas_call(
        body,
        grid=(nb, t // tt),
        in_specs=[tok(conv_dim), tok(v_dim), tok(LANES),
                  pl.BlockSpec((None, SUBLANES, conv_dim), lambda b, i: (b, 0, 0)),
                  pl.BlockSpec((None, heads, dk, dv), lambda b, i: (b, 0, 0, 0)),
                  _resident((GDN_CONV, conv_dim)), _resident((2, LANES)), _resident((1, dv)),
                  _resident((LANES, LANES))],
        out_specs=[tok(v_dim),
                   pl.BlockSpec((None, SUBLANES, conv_dim), lambda b, i: (b, 0, 0)),
                   pl.BlockSpec((None, heads, dk, dv), lambda b, i: (b, 0, 0, 0))],
        out_shape=[jax.ShapeDtypeStruct((nb, t, v_dim), F32),
                   jax.ShapeDtypeStruct((nb, SUBLANES, conv_dim), F32),
                   jax.ShapeDtypeStruct((nb, heads, dk, dv), F32)],
        scratch_shapes=[pltpu.VMEM((tt + SUBLANES, conv_dim), F32),
                        pltpu.VMEM((tt, heads * dk), F32), pltpu.VMEM((tt, heads * dk), F32),
                        pltpu.VMEM((tt, v_dim), F32), pltpu.VMEM((tt, LANES), F32), pltpu.VMEM((tt, LANES), F32),
                        pltpu.VMEM((heads, dk, dv), F32)],
        compiler_params=_params("arbitrary", "arbitrary"),
        name="gdn",
    )(qkv, z, ba, conv_prev, s0, conv_w, aparams, norm_g.reshape(1, dv), mseg)


def _attn_body(q_ref, k_ref, c_ref, o_ref, m_scr, l_scr, acc_scr, *, heads, tq):
    i = pl.program_id(1)
    j = pl.program_id(2)

    @pl.when(j == 0)
    def _():
        m_scr[...] = jnp.full(m_scr.shape, NEG_BIG, F32)
        l_scr[...] = jnp.zeros(l_scr.shape, F32)
        acc_scr[...] = jnp.zeros(acc_scr.shape, F32)

    def step(masked):
        cb = c_ref[...]
        if masked:
            row = lax.broadcasted_iota(jnp.int32, (tq, tq), 0)
            col = lax.broadcasted_iota(jnp.int32, (tq, tq), 1)
            keep = row >= col
        for h in range(heads):
            s = _dot_t(q_ref[h], k_ref[h])
            if masked:
                s = jnp.where(keep, s, NEG_BIG)
            m_prev = m_scr[h][:, :1]
            l_prev = l_scr[h][:, :1]
            m_new = jnp.maximum(m_prev, jnp.max(s, axis=1, keepdims=True))
            alpha = jnp.exp(m_prev - m_new)
            p = jnp.exp(s - m_new)
            l_new = alpha * l_prev + jnp.sum(p, axis=1, keepdims=True)
            acc_scr[h] = acc_scr[h] * alpha + _dot(p.astype(BF16), cb)
            m_scr[h] = jnp.broadcast_to(m_new, (tq, LANES))
            l_scr[h] = jnp.broadcast_to(l_new, (tq, LANES))

    @pl.when(j < i)
    def _():
        step(False)

    @pl.when(j == i)
    def _():
        step(True)
        for h in range(heads):
            o_ref[:, h * LANES:(h + 1) * LANES] = (acc_scr[h] / l_scr[h][:, :1]).astype(BF16)


def _attn_prompt(q128, k128, cbf):
    nb, heads, t, _ = q128.shape
    tq = min(ATTN_TILE, t)
    assert t % tq == 0
    n = t // tq
    body = functools.partial(_attn_body, heads=heads, tq=tq)
    return pl.pallas_call(
        body,
        grid=(nb, n, n),
        in_specs=[pl.BlockSpec((None, heads, tq, LANES), lambda b, i, j: (b, 0, i, 0)),
                  pl.BlockSpec((None, heads, tq, LANES), lambda b, i, j: (b, 0, jnp.minimum(i, j), 0)),
                  pl.BlockSpec((None, tq, LANES), lambda b, i, j: (b, jnp.minimum(i, j), 0))],
        out_specs=pl.BlockSpec((None, tq, heads * LANES), lambda b, i, j: (b, i, 0)),
        out_shape=jax.ShapeDtypeStruct((nb, t, heads * LANES), BF16),
        scratch_shapes=[pltpu.VMEM((heads, tq, LANES), F32), pltpu.VMEM((heads, tq, LANES), F32),
                        pltpu.VMEM((heads, tq, LANES), F32)],
        compiler_params=_params("arbitrary", "arbitrary", "arbitrary"),
        name="attn_prompt",
    )(q128, k128, cbf)


def _attn_sample_body(pt_ref, qabs_ref, qrope_ref, q128_ref, k128n_ref, cn_ref, wukt_ref, *rest,
                      heads, dn, s_new, pages):
    c_pages, kr_pages = rest[:pages], rest[pages:2 * pages]
    o_ref, cb, krb, m_scr, l_scr, acc_scr = rest[2 * pages:]
    j = pl.program_id(1)
    rows = s_new * heads

    @pl.when(j == 0)
    def _():
        m_scr[...] = jnp.full(m_scr.shape, NEG_BIG, F32)
        l_scr[...] = jnp.zeros(l_scr.shape, F32)
        acc_scr[...] = jnp.zeros(acc_scr.shape, F32)

    for i in range(pages):
        cb[i * PAGE_SIZE:(i + 1) * PAGE_SIZE, :] = c_pages[i][...].astype(BF16)
        krb[i * PAGE_SIZE:(i + 1) * PAGE_SIZE, :] = kr_pages[i][...].astype(BF16)
    cbv = cb[...]
    knt = _dot_t(wukt_ref[...], cbv)
    ssq = jnp.concatenate([jnp.sum(knt[h * dn:(h + 1) * dn, :] ** 2, axis=0, keepdims=True)
                           for h in range(heads)], axis=0)
    r = lax.rsqrt(ssq * (1.0 / dn) + NORM_EPS)
    s = (_dot_t(qabs_ref[...], cbv) * jnp.concatenate([r] * s_new, axis=0)
         + _dot_t(qrope_ref[...], krb[...]))
    m_prev = m_scr[:, :1]
    m_new = jnp.maximum(m_prev, jnp.max(s, axis=1, keepdims=True))
    alpha = jnp.exp(m_prev - m_new)
    p = jnp.exp(s - m_new)
    l_new = alpha * l_scr[:, :1] + jnp.sum(p, axis=1, keepdims=True)
    acc = acc_scr[...] * alpha + _dot(p.astype(BF16), cbv)

    @pl.when(j < pl.num_programs(1) - 1)
    def _():
        m_scr[...] = jnp.broadcast_to(m_new, (rows, LANES))
        l_scr[...] = jnp.broadcast_to(l_new, (rows, LANES))
        acc_scr[...] = acc

    @pl.when(j == pl.num_programs(1) - 1)
    def _():
        qf = q128_ref[...].astype(F32)
        assert heads & (heads - 1) == 0
        tok = lax.shift_right_logical(lax.broadcasted_iota(jnp.int32, (rows, 1), 0), heads.bit_length() - 1)
        sn = []
        for t in range(s_new):
            kt = jnp.concatenate([k128n_ref[t].astype(F32)] * s_new, axis=0)
            st = jnp.sum(qf * kt, axis=1, keepdims=True)
            sn.append(jnp.where(tok >= t, st, NEG_BIG))
        m_fin = m_new
        for st in sn:
            m_fin = jnp.maximum(m_fin, st)
        a2 = jnp.exp(m_new - m_fin)
        l_fin = l_new * a2
        acc_fin = acc * a2
        cn = cn_ref[...].astype(BF16).astype(F32)
        for t in range(s_new):
            pt = jnp.exp(sn[t] - m_fin)
            l_fin = l_fin + pt
            acc_fin = acc_fin + pt.astype(BF16).astype(F32) * cn[t:t + 1, :]
        o_ref[...] = (acc_fin / l_fin).astype(BF16)


def _attn_sample(page_table, qabs, qrope, q128, k128n, c_new, wukt, cache_c, cache_kr, *, heads, dn):
    nseq, rows, _ = qabs.shape
    s_new = rows // heads
    n_pages = page_table.shape[1]
    rope = cache_kr.shape[-1]
    pages = min(PAGES_PER_STEP, n_pages)
    assert n_pages % pages == 0
    body = functools.partial(_attn_sample_body, heads=heads, dn=dn, s_new=s_new, pages=pages)

    def page_spec(i, width):
        return pl.BlockSpec((None, PAGE_SIZE, width),
                            lambda b, j, pt: (pt[b * n_pages + j * pages + i], 0, 0))

    per_seq = lambda shape: pl.BlockSpec((None,) + shape, lambda b, j, pt: (b,) + (0,) * len(shape))
    grid_spec = pltpu.PrefetchScalarGridSpec(
        num_scalar_prefetch=1,
        grid=(nseq, n_pages // pages),
        in_specs=[per_seq((rows, LANES)), per_seq((rows, rope)), per_seq((rows, LANES)),
                  per_seq((s_new, heads, LANES)), per_seq((s_new, LANES)),
                  pl.BlockSpec(wukt.shape, lambda b, j, pt: (0, 0))]
                 + [page_spec(i, LANES) for i in range(pages)]
                 + [page_spec(i, rope) for i in range(pages)],
        out_specs=per_seq((rows, LANES)),
        scratch_shapes=[pltpu.VMEM((pages * PAGE_SIZE, LANES), BF16), pltpu.VMEM((pages * PAGE_SIZE, rope), BF16),
                        pltpu.VMEM((rows, LANES), F32), pltpu.VMEM((rows, LANES), F32),
                        pltpu.VMEM((rows, LANES), F32)],
    )
    return pl.pallas_call(
        body, grid_spec=grid_spec,
        out_shape=jax.ShapeDtypeStruct((nseq, rows, LANES), BF16),
        compiler_params=_params("arbitrary", "arbitrary"),
        name="attn_sample",
    )(page_table.reshape(-1), qabs, qrope, q128, k128n, c_new, wukt,
      *([cache_c] * pages), *([cache_kr] * pages))


def _outproj_body(x_ref, m_ref, gdn_ref, ctx_ref, wuv_ref, wog_ref, wom_ref, o_ref, *, per_token):
    gate = _mod_row(m_ref, 5, per_token)
    mla = _dot(ctx_ref[...], wuv_ref[...])
    mix = _dot(gdn_ref[...].astype(BF16), wog_ref[...]) + _dot(mla.astype(BF16), wom_ref[...])
    o_ref[...] = x_ref[...] + gate * mix


def _outproj(x, mods, gdn_out, ctx, wuv_bd, wo_g, wo_m, *, per_token):
    nb, t, d = x.shape
    tm = min(TOKEN_TILE, t)
    tok = lambda w: pl.BlockSpec((None, tm, w), lambda b, i: (b, i, 0))
    body = functools.partial(_outproj_body, per_token=per_token)
    return pl.pallas_call(
        body,
        grid=(nb, t // tm),
        in_specs=[tok(d), _mods_spec(per_token, tm, d, 1), tok(gdn_out.shape[-1]), tok(ctx.shape[-1]),
                  _resident(wuv_bd.shape), _resident(wo_g.shape), _resident(wo_m.shape)],
        out_specs=tok(d),
        out_shape=jax.ShapeDtypeStruct(x.shape, F32),
        compiler_params=_params("arbitrary", "arbitrary"),
        name="outproj",
    )(x, mods, gdn_out, ctx, wuv_bd, wo_g, wo_m)


def _rope_table(pos, half):
    inv = ROPE_THETA ** (-jnp.arange(half, dtype=F32) / half)
    ang = pos[:, None] * inv[None, :]
    cos2 = jnp.concatenate([jnp.cos(ang), jnp.cos(ang)], axis=1)
    sin2 = jnp.concatenate([-jnp.sin(ang), jnp.sin(ang)], axis=1)
    t = pos.shape[0]
    pad = LANES - 4 * half
    one, zero = jnp.ones((t, pad), F32), jnp.zeros((t, pad), F32)
    return jnp.concatenate([one, cos2, sin2, zero, cos2, cos2, zero, sin2, sin2], axis=1)


def _swap_halves(a, axis=-1):
    lo, hi = jnp.split(a, 2, axis=axis)
    return jnp.concatenate([hi, lo], axis=axis)


def _prep_layer(lp, dims):
    heads, dn, dr, rank, g_heads, conv_dim, v_dim = dims
    assert dn + 2 * dr == LANES and rank == LANES and 2 * g_heads <= LANES
    w_in = lp['w_in']
    d = w_in.shape[0]
    o = 0
    w_qkv = w_in[:, o:o + conv_dim]; o += conv_dim
    w_z = w_in[:, o:o + v_dim]; o += v_dim
    w_b = w_in[:, o:o + g_heads]; o += g_heads
    w_a = w_in[:, o:o + g_heads]; o += g_heads
    w_q = w_in[:, o:o + heads * (dn + dr)].reshape(d, heads, dn + dr); o += heads * (dn + dr)
    w_c = w_in[:, o:o + rank]; o += rank
    w_kr = w_in[:, o:o + dr]
    zeros = lambda n: jnp.zeros((d, n), F32)
    w_q128 = jnp.concatenate([w_q, _swap_halves(w_q[:, :, dn:])], axis=2).reshape(d, heads * LANES)
    w_krs = _swap_halves(w_kr)
    w_all = jnp.concatenate([w_qkv, w_z, w_b, w_a, zeros(LANES - 2 * g_heads), w_q128, w_c,
                             zeros(dn), w_kr, w_kr, zeros(dn), w_krs, w_krs], axis=1).astype(BF16)
    scale = (dn + dr) ** -0.5
    qr_g = lp['mla_qr_norm']
    gq = (jnp.concatenate([lp['mla_qn_norm'], qr_g, _swap_halves(qr_g)]) * scale).reshape(1, LANES)
    lane = jnp.arange(LANES)
    seg = jnp.where(lane < dn, 0, jnp.where(lane < dn + dr, 1, 2))
    seg_len = jnp.where(lane < dn, dn, dr).astype(F32)
    mq = jnp.where(seg[:, None] == seg[None, :], 1.0 / seg_len[None, :], 0.0).astype(BF16)
    mk = jnp.where((lane[:, None] < dn) & (lane[None, :] < dn), 1.0 / dn, 0.0).astype(BF16)
    gk = jnp.concatenate([lp['mla_kn_norm'], jnp.zeros((LANES - dn,), F32)]).reshape(1, LANES)
    kr_g = lp['mla_kr_norm']
    zdn = jnp.zeros((dn,), F32)
    gkr = jnp.stack([jnp.concatenate([zdn, kr_g, kr_g]),
                     jnp.concatenate([zdn, _swap_halves(kr_g), _swap_halves(kr_g)])])
    w_uk = lp['mla_w_uk']
    wuk = jnp.concatenate([w_uk, jnp.zeros((rank, heads, LANES - dn), F32)], axis=2)
    wuk = wuk.reshape(rank, heads * LANES).astype(BF16)
    wukt = w_uk.transpose(1, 2, 0).reshape(heads * dn, rank).astype(BF16)
    absorb = jnp.concatenate([w_uk.transpose(1, 2, 0) * lp['mla_kn_norm'][None, :, None],
                              jnp.zeros((heads, LANES - dn, rank), F32)], axis=1)
    fold = jnp.zeros((LANES, LANES), F32)
    fold = fold.at[dn + jnp.arange(dr), jnp.arange(dr)].set(1.0).at[dn + dr + jnp.arange(dr), jnp.arange(dr)].set(1.0)
    wqa = jnp.concatenate([absorb, jnp.broadcast_to(fold, (heads, LANES, LANES))], axis=2).astype(BF16)
    w_uv = lp['mla_w_uv']
    mv = w_uv.shape[2]
    wuv_bd = (w_uv.transpose(1, 0, 2)[:, :, None, :] * jnp.eye(heads, dtype=F32)[:, None, :, None])
    wuv_bd = wuv_bd.reshape(heads * rank, heads * mv).astype(BF16)
    dff = lp['ffn1_wo'].shape[0]
    lane_h = lane // (LANES // 2)
    ms = jnp.where(lane_h[:, None] == lane_h[None, :], 1.0, 0.0).astype(BF16)
    aparams = jnp.zeros((2, LANES), F32)
    aparams = aparams.at[0, g_heads:2 * g_heads].set(jnp.exp(lp['gdn_a_log']))
    aparams = aparams.at[1, g_heads:2 * g_heads].set(lp['gdn_dt_bias'])
    return dict(
        w_all=w_all, gq=gq, mq=mq, gc=lp['mla_ckv_norm'].reshape(1, LANES), wuk=wuk, mk=mk, gk=gk, gkr=gkr,
        wukt=wukt, wqa=wqa, wuv_bd=wuv_bd, ms=ms, aparams=aparams,
        wo_g=lp['w_out'][:v_dim].astype(BF16), wo_m=lp['w_out'][v_dim:].astype(BF16),
        f1=(lp['ffn1_wi'][:, :dff].astype(BF16), lp['ffn1_wi'][:, dff:].astype(BF16), lp['ffn1_wo'].astype(BF16)),
        f2=(lp['ffn2_wi'][:, :dff].astype(BF16), lp['ffn2_wi'][:, dff:].astype(BF16), lp['ffn2_wo'].astype(BF16)),
    )


def _layer(x, mods, tab, conv_prev, s0, lp, w, lay, dims, *, per_token, t_valid, chunk, attend):
    heads = dims[0]
    x = _ffn(x, mods, lp['norm_ffn1'], *w['f1'], mod0=0, per_token=per_token)
    outs = _inproj(x, mods, lp['norm_mix'], w['w_all'], tab, w['gq'], w['mq'], w['gc'], w['wuk'], w['mk'],
                   w['gk'], w['gkr'], w['wqa'] if per_token else None, lay=lay, heads=heads, per_token=per_token)
    qkv, z, ba, q128, k128, c, cbf, krot = outs[:8]
    pad = qkv.shape[1] - t_valid
    gdn_out, nconv, s_new = _gdn(qkv, z, ba, conv_prev, s0, lp['gdn_conv_w'], w['aparams'], lp['gdn_norm'],
                                 w['ms'], t_valid=t_valid, chunk=chunk)
    ctx = attend(outs)
    x = _outproj(x, mods, gdn_out, ctx, w['wuv_bd'], w['wo_g'], w['wo_m'], per_token=per_token)
    x = _ffn(x, mods, lp['norm_ffn2'], *w['f2'], mod0=6, per_token=per_token)
    del pad
    return x, c, krot, nconv, s_new


def kernel(x_prompt, x_sample, cache_ckv, cache_krope, state_conv, state_gdn, page_table, c_prompt, c_sample,
           ada_w, ada_b, norm_ffn1, ffn1_wi, ffn1_wo, norm_mix, w_in, gdn_conv_w, gdn_a_log, gdn_dt_bias, gdn_norm,
           mla_qn_norm, mla_qr_norm, mla_ckv_norm, mla_kr_norm, mla_kn_norm, mla_w_uk, mla_w_uv, w_out,
           norm_ffn2, ffn2_wi, ffn2_wo):
    depth = ada_w.shape[0]
    bp, tp, d = x_prompt.shape
    bs, ts, _ = x_sample.shape
    g_heads, dk, dv = state_gdn.shape[2:]
    conv_dim = state_conv.shape[-1]
    v_dim = g_heads * dv
    rank, heads, dn = mla_w_uk.shape[1:]
    dr = mla_qr_norm.shape[1]
    past = page_table.shape[1] * PAGE_SIZE
    dims = (heads, dn, dr, rank, g_heads, conv_dim, v_dim)
    lay = _InLayout(conv_dim, v_dim, heads)
    ts_pad = -(-ts // SUBLANES) * SUBLANES

    tab_p = _rope_table(jnp.arange(tp, dtype=F32), dr // 2)
    tab_s = jnp.tile(_rope_table(past + jnp.arange(ts_pad, dtype=F32), dr // 2), (bs, 1))
    cond = jnp.concatenate([c_prompt, c_sample], axis=0)
    rows = -(-cond.shape[0] // SUBLANES) * SUBLANES
    cond = jnp.pad(cond, ((0, rows - cond.shape[0]), (0, 0)))

    yp = x_prompt
    ys = jnp.pad(x_sample, ((0, 0), (0, ts_pad - ts), (0, 0))).reshape(1, bs * ts_pad, d)
    outs_p, outs_s = [], []
    for l in range(depth):
        lp = dict(norm_ffn1=norm_ffn1[l], ffn1_wi=ffn1_wi[l], ffn1_wo=ffn1_wo[l], norm_mix=norm_mix[l], w_in=w_in[l],
                  gdn_conv_w=gdn_conv_w[l], gdn_a_log=gdn_a_log[l], gdn_dt_bias=gdn_dt_bias[l], gdn_norm=gdn_norm[l],
                  mla_qn_norm=mla_qn_norm[l], mla_qr_norm=mla_qr_norm[l], mla_ckv_norm=mla_ckv_norm[l],
                  mla_kr_norm=mla_kr_norm[l], mla_kn_norm=mla_kn_norm[l], mla_w_uk=mla_w_uk[l],
                  mla_w_uv=mla_w_uv[l], w_out=w_out[l], norm_ffn2=norm_ffn2[l], ffn2_wi=ffn2_wi[l],
                  ffn2_wo=ffn2_wo[l])
        w = _prep_layer(lp, dims)
        mods = _mods(cond, ada_w[l], ada_b[l])
        mods_p = mods[:bp].reshape(bp, N_MOD, d)
        mods_s = jnp.repeat(mods[bp:bp + bs].reshape(bs, N_MOD, d), ts_pad, axis=0).transpose(1, 0, 2)

        conv0 = jnp.zeros((bp, SUBLANES, conv_dim), F32)
        s0 = jnp.zeros((bp, g_heads, dk, dv), F32)
        attend_p = lambda o: _attn_prompt(o[3], o[4], o[6])
        yp, c_p, kr_p, cv_p, s_p = _layer(yp, mods_p, tab_p, conv0, s0, lp, w, lay, dims, per_token=False,
                                          t_valid=min(GDN_TILE, tp), chunk=min(GDN_CHUNK, tp), attend=attend_p)
        outs_p.append((c_p, kr_p[..., dn:dn + dr], cv_p[:, SUBLANES - (GDN_CONV - 1):], s_p))

        cache_c = cache_ckv[l]
        cache_kr = cache_krope[l]
        conv_prev = jnp.pad(state_conv[l], ((0, 0), (SUBLANES - (GDN_CONV - 1), 0), (0, 0)))

        def attend_s(o):
            q128, k128, c_new, qabs, qrope = o[3], o[4], o[5], o[8], o[9]
            by_seq = lambda a: a[0].reshape(heads, bs, ts_pad, -1)[:, :, :ts].transpose(1, 2, 0, 3)
            flat = lambda a: by_seq(a).reshape(bs, ts * heads, -1)
            ctx = _attn_sample(page_table, flat(qabs), flat(qrope)[..., :dr], flat(q128), by_seq(k128),
                               c_new.reshape(bs, ts_pad, -1)[:, :ts], w['wukt'], cache_c, cache_kr,
                               heads=heads, dn=dn)
            ctx = ctx.reshape(bs, ts, heads * rank)
            return jnp.pad(ctx, ((0, 0), (0, ts_pad - ts), (0, 0))).reshape(1, bs * ts_pad, heads * rank)

        ys, c_s, kr_s, cv_s, s_s = _layer_sample(ys, mods_s, tab_s, conv_prev, state_gdn[l], lp, w, lay, dims,
                                                 bs, ts, ts_pad, attend_s)
        outs_s.append((c_s, kr_s, cv_s, s_s))

    stack = lambda outs, k: jnp.stack([o[k] for o in outs])
    ys = ys.reshape(bs, ts_pad, d)[:, :ts]
    return (yp, ys, stack(outs_p, 0), stack(outs_p, 1), stack(outs_p, 2), stack(outs_p, 3),
            stack(outs_s, 0), stack(outs_s, 1), stack(outs_s, 2), stack(outs_s, 3))


def _layer_sample(x, mods, tab, conv_prev, s0, lp, w, lay, dims, bs, ts, ts_pad, attend):
    heads, dn, dr = dims[0], dims[1], dims[2]
    d = x.shape[-1]
    x = _ffn(x, mods, lp['norm_ffn1'], *w['f1'], mod0=0, per_token=True)
    outs = _inproj(x, mods, lp['norm_mix'], w['w_all'], tab, w['gq'], w['mq'], w['gc'], w['wuk'], w['mk'],
                   w['gk'], w['gkr'], w['wqa'], lay=lay, heads=heads, per_token=True)
    qkv, z, ba, c, krot = outs[0], outs[1], outs[2], outs[5], outs[7]
    seq = lambda a: a.reshape(bs, ts_pad, a.shape[-1])
    gdn_out, nconv, s_new = _gdn(seq(qkv), seq(z), seq(ba), conv_prev, s0, lp['gdn_conv_w'], w['aparams'],
                                 lp['gdn_norm'], w['ms'], t_valid=ts, chunk=ts_pad)
    ctx = attend(outs)
    x = _outproj(x, mods, gdn_out.reshape(1, bs * ts_pad, -1), ctx, w['wuv_bd'], w['wo_g'], w['wo_m'],
                 per_token=True)
    x = _ffn(x, mods, lp['norm_ffn2'], *w['f2'], mod0=6, per_token=True)
    c_s = seq(c)[:, :ts]
    kr_s = seq(krot)[:, :ts, dn:dn + dr]
    return x, c_s, kr_s, nconv[:, SUBLANES - (GDN_CONV - 1):], s_new
```

```python
import functools
import math

import jax
import jax.numpy as jnp
from jax import lax
from jax.experimental import pallas as pl
from jax.experimental.pallas import tpu as pltpu

F32 = jnp.float32
BF16 = jnp.bfloat16

NORM_EPS = 1e-6
ROPE_THETA = 10000.0
PAGE_SIZE = 128
GDN_CONV = 4
GDN_CHUNK = 64
N_MOD = 9
LANES = 128
SUBLANES = 8
NEG_BIG = -1e30
VMEM_LIMIT = 56 * 1024 * 1024

TOKEN_TILE = 512
ATTN_TILE = 512
GDN_TILE = 256
PAGES_PER_STEP = 16


def _dot(a, b):
    return jnp.dot(a, b, preferred_element_type=F32)


def _dot_t(a, b):
    return lax.dot_general(a, b, (((1,), (1,)), ((), ())), preferred_element_type=F32)


def _dot_ta(a, b):
    return lax.dot_general(a, b, (((0,), (0,)), ((), ())), preferred_element_type=F32)


def _bmm(a, b):
    return lax.dot_general(a, b, (((2,), (1,)), ((0,), (0,))), preferred_element_type=F32)


def _bmm_t(a, b):
    return lax.dot_general(a, b, (((2,), (2,)), ((0,), (0,))), preferred_element_type=F32)


def _sigmoid(x):
    return 1.0 / (1.0 + jnp.exp(-x))


def _silu(x):
    return x * _sigmoid(x)


def _params(*sem):
    return pltpu.CompilerParams(dimension_semantics=sem, vmem_limit_bytes=VMEM_LIMIT)


def _resident(shape):
    nd = len(shape)
    return pl.BlockSpec(shape, lambda *_: (0,) * nd, pipeline_mode=pl.Buffered(1))


MODS_PER_SUBLAYER = 3


def _mod_row(m_ref, k, per_token):
    return m_ref[k % MODS_PER_SUBLAYER] if per_token else m_ref[k:k + 1, :]


def _mod_norm(x, gain, shift, scale):
    y = x * lax.rsqrt(jnp.mean(x * x, axis=-1, keepdims=True) + NORM_EPS)
    return (y * gain) * (1.0 + scale) + shift


def _mods_spec(per_token, tm, d, sublayer):
    if per_token:
        return pl.BlockSpec((MODS_PER_SUBLAYER, tm, d), lambda b, i: (sublayer, i, 0))
    return pl.BlockSpec((None, N_MOD, d), lambda b, i: (b, 0, 0))


def _mods_body(c_ref, w_ref, b_ref, o_ref):
    c = c_ref[...]
    o_ref[...] = _dot(_silu(c).astype(BF16), w_ref[...].astype(BF16)) + b_ref[...]


def _mods(cond, ada_w, ada_b):
    rows, d = cond.shape
    n = ada_w.shape[1]
    tn = d
    return pl.pallas_call(
        _mods_body,
        grid=(n // tn,),
        in_specs=[pl.BlockSpec((rows, d), lambda j: (0, 0)),
                  pl.BlockSpec((d, tn), lambda j: (0, j)),
                  pl.BlockSpec((1, tn), lambda j: (0, j))],
        out_specs=pl.BlockSpec((rows, tn), lambda j: (0, j)),
        out_shape=jax.ShapeDtypeStruct((rows, n), F32),
        compiler_params=_params("arbitrary"),
        name="mods",
    )(cond, ada_w, ada_b.reshape(1, n))


def _ffn_body(x_ref, m_ref, g_ref, wg_ref, wu_ref, wo_ref, o_ref, *, mod0, per_token, fc):
    x = x_ref[...]
    shift, scale, gate = (_mod_row(m_ref, mod0 + k, per_token) for k in range(3))
    h = _mod_norm(x, g_ref[...], shift, scale).astype(BF16)
    acc = jnp.zeros(x.shape, F32)
    for c in range(wg_ref.shape[1] // fc):
        lo, hi = c * fc, (c + 1) * fc
        a = _silu(_dot(h, wg_ref[:, lo:hi])) * _dot(h, wu_ref[:, lo:hi])
        acc = acc + _dot(a.astype(BF16), wo_ref[lo:hi, :])
    o_ref[...] = x + 0.5 * gate * acc


def _ffn(x, mods, gain, wg, wu, wo, *, mod0, per_token):
    nb, t, d = x.shape
    tm = min(TOKEN_TILE, t)
    dff = wg.shape[1]
    fc = 2 * LANES
    assert t % tm == 0 and dff % fc == 0
    body = functools.partial(_ffn_body, mod0=mod0, per_token=per_token, fc=fc)
    return pl.pallas_call(
        body,
        grid=(nb, t // tm),
        in_specs=[pl.BlockSpec((None, tm, d), lambda b, i: (b, i, 0)),
                  _mods_spec(per_token, tm, d, mod0 // MODS_PER_SUBLAYER),
                  _resident((1, d)), _resident((d, dff)), _resident((d, dff)), _resident((dff, d))],
        out_specs=pl.BlockSpec((None, tm, d), lambda b, i: (b, i, 0)),
        out_shape=jax.ShapeDtypeStruct(x.shape, F32),
        compiler_params=_params("arbitrary", "arbitrary"),
        name="ffn",
    )(x, mods, gain.reshape(1, d), wg, wu, wo)


class _InLayout:
    def __init__(self, conv_dim, v_dim, heads):
        self.qkv = (0, conv_dim)
        self.z = (conv_dim, conv_dim + v_dim)
        self.ba = (self.z[1], self.z[1] + LANES)
        self.q = (self.ba[1], self.ba[1] + heads * LANES)
        self.ckv = (self.q[1], self.q[1] + LANES)
        self.kr = (self.ckv[1], self.ckv[1] + LANES)
        self.krs = (self.kr[1], self.kr[1] + LANES)
        self.total = self.krs[1]


def _inproj_body(x_ref, m_ref, g_ref, w_ref, tab_ref, gq_ref, mq_ref, gc_ref, wuk_ref, mk_ref, gk_ref, gkr_ref,
                 *rest, lay, heads, per_token, sample):
    if sample:
        wqa_ref, rest = rest[0], rest[1:]
    qkv_ref, z_ref, ba_ref, q128_ref, k128_ref, c_ref, cbf_ref, krot_ref = rest[:8]
    x = x_ref[...]
    shift, scale = _mod_row(m_ref, 3, per_token), _mod_row(m_ref, 4, per_token)
    h = _mod_norm(x, g_ref[...], shift, scale).astype(BF16)
    p = _dot(h, w_ref[...])
    qkv_ref[...] = p[:, lay.qkv[0]:lay.qkv[1]]
    z_ref[...] = p[:, lay.z[0]:lay.z[1]]
    ba_ref[...] = p[:, lay.ba[0]:lay.ba[1]]

    tab = tab_ref[...]
    tab_q, tab_c, tab_s = tab[:, :LANES], tab[:, LANES:2 * LANES], tab[:, 2 * LANES:]
    for hh in range(heads):
        qh = p[:, lay.q[0] + hh * LANES: lay.q[0] + (hh + 1) * LANES]
        msq = _dot((qh * qh).astype(BF16), mq_ref[...])
        qn = qh * lax.rsqrt(msq + NORM_EPS) * gq_ref[...] * tab_q
        q128_ref[hh] = qn.astype(BF16)
        if sample:
            qabs_ref, qrope_ref = rest[8], rest[9]
            qa = _dot(qn.astype(BF16), wqa_ref[hh])
            qabs_ref[hh] = qa[:, :LANES].astype(BF16)
            qrope_ref[hh] = qa[:, LANES:2 * LANES].astype(BF16)

    ckv = p[:, lay.ckv[0]:lay.ckv[1]]
    c = ckv * lax.rsqrt(jnp.mean(ckv * ckv, axis=-1, keepdims=True) + NORM_EPS) * gc_ref[...]
    c_ref[...] = c
    cb = c.astype(BF16)
    cbf_ref[...] = cb
    kr = p[:, lay.kr[0]:lay.kr[1]]
    krs = p[:, lay.krs[0]:lay.krs[1]]
    inv = lax.rsqrt(jnp.sum(kr * kr, axis=-1, keepdims=True) * (2.0 / LANES) + NORM_EPS)
    krot = kr * inv * gkr_ref[0:1, :] * tab_c + krs * inv * gkr_ref[1:2, :] * tab_s
    krot_ref[...] = krot
    knr = _dot(cb, wuk_ref[...])
    for hh in range(heads):
        kh = knr[:, hh * LANES:(hh + 1) * LANES]
        msq = _dot((kh * kh).astype(BF16), mk_ref[...])
        k128_ref[hh] = (kh * lax.rsqrt(msq + NORM_EPS) * gk_ref[...] + krot).astype(BF16)


def _inproj(x, mods, gain, w_all, tab, gq, mq, gc, wuk, mk, gk, gkr, wqa, *, lay, heads, per_token):
    nb, t, d = x.shape
    tm = min(TOKEN_TILE, t)
    assert t % tm == 0
    sample = wqa is not None
    body = functools.partial(_inproj_body, lay=lay, heads=heads, per_token=per_token, sample=sample)
    tok = lambda w: pl.BlockSpec((None, tm, w), lambda b, i: (b, i, 0))
    hd = lambda w: pl.BlockSpec((None, heads, tm, w), lambda b, i: (b, 0, i, 0))
    conv_dim, v_dim = lay.qkv[1], lay.z[1] - lay.z[0]
    in_specs = [tok(d), _mods_spec(per_token, tm, d, 1), _resident((1, d)), _resident(w_all.shape),
                pl.BlockSpec((tm, 3 * LANES), lambda b, i: (i, 0)),
                _resident((1, LANES)), _resident((LANES, LANES)), _resident((1, LANES)),
                _resident(wuk.shape), _resident((LANES, LANES)), _resident((1, LANES)), _resident((2, LANES))]
    args = [x, mods, gain.reshape(1, d), w_all, tab, gq, mq, gc, wuk, mk, gk, gkr]
    out_specs = [tok(conv_dim), tok(v_dim), tok(LANES), hd(LANES), hd(LANES), tok(LANES), tok(LANES), tok(LANES)]
    out_shape = [jax.ShapeDtypeStruct((nb, t, conv_dim), F32), jax.ShapeDtypeStruct((nb, t, v_dim), F32),
                 jax.ShapeDtypeStruct((nb, t, LANES), F32),
                 jax.ShapeDtypeStruct((nb, heads, t, LANES), BF16), jax.ShapeDtypeStruct((nb, heads, t, LANES), BF16),
                 jax.ShapeDtypeStruct((nb, t, LANES), F32), jax.ShapeDtypeStruct((nb, t, LANES), BF16),
                 jax.ShapeDtypeStruct((nb, t, LANES), F32)]
    if sample:
        in_specs.append(_resident(wqa.shape))
        args.append(wqa)
        out_specs += [hd(LANES), hd(LANES)]
        out_shape += [jax.ShapeDtypeStruct((nb, heads, t, LANES), BF16)] * 2
    return pl.pallas_call(
        body, grid=(nb, t // tm), in_specs=in_specs, out_specs=out_specs, out_shape=out_shape,
        compiler_params=_params("arbitrary", "arbitrary"), name="inproj",
    )(*args)


def _unit_lower_inverse(a):
    n = a.shape[-1]
    assert n & (n - 1) == 0
    row = lax.broadcasted_iota(jnp.int32, (n, n), 0)
    col = lax.broadcasted_iota(jnp.int32, (n, n), 1)
    x = jnp.broadcast_to(jnp.where(row == col, 1.0, 0.0), a.shape)
    b = 1
    while b < n:
        lo_mask = (jnp.bitwise_xor(row, col) < 2 * b) & (jnp.bitwise_and(row, b) != 0) & (jnp.bitwise_and(col, b) == 0)
        lo = jnp.where(lo_mask, a, 0.0)
        if b == 1:
            x = x - lo
        else:
            xb = x.astype(BF16)
            x = x - _bmm(xb, _bmm(lo.astype(BF16), xb).astype(BF16))
        b *= 2
    return x


def _gdn_body(qkv_ref, z_ref, ba_ref, cprev_ref, s0_ref, cw_ref, ap_ref, ng_ref, ms_ref,
              o_ref, nconv_ref, sout_ref, xbuf, s_scr,
              *, tt, t_valid, chunk, heads, dk, dv):
    j = pl.program_id(1)
    hist = SUBLANES

    @pl.when(j == 0)
    def _():
        xbuf[0:hist, :] = cprev_ref[...]
        s_scr[...] = s0_ref[...]

    xbuf[hist:hist + tt, :] = qkv_ref[...]
    conv = xbuf[hist - 3:hist - 3 + tt, :] * cw_ref[0:1, :]
    for jj in range(1, GDN_CONV):
        conv = conv + xbuf[hist - 3 + jj:hist - 3 + jj + tt, :] * cw_ref[jj:jj + 1, :]
    y = _silu(conv)
    tail = xbuf[t_valid:t_valid + hist, :]
    nconv_ref[...] = tail
    xbuf[0:hist, :] = tail

    qk_dim = heads * dk

    def l2(v):
        parts = [_dot((v[:, i * LANES:(i + 1) * LANES] ** 2).astype(BF16), ms_ref[...])
                 for i in range(v.shape[1] // LANES)]
        return v * lax.rsqrt(jnp.concatenate(parts, axis=1) + NORM_EPS)

    q = l2(y[:, :qk_dim]) * (dk ** -0.5)
    k = l2(y[:, qk_dim:2 * qk_dim])
    v = y[:, 2 * qk_dim:]
    ba = ba_ref[...]
    beta = _sigmoid(ba)
    xg = ba + ap_ref[1:2, :]
    g = -ap_ref[0:1, :] * (jnp.maximum(xg, 0.0) + jnp.log1p(jnp.exp(-jnp.abs(xg))))
    if t_valid < tt:
        keep = lax.broadcasted_iota(jnp.int32, (tt, 1), 0) < t_valid
        q, k, v = (jnp.where(keep, a, 0.0) for a in (q, k, v))
        beta, g = jnp.where(keep, beta, 0.0), jnp.where(keep, g, 0.0)
    row = lax.broadcasted_iota(jnp.int32, (tt, tt), 0)
    col = lax.broadcasted_iota(jnp.int32, (tt, tt), 1)
    assert chunk & (chunk - 1) == 0
    tri = jnp.where((row >= col) & (jnp.bitwise_xor(row, col) < chunk), 1.0, 0.0).astype(BF16)
    g_hi = g.astype(BF16)
    g_lo = (g - g_hi.astype(F32)).astype(BF16)
    gc = _dot(tri, g_hi) + _dot(tri, g_lo)
    gct = gc.T
    z = z_ref[...]

    crow = lax.broadcasted_iota(jnp.int32, (chunk, chunk), 0)
    ccol = lax.broadcasted_iota(jnp.int32, (chunk, chunk), 1)
    causal = crow >= ccol
    strict = crow > ccol
    stack = lambda xs: jnp.stack(xs, axis=0)

    s3 = s_scr[...]
    for c in range(tt // chunk):
        r0, r1 = c * chunk, (c + 1) * chunk
        qkb, kbf, vbeta, kbeg, qdec, kdec, decay, gl, zc = ([] for _ in range(9))
        for h in range(heads):
            qh = q[r0:r1, h * dk:(h + 1) * dk]
            kh = k[r0:r1, h * dk:(h + 1) * dk]
            vh = v[r0:r1, h * dv:(h + 1) * dv]
            bh = beta[r0:r1, h:h + 1]
            gcol = gc[r0:r1, heads + h:heads + h + 1]
            grow = gct[heads + h:heads + h + 1, r0:r1]
            glast = grow[:, chunk - 1:chunk]
            eg = jnp.exp(gcol)
            kb = kh * bh
            qkb.append(jnp.concatenate([qh, kb], axis=0).astype(BF16))
            kbf.append(kh.astype(BF16))
            vbeta.append((vh * bh).astype(BF16))
            kbeg.append((kb * eg).astype(BF16))
            qdec.append((qh * eg).astype(BF16))
            kdec.append((kh * jnp.exp(glast - gcol)).astype(BF16))
            decay.append(jnp.exp(jnp.where(causal, jnp.broadcast_to(gcol, (chunk, chunk)) - grow, NEG_BIG)))
            gl.append(jnp.exp(glast))
            zc.append(z[r0:r1, h * dv:(h + 1) * dv])
        decay3 = stack(decay)
        sc = _bmm_t(stack(qkb), stack(kbf))
        qkm = (sc[:, :chunk] * decay3).astype(BF16)
        m = jnp.where(strict, sc[:, chunk:] * decay3, 0.0)
        tinv = _unit_lower_inverse(m).astype(BF16)
        u = _bmm(tinv, stack(vbeta))
        w = _bmm(tinv, stack(kbeg))
        sb = s3.astype(BF16)
        ws_qs = _bmm(jnp.concatenate([w.astype(BF16), stack(qdec)], axis=1), sb)
        vnb = (u - ws_qs[:, :chunk]).astype(BF16)
        o = ws_qs[:, chunk:] + _bmm(qkm, vnb)
        kd3 = stack(kdec)
        upd = stack([_dot_ta(kd3[h], vnb[h]) for h in range(heads)])
        s3 = s3 * stack(gl) + upd
        on = o * lax.rsqrt(jnp.mean(o * o, axis=-1, keepdims=True) + NORM_EPS) * ng_ref[...] * _silu(stack(zc))
        for h in range(heads):
            o_ref[r0:r1, h * dv:(h + 1) * dv] = on[h]
    s_scr[...] = s3
    sout_ref[...] = s3


def _gdn(qkv, z, ba, conv_prev, s0, conv_w, aparams, norm_g, mseg, *, t_valid, chunk):
    nb, t, conv_dim = qkv.shape
    heads, dk, dv = s0.shape[1:]
    tt = min(GDN_TILE, t)
    assert t % tt == 0 and tt % chunk == 0 and (t_valid == tt or t == tt)
    body = functools.partial(_gdn_body, tt=tt, t_valid=t_valid, chunk=chunk, heads=heads, dk=dk, dv=dv)
    v_dim = heads * dv
    tok = lambda w: pl.BlockSpec((None, tt, w), lambda b, i: (b, i, 0))
    return pl.pallas_call(
        body,
        grid=(nb, t // tt),
        in_specs=[tok(conv_dim), tok(v_dim), tok(LANES),
                  pl.BlockSpec((None, SUBLANES, conv_dim), lambda b, i: (b, 0, 0)),
                  pl.BlockSpec((None, heads, dk, dv), lambda b, i: (b, 0, 0, 0)),
                  _resident((GDN_CONV, conv_dim)), _resident((2, LANES)), _resident((1, dv)),
                  _resident((LANES, LANES))],
        out_specs=[tok(v_dim),
                   pl.BlockSpec((None, SUBLANES, conv_dim), lambda b, i: (b, 0, 0)),
                   pl.BlockSpec((None, heads, dk, dv), lambda b, i: (b, 0, 0, 0))],
        out_shape=[jax.ShapeDtypeStruct((nb, t, v_dim), F32),
                   jax.ShapeDtypeStruct((nb, SUBLANES, conv_dim), F32),
                   jax.ShapeDtypeStruct((nb, heads, dk, dv), F32)],
        scratch_shapes=[pltpu.VMEM((tt + SUBLANES, conv_dim), F32), pltpu.VMEM((heads, dk, dv), F32)],
        compiler_params=_params("arbitrary", "arbitrary"),
        name="gdn",
    )(qkv, z, ba, conv_prev, s0, conv_w, aparams, norm_g.reshape(1, dv), mseg)


def _attn_body(q_ref, k_ref, c_ref, o_ref, m_scr, l_scr, acc_scr, *, heads, tq):
    i = pl.program_id(1)
    j = pl.program_id(2)

    @pl.when(j == 0)
    def _():
        m_scr[...] = jnp.full(m_scr.shape, NEG_BIG, F32)
        l_scr[...] = jnp.zeros(l_scr.shape, F32)
        acc_scr[...] = jnp.zeros(acc_scr.shape, F32)

    def step(masked):
        cb = c_ref[...]
        if masked:
            row = lax.broadcasted_iota(jnp.int32, (tq, tq), 0)
            col = lax.broadcasted_iota(jnp.int32, (tq, tq), 1)
            keep = row >= col
        for h in range(heads):
            s = _dot_t(q_ref[h], k_ref[h])
            if masked:
                s = jnp.where(keep, s, NEG_BIG)
            m_prev = m_scr[h][:, :1]
            l_prev = l_scr[h][:, :1]
            m_new = jnp.maximum(m_prev, jnp.max(s, axis=1, keepdims=True))
            alpha = jnp.exp(m_prev - m_new)
            p = jnp.exp(s - m_new)
            l_new = alpha * l_prev + jnp.sum(p, axis=1, keepdims=True)
            acc_scr[h] = acc_scr[h] * alpha + _dot(p.astype(BF16), cb)
            m_scr[h] = jnp.broadcast_to(m_new, (tq, LANES))
            l_scr[h] = jnp.broadcast_to(l_new, (tq, LANES))

    @pl.when(j < i)
    def _():
        step(False)

    @pl.when(j == i)
    def _():
        step(True)
        for h in range(heads):
            o_ref[:, h * LANES:(h + 1) * LANES] = (acc_scr[h] / l_scr[h][:, :1]).astype(BF16)


def _attn_prompt(q128, k128, cbf):
    nb, heads, t, _ = q128.shape
    tq = min(ATTN_TILE, t)
    assert t % tq == 0
    n = t // tq
    body = functools.partial(_attn_body, heads=heads, tq=tq)
    return pl.pallas_call(
        body,
        grid=(nb, n, n),
        in_specs=[pl.BlockSpec((None, heads, tq, LANES), lambda b, i, j: (b, 0, i, 0)),
                  pl.BlockSpec((None, heads, tq, LANES), lambda b, i, j: (b, 0, jnp.minimum(i, j), 0)),
                  pl.BlockSpec((None, tq, LANES), lambda b, i, j: (b, jnp.minimum(i, j), 0))],
        out_specs=pl.BlockSpec((None, tq, heads * LANES), lambda b, i, j: (b, i, 0)),
        out_shape=jax.ShapeDtypeStruct((nb, t, heads * LANES), BF16),
        scratch_shapes=[pltpu.VMEM((heads, tq, LANES), F32), pltpu.VMEM((heads, tq, LANES), F32),
                        pltpu.VMEM((heads, tq, LANES), F32)],
        compiler_params=_params("arbitrary", "arbitrary", "arbitrary"),
        name="attn_prompt",
    )(q128, k128, cbf)


def _attn_sample_body(pt_ref, qabs_ref, qrope_ref, q128_ref, k128n_ref, cn_ref, wukt_ref, *rest,
                      heads, dn, s_new, pages):
    c_pages, kr_pages = rest[:pages], rest[pages:2 * pages]
    o_ref, cb, krb, m_scr, l_scr, acc_scr = rest[2 * pages:]
    j = pl.program_id(1)
    rows = s_new * heads

    @pl.when(j == 0)
    def _():
        m_scr[...] = jnp.full(m_scr.shape, NEG_BIG, F32)
        l_scr[...] = jnp.zeros(l_scr.shape, F32)
        acc_scr[...] = jnp.zeros(acc_scr.shape, F32)

    for i in range(pages):
        cb[i * PAGE_SIZE:(i + 1) * PAGE_SIZE, :] = c_pages[i][...].astype(BF16)
        krb[i * PAGE_SIZE:(i + 1) * PAGE_SIZE, :] = kr_pages[i][...].astype(BF16)
    cbv = cb[...]
    knt = _dot_t(wukt_ref[...], cbv)
    ssq = jnp.concatenate([jnp.sum(knt[h * dn:(h + 1) * dn, :] ** 2, axis=0, keepdims=True)
                           for h in range(heads)], axis=0)
    r = lax.rsqrt(ssq * (1.0 / dn) + NORM_EPS)
    s = (_dot_t(qabs_ref[...], cbv) * jnp.concatenate([r] * s_new, axis=0)
         + _dot_t(qrope_ref[...], krb[...]))
    m_prev = m_scr[:, :1]
    m_new = jnp.maximum(m_prev, jnp.max(s, axis=1, keepdims=True))
    alpha = jnp.exp(m_prev - m_new)
    p = jnp.exp(s - m_new)
    l_new = alpha * l_scr[:, :1] + jnp.sum(p, axis=1, keepdims=True)
    acc = acc_scr[...] * alpha + _dot(p.astype(BF16), cbv)

    @pl.when(j < pl.num_programs(1) - 1)
    def _():
        m_scr[...] = jnp.broadcast_to(m_new, (rows, LANES))
        l_scr[...] = jnp.broadcast_to(l_new, (rows, LANES))
        acc_scr[...] = acc

    @pl.when(j == pl.num_programs(1) - 1)
    def _():
        qf = q128_ref[...].astype(F32)
        assert heads & (heads - 1) == 0
        tok = lax.shift_right_logical(lax.broadcasted_iota(jnp.int32, (rows, 1), 0), heads.bit_length() - 1)
        sn = []
        for t in range(s_new):
            kt = jnp.concatenate([k128n_ref[t].astype(F32)] * s_new, axis=0)
            st = jnp.sum(qf * kt, axis=1, keepdims=True)
            sn.append(jnp.where(tok >= t, st, NEG_BIG))
        m_fin = m_new
        for st in sn:
            m_fin = jnp.maximum(m_fin, st)
        a2 = jnp.exp(m_new - m_fin)
        l_fin = l_new * a2
        acc_fin = acc * a2
        cn = cn_ref[...].astype(BF16).astype(F32)
        for t in range(s_new):
            pt = jnp.exp(sn[t] - m_fin)
            l_fin = l_fin + pt
            acc_fin = acc_fin + pt.astype(BF16).astype(F32) * cn[t:t + 1, :]
        o_ref[...] = (acc_fin / l_fin).astype(BF16)


def _attn_sample(page_table, qabs, qrope, q128, k128n, c_new, wukt, cache_c, cache_kr, *, layer, heads, dn):
    nseq, rows, _ = qabs.shape
    s_new = rows // heads
    n_pages = page_table.shape[1]
    rope = cache_kr.shape[-1]
    pages = min(PAGES_PER_STEP, n_pages)
    assert n_pages % pages == 0
    body = functools.partial(_attn_sample_body, heads=heads, dn=dn, s_new=s_new, pages=pages)

    def page_spec(i, width):
        return pl.BlockSpec((None, None, PAGE_SIZE, width),
                            lambda b, j, pt: (layer, pt[b * n_pages + j * pages + i], 0, 0))

    per_seq = lambda shape: pl.BlockSpec((None,) + shape, lambda b, j, pt: (b,) + (0,) * len(shape))
    grid_spec = pltpu.PrefetchScalarGridSpec(
        num_scalar_prefetch=1,
        grid=(nseq, n_pages // pages),
        in_specs=[per_seq((rows, LANES)), per_seq((rows, rope)), per_seq((rows, LANES)),
                  per_seq((s_new, heads, LANES)), per_seq((s_new, LANES)),
                  pl.BlockSpec(wukt.shape, lambda b, j, pt: (0, 0))]
                 + [page_spec(i, LANES) for i in range(pages)]
                 + [page_spec(i, rope) for i in range(pages)],
        out_specs=per_seq((rows, LANES)),
        scratch_shapes=[pltpu.VMEM((pages * PAGE_SIZE, LANES), BF16), pltpu.VMEM((pages * PAGE_SIZE, rope), BF16),
                        pltpu.VMEM((rows, LANES), F32), pltpu.VMEM((rows, LANES), F32),
                        pltpu.VMEM((rows, LANES), F32)],
    )
    return pl.pallas_call(
        body, grid_spec=grid_spec,
        out_shape=jax.ShapeDtypeStruct((nseq, rows, LANES), BF16),
        compiler_params=_params("arbitrary", "arbitrary"),
        name="attn_sample",
    )(page_table.reshape(-1), qabs, qrope, q128, k128n, c_new, wukt,
      *([cache_c] * pages), *([cache_kr] * pages))


def _outproj_body(x_ref, m_ref, gdn_ref, ctx_ref, wuv_ref, wog_ref, wom_ref, o_ref, *, per_token):
    gate = _mod_row(m_ref, 5, per_token)
    mla = _dot(ctx_ref[...], wuv_ref[...])
    mix = _dot(gdn_ref[...].astype(BF16), wog_ref[...]) + _dot(mla.astype(BF16), wom_ref[...])
    o_ref[...] = x_ref[...] + gate * mix


def _outproj(x, mods, gdn_out, ctx, wuv_bd, wo_g, wo_m, *, per_token):
    nb, t, d = x.shape
    tm = min(TOKEN_TILE, t)
    tok = lambda w: pl.BlockSpec((None, tm, w), lambda b, i: (b, i, 0))
    body = functools.partial(_outproj_body, per_token=per_token)
    return pl.pallas_call(
        body,
        grid=(nb, t // tm),
        in_specs=[tok(d), _mods_spec(per_token, tm, d, 1), tok(gdn_out.shape[-1]), tok(ctx.shape[-1]),
                  _resident(wuv_bd.shape), _resident(wo_g.shape), _resident(wo_m.shape)],
        out_specs=tok(d),
        out_shape=jax.ShapeDtypeStruct(x.shape, F32),
        compiler_params=_params("arbitrary", "arbitrary"),
        name="outproj",
    )(x, mods, gdn_out, ctx, wuv_bd, wo_g, wo_m)


def _rope_table(pos, half):
    inv = ROPE_THETA ** (-jnp.arange(half, dtype=F32) / half)
    ang = pos[:, None] * inv[None, :]
    cos2 = jnp.concatenate([jnp.cos(ang), jnp.cos(ang)], axis=1)
    sin2 = jnp.concatenate([-jnp.sin(ang), jnp.sin(ang)], axis=1)
    t = pos.shape[0]
    pad = LANES - 4 * half
    one, zero = jnp.ones((t, pad), F32), jnp.zeros((t, pad), F32)
    return jnp.concatenate([one, cos2, sin2, zero, cos2, cos2, zero, sin2, sin2], axis=1)


def _swap_halves(a, axis=-1):
    lo, hi = jnp.split(a, 2, axis=axis)
    return jnp.concatenate([hi, lo], axis=axis)


def _prep_layer(lp, dims):
    heads, dn, dr, rank, g_heads, conv_dim, v_dim = dims
    assert dn + 2 * dr == LANES and rank == LANES and 2 * g_heads <= LANES
    w_in = lp['w_in']
    d = w_in.shape[0]
    o = 0
    w_qkv = w_in[:, o:o + conv_dim]; o += conv_dim
    w_z = w_in[:, o:o + v_dim]; o += v_dim
    w_b = w_in[:, o:o + g_heads]; o += g_heads
    w_a = w_in[:, o:o + g_heads]; o += g_heads
    w_q = w_in[:, o:o + heads * (dn + dr)].reshape(d, heads, dn + dr); o += heads * (dn + dr)
    w_c = w_in[:, o:o + rank]; o += rank
    w_kr = w_in[:, o:o + dr]
    zeros = lambda n: jnp.zeros((d, n), F32)
    w_q128 = jnp.concatenate([w_q, _swap_halves(w_q[:, :, dn:])], axis=2).reshape(d, heads * LANES)
    w_krs = _swap_halves(w_kr)
    w_all = jnp.concatenate([w_qkv, w_z, w_b, w_a, zeros(LANES - 2 * g_heads), w_q128, w_c,
                             zeros(dn), w_kr, w_kr, zeros(dn), w_krs, w_krs], axis=1).astype(BF16)
    scale = (dn + dr) ** -0.5
    qr_g = lp['mla_qr_norm']
    gq = (jnp.concatenate([lp['mla_qn_norm'], qr_g, _swap_halves(qr_g)]) * scale).reshape(1, LANES)
    lane = jnp.arange(LANES)
    seg = jnp.where(lane < dn, 0, jnp.where(lane < dn + dr, 1, 2))
    seg_len = jnp.where(lane < dn, dn, dr).astype(F32)
    mq = jnp.where(seg[:, None] == seg[None, :], 1.0 / seg_len[None, :], 0.0).astype(BF16)
    mk = jnp.where((lane[:, None] < dn) & (lane[None, :] < dn), 1.0 / dn, 0.0).astype(BF16)
    gk = jnp.concatenate([lp['mla_kn_norm'], jnp.zeros((LANES - dn,), F32)]).reshape(1, LANES)
    kr_g = lp['mla_kr_norm']
    zdn = jnp.zeros((dn,), F32)
    gkr = jnp.stack([jnp.concatenate([zdn, kr_g, kr_g]),
                     jnp.concatenate([zdn, _swap_halves(kr_g), _swap_halves(kr_g)])])
    w_uk = lp['mla_w_uk']
    wuk = jnp.concatenate([w_uk, jnp.zeros((rank, heads, LANES - dn), F32)], axis=2)
    wuk = wuk.reshape(rank, heads * LANES).astype(BF16)
    wukt = w_uk.transpose(1, 2, 0).reshape(heads * dn, rank).astype(BF16)
    absorb = jnp.concatenate([w_uk.transpose(1, 2, 0) * lp['mla_kn_norm'][None, :, None],
                              jnp.zeros((heads, LANES - dn, rank), F32)], axis=1)
    fold = jnp.zeros((LANES, LANES), F32)
    fold = fold.at[dn + jnp.arange(dr), jnp.arange(dr)].set(1.0).at[dn + dr + jnp.arange(dr), jnp.arange(dr)].set(1.0)
    wqa = jnp.concatenate([absorb, jnp.broadcast_to(fold, (heads, LANES, LANES))], axis=2).astype(BF16)
    w_uv = lp['mla_w_uv']
    mv = w_uv.shape[2]
    wuv_bd = (w_uv.transpose(1, 0, 2)[:, :, None, :] * jnp.eye(heads, dtype=F32)[:, None, :, None])
    wuv_bd = wuv_bd.reshape(heads * rank, heads * mv).astype(BF16)
    dff = lp['ffn1_wo'].shape[0]
    lane_h = lane // (LANES // 2)
    ms = jnp.where(lane_h[:, None] == lane_h[None, :], 1.0, 0.0).astype(BF16)
    aparams = jnp.zeros((2, LANES), F32)
    aparams = aparams.at[0, g_heads:2 * g_heads].set(jnp.exp(lp['gdn_a_log']))
    aparams = aparams.at[1, g_heads:2 * g_heads].set(lp['gdn_dt_bias'])
    return dict(
        w_all=w_all, gq=gq, mq=mq, gc=lp['mla_ckv_norm'].reshape(1, LANES), wuk=wuk, mk=mk, gk=gk, gkr=gkr,
        wukt=wukt, wqa=wqa, wuv_bd=wuv_bd, ms=ms, aparams=aparams,
        wo_g=lp['w_out'][:v_dim].astype(BF16), wo_m=lp['w_out'][v_dim:].astype(BF16),
        f1=(lp['ffn1_wi'][:, :dff].astype(BF16), lp['ffn1_wi'][:, dff:].astype(BF16), lp['ffn1_wo'].astype(BF16)),
        f2=(lp['ffn2_wi'][:, :dff].astype(BF16), lp['ffn2_wi'][:, dff:].astype(BF16), lp['ffn2_wo'].astype(BF16)),
    )


def _layer(x, mods, tab, conv_prev, s0, lp, w, lay, dims, *, per_token, t_valid, chunk, attend):
    heads = dims[0]
    x = _ffn(x, mods, lp['norm_ffn1'], *w['f1'], mod0=0, per_token=per_token)
    outs = _inproj(x, mods, lp['norm_mix'], w['w_all'], tab, w['gq'], w['mq'], w['gc'], w['wuk'], w['mk'],
                   w['gk'], w['gkr'], w['wqa'] if per_token else None, lay=lay, heads=heads, per_token=per_token)
    qkv, z, ba, q128, k128, c, cbf, krot = outs[:8]
    pad = qkv.shape[1] - t_valid
    gdn_out, nconv, s_new = _gdn(qkv, z, ba, conv_prev, s0, lp['gdn_conv_w'], w['aparams'], lp['gdn_norm'],
                                 w['ms'], t_valid=t_valid, chunk=chunk)
    ctx = attend(outs)
    x = _outproj(x, mods, gdn_out, ctx, w['wuv_bd'], w['wo_g'], w['wo_m'], per_token=per_token)
    x = _ffn(x, mods, lp['norm_ffn2'], *w['f2'], mod0=6, per_token=per_token)
    del pad
    return x, c, krot, nconv, s_new


def kernel(x_prompt, x_sample, cache_ckv, cache_krope, state_conv, state_gdn, page_table, c_prompt, c_sample,
           ada_w, ada_b, norm_ffn1, ffn1_wi, ffn1_wo, norm_mix, w_in, gdn_conv_w, gdn_a_log, gdn_dt_bias, gdn_norm,
           mla_qn_norm, mla_qr_norm, mla_ckv_norm, mla_kr_norm, mla_kn_norm, mla_w_uk, mla_w_uv, w_out,
           norm_ffn2, ffn2_wi, ffn2_wo):
    depth = ada_w.shape[0]
    bp, tp, d = x_prompt.shape
    bs, ts, _ = x_sample.shape
    g_heads, dk, dv = state_gdn.shape[2:]
    conv_dim = state_conv.shape[-1]
    v_dim = g_heads * dv
    rank, heads, dn = mla_w_uk.shape[1:]
    dr = mla_qr_norm.shape[1]
    past = page_table.shape[1] * PAGE_SIZE
    dims = (heads, dn, dr, rank, g_heads, conv_dim, v_dim)
    lay = _InLayout(conv_dim, v_dim, heads)
    ts_pad = -(-ts // SUBLANES) * SUBLANES

    tab_p = _rope_table(jnp.arange(tp, dtype=F32), dr // 2)
    tab_s = jnp.tile(_rope_table(past + jnp.arange(ts_pad, dtype=F32), dr // 2), (bs, 1))
    cond = jnp.concatenate([c_prompt, c_sample], axis=0)
    rows = -(-cond.shape[0] // SUBLANES) * SUBLANES
    cond = jnp.pad(cond, ((0, rows - cond.shape[0]), (0, 0)))

    yp = x_prompt
    ys = jnp.pad(x_sample, ((0, 0), (0, ts_pad - ts), (0, 0))).reshape(1, bs * ts_pad, d)
    outs_p, outs_s = [], []
    for l in range(depth):
        lp = dict(norm_ffn1=norm_ffn1[l], ffn1_wi=ffn1_wi[l], ffn1_wo=ffn1_wo[l], norm_mix=norm_mix[l], w_in=w_in[l],
                  gdn_conv_w=gdn_conv_w[l], gdn_a_log=gdn_a_log[l], gdn_dt_bias=gdn_dt_bias[l], gdn_norm=gdn_norm[l],
                  mla_qn_norm=mla_qn_norm[l], mla_qr_norm=mla_qr_norm[l], mla_ckv_norm=mla_ckv_norm[l],
                  mla_kr_norm=mla_kr_norm[l], mla_kn_norm=mla_kn_norm[l], mla_w_uk=mla_w_uk[l],
                  mla_w_uv=mla_w_uv[l], w_out=w_out[l], norm_ffn2=norm_ffn2[l], ffn2_wi=ffn2_wi[l],
                  ffn2_wo=ffn2_wo[l])
        w = _prep_layer(lp, dims)
        mods = _mods(cond, ada_w[l], ada_b[l])
        mods_p = mods[:bp].reshape(bp, N_MOD, d)
        mods_s = jnp.repeat(mods[bp:bp + bs].reshape(bs, N_MOD, d), ts_pad, axis=0).transpose(1, 0, 2)

        conv0 = jnp.zeros((bp, SUBLANES, conv_dim), F32)
        s0 = jnp.zeros((bp, g_heads, dk, dv), F32)
        attend_p = lambda o: _attn_prompt(o[3], o[4], o[6])
        yp, c_p, kr_p, cv_p, s_p = _layer(yp, mods_p, tab_p, conv0, s0, lp, w, lay, dims, per_token=False,
                                          t_valid=min(GDN_TILE, tp), chunk=min(GDN_CHUNK, tp), attend=attend_p)
        outs_p.append((c_p, kr_p[..., dn:dn + dr], cv_p[:, SUBLANES - (GDN_CONV - 1):], s_p))

        conv_prev = jnp.pad(state_conv[l], ((0, 0), (SUBLANES - (GDN_CONV - 1), 0), (0, 0)))

        def attend_s(o, l=l, w=w):
            q128, k128, c_new, qabs, qrope = o[3], o[4], o[5], o[8], o[9]
            by_seq = lambda a: a[0].reshape(heads, bs, ts_pad, -1)[:, :, :ts].transpose(1, 2, 0, 3)
            flat = lambda a: by_seq(a).reshape(bs, ts * heads, -1)
            ctx = _attn_sample(page_table, flat(qabs), flat(qrope)[..., :dr], flat(q128), by_seq(k128),
                               c_new.reshape(bs, ts_pad, -1)[:, :ts], w['wukt'], cache_ckv, cache_krope,
                               layer=l, heads=heads, dn=dn)
            ctx = ctx.reshape(bs, ts, heads * rank)
            return jnp.pad(ctx, ((0, 0), (0, ts_pad - ts), (0, 0))).reshape(1, bs * ts_pad, heads * rank)

        ys, c_s, kr_s, cv_s, s_s = _layer_sample(ys, mods_s, tab_s, conv_prev, state_gdn[l], lp, w, lay, dims,
                                                 bs, ts, ts_pad, attend_s)
        outs_s.append((c_s, kr_s, cv_s, s_s))

    stack = lambda outs, k: jnp.stack([o[k] for o in outs])
    ys = ys.reshape(bs, ts_pad, d)[:, :ts]
    return (yp, ys, stack(outs_p, 0), stack(outs_p, 1), stack(outs_p, 2), stack(outs_p, 3),
            stack(outs_s, 0), stack(outs_s, 1), stack(outs_s, 2), stack(outs_s, 3))


def _layer_sample(x, mods, tab, conv_prev, s0, lp, w, lay, dims, bs, ts, ts_pad, attend):
    heads, dn, dr = dims[0], dims[1], dims[2]
    d = x.shape[-1]
    x = _ffn(x, mods, lp['norm_ffn1'], *w['f1'], mod0=0, per_token=True)
    outs = _inproj(x, mods, lp['norm_mix'], w['w_all'], tab, w['gq'], w['mq'], w['gc'], w['wuk'], w['mk'],
                   w['gk'], w['gkr'], w['wqa'], lay=lay, heads=heads, per_token=True)
    qkv, z, ba, c, krot = outs[0], outs[1], outs[2], outs[5], outs[7]
    seq = lambda a: a.reshape(bs, ts_pad, a.shape[-1])
    gdn_out, nconv, s_new = _gdn(seq(qkv), seq(z), seq(ba), conv_prev, s0, lp['gdn_conv_w'], w['aparams'],
                                 lp['gdn_norm'], w['ms'], t_valid=ts, chunk=ts_pad)
    ctx = attend(outs)
    x = _outproj(x, mods, gdn_out.reshape(1, bs * ts_pad, -1), ctx, w['wuv_bd'], w['wo_g'], w['wo_m'],
                 per_token=True)
    x = _ffn(x, mods, lp['norm_ffn2'], *w['f2'], mod0=6, per_token=True)
    c_s = seq(c)[:, :ts]
    kr_s = seq(krot)[:, :ts, dn:dn + dr]
    return x, c_s, kr_s, nconv[:, SUBLANES - (GDN_CONV - 1):], s_new
```

```python
import functools
import math

import jax
import jax.numpy as jnp
from jax import lax
from jax.experimental import pallas as pl
from jax.experimental.pallas import tpu as pltpu

F32 = jnp.float32
BF16 = jnp.bfloat16

NORM_EPS = 1e-6
ROPE_THETA = 10000.0
PAGE_SIZE = 128
GDN_CONV = 4
GDN_CHUNK = 64
N_MOD = 9
LANES = 128
SUBLANES = 8
NEG_BIG = -1e30
VMEM_LIMIT = 56 * 1024 * 1024

TOKEN_TILE = 512
ATTN_TILE = 512
GDN_TILE = 256
SUB_PAGES = 16


def _dot(a, b):
    return jnp.dot(a, b, preferred_element_type=F32)


def _dot_t(a, b):
    return lax.dot_general(a, b, (((1,), (1,)), ((), ())), preferred_element_type=F32)


def _dot_ta(a, b):
    return lax.dot_general(a, b, (((0,), (0,)), ((), ())), preferred_element_type=F32)


def _bmm(a, b):
    return lax.dot_general(a, b, (((2,), (1,)), ((0,), (0,))), preferred_element_type=F32)


def _bmm_t(a, b):
    return lax.dot_general(a, b, (((2,), (2,)), ((0,), (0,))), preferred_element_type=F32)


def _sigmoid(x):
    return 1.0 / (1.0 + jnp.exp(-x))


def _silu(x):
    return x * _sigmoid(x)


def _params(*sem):
    return pltpu.CompilerParams(dimension_semantics=sem, vmem_limit_bytes=VMEM_LIMIT)


def _resident(shape):
    nd = len(shape)
    return pl.BlockSpec(shape, lambda *_: (0,) * nd, pipeline_mode=pl.Buffered(1))


MODS_PER_SUBLAYER = 3


def _mod_row(m_ref, k, per_token):
    return m_ref[k % MODS_PER_SUBLAYER] if per_token else m_ref[k:k + 1, :]


def _mod_norm(x, gain, shift, scale):
    y = x * lax.rsqrt(jnp.mean(x * x, axis=-1, keepdims=True) + NORM_EPS)
    return (y * gain) * (1.0 + scale) + shift


def _mods_spec(per_token, tm, d, sublayer):
    if per_token:
        return pl.BlockSpec((MODS_PER_SUBLAYER, tm, d), lambda b, i: (sublayer, i, 0))
    return pl.BlockSpec((None, N_MOD, d), lambda b, i: (b, 0, 0))


def _mods_body(c_ref, w_ref, b_ref, o_ref):
    c = c_ref[...]
    o_ref[...] = _dot(_silu(c).astype(BF16), w_ref[...].astype(BF16)) + b_ref[...]


def _mods(cond, ada_w, ada_b):
    rows, d = cond.shape
    n = ada_w.shape[1]
    tn = d
    return pl.pallas_call(
        _mods_body,
        grid=(n // tn,),
        in_specs=[pl.BlockSpec((rows, d), lambda j: (0, 0)),
                  pl.BlockSpec((d, tn), lambda j: (0, j)),
                  pl.BlockSpec((1, tn), lambda j: (0, j))],
        out_specs=pl.BlockSpec((rows, tn), lambda j: (0, j)),
        out_shape=jax.ShapeDtypeStruct((rows, n), F32),
        compiler_params=_params("arbitrary"),
        name="mods",
    )(cond, ada_w, ada_b.reshape(1, n))


def _ffn_body(x_ref, m_ref, g_ref, wg_ref, wu_ref, wo_ref, o_ref, *, mod0, per_token, fc):
    x = x_ref[...]
    shift, scale, gate = (_mod_row(m_ref, mod0 + k, per_token) for k in range(3))
    h = _mod_norm(x, g_ref[...], shift, scale).astype(BF16)
    acc = jnp.zeros(x.shape, F32)
    for c in range(wg_ref.shape[1] // fc):
        lo, hi = c * fc, (c + 1) * fc
        a = _silu(_dot(h, wg_ref[:, lo:hi])) * _dot(h, wu_ref[:, lo:hi])
        acc = acc + _dot(a.astype(BF16), wo_ref[lo:hi, :])
    o_ref[...] = x + 0.5 * gate * acc


def _ffn(x, mods, gain, wg, wu, wo, *, mod0, per_token):
    nb, t, d = x.shape
    tm = min(TOKEN_TILE, t)
    dff = wg.shape[1]
    fc = 2 * LANES
    assert t % tm == 0 and dff % fc == 0
    body = functools.partial(_ffn_body, mod0=mod0, per_token=per_token, fc=fc)
    return pl.pallas_call(
        body,
        grid=(nb, t // tm),
        in_specs=[pl.BlockSpec((None, tm, d), lambda b, i: (b, i, 0)),
                  _mods_spec(per_token, tm, d, mod0 // MODS_PER_SUBLAYER),
                  _resident((1, d)), _resident((d, dff)), _resident((d, dff)), _resident((dff, d))],
        out_specs=pl.BlockSpec((None, tm, d), lambda b, i: (b, i, 0)),
        out_shape=jax.ShapeDtypeStruct(x.shape, F32),
        compiler_params=_params("arbitrary", "arbitrary"),
        name="ffn",
    )(x, mods, gain.reshape(1, d), wg, wu, wo)


class _InLayout:
    def __init__(self, conv_dim, v_dim, heads):
        self.qkv = (0, conv_dim)
        self.z = (conv_dim, conv_dim + v_dim)
        self.ba = (self.z[1], self.z[1] + LANES)
        self.q = (self.ba[1], self.ba[1] + heads * LANES)
        self.ckv = (self.q[1], self.q[1] + LANES)
        self.kr = (self.ckv[1], self.ckv[1] + LANES)
        self.krs = (self.kr[1], self.kr[1] + LANES)
        self.total = self.krs[1]


def _inproj_body(x_ref, m_ref, g_ref, w_ref, tab_ref, gq_ref, mq_ref, gc_ref, wuk_ref, mk_ref, gk_ref, gkr_ref,
                 *rest, lay, heads, per_token, sample):
    if sample:
        wqa_ref, rest = rest[0], rest[1:]
    qkv_ref, z_ref, ba_ref, q128_ref, k128_ref, c_ref, cbf_ref, krot_ref = rest[:8]
    x = x_ref[...]
    shift, scale = _mod_row(m_ref, 3, per_token), _mod_row(m_ref, 4, per_token)
    h = _mod_norm(x, g_ref[...], shift, scale).astype(BF16)
    p = _dot(h, w_ref[...])
    qkv_ref[...] = p[:, lay.qkv[0]:lay.qkv[1]]
    z_ref[...] = p[:, lay.z[0]:lay.z[1]]
    ba_ref[...] = p[:, lay.ba[0]:lay.ba[1]]

    tab = tab_ref[...]
    tab_q, tab_c, tab_s = tab[:, :LANES], tab[:, LANES:2 * LANES], tab[:, 2 * LANES:]
    for hh in range(heads):
        qh = p[:, lay.q[0] + hh * LANES: lay.q[0] + (hh + 1) * LANES]
        msq = _dot((qh * qh).astype(BF16), mq_ref[...])
        qn = qh * lax.rsqrt(msq + NORM_EPS) * gq_ref[...] * tab_q
        q128_ref[hh] = qn.astype(BF16)
        if sample:
            qabs_ref, qrope_ref = rest[8], rest[9]
            qa = _dot(qn.astype(BF16), wqa_ref[hh])
            qabs_ref[hh] = qa[:, :LANES].astype(BF16)
            qrope_ref[hh] = qa[:, LANES:2 * LANES].astype(BF16)

    ckv = p[:, lay.ckv[0]:lay.ckv[1]]
    c = ckv * lax.rsqrt(jnp.mean(ckv * ckv, axis=-1, keepdims=True) + NORM_EPS) * gc_ref[...]
    c_ref[...] = c
    cb = c.astype(BF16)
    cbf_ref[...] = cb
    kr = p[:, lay.kr[0]:lay.kr[1]]
    krs = p[:, lay.krs[0]:lay.krs[1]]
    inv = lax.rsqrt(jnp.sum(kr * kr, axis=-1, keepdims=True) * (2.0 / LANES) + NORM_EPS)
    krot = kr * inv * gkr_ref[0:1, :] * tab_c + krs * inv * gkr_ref[1:2, :] * tab_s
    krot_ref[...] = krot
    knr = _dot(cb, wuk_ref[...])
    for hh in range(heads):
        kh = knr[:, hh * LANES:(hh + 1) * LANES]
        msq = _dot((kh * kh).astype(BF16), mk_ref[...])
        k128_ref[hh] = (kh * lax.rsqrt(msq + NORM_EPS) * gk_ref[...] + krot).astype(BF16)


def _inproj(x, mods, gain, w_all, tab, gq, mq, gc, wuk, mk, gk, gkr, wqa, *, lay, heads, per_token):
    nb, t, d = x.shape
    tm = min(TOKEN_TILE, t)
    assert t % tm == 0
    sample = wqa is not None
    body = functools.partial(_inproj_body, lay=lay, heads=heads, per_token=per_token, sample=sample)
    tok = lambda w: pl.BlockSpec((None, tm, w), lambda b, i: (b, i, 0))
    hd = lambda w: pl.BlockSpec((None, heads, tm, w), lambda b, i: (b, 0, i, 0))
    conv_dim, v_dim = lay.qkv[1], lay.z[1] - lay.z[0]
    in_specs = [tok(d), _mods_spec(per_token, tm, d, 1), _resident((1, d)), _resident(w_all.shape),
                pl.BlockSpec((tm, 3 * LANES), lambda b, i: (i, 0)),
                _resident((1, LANES)), _resident((LANES, LANES)), _resident((1, LANES)),
                _resident(wuk.shape), _resident((LANES, LANES)), _resident((1, LANES)), _resident((2, LANES))]
    args = [x, mods, gain.reshape(1, d), w_all, tab, gq, mq, gc, wuk, mk, gk, gkr]
    out_specs = [tok(conv_dim), tok(v_dim), tok(LANES), hd(LANES), hd(LANES), tok(LANES), tok(LANES), tok(LANES)]
    out_shape = [jax.ShapeDtypeStruct((nb, t, conv_dim), F32), jax.ShapeDtypeStruct((nb, t, v_dim), F32),
                 jax.ShapeDtypeStruct((nb, t, LANES), F32),
                 jax.ShapeDtypeStruct((nb, heads, t, LANES), BF16), jax.ShapeDtypeStruct((nb, heads, t, LANES), BF16),
                 jax.ShapeDtypeStruct((nb, t, LANES), F32), jax.ShapeDtypeStruct((nb, t, LANES), BF16),
                 jax.ShapeDtypeStruct((nb, t, LANES), F32)]
    if sample:
        in_specs.append(_resident(wqa.shape))
        args.append(wqa)
        out_specs += [hd(LANES), hd(LANES)]
        out_shape += [jax.ShapeDtypeStruct((nb, heads, t, LANES), BF16)] * 2
    return pl.pallas_call(
        body, grid=(nb, t // tm), in_specs=in_specs, out_specs=out_specs, out_shape=out_shape,
        compiler_params=_params("arbitrary", "arbitrary"), name="inproj",
    )(*args)


def _unit_lower_inverse(a):
    n = a.shape[-1]
    assert n & (n - 1) == 0
    row = lax.broadcasted_iota(jnp.int32, (n, n), 0)
    col = lax.broadcasted_iota(jnp.int32, (n, n), 1)
    x = jnp.broadcast_to(jnp.where(row == col, 1.0, 0.0), a.shape)
    b = 1
    while b < n:
        lo_mask = (jnp.bitwise_xor(row, col) < 2 * b) & (jnp.bitwise_and(row, b) != 0) & (jnp.bitwise_and(col, b) == 0)
        lo = jnp.where(lo_mask, a, 0.0)
        if b == 1:
            x = x - lo
        else:
            xb = x.astype(BF16)
            x = x - _bmm(xb, _bmm(lo.astype(BF16), xb).astype(BF16))
        b *= 2
    return x


def _gdn_body(qkv_ref, z_ref, ba_ref, cprev_ref, s0_ref, cw_ref, ap_ref, ng_ref, ms_ref,
              o_ref, nconv_ref, sout_ref, xbuf, s_scr,
              *, tt, t_valid, chunk, heads, dk, dv):
    j = pl.program_id(1)
    hist = SUBLANES

    @pl.when(j == 0)
    def _():
        xbuf[0:hist, :] = cprev_ref[...]
        s_scr[...] = s0_ref[...]

    xbuf[hist:hist + tt, :] = qkv_ref[...]
    conv = xbuf[hist - 3:hist - 3 + tt, :] * cw_ref[0:1, :]
    for jj in range(1, GDN_CONV):
        conv = conv + xbuf[hist - 3 + jj:hist - 3 + jj + tt, :] * cw_ref[jj:jj + 1, :]
    y = _silu(conv)
    tail = xbuf[t_valid:t_valid + hist, :]
    nconv_ref[...] = tail
    xbuf[0:hist, :] = tail

    qk_dim = heads * dk

    def l2(v):
        parts = [_dot((v[:, i * LANES:(i + 1) * LANES] ** 2).astype(BF16), ms_ref[...])
                 for i in range(v.shape[1] // LANES)]
        return v * lax.rsqrt(jnp.concatenate(parts, axis=1) + NORM_EPS)

    q = l2(y[:, :qk_dim]) * (dk ** -0.5)
    k = l2(y[:, qk_dim:2 * qk_dim])
    v = y[:, 2 * qk_dim:]
    ba = ba_ref[...]
    beta = _sigmoid(ba)
    xg = ba + ap_ref[1:2, :]
    g = -ap_ref[0:1, :] * (jnp.maximum(xg, 0.0) + jnp.log1p(jnp.exp(-jnp.abs(xg))))
    if t_valid < tt:
        keep = lax.broadcasted_iota(jnp.int32, (tt, 1), 0) < t_valid
        q, k, v = (jnp.where(keep, a, 0.0) for a in (q, k, v))
        beta, g = jnp.where(keep, beta, 0.0), jnp.where(keep, g, 0.0)
    row = lax.broadcasted_iota(jnp.int32, (tt, tt), 0)
    col = lax.broadcasted_iota(jnp.int32, (tt, tt), 1)
    assert chunk & (chunk - 1) == 0
    tri = jnp.where((row >= col) & (jnp.bitwise_xor(row, col) < chunk), 1.0, 0.0).astype(BF16)
    g_hi = g.astype(BF16)
    g_lo = (g - g_hi.astype(F32)).astype(BF16)
    gc = _dot(tri, g_hi) + _dot(tri, g_lo)
    gct = gc.T
    z = z_ref[...]

    crow = lax.broadcasted_iota(jnp.int32, (chunk, chunk), 0)
    ccol = lax.broadcasted_iota(jnp.int32, (chunk, chunk), 1)
    causal = crow >= ccol
    strict = crow > ccol
    stack = lambda xs: jnp.stack(xs, axis=0)

    s3 = s_scr[...]
    for c in range(tt // chunk):
        r0, r1 = c * chunk, (c + 1) * chunk
        qkb, kbf, vbeta, kbeg, qdec, kdec, decay, gl, zc = ([] for _ in range(9))
        for h in range(heads):
            qh = q[r0:r1, h * dk:(h + 1) * dk]
            kh = k[r0:r1, h * dk:(h + 1) * dk]
            vh = v[r0:r1, h * dv:(h + 1) * dv]
            bh = beta[r0:r1, h:h + 1]
            gcol = gc[r0:r1, heads + h:heads + h + 1]
            grow = gct[heads + h:heads + h + 1, r0:r1]
            glast = grow[:, chunk - 1:chunk]
            eg = jnp.exp(gcol)
            kb = kh * bh
            qkb.append(jnp.concatenate([qh, kb], axis=0).astype(BF16))
            kbf.append(kh.astype(BF16))
            vbeta.append((vh * bh).astype(BF16))
            kbeg.append((kb * eg).astype(BF16))
            qdec.append((qh * eg).astype(BF16))
            kdec.append((kh * jnp.exp(glast - gcol)).astype(BF16))
            decay.append(jnp.exp(jnp.where(causal, jnp.broadcast_to(gcol, (chunk, chunk)) - grow, NEG_BIG)))
            gl.append(jnp.exp(glast))
            zc.append(z[r0:r1, h * dv:(h + 1) * dv])
        decay3 = stack(decay)
        sc = _bmm_t(stack(qkb), stack(kbf))
        qkm = (sc[:, :chunk] * decay3).astype(BF16)
        m = jnp.where(strict, sc[:, chunk:] * decay3, 0.0)
        tinv = _unit_lower_inverse(m).astype(BF16)
        u = _bmm(tinv, stack(vbeta))
        w = _bmm(tinv, stack(kbeg))
        sb = s3.astype(BF16)
        ws_qs = _bmm(jnp.concatenate([w.astype(BF16), stack(qdec)], axis=1), sb)
        vnb = (u - ws_qs[:, :chunk]).astype(BF16)
        o = ws_qs[:, chunk:] + _bmm(qkm, vnb)
        kd3 = stack(kdec)
        upd = stack([_dot_ta(kd3[h], vnb[h]) for h in range(heads)])
        s3 = s3 * stack(gl) + upd
        on = o * lax.rsqrt(jnp.mean(o * o, axis=-1, keepdims=True) + NORM_EPS) * ng_ref[...] * _silu(stack(zc))
        for h in range(heads):
            o_ref[r0:r1, h * dv:(h + 1) * dv] = on[h]
    s_scr[...] = s3
    sout_ref[...] = s3


def _gdn(qkv, z, ba, conv_prev, s0, conv_w, aparams, norm_g, mseg, *, t_valid, chunk):
    nb, t, conv_dim = qkv.shape
    heads, dk, dv = s0.shape[1:]
    tt = min(GDN_TILE, t)
    assert t % tt == 0 and tt % chunk == 0 and (t_valid == tt or t == tt)
    body = functools.partial(_gdn_body, tt=tt, t_valid=t_valid, chunk=chunk, heads=heads, dk=dk, dv=dv)
    v_dim = heads * dv
    tok = lambda w: pl.BlockSpec((None, tt, w), lambda b, i: (b, i, 0))
    return pl.pallas_call(
        body,
        grid=(nb, t // tt),
        in_specs=[tok(conv_dim), tok(v_dim), tok(LANES),
                  pl.BlockSpec((None, SUBLANES, conv_dim), lambda b, i: (b, 0, 0)),
                  pl.BlockSpec((None, heads, dk, dv), lambda b, i: (b, 0, 0, 0)),
                  _resident((GDN_CONV, conv_dim)), _resident((2, LANES)), _resident((1, dv)),
                  _resident((LANES, LANES))],
        out_specs=[tok(v_dim),
                   pl.BlockSpec((None, SUBLANES, conv_dim), lambda b, i: (b, 0, 0)),
                   pl.BlockSpec((None, heads, dk, dv), lambda b, i: (b, 0, 0, 0))],
        out_shape=[jax.ShapeDtypeStruct((nb, t, v_dim), F32),
                   jax.ShapeDtypeStruct((nb, SUBLANES, conv_dim), F32),
                   jax.ShapeDtypeStruct((nb, heads, dk, dv), F32)],
        scratch_shapes=[pltpu.VMEM((tt + SUBLANES, conv_dim), F32), pltpu.VMEM((heads, dk, dv), F32)],
        compiler_params=_params("arbitrary", "arbitrary"),
        name="gdn",
    )(qkv, z, ba, conv_prev, s0, conv_w, aparams, norm_g.reshape(1, dv), mseg)


def _attn_body(q_ref, k_ref, ct_ref, o_ref, m_scr, acc_scr, *, heads, tq, rank):
    i = pl.program_id(1)
    j = pl.program_id(2)

    @pl.when(j == 0)
    def _():
        m_scr[...] = jnp.full(m_scr.shape, NEG_BIG, F32)
        acc_scr[...] = jnp.zeros(acc_scr.shape, F32)

    def step(masked):
        ct = ct_ref[...]
        m_old = [m_scr[h] for h in range(heads)]
        acc_old = [acc_scr[h] for h in range(heads)]
        if masked:
            key = lax.broadcasted_iota(jnp.int32, (tq, tq), 0)
            qry = lax.broadcasted_iota(jnp.int32, (tq, tq), 1)
            keep = key <= qry
        m_out, acc_out = [], []
        for h in range(heads):
            st = _dot_t(k_ref[h], q_ref[h])
            if masked:
                st = jnp.where(keep, st, NEG_BIG)
            m_new = jnp.maximum(m_old[h], jnp.max(st, axis=0, keepdims=True))
            alpha = jnp.exp(m_old[h] - m_new)
            pt = jnp.exp(st - m_new).astype(BF16)
            acc_out.append(acc_old[h] * alpha + _dot(ct, pt))
            m_out.append(m_new)
        for h in range(heads):
            m_scr[h] = m_out[h]
            acc_scr[h] = acc_out[h]
        return acc_out

    @pl.when(j < i)
    def _():
        step(False)

    @pl.when(j == i)
    def _():
        acc = step(True)
        for h in range(heads):
            ctx = acc[h][:rank, :] / acc[h][rank:rank + 1, :]
            o_ref[:, h * rank:(h + 1) * rank] = ctx.T.astype(BF16)


def _attn_prompt(q128, k128, ct_ext):
    nb, heads, t, _ = q128.shape
    rows = ct_ext.shape[1]
    rank = rows - SUBLANES
    tq = min(ATTN_TILE, t)
    assert t % tq == 0
    n = t // tq
    body = functools.partial(_attn_body, heads=heads, tq=tq, rank=rank)
    return pl.pallas_call(
        body,
        grid=(nb, n, n),
        in_specs=[pl.BlockSpec((None, heads, tq, LANES), lambda b, i, j: (b, 0, i, 0)),
                  pl.BlockSpec((None, heads, tq, LANES), lambda b, i, j: (b, 0, jnp.minimum(i, j), 0)),
                  pl.BlockSpec((None, rows, tq), lambda b, i, j: (b, 0, jnp.minimum(i, j)))],
        out_specs=pl.BlockSpec((None, tq, heads * rank), lambda b, i, j: (b, i, 0)),
        out_shape=jax.ShapeDtypeStruct((nb, t, heads * rank), BF16),
        scratch_shapes=[pltpu.VMEM((heads, 1, tq), F32), pltpu.VMEM((heads, rows, tq), F32)],
        compiler_params=_params("arbitrary", "arbitrary", "arbitrary"),
        name="attn_prompt",
    )(q128, k128, ct_ext)


def _attn_sample_body(pt_ref, qabs_ref, qrope_ref, q128_ref, k128n_ref, cn_ref, wukt_ref, cache_c, cache_kr,
                      o_ref, cbuf, kbuf, cb, krb, s_all, sem, *, layer, heads, dn, s_new, n_pages, sub_pages):
    b = pl.program_id(0)
    nseq = pl.num_programs(0)
    slot = lax.rem(b, 2)
    rows = s_new * heads
    n_sub = n_pages // sub_pages
    sub_keys = sub_pages * PAGE_SIZE

    def page_copies(seq, slot_):
        out = []
        for i in range(n_pages):
            page = pt_ref[seq * n_pages + i]
            out.append(pltpu.make_async_copy(cache_c.at[layer, page],
                                             cbuf.at[slot_, pl.ds(i * PAGE_SIZE, PAGE_SIZE)], sem.at[0, slot_]))
            out.append(pltpu.make_async_copy(cache_kr.at[layer, page], kbuf.at[slot_, i], sem.at[1, slot_]))
        return out

    @pl.when(b == 0)
    def _():
        for cp in page_copies(0, 0):
            cp.start()

    for cp in page_copies(b, slot):
        cp.wait()

    @pl.when(b + 1 < nseq)
    def _():
        for cp in page_copies(b + 1, 1 - slot):
            cp.start()

    lhs = jnp.concatenate([wukt_ref[...], qabs_ref[...]], axis=0)
    nk = heads * dn
    qrope = qrope_ref[...]
    m_run = jnp.full((rows, 1), NEG_BIG, F32)
    for sb in range(n_sub):
        for i in range(sub_pages):
            pg = sb * sub_pages + i
            cb[sb, i * PAGE_SIZE:(i + 1) * PAGE_SIZE, :] = cbuf[slot, pg * PAGE_SIZE:(pg + 1) * PAGE_SIZE, :].astype(BF16)
            krb[sb, :, i * PAGE_SIZE:(i + 1) * PAGE_SIZE] = kbuf[slot, pg].astype(BF16)
        kq = _dot_t(lhs, cb[sb])
        ssq = jnp.concatenate([jnp.sum(kq[h * dn:(h + 1) * dn, :] ** 2, axis=0, keepdims=True)
                               for h in range(heads)], axis=0)
        r = lax.rsqrt(ssq * (1.0 / dn) + NORM_EPS)
        s = kq[nk:, :] * jnp.concatenate([r] * s_new, axis=0) + _dot(qrope, krb[sb])
        s_all[sb] = s
        m_run = jnp.maximum(m_run, jnp.max(s, axis=1, keepdims=True))

    qf = q128_ref[...].astype(F32)
    assert heads & (heads - 1) == 0
    tok = lax.shift_right_logical(lax.broadcasted_iota(jnp.int32, (rows, 1), 0), heads.bit_length() - 1)
    sn = []
    for t in range(s_new):
        kt = jnp.concatenate([k128n_ref[t].astype(F32)] * s_new, axis=0)
        st = jnp.sum(qf * kt, axis=1, keepdims=True)
        sn.append(jnp.where(tok >= t, st, NEG_BIG))
        m_run = jnp.maximum(m_run, sn[-1])

    l = jnp.zeros((rows, 1), F32)
    acc = jnp.zeros((rows, LANES), F32)
    for sb in range(n_sub):
        p = jnp.exp(s_all[sb] - m_run)
        l = l + jnp.sum(p, axis=1, keepdims=True)
        acc = acc + _dot(p.astype(BF16), cb[sb])
    cn = cn_ref[...].astype(BF16).astype(F32)
    for t in range(s_new):
        pt = jnp.exp(sn[t] - m_run)
        l = l + pt
        acc = acc + pt.astype(BF16).astype(F32) * cn[t:t + 1, :]
    o_ref[...] = (acc / l).astype(BF16)


def _attn_sample(page_table, qabs, qrope, q128, k128n, c_new, wukt, cache_c, cache_kr, *, layer, heads, dn):
    nseq, rows, _ = qabs.shape
    s_new = rows // heads
    n_pages = page_table.shape[1]
    rank = cache_c.shape[-1]
    rope = cache_kr.shape[-2]
    sub_pages = min(SUB_PAGES, n_pages)
    assert n_pages % sub_pages == 0
    n_sub = n_pages // sub_pages
    sub_keys = sub_pages * PAGE_SIZE
    body = functools.partial(_attn_sample_body, layer=layer, heads=heads, dn=dn, s_new=s_new, n_pages=n_pages,
                             sub_pages=sub_pages)
    per_seq = lambda shape: pl.BlockSpec((None,) + shape, lambda b, pt: (b,) + (0,) * len(shape))
    grid_spec = pltpu.PrefetchScalarGridSpec(
        num_scalar_prefetch=1,
        grid=(nseq,),
        in_specs=[per_seq((rows, LANES)), per_seq((rows, rope)), per_seq((rows, LANES)),
                  per_seq((s_new, heads, LANES)), per_seq((s_new, LANES)),
                  pl.BlockSpec(wukt.shape, lambda b, pt: (0, 0)),
                  pl.BlockSpec(memory_space=pl.ANY), pl.BlockSpec(memory_space=pl.ANY)],
        out_specs=per_seq((rows, LANES)),
        scratch_shapes=[pltpu.VMEM((2, n_pages * PAGE_SIZE, rank), F32),
                        pltpu.VMEM((2, n_pages, rope, PAGE_SIZE), F32),
                        pltpu.VMEM((n_sub, sub_keys, rank), BF16),
                        pltpu.VMEM((n_sub, rope, sub_keys), BF16),
                        pltpu.VMEM((n_sub, rows, sub_keys), F32),
                        pltpu.SemaphoreType.DMA((2, 2))],
    )
    return pl.pallas_call(
        body, grid_spec=grid_spec,
        out_shape=jax.ShapeDtypeStruct((nseq, rows, LANES), BF16),
        compiler_params=_params("arbitrary"),
        name="attn_sample",
    )(page_table.reshape(-1), qabs, qrope, q128, k128n, c_new, wukt, cache_c, cache_kr)


def _outproj_body(x_ref, m_ref, gdn_ref, ctx_ref, wuv_ref, wog_ref, wom_ref, o_ref, *, per_token):
    gate = _mod_row(m_ref, 5, per_token)
    mla = _dot(ctx_ref[...], wuv_ref[...])
    mix = _dot(gdn_ref[...].astype(BF16), wog_ref[...]) + _dot(mla.astype(BF16), wom_ref[...])
    o_ref[...] = x_ref[...] + gate * mix


def _outproj(x, mods, gdn_out, ctx, wuv_bd, wo_g, wo_m, *, per_token):
    nb, t, d = x.shape
    tm = min(TOKEN_TILE, t)
    tok = lambda w: pl.BlockSpec((None, tm, w), lambda b, i: (b, i, 0))
    body = functools.partial(_outproj_body, per_token=per_token)
    return pl.pallas_call(
        body,
        grid=(nb, t // tm),
        in_specs=[tok(d), _mods_spec(per_token, tm, d, 1), tok(gdn_out.shape[-1]), tok(ctx.shape[-1]),
                  _resident(wuv_bd.shape), _resident(wo_g.shape), _resident(wo_m.shape)],
        out_specs=tok(d),
        out_shape=jax.ShapeDtypeStruct(x.shape, F32),
        compiler_params=_params("arbitrary", "arbitrary"),
        name="outproj",
    )(x, mods, gdn_out, ctx, wuv_bd, wo_g, wo_m)


def _rope_table(pos, half):
    inv = ROPE_THETA ** (-jnp.arange(half, dtype=F32) / half)
    ang = pos[:, None] * inv[None, :]
    cos2 = jnp.concatenate([jnp.cos(ang), jnp.cos(ang)], axis=1)
    sin2 = jnp.concatenate([-jnp.sin(ang), jnp.sin(ang)], axis=1)
    t = pos.shape[0]
    pad = LANES - 4 * half
    one, zero = jnp.ones((t, pad), F32), jnp.zeros((t, pad), F32)
    return jnp.concatenate([one, cos2, sin2, zero, cos2, cos2, zero, sin2, sin2], axis=1)


def _swap_halves(a, axis=-1):
    lo, hi = jnp.split(a, 2, axis=axis)
    return jnp.concatenate([hi, lo], axis=axis)


def _prep_layer(lp, dims):
    heads, dn, dr, rank, g_heads, conv_dim, v_dim = dims
    assert dn + 2 * dr == LANES and rank == LANES and 2 * g_heads <= LANES
    w_in = lp['w_in']
    d = w_in.shape[0]
    o = 0
    w_qkv = w_in[:, o:o + conv_dim]; o += conv_dim
    w_z = w_in[:, o:o + v_dim]; o += v_dim
    w_b = w_in[:, o:o + g_heads]; o += g_heads
    w_a = w_in[:, o:o + g_heads]; o += g_heads
    w_q = w_in[:, o:o + heads * (dn + dr)].reshape(d, heads, dn + dr); o += heads * (dn + dr)
    w_c = w_in[:, o:o + rank]; o += rank
    w_kr = w_in[:, o:o + dr]
    zeros = lambda n: jnp.zeros((d, n), F32)
    w_q128 = jnp.concatenate([w_q, _swap_halves(w_q[:, :, dn:])], axis=2).reshape(d, heads * LANES)
    w_krs = _swap_halves(w_kr)
    w_all = jnp.concatenate([w_qkv, w_z, w_b, w_a, zeros(LANES - 2 * g_heads), w_q128, w_c,
                             zeros(dn), w_kr, w_kr, zeros(dn), w_krs, w_krs], axis=1).astype(BF16)
    scale = (dn + dr) ** -0.5
    qr_g = lp['mla_qr_norm']
    gq = (jnp.concatenate([lp['mla_qn_norm'], qr_g, _swap_halves(qr_g)]) * scale).reshape(1, LANES)
    lane = jnp.arange(LANES)
    seg = jnp.where(lane < dn, 0, jnp.where(lane < dn + dr, 1, 2))
    seg_len = jnp.where(lane < dn, dn, dr).astype(F32)
    mq = jnp.where(seg[:, None] == seg[None, :], 1.0 / seg_len[None, :], 0.0).astype(BF16)
    mk = jnp.where((lane[:, None] < dn) & (lane[None, :] < dn), 1.0 / dn, 0.0).astype(BF16)
    gk = jnp.concatenate([lp['mla_kn_norm'], jnp.zeros((LANES - dn,), F32)]).reshape(1, LANES)
    kr_g = lp['mla_kr_norm']
    zdn = jnp.zeros((dn,), F32)
    gkr = jnp.stack([jnp.concatenate([zdn, kr_g, kr_g]),
                     jnp.concatenate([zdn, _swap_halves(kr_g), _swap_halves(kr_g)])])
    w_uk = lp['mla_w_uk']
    wuk = jnp.concatenate([w_uk, jnp.zeros((rank, heads, LANES - dn), F32)], axis=2)
    wuk = wuk.reshape(rank, heads * LANES).astype(BF16)
    wukt = w_uk.transpose(1, 2, 0).reshape(heads * dn, rank).astype(BF16)
    absorb = jnp.concatenate([w_uk.transpose(1, 2, 0) * lp['mla_kn_norm'][None, :, None],
                              jnp.zeros((heads, LANES - dn, rank), F32)], axis=1)
    fold = jnp.zeros((LANES, LANES), F32)
    fold = fold.at[dn + jnp.arange(dr), jnp.arange(dr)].set(1.0).at[dn + dr + jnp.arange(dr), jnp.arange(dr)].set(1.0)
    wqa = jnp.concatenate([absorb, jnp.broadcast_to(fold, (heads, LANES, LANES))], axis=2).astype(BF16)
    w_uv = lp['mla_w_uv']
    mv = w_uv.shape[2]
    wuv_bd = (w_uv.transpose(1, 0, 2)[:, :, None, :] * jnp.eye(heads, dtype=F32)[:, None, :, None])
    wuv_bd = wuv_bd.reshape(heads * rank, heads * mv).astype(BF16)
    dff = lp['ffn1_wo'].shape[0]
    lane_h = lane // (LANES // 2)
    ms = jnp.where(lane_h[:, None] == lane_h[None, :], 1.0, 0.0).astype(BF16)
    aparams = jnp.zeros((2, LANES), F32)
    aparams = aparams.at[0, g_heads:2 * g_heads].set(jnp.exp(lp['gdn_a_log']))
    aparams = aparams.at[1, g_heads:2 * g_heads].set(lp['gdn_dt_bias'])
    return dict(
        w_all=w_all, gq=gq, mq=mq, gc=lp['mla_ckv_norm'].reshape(1, LANES), wuk=wuk, mk=mk, gk=gk, gkr=gkr,
        wukt=wukt, wqa=wqa, wuv_bd=wuv_bd, ms=ms, aparams=aparams,
        wo_g=lp['w_out'][:v_dim].astype(BF16), wo_m=lp['w_out'][v_dim:].astype(BF16),
        f1=(lp['ffn1_wi'][:, :dff].astype(BF16), lp['ffn1_wi'][:, dff:].astype(BF16), lp['ffn1_wo'].astype(BF16)),
        f2=(lp['ffn2_wi'][:, :dff].astype(BF16), lp['ffn2_wi'][:, dff:].astype(BF16), lp['ffn2_wo'].astype(BF16)),
    )


def _layer(x, mods, tab, conv_prev, s0, lp, w, lay, dims, *, per_token, t_valid, chunk, attend):
    heads = dims[0]
    x = _ffn(x, mods, lp['norm_ffn1'], *w['f1'], mod0=0, per_token=per_token)
    outs = _inproj(x, mods, lp['norm_mix'], w['w_all'], tab, w['gq'], w['mq'], w['gc'], w['wuk'], w['mk'],
                   w['gk'], w['gkr'], w['wqa'] if per_token else None, lay=lay, heads=heads, per_token=per_token)
    qkv, z, ba, q128, k128, c, cbf, krot = outs[:8]
    pad = qkv.shape[1] - t_valid
    gdn_out, nconv, s_new = _gdn(qkv, z, ba, conv_prev, s0, lp['gdn_conv_w'], w['aparams'], lp['gdn_norm'],
                                 w['ms'], t_valid=t_valid, chunk=chunk)
    ctx = attend(outs)
    x = _outproj(x, mods, gdn_out, ctx, w['wuv_bd'], w['wo_g'], w['wo_m'], per_token=per_token)
    x = _ffn(x, mods, lp['norm_ffn2'], *w['f2'], mod0=6, per_token=per_token)
    del pad
    return x, c, krot, nconv, s_new


def kernel(x_prompt, x_sample, cache_ckv, cache_krope, state_conv, state_gdn, page_table, c_prompt, c_sample,
           ada_w, ada_b, norm_ffn1, ffn1_wi, ffn1_wo, norm_mix, w_in, gdn_conv_w, gdn_a_log, gdn_dt_bias, gdn_norm,
           mla_qn_norm, mla_qr_norm, mla_ckv_norm, mla_kr_norm, mla_kn_norm, mla_w_uk, mla_w_uv, w_out,
           norm_ffn2, ffn2_wi, ffn2_wo):
    depth = ada_w.shape[0]
    bp, tp, d = x_prompt.shape
    bs, ts, _ = x_sample.shape
    g_heads, dk, dv = state_gdn.shape[2:]
    conv_dim = state_conv.shape[-1]
    v_dim = g_heads * dv
    rank, heads, dn = mla_w_uk.shape[1:]
    dr = mla_qr_norm.shape[1]
    past = page_table.shape[1] * PAGE_SIZE
    dims = (heads, dn, dr, rank, g_heads, conv_dim, v_dim)
    lay = _InLayout(conv_dim, v_dim, heads)
    ts_pad = -(-ts // SUBLANES) * SUBLANES

    tab_p = _rope_table(jnp.arange(tp, dtype=F32), dr // 2)
    tab_s = jnp.tile(_rope_table(past + jnp.arange(ts_pad, dtype=F32), dr // 2), (bs, 1))
    cond = jnp.concatenate([c_prompt, c_sample], axis=0)
    rows = -(-cond.shape[0] // SUBLANES) * SUBLANES
    cond = jnp.pad(cond, ((0, rows - cond.shape[0]), (0, 0)))

    yp = x_prompt
    ys = jnp.pad(x_sample, ((0, 0), (0, ts_pad - ts), (0, 0))).reshape(1, bs * ts_pad, d)
    outs_p, outs_s = [], []
    for l in range(depth):
        lp = dict(norm_ffn1=norm_ffn1[l], ffn1_wi=ffn1_wi[l], ffn1_wo=ffn1_wo[l], norm_mix=norm_mix[l], w_in=w_in[l],
                  gdn_conv_w=gdn_conv_w[l], gdn_a_log=gdn_a_log[l], gdn_dt_bias=gdn_dt_bias[l], gdn_norm=gdn_norm[l],
                  mla_qn_norm=mla_qn_norm[l], mla_qr_norm=mla_qr_norm[l], mla_ckv_norm=mla_ckv_norm[l],
                  mla_kr_norm=mla_kr_norm[l], mla_kn_norm=mla_kn_norm[l], mla_w_uk=mla_w_uk[l],
                  mla_w_uv=mla_w_uv[l], w_out=w_out[l], norm_ffn2=norm_ffn2[l], ffn2_wi=ffn2_wi[l],
                  ffn2_wo=ffn2_wo[l])
        w = _prep_layer(lp, dims)
        mods = _mods(cond, ada_w[l], ada_b[l])
        mods_p = mods[:bp].reshape(bp, N_MOD, d)
        mods_s = jnp.repeat(mods[bp:bp + bs].reshape(bs, N_MOD, d), ts_pad, axis=0).transpose(1, 0, 2)

        conv0 = jnp.zeros((bp, SUBLANES, conv_dim), F32)
        s0 = jnp.zeros((bp, g_heads, dk, dv), F32)
        def attend_p(o):
            ct = jnp.swapaxes(o[6], 1, 2)
            ones = jnp.ones((bp, SUBLANES, tp), BF16)
            return _attn_prompt(o[3], o[4], jnp.concatenate([ct, ones], axis=1))
        yp, c_p, kr_p, cv_p, s_p = _layer(yp, mods_p, tab_p, conv0, s0, lp, w, lay, dims, per_token=False,
                                          t_valid=min(GDN_TILE, tp), chunk=min(GDN_CHUNK, tp), attend=attend_p)
        outs_p.append((c_p, kr_p[..., dn:dn + dr], cv_p[:, SUBLANES - (GDN_CONV - 1):], s_p))

        conv_prev = jnp.pad(state_conv[l], ((0, 0), (SUBLANES - (GDN_CONV - 1), 0), (0, 0)))

        def attend_s(o, l=l, w=w):
            q128, k128, c_new, qabs, qrope = o[3], o[4], o[5], o[8], o[9]
            by_seq = lambda a: a[0].reshape(heads, bs, ts_pad, -1)[:, :, :ts].transpose(1, 2, 0, 3)
            flat = lambda a: by_seq(a).reshape(bs, ts * heads, -1)
            ctx = _attn_sample(page_table, flat(qabs), flat(qrope)[..., :dr], flat(q128), by_seq(k128),
                               c_new.reshape(bs, ts_pad, -1)[:, :ts], w['wukt'], cache_ckv, jnp.swapaxes(cache_krope, 2, 3),
                               layer=l, heads=heads, dn=dn)
            ctx = ctx.reshape(bs, ts, heads * rank)
            return jnp.pad(ctx, ((0, 0), (0, ts_pad - ts), (0, 0))).reshape(1, bs * ts_pad, heads * rank)

        ys, c_s, kr_s, cv_s, s_s = _layer_sample(ys, mods_s, tab_s, conv_prev, state_gdn[l], lp, w, lay, dims,
                                                 bs, ts, ts_pad, attend_s)
        outs_s.append((c_s, kr_s, cv_s, s_s))

    stack = lambda outs, k: jnp.stack([o[k] for o in outs])
    ys = ys.reshape(bs, ts_pad, d)[:, :ts]
    return (yp, ys, stack(outs_p, 0), stack(outs_p, 1), stack(outs_p, 2), stack(outs_p, 3),
            stack(outs_s, 0), stack(outs_s, 1), stack(outs_s, 2), stack(outs_s, 3))


def _layer_sample(x, mods, tab, conv_prev, s0, lp, w, lay, dims, bs, ts, ts_pad, attend):
    heads, dn, dr = dims[0], dims[1], dims[2]
    d = x.shape[-1]
    x = _ffn(x, mods, lp['norm_ffn1'], *w['f1'], mod0=0, per_token=True)
    outs = _inproj(x, mods, lp['norm_mix'], w['w_all'], tab, w['gq'], w['mq'], w['gc'], w['wuk'], w['mk'],
                   w['gk'], w['gkr'], w['wqa'], lay=lay, heads=heads, per_token=True)
    qkv, z, ba, c, krot = outs[0], outs[1], outs[2], outs[5], outs[7]
    seq = lambda a: a.reshape(bs, ts_pad, a.shape[-1])
    gdn_out, nconv, s_new = _gdn(seq(qkv), seq(z), seq(ba), conv_prev, s0, lp['gdn_conv_w'], w['aparams'],
                                 lp['gdn_norm'], w['ms'], t_valid=ts, chunk=ts_pad)
    ctx = attend(outs)
    x = _outproj(x, mods, gdn_out.reshape(1, bs * ts_pad, -1), ctx, w['wuv_bd'], w['wo_g'], w['wo_m'],
                 per_token=True)
    x = _ffn(x, mods, lp['norm_ffn2'], *w['f2'], mod0=6, per_token=True)
    c_s = seq(c)[:, :ts]
    kr_s = seq(krot)[:, :ts, dn:dn + dr]
    return x, c_s, kr_s, nconv[:, SUBLANES - (GDN_CONV - 1):], s_new
```

```python
import functools
import math

import jax
import jax.numpy as jnp
from jax import lax
from jax.experimental import pallas as pl
from jax.experimental.pallas import tpu as pltpu

F32 = jnp.float32
BF16 = jnp.bfloat16

NORM_EPS = 1e-6
ROPE_THETA = 10000.0
PAGE_SIZE = 128
GDN_CONV = 4
GDN_CHUNK = 128
N_MOD = 9
LANES = 128
SUBLANES = 8
NEG_BIG = -1e30
VMEM_LIMIT = 56 * 1024 * 1024

TOKEN_TILE = 512
ATTN_TILE = 512
GDN_TILE = 256
SUB_PAGES = 16


def _dot(a, b):
    return jnp.dot(a, b, preferred_element_type=F32)


def _dot_t(a, b):
    return lax.dot_general(a, b, (((1,), (1,)), ((), ())), preferred_element_type=F32)


def _dot_ta(a, b):
    return lax.dot_general(a, b, (((0,), (0,)), ((), ())), preferred_element_type=F32)


def _bmm(a, b):
    return lax.dot_general(a, b, (((2,), (1,)), ((0,), (0,))), preferred_element_type=F32)


def _bmm_t(a, b):
    return lax.dot_general(a, b, (((2,), (2,)), ((0,), (0,))), preferred_element_type=F32)


def _sigmoid(x):
    return 1.0 / (1.0 + jnp.exp(-x))


def _silu(x):
    return x * _sigmoid(x)


def _params(*sem):
    return pltpu.CompilerParams(dimension_semantics=sem, vmem_limit_bytes=VMEM_LIMIT)


def _resident(shape):
    nd = len(shape)
    return pl.BlockSpec(shape, lambda *_: (0,) * nd, pipeline_mode=pl.Buffered(1))


MODS_PER_SUBLAYER = 3


def _mod_row(m_ref, k, per_token):
    return m_ref[k % MODS_PER_SUBLAYER] if per_token else m_ref[k:k + 1, :]


def _mod_norm(x, gain, shift, scale):
    y = x * lax.rsqrt(jnp.mean(x * x, axis=-1, keepdims=True) + NORM_EPS)
    return (y * gain) * (1.0 + scale) + shift


def _mods_spec(per_token, tm, d, sublayer):
    if per_token:
        return pl.BlockSpec((MODS_PER_SUBLAYER, tm, d), lambda b, i: (sublayer, i, 0))
    return pl.BlockSpec((None, N_MOD, d), lambda b, i: (b, 0, 0))


def _mods_body(c_ref, w_ref, b_ref, o_ref):
    c = c_ref[...]
    o_ref[...] = _dot(_silu(c).astype(BF16), w_ref[...].astype(BF16)) + b_ref[...]


def _mods(cond, ada_w, ada_b):
    rows, d = cond.shape
    n = ada_w.shape[1]
    tn = d
    return pl.pallas_call(
        _mods_body,
        grid=(n // tn,),
        in_specs=[pl.BlockSpec((rows, d), lambda j: (0, 0)),
                  pl.BlockSpec((d, tn), lambda j: (0, j)),
                  pl.BlockSpec((1, tn), lambda j: (0, j))],
        out_specs=pl.BlockSpec((rows, tn), lambda j: (0, j)),
        out_shape=jax.ShapeDtypeStruct((rows, n), F32),
        compiler_params=_params("arbitrary"),
        name="mods",
    )(cond, ada_w, ada_b.reshape(1, n))


def _ffn_body(x_ref, m_ref, g_ref, wi_ref, wo_ref, o_ref, *, mod0, per_token, fc):
    x = x_ref[...]
    shift, scale, gate = (_mod_row(m_ref, mod0 + k, per_token) for k in range(3))
    h = _mod_norm(x, g_ref[...], shift, scale).astype(BF16)
    acc = jnp.zeros(x.shape, F32)
    dff = wo_ref.shape[0]
    for c in range(dff // fc):
        lo, hi = c * fc, (c + 1) * fc
        a = _silu(_dot(h, wi_ref[:, lo:hi])) * _dot(h, wi_ref[:, dff + lo:dff + hi])
        acc = acc + _dot(a.astype(BF16), wo_ref[lo:hi, :])
    o_ref[...] = x + 0.5 * gate * acc


def _ffn(x, mods, gain, wi, wo, *, mod0, per_token):
    nb, t, d = x.shape
    tm = min(TOKEN_TILE, t)
    dff = wo.shape[0]
    fc = 2 * LANES
    assert t % tm == 0 and dff % fc == 0
    body = functools.partial(_ffn_body, mod0=mod0, per_token=per_token, fc=fc)
    return pl.pallas_call(
        body,
        grid=(nb, t // tm),
        in_specs=[pl.BlockSpec((None, tm, d), lambda b, i: (b, i, 0)),
                  _mods_spec(per_token, tm, d, mod0 // MODS_PER_SUBLAYER),
                  _resident((1, d)), _resident((d, 2 * dff)), _resident((dff, d))],
        out_specs=pl.BlockSpec((None, tm, d), lambda b, i: (b, i, 0)),
        out_shape=jax.ShapeDtypeStruct(x.shape, F32),
        compiler_params=_params("arbitrary", "arbitrary"),
        name="ffn",
    )(x, mods, gain.reshape(1, d), wi, wo)


class _InLayout:
    def __init__(self, conv_dim, v_dim, heads):
        self.qkv = (0, conv_dim)
        self.z = (conv_dim, conv_dim + v_dim)
        self.ba = (self.z[1], self.z[1] + LANES)
        self.q = (self.ba[1], self.ba[1] + heads * LANES)
        self.ckv = (self.q[1], self.q[1] + LANES)
        self.kr = (self.ckv[1], self.ckv[1] + LANES)
        self.krs = (self.kr[1], self.kr[1] + LANES)
        self.total = self.krs[1]


def _inproj_body(x_ref, m_ref, g_ref, w_ref, tab_ref, gq_ref, mq_ref, gc_ref, wuk_ref, mk_ref, gk_ref, gkr_ref,
                 *rest, lay, heads, per_token, sample):
    if sample:
        wqa_ref, rest = rest[0], rest[1:]
    qkv_ref, z_ref, ba_ref, q128_ref, k128_ref, c_ref, krot_ref = rest[:7]
    x = x_ref[...]
    shift, scale = _mod_row(m_ref, 3, per_token), _mod_row(m_ref, 4, per_token)
    h = _mod_norm(x, g_ref[...], shift, scale).astype(BF16)
    p = _dot(h, w_ref[...])
    qkv_ref[...] = p[:, lay.qkv[0]:lay.qkv[1]]
    z_ref[...] = p[:, lay.z[0]:lay.z[1]]
    ba_ref[...] = p[:, lay.ba[0]:lay.ba[1]]

    tab = tab_ref[...]
    tab_q, tab_c, tab_s = tab[:, :LANES], tab[:, LANES:2 * LANES], tab[:, 2 * LANES:]
    for hh in range(heads):
        qh = p[:, lay.q[0] + hh * LANES: lay.q[0] + (hh + 1) * LANES]
        msq = _dot((qh * qh).astype(BF16), mq_ref[...])
        qn = qh * lax.rsqrt(msq + NORM_EPS) * gq_ref[...] * tab_q
        q128_ref[hh] = qn.astype(BF16)
        if sample:
            qabs_ref, qrope_ref = rest[7], rest[8]
            qa = _dot(qn.astype(BF16), wqa_ref[hh])
            qabs_ref[hh] = qa[:, :LANES].astype(BF16)
            qrope_ref[hh] = qa[:, LANES:2 * LANES].astype(BF16)

    ckv = p[:, lay.ckv[0]:lay.ckv[1]]
    c = ckv * lax.rsqrt(jnp.mean(ckv * ckv, axis=-1, keepdims=True) + NORM_EPS) * gc_ref[...]
    c_ref[...] = c
    cb = c.astype(BF16)
    if not sample:
        ct_ref = rest[7]
        ct_ref[...] = jnp.concatenate([c.T, jnp.ones((SUBLANES, c.shape[0]), F32)], axis=0).astype(BF16)
    kr = p[:, lay.kr[0]:lay.kr[1]]
    krs = p[:, lay.krs[0]:lay.krs[1]]
    inv = lax.rsqrt(jnp.sum(kr * kr, axis=-1, keepdims=True) * (2.0 / LANES) + NORM_EPS)
    krot = kr * inv * gkr_ref[0:1, :] * tab_c + krs * inv * gkr_ref[1:2, :] * tab_s
    krot_ref[...] = krot
    knr = _dot(cb, wuk_ref[...])
    for hh in range(heads):
        kh = knr[:, hh * LANES:(hh + 1) * LANES]
        msq = _dot((kh * kh).astype(BF16), mk_ref[...])
        k128_ref[hh] = (kh * lax.rsqrt(msq + NORM_EPS) * gk_ref[...] + krot).astype(BF16)


def _inproj(x, mods, gain, w_all, tab, gq, mq, gc, wuk, mk, gk, gkr, wqa, *, lay, heads, per_token):
    nb, t, d = x.shape
    tm = min(TOKEN_TILE, t)
    assert t % tm == 0
    sample = wqa is not None
    body = functools.partial(_inproj_body, lay=lay, heads=heads, per_token=per_token, sample=sample)
    tok = lambda w: pl.BlockSpec((None, tm, w), lambda b, i: (b, i, 0))
    hd = lambda w: pl.BlockSpec((None, heads, tm, w), lambda b, i: (b, 0, i, 0))
    conv_dim, v_dim = lay.qkv[1], lay.z[1] - lay.z[0]
    in_specs = [tok(d), _mods_spec(per_token, tm, d, 1), _resident((1, d)), _resident(w_all.shape),
                pl.BlockSpec((tm, 3 * LANES), lambda b, i: (i, 0)),
                _resident((1, LANES)), _resident((LANES, LANES)), _resident((1, LANES)),
                _resident(wuk.shape), _resident((LANES, LANES)), _resident((1, LANES)), _resident((2, LANES))]
    args = [x, mods, gain.reshape(1, d), w_all, tab, gq, mq, gc, wuk, mk, gk, gkr]
    out_specs = [tok(conv_dim), tok(v_dim), tok(LANES), hd(LANES), hd(LANES), tok(LANES), tok(LANES)]
    out_shape = [jax.ShapeDtypeStruct((nb, t, conv_dim), F32), jax.ShapeDtypeStruct((nb, t, v_dim), F32),
                 jax.ShapeDtypeStruct((nb, t, LANES), F32),
                 jax.ShapeDtypeStruct((nb, heads, t, LANES), BF16), jax.ShapeDtypeStruct((nb, heads, t, LANES), BF16),
                 jax.ShapeDtypeStruct((nb, t, LANES), F32), jax.ShapeDtypeStruct((nb, t, LANES), F32)]
    if sample:
        in_specs.append(_resident(wqa.shape))
        args.append(wqa)
        out_specs += [hd(LANES), hd(LANES)]
        out_shape += [jax.ShapeDtypeStruct((nb, heads, t, LANES), BF16)] * 2
    else:
        out_specs.append(pl.BlockSpec((None, LANES + SUBLANES, tm), lambda b, i: (b, 0, i)))
        out_shape.append(jax.ShapeDtypeStruct((nb, LANES + SUBLANES, t), BF16))
    return pl.pallas_call(
        body, grid=(nb, t // tm), in_specs=in_specs, out_specs=out_specs, out_shape=out_shape,
        compiler_params=_params("arbitrary", "arbitrary"), name="inproj",
    )(*args)


def _unit_lower_inverse(a):
    n = a.shape[-1]
    assert n & (n - 1) == 0
    row = lax.broadcasted_iota(jnp.int32, (n, n), 0)
    col = lax.broadcasted_iota(jnp.int32, (n, n), 1)
    x = jnp.broadcast_to(jnp.where(row == col, 1.0, 0.0), a.shape)
    b = 1
    while b < n:
        lo_mask = (jnp.bitwise_xor(row, col) < 2 * b) & (jnp.bitwise_and(row, b) != 0) & (jnp.bitwise_and(col, b) == 0)
        lo = jnp.where(lo_mask, a, 0.0)
        if b == 1:
            x = x - lo
        else:
            xb = x.astype(BF16)
            x = x - _bmm(xb, _bmm(lo.astype(BF16), xb).astype(BF16))
        b *= 2
    return x


def _gdn_body(qkv_ref, z_ref, ba_ref, cprev_ref, s0_ref, cw_ref, ap_ref, ng_ref, ms_ref,
              o_ref, nconv_ref, sout_ref, xbuf, s_scr,
              *, tt, t_valid, chunk, heads, dk, dv):
    j = pl.program_id(1)
    hist = SUBLANES

    @pl.when(j == 0)
    def _():
        xbuf[0:hist, :] = cprev_ref[...]
        s_scr[...] = s0_ref[...]

    xbuf[hist:hist + tt, :] = qkv_ref[...]
    conv = xbuf[hist - 3:hist - 3 + tt, :] * cw_ref[0:1, :]
    for jj in range(1, GDN_CONV):
        conv = conv + xbuf[hist - 3 + jj:hist - 3 + jj + tt, :] * cw_ref[jj:jj + 1, :]
    y = _silu(conv)
    tail = xbuf[t_valid:t_valid + hist, :]
    nconv_ref[...] = tail
    xbuf[0:hist, :] = tail

    qk_dim = heads * dk

    def l2(v):
        parts = [_dot((v[:, i * LANES:(i + 1) * LANES] ** 2).astype(BF16), ms_ref[...])
                 for i in range(v.shape[1] // LANES)]
        return v * lax.rsqrt(jnp.concatenate(parts, axis=1) + NORM_EPS)

    q = l2(y[:, :qk_dim]) * (dk ** -0.5)
    k = l2(y[:, qk_dim:2 * qk_dim])
    v = y[:, 2 * qk_dim:]
    ba = ba_ref[...]
    beta = _sigmoid(ba)
    xg = ba + ap_ref[1:2, :]
    g = -ap_ref[0:1, :] * (jnp.maximum(xg, 0.0) + jnp.log1p(jnp.exp(-jnp.abs(xg))))
    if t_valid < tt:
        keep = lax.broadcasted_iota(jnp.int32, (tt, 1), 0) < t_valid
        q, k, v = (jnp.where(keep, a, 0.0) for a in (q, k, v))
        beta, g = jnp.where(keep, beta, 0.0), jnp.where(keep, g, 0.0)
    row = lax.broadcasted_iota(jnp.int32, (tt, tt), 0)
    col = lax.broadcasted_iota(jnp.int32, (tt, tt), 1)
    assert chunk & (chunk - 1) == 0
    tri = jnp.where((row >= col) & (jnp.bitwise_xor(row, col) < chunk), 1.0, 0.0).astype(BF16)
    g_hi = g.astype(BF16)
    g_lo = (g - g_hi.astype(F32)).astype(BF16)
    gc = _dot(tri, g_hi) + _dot(tri, g_lo)
    gct = gc.T
    z = z_ref[...]

    crow = lax.broadcasted_iota(jnp.int32, (chunk, chunk), 0)
    ccol = lax.broadcasted_iota(jnp.int32, (chunk, chunk), 1)
    causal = crow >= ccol
    strict = crow > ccol
    stack = lambda xs: jnp.stack(xs, axis=0)

    s3 = s_scr[...]
    for c in range(tt // chunk):
        r0, r1 = c * chunk, (c + 1) * chunk
        qkb, kbf, vbeta, kbeg, qdec, kdec, decay, gl, zc = ([] for _ in range(9))
        for h in range(heads):
            qh = q[r0:r1, h * dk:(h + 1) * dk]
            kh = k[r0:r1, h * dk:(h + 1) * dk]
            vh = v[r0:r1, h * dv:(h + 1) * dv]
            bh = beta[r0:r1, h:h + 1]
            gcol = gc[r0:r1, heads + h:heads + h + 1]
            grow = gct[heads + h:heads + h + 1, r0:r1]
            glast = grow[:, chunk - 1:chunk]
            eg = jnp.exp(gcol)
            kb = kh * bh
            qkb.append(jnp.concatenate([qh, kb], axis=0).astype(BF16))
            kbf.append(kh.astype(BF16))
            vbeta.append((vh * bh).astype(BF16))
            kbeg.append((kb * eg).astype(BF16))
            qdec.append((qh * eg).astype(BF16))
            kdec.append((kh * jnp.exp(glast - gcol)).astype(BF16))
            decay.append(jnp.exp(jnp.where(causal, jnp.broadcast_to(gcol, (chunk, chunk)) - grow, NEG_BIG)))
            gl.append(jnp.exp(glast))
            zc.append(z[r0:r1, h * dv:(h + 1) * dv])
        decay3 = stack(decay)
        sc = _bmm_t(stack(qkb), stack(kbf))
        qkm = (sc[:, :chunk] * decay3).astype(BF16)
        m = jnp.where(strict, sc[:, chunk:] * decay3, 0.0)
        tinv = _unit_lower_inverse(m).astype(BF16)
        u = _bmm(tinv, stack(vbeta))
        w = _bmm(tinv, stack(kbeg))
        sb = s3.astype(BF16)
        ws_qs = _bmm(jnp.concatenate([w.astype(BF16), stack(qdec)], axis=1), sb)
        vnb = (u - ws_qs[:, :chunk]).astype(BF16)
        o = ws_qs[:, chunk:] + _bmm(qkm, vnb)
        kd3 = stack(kdec)
        upd = stack([_dot_ta(kd3[h], vnb[h]) for h in range(heads)])
        s3 = s3 * stack(gl) + upd
        on = o * lax.rsqrt(jnp.mean(o * o, axis=-1, keepdims=True) + NORM_EPS) * ng_ref[...] * _silu(stack(zc))
        for h in range(heads):
            o_ref[r0:r1, h * dv:(h + 1) * dv] = on[h]
    s_scr[...] = s3
    sout_ref[...] = s3


def _gdn(qkv, z, ba, conv_prev, s0, conv_w, aparams, norm_g, mseg, *, t_valid, chunk):
    nb, t, conv_dim = qkv.shape
    heads, dk, dv = s0.shape[1:]
    tt = min(GDN_TILE, t)
    assert t % tt == 0 and tt % chunk == 0 and (t_valid == tt or t == tt)
    body = functools.partial(_gdn_body, tt=tt, t_valid=t_valid, chunk=chunk, heads=heads, dk=dk, dv=dv)
    v_dim = heads * dv
    tok = lambda w: pl.BlockSpec((None, tt, w), lambda b, i: (b, i, 0))
    return pl.pallas_call(
        body,
        grid=(nb, t // tt),
        in_specs=[tok(conv_dim), tok(v_dim), tok(LANES),
                  pl.BlockSpec((None, SUBLANES, conv_dim), lambda b, i: (b, 0, 0)),
                  pl.BlockSpec((None, heads, dk, dv), lambda b, i: (b, 0, 0, 0)),
                  _resident((GDN_CONV, conv_dim)), _resident((2, LANES)), _resident((1, dv)),
                  _resident((LANES, LANES))],
        out_specs=[tok(v_dim),
                   pl.BlockSpec((None, SUBLANES, conv_dim), lambda b, i: (b, 0, 0)),
                   pl.BlockSpec((None, heads, dk, dv), lambda b, i: (b, 0, 0, 0))],
        out_shape=[jax.ShapeDtypeStruct((nb, t, v_dim), F32),
                   jax.ShapeDtypeStruct((nb, SUBLANES, conv_dim), F32),
                   jax.ShapeDtypeStruct((nb, heads, dk, dv), F32)],
        scratch_shapes=[pltpu.VMEM((tt + SUBLANES, conv_dim), F32), pltpu.VMEM((heads, dk, dv), F32)],
        compiler_params=_params("arbitrary", "arbitrary"),
        name="gdn",
    )(qkv, z, ba, conv_prev, s0, conv_w, aparams, norm_g.reshape(1, dv), mseg)


def _gdn_sample_body(xq_ref, xk_ref, xv_ref, pq_ref, pk_ref, pv_ref, wq_ref, wk_ref, wv_ref, z_ref, b_ref, a_ref,
                     ap_ref, ng_ref, s_ref, o_ref, sout_ref, kq_scr, *, ts, heads, dk, dv):
    h = pl.program_id(0)
    nb = xq_ref.shape[-1]

    def conv(x_ref, p_ref, w_ref):
        taps = GDN_CONV - 1
        xin = [p_ref[i] for i in range(taps)] + [x_ref[t] for t in range(ts)]
        w = [jnp.broadcast_to(w_ref[:, jj:jj + 1], xin[0].shape) for jj in range(GDN_CONV)]
        out = []
        for t in range(ts):
            acc = xin[t] * w[0]
            for jj in range(1, GDN_CONV):
                acc = acc + xin[t + jj] * w[jj]
            out.append(_silu(acc))
        return out

    def l2(x):
        return x * lax.rsqrt(jnp.sum(x * x, axis=0, keepdims=True) + NORM_EPS)

    q = [l2(x) * (dk ** -0.5) for x in conv(xq_ref, pq_ref, wq_ref)]
    k = [l2(x) for x in conv(xk_ref, pk_ref, wk_ref)]
    v = conv(xv_ref, pv_ref, wv_ref)
    for t in range(ts):
        kq_scr[t] = k[t]
        kq_scr[ts + t] = q[t]
    a_h = ap_ref[0, h]
    dt_h = ap_ref[1, h]
    beta, decay = [], []
    for t in range(ts):
        beta.append(_sigmoid(b_ref[t, pl.ds(h, 1), :]))
        xg = a_ref[t, pl.ds(h, 1), :] + dt_h
        decay.append(jnp.exp(-a_h * (jnp.maximum(xg, 0.0) + jnp.log1p(jnp.exp(-jnp.abs(xg))))))

    def row(i, kk):
        return jnp.broadcast_to(kq_scr[i, pl.ds(kk, 1), :], (dv, nb))

    def first(kk, ks):
        return ks + row(0, kk) * s_ref[kk]

    ks = lax.fori_loop(0, dk, first, jnp.zeros((dv, nb), F32), unroll=8)
    ng = jnp.broadcast_to(ng_ref[...], (dv, nb))
    for t in range(ts):
        d = beta[t] * (v[t] - decay[t] * ks)
        src = s_ref if t == 0 else sout_ref
        last = t == ts - 1

        def step(kk, carry, t=t, d=d, src=src, last=last):
            o_acc, ks_next = carry
            s_new = decay[t] * src[kk] + row(t, kk) * d
            sout_ref[kk] = s_new
            o_acc = o_acc + row(ts + t, kk) * s_new
            if not last:
                ks_next = ks_next + row(t + 1, kk) * s_new
            return o_acc, ks_next

        zero = jnp.zeros((dv, nb), F32)
        o, ks = lax.fori_loop(0, dk, step, (zero, zero), unroll=8)
        on = o * lax.rsqrt(jnp.mean(o * o, axis=0, keepdims=True) + NORM_EPS) * ng
        o_ref[t] = on * _silu(z_ref[t])


def _gdn_sample(x_t, prev_t, z_t, b_t, a_t, s_t, conv_w_t, aparams, norm_g):
    ts, conv_dim, nb = x_t.shape
    heads, dk, dv, _ = s_t.shape
    assert dk == dv and conv_dim == 3 * heads * dk
    taps = GDN_CONV - 1
    body = functools.partial(_gdn_sample_body, ts=ts, heads=heads, dk=dk, dv=dv)
    part = lambda rows, off: pl.BlockSpec((rows, dk, nb), lambda h: (0, off + h, 0))
    wpart = lambda off: pl.BlockSpec((dk, GDN_CONV), lambda h: (off + h, 0))
    whole = lambda a: pl.BlockSpec(a.shape, lambda h: (0,) * a.ndim)
    state = pl.BlockSpec((None, dk, dv, nb), lambda h: (h, 0, 0, 0))
    return pl.pallas_call(
        body,
        grid=(heads,),
        in_specs=[part(ts, 0), part(ts, heads), part(ts, 2 * heads),
                  part(taps, 0), part(taps, heads), part(taps, 2 * heads),
                  wpart(0), wpart(heads), wpart(2 * heads),
                  part(ts, 0), whole(b_t), whole(a_t),
                  pl.BlockSpec(memory_space=pltpu.SMEM), pl.BlockSpec((dv, 1), lambda h: (0, 0)), state],
        out_specs=[part(ts, 0), state],
        out_shape=[jax.ShapeDtypeStruct((ts, heads * dv, nb), F32), jax.ShapeDtypeStruct(s_t.shape, F32)],
        scratch_shapes=[pltpu.VMEM((2 * ts, dk, nb), F32)],
        compiler_params=_params("arbitrary"),
        name="gdn_sample",
    )(x_t, x_t, x_t, prev_t, prev_t, prev_t, conv_w_t, conv_w_t, conv_w_t, z_t, b_t, a_t, aparams,
      norm_g.reshape(dv, 1), s_t)


def _attn_body(q_ref, k_ref, ct_ref, o_ref, m_scr, acc_scr, *, heads, tq, rank):
    i = pl.program_id(1)
    j = pl.program_id(2)

    @pl.when(j == 0)
    def _():
        m_scr[...] = jnp.full(m_scr.shape, NEG_BIG, F32)
        acc_scr[...] = jnp.zeros(acc_scr.shape, F32)

    def step(masked):
        ct = ct_ref[...]
        m_old = [m_scr[h] for h in range(heads)]
        acc_old = [acc_scr[h] for h in range(heads)]
        if masked:
            key = lax.broadcasted_iota(jnp.int32, (tq, tq), 0)
            qry = lax.broadcasted_iota(jnp.int32, (tq, tq), 1)
            keep = key <= qry
        m_out, acc_out = [], []
        for h in range(heads):
            st = _dot_t(k_ref[h], q_ref[h])
            if masked:
                st = jnp.where(keep, st, NEG_BIG)
            m_new = jnp.maximum(m_old[h], jnp.max(st, axis=0, keepdims=True))
            alpha = jnp.exp(m_old[h] - m_new)
            pt = jnp.exp(st - m_new).astype(BF16)
            acc_out.append(acc_old[h] * alpha + _dot(ct, pt))
            m_out.append(m_new)
        for h in range(heads):
            m_scr[h] = m_out[h]
            acc_scr[h] = acc_out[h]
        return acc_out

    @pl.when(j < i)
    def _():
        step(False)

    @pl.when(j == i)
    def _():
        acc = step(True)
        for h in range(heads):
            ctx = acc[h][:rank, :] / acc[h][rank:rank + 1, :]
            o_ref[:, h * rank:(h + 1) * rank] = ctx.T.astype(BF16)


def _attn_prompt(q128, k128, ct_ext):
    nb, heads, t, _ = q128.shape
    rows = ct_ext.shape[1]
    rank = rows - SUBLANES
    tq = min(ATTN_TILE, t)
    assert t % tq == 0
    n = t // tq
    body = functools.partial(_attn_body, heads=heads, tq=tq, rank=rank)
    return pl.pallas_call(
        body,
        grid=(nb, n, n),
        in_specs=[pl.BlockSpec((None, heads, tq, LANES), lambda b, i, j: (b, 0, i, 0)),
                  pl.BlockSpec((None, heads, tq, LANES), lambda b, i, j: (b, 0, jnp.minimum(i, j), 0)),
                  pl.BlockSpec((None, rows, tq), lambda b, i, j: (b, 0, jnp.minimum(i, j)))],
        out_specs=pl.BlockSpec((None, tq, heads * rank), lambda b, i, j: (b, i, 0)),
        out_shape=jax.ShapeDtypeStruct((nb, t, heads * rank), BF16),
        scratch_shapes=[pltpu.VMEM((heads, 1, tq), F32), pltpu.VMEM((heads, rows, tq), F32)],
        compiler_params=_params("arbitrary", "arbitrary", "arbitrary"),
        name="attn_prompt",
    )(q128, k128, ct_ext)


def _attn_sample_body(pt_ref, qabs_ref, qrope_ref, q128_ref, k128n_ref, cn_ref, wukt_ref, cache_c, cache_kr,
                      o_ref, cbuf, kbuf, cb, krb, s_all, sem, *, layer, heads, dn, s_new, n_pages, sub_pages):
    b = pl.program_id(0)
    nseq = pl.num_programs(0)
    slot = lax.rem(b, 2)
    rows = s_new * heads
    n_sub = n_pages // sub_pages
    sub_keys = sub_pages * PAGE_SIZE

    def page_copies(seq, slot_):
        out = []
        for i in range(n_pages):
            page = pt_ref[seq * n_pages + i]
            out.append(pltpu.make_async_copy(cache_c.at[layer, page],
                                             cbuf.at[slot_, pl.ds(i * PAGE_SIZE, PAGE_SIZE)], sem.at[0, slot_]))
            out.append(pltpu.make_async_copy(cache_kr.at[layer, page], kbuf.at[slot_, i], sem.at[1, slot_]))
        return out

    @pl.when(b == 0)
    def _():
        for cp in page_copies(0, 0):
            cp.start()

    for cp in page_copies(b, slot):
        cp.wait()

    @pl.when(b + 1 < nseq)
    def _():
        for cp in page_copies(b + 1, 1 - slot):
            cp.start()

    lhs = jnp.concatenate([wukt_ref[...], qabs_ref[...]], axis=0)
    nk = heads * dn
    qrope = qrope_ref[...]
    m_run = jnp.full((rows, 1), NEG_BIG, F32)
    for sb in range(n_sub):
        for i in range(sub_pages):
            pg = sb * sub_pages + i
            cb[sb, i * PAGE_SIZE:(i + 1) * PAGE_SIZE, :] = cbuf[slot, pg * PAGE_SIZE:(pg + 1) * PAGE_SIZE, :].astype(BF16)
            krb[sb, :, i * PAGE_SIZE:(i + 1) * PAGE_SIZE] = kbuf[slot, pg].astype(BF16)
        kq = _dot_t(lhs, cb[sb])
        ssq = jnp.concatenate([jnp.sum(kq[h * dn:(h + 1) * dn, :] ** 2, axis=0, keepdims=True)
                               for h in range(heads)], axis=0)
        r = lax.rsqrt(ssq * (1.0 / dn) + NORM_EPS)
        s = kq[nk:, :] * jnp.concatenate([r] * s_new, axis=0) + _dot(qrope, krb[sb])
        s_all[sb] = s
        m_run = jnp.maximum(m_run, jnp.max(s, axis=1, keepdims=True))

    qf = q128_ref[...].astype(F32)
    assert heads & (heads - 1) == 0
    tok = lax.shift_right_logical(lax.broadcasted_iota(jnp.int32, (rows, 1), 0), heads.bit_length() - 1)
    sn = []
    for t in range(s_new):
        kt = jnp.concatenate([k128n_ref[t].astype(F32)] * s_new, axis=0)
        st = jnp.sum(qf * kt, axis=1, keepdims=True)
        sn.append(jnp.where(tok >= t, st, NEG_BIG))
        m_run = jnp.maximum(m_run, sn[-1])

    l = jnp.zeros((rows, 1), F32)
    acc = jnp.zeros((rows, LANES), F32)
    for sb in range(n_sub):
        p = jnp.exp(s_all[sb] - m_run)
        l = l + jnp.sum(p, axis=1, keepdims=True)
        acc = acc + _dot(p.astype(BF16), cb[sb])
    cn = cn_ref[...].astype(BF16).astype(F32)
    for t in range(s_new):
        pt = jnp.exp(sn[t] - m_run)
        l = l + pt
        acc = acc + pt.astype(BF16).astype(F32) * cn[t:t + 1, :]
    o_ref[...] = (acc / l).astype(BF16)


def _attn_sample(page_table, qabs, qrope, q128, k128n, c_new, wukt, cache_c, cache_kr, *, layer, heads, dn):
    nseq, rows, _ = qabs.shape
    s_new = rows // heads
    n_pages = page_table.shape[1]
    rank = cache_c.shape[-1]
    rope = cache_kr.shape[-2]
    sub_pages = min(SUB_PAGES, n_pages)
    assert n_pages % sub_pages == 0
    n_sub = n_pages // sub_pages
    sub_keys = sub_pages * PAGE_SIZE
    body = functools.partial(_attn_sample_body, layer=layer, heads=heads, dn=dn, s_new=s_new, n_pages=n_pages,
                             sub_pages=sub_pages)
    per_seq = lambda shape: pl.BlockSpec((None,) + shape, lambda b, pt: (b,) + (0,) * len(shape))
    grid_spec = pltpu.PrefetchScalarGridSpec(
        num_scalar_prefetch=1,
        grid=(nseq,),
        in_specs=[per_seq((rows, LANES)), per_seq((rows, rope)), per_seq((rows, LANES)),
                  per_seq((s_new, heads, LANES)), per_seq((s_new, LANES)),
                  pl.BlockSpec(wukt.shape, lambda b, pt: (0, 0)),
                  pl.BlockSpec(memory_space=pl.ANY), pl.BlockSpec(memory_space=pl.ANY)],
        out_specs=per_seq((rows, LANES)),
        scratch_shapes=[pltpu.VMEM((2, n_pages * PAGE_SIZE, rank), F32),
                        pltpu.VMEM((2, n_pages, rope, PAGE_SIZE), F32),
                        pltpu.VMEM((n_sub, sub_keys, rank), BF16),
                        pltpu.VMEM((n_sub, rope, sub_keys), BF16),
                        pltpu.VMEM((n_sub, rows, sub_keys), F32),
                        pltpu.SemaphoreType.DMA((2, 2))],
    )
    return pl.pallas_call(
        body, grid_spec=grid_spec,
        out_shape=jax.ShapeDtypeStruct((nseq, rows, LANES), BF16),
        compiler_params=_params("arbitrary"),
        name="attn_sample",
    )(page_table.reshape(-1), qabs, qrope, q128, k128n, c_new, wukt, cache_c, cache_kr)


def _outproj_body(x_ref, m_ref, gdn_ref, ctx_ref, wuv_ref, wog_ref, wom_ref, o_ref, *, per_token):
    gate = _mod_row(m_ref, 5, per_token)
    mla = _dot(ctx_ref[...], wuv_ref[...])
    mix = _dot(gdn_ref[...].astype(BF16), wog_ref[...]) + _dot(mla.astype(BF16), wom_ref[...])
    o_ref[...] = x_ref[...] + gate * mix


def _outproj(x, mods, gdn_out, ctx, wuv_bd, wo_g, wo_m, *, per_token):
    nb, t, d = x.shape
    tm = min(TOKEN_TILE, t)
    tok = lambda w: pl.BlockSpec((None, tm, w), lambda b, i: (b, i, 0))
    body = functools.partial(_outproj_body, per_token=per_token)
    return pl.pallas_call(
        body,
        grid=(nb, t // tm),
        in_specs=[tok(d), _mods_spec(per_token, tm, d, 1), tok(gdn_out.shape[-1]), tok(ctx.shape[-1]),
                  _resident(wuv_bd.shape), _resident(wo_g.shape), _resident(wo_m.shape)],
        out_specs=tok(d),
        out_shape=jax.ShapeDtypeStruct(x.shape, F32),
        compiler_params=_params("arbitrary", "arbitrary"),
        name="outproj",
    )(x, mods, gdn_out, ctx, wuv_bd, wo_g, wo_m)


def _rope_table(pos, half):
    inv = ROPE_THETA ** (-jnp.arange(half, dtype=F32) / half)
    ang = pos[:, None] * inv[None, :]
    cos2 = jnp.concatenate([jnp.cos(ang), jnp.cos(ang)], axis=1)
    sin2 = jnp.concatenate([-jnp.sin(ang), jnp.sin(ang)], axis=1)
    t = pos.shape[0]
    pad = LANES - 4 * half
    one, zero = jnp.ones((t, pad), F32), jnp.zeros((t, pad), F32)
    return jnp.concatenate([one, cos2, sin2, zero, cos2, cos2, zero, sin2, sin2], axis=1)


def _swap_halves(a, axis=-1):
    lo, hi = jnp.split(a, 2, axis=axis)
    return jnp.concatenate([hi, lo], axis=axis)


def _prep_layer(lp, dims):
    heads, dn, dr, rank, g_heads, conv_dim, v_dim = dims
    assert dn + 2 * dr == LANES and rank == LANES and 2 * g_heads <= LANES
    w_in = lp['w_in']
    d = w_in.shape[0]
    o = 0
    w_qkv = w_in[:, o:o + conv_dim]; o += conv_dim
    w_z = w_in[:, o:o + v_dim]; o += v_dim
    w_b = w_in[:, o:o + g_heads]; o += g_heads
    w_a = w_in[:, o:o + g_heads]; o += g_heads
    w_q = w_in[:, o:o + heads * (dn + dr)].reshape(d, heads, dn + dr); o += heads * (dn + dr)
    w_c = w_in[:, o:o + rank]; o += rank
    w_kr = w_in[:, o:o + dr]
    zeros = lambda n: jnp.zeros((d, n), F32)
    w_q128 = jnp.concatenate([w_q, _swap_halves(w_q[:, :, dn:])], axis=2).reshape(d, heads * LANES)
    w_krs = _swap_halves(w_kr)
    w_all = jnp.concatenate([w_qkv, w_z, w_b, w_a, zeros(LANES - 2 * g_heads), w_q128, w_c,
                             zeros(dn), w_kr, w_kr, zeros(dn), w_krs, w_krs], axis=1).astype(BF16)
    scale = (dn + dr) ** -0.5
    qr_g = lp['mla_qr_norm']
    gq = (jnp.concatenate([lp['mla_qn_norm'], qr_g, _swap_halves(qr_g)]) * scale).reshape(1, LANES)
    lane = jnp.arange(LANES)
    seg = jnp.where(lane < dn, 0, jnp.where(lane < dn + dr, 1, 2))
    seg_len = jnp.where(lane < dn, dn, dr).astype(F32)
    mq = jnp.where(seg[:, None] == seg[None, :], 1.0 / seg_len[None, :], 0.0).astype(BF16)
    mk = jnp.where((lane[:, None] < dn) & (lane[None, :] < dn), 1.0 / dn, 0.0).astype(BF16)
    gk = jnp.concatenate([lp['mla_kn_norm'], jnp.zeros((LANES - dn,), F32)]).reshape(1, LANES)
    kr_g = lp['mla_kr_norm']
    zdn = jnp.zeros((dn,), F32)
    gkr = jnp.stack([jnp.concatenate([zdn, kr_g, kr_g]),
                     jnp.concatenate([zdn, _swap_halves(kr_g), _swap_halves(kr_g)])])
    w_uk = lp['mla_w_uk']
    wuk = jnp.concatenate([w_uk, jnp.zeros((rank, heads, LANES - dn), F32)], axis=2)
    wuk = wuk.reshape(rank, heads * LANES).astype(BF16)
    wukt = w_uk.transpose(1, 2, 0).reshape(heads * dn, rank).astype(BF16)
    absorb = jnp.concatenate([w_uk.transpose(1, 2, 0) * lp['mla_kn_norm'][None, :, None],
                              jnp.zeros((heads, LANES - dn, rank), F32)], axis=1)
    fold = jnp.zeros((LANES, LANES), F32)
    fold = fold.at[dn + jnp.arange(dr), jnp.arange(dr)].set(1.0).at[dn + dr + jnp.arange(dr), jnp.arange(dr)].set(1.0)
    wqa = jnp.concatenate([absorb, jnp.broadcast_to(fold, (heads, LANES, LANES))], axis=2).astype(BF16)
    w_uv = lp['mla_w_uv']
    mv = w_uv.shape[2]
    wuv_bd = (w_uv.transpose(1, 0, 2)[:, :, None, :] * jnp.eye(heads, dtype=F32)[:, None, :, None])
    wuv_bd = wuv_bd.reshape(heads * rank, heads * mv).astype(BF16)
    lane_h = lane // (LANES // 2)
    ms = jnp.where(lane_h[:, None] == lane_h[None, :], 1.0, 0.0).astype(BF16)
    aparams = jnp.zeros((2, LANES), F32)
    aparams = aparams.at[0, g_heads:2 * g_heads].set(jnp.exp(lp['gdn_a_log']))
    aparams = aparams.at[1, g_heads:2 * g_heads].set(lp['gdn_dt_bias'])
    return dict(
        w_all=w_all, gq=gq, mq=mq, gc=lp['mla_ckv_norm'].reshape(1, LANES), wuk=wuk, mk=mk, gk=gk, gkr=gkr,
        wukt=wukt, wqa=wqa, wuv_bd=wuv_bd, ms=ms, aparams=aparams,
        wo_g=lp['w_out'][:v_dim].astype(BF16), wo_m=lp['w_out'][v_dim:].astype(BF16),
        f1=(lp['ffn1_wi'].astype(BF16), lp['ffn1_wo'].astype(BF16)),
        f2=(lp['ffn2_wi'].astype(BF16), lp['ffn2_wo'].astype(BF16)),
    )


def _layer(x, mods, tab, conv_prev, s0, lp, w, lay, dims, *, per_token, t_valid, chunk, attend):
    heads = dims[0]
    x = _ffn(x, mods, lp['norm_ffn1'], *w['f1'], mod0=0, per_token=per_token)
    outs = _inproj(x, mods, lp['norm_mix'], w['w_all'], tab, w['gq'], w['mq'], w['gc'], w['wuk'], w['mk'],
                   w['gk'], w['gkr'], w['wqa'] if per_token else None, lay=lay, heads=heads, per_token=per_token)
    qkv, z, ba, c, krot = outs[0], outs[1], outs[2], outs[5], outs[6]
    gdn_out, nconv, s_new = _gdn(qkv, z, ba, conv_prev, s0, lp['gdn_conv_w'], w['aparams'], lp['gdn_norm'],
                                 w['ms'], t_valid=t_valid, chunk=chunk)
    ctx = attend(outs)
    x = _outproj(x, mods, gdn_out, ctx, w['wuv_bd'], w['wo_g'], w['wo_m'], per_token=per_token)
    x = _ffn(x, mods, lp['norm_ffn2'], *w['f2'], mod0=6, per_token=per_token)
    return x, c, krot, nconv, s_new


def kernel(x_prompt, x_sample, cache_ckv, cache_krope, state_conv, state_gdn, page_table, c_prompt, c_sample,
           ada_w, ada_b, norm_ffn1, ffn1_wi, ffn1_wo, norm_mix, w_in, gdn_conv_w, gdn_a_log, gdn_dt_bias, gdn_norm,
           mla_qn_norm, mla_qr_norm, mla_ckv_norm, mla_kr_norm, mla_kn_norm, mla_w_uk, mla_w_uv, w_out,
           norm_ffn2, ffn2_wi, ffn2_wo):
    depth = ada_w.shape[0]
    bp, tp, d = x_prompt.shape
    bs, ts, _ = x_sample.shape
    g_heads, dk, dv = state_gdn.shape[2:]
    conv_dim = state_conv.shape[-1]
    v_dim = g_heads * dv
    rank, heads, dn = mla_w_uk.shape[1:]
    dr = mla_qr_norm.shape[1]
    past = page_table.shape[1] * PAGE_SIZE
    dims = (heads, dn, dr, rank, g_heads, conv_dim, v_dim)
    lay = _InLayout(conv_dim, v_dim, heads)

    tab_p = _rope_table(jnp.arange(tp, dtype=F32), dr // 2)
    tab_s = jnp.tile(_rope_table(past + jnp.arange(ts, dtype=F32), dr // 2), (bs, 1))
    cond = jnp.concatenate([c_prompt, c_sample], axis=0)
    rows = -(-cond.shape[0] // SUBLANES) * SUBLANES
    cond = jnp.pad(cond, ((0, rows - cond.shape[0]), (0, 0)))

    yp = x_prompt
    ys = x_sample.reshape(1, bs * ts, d)
    outs_p, outs_s = [], []
    for l in range(depth):
        lp = dict(norm_ffn1=norm_ffn1[l], ffn1_wi=ffn1_wi[l], ffn1_wo=ffn1_wo[l], norm_mix=norm_mix[l], w_in=w_in[l],
                  gdn_conv_w=gdn_conv_w[l], gdn_a_log=gdn_a_log[l], gdn_dt_bias=gdn_dt_bias[l], gdn_norm=gdn_norm[l],
                  mla_qn_norm=mla_qn_norm[l], mla_qr_norm=mla_qr_norm[l], mla_ckv_norm=mla_ckv_norm[l],
                  mla_kr_norm=mla_kr_norm[l], mla_kn_norm=mla_kn_norm[l], mla_w_uk=mla_w_uk[l],
                  mla_w_uv=mla_w_uv[l], w_out=w_out[l], norm_ffn2=norm_ffn2[l], ffn2_wi=ffn2_wi[l],
                  ffn2_wo=ffn2_wo[l])
        w = _prep_layer(lp, dims)
        mods = _mods(cond, ada_w[l], ada_b[l])
        mods_p = mods[:bp].reshape(bp, N_MOD, d)
        mods_s = jnp.repeat(mods[bp:bp + bs].reshape(bs, N_MOD, d), ts, axis=0).transpose(1, 0, 2)

        conv0 = jnp.zeros((bp, SUBLANES, conv_dim), F32)
        s0 = jnp.zeros((bp, g_heads, dk, dv), F32)
        attend_p = lambda o: _attn_prompt(o[3], o[4], o[7])
        yp, c_p, kr_p, cv_p, s_p = _layer(yp, mods_p, tab_p, conv0, s0, lp, w, lay, dims, per_token=False,
                                          t_valid=min(GDN_TILE, tp), chunk=min(GDN_CHUNK, tp), attend=attend_p)
        outs_p.append((c_p, kr_p[..., dn:dn + dr], cv_p[:, SUBLANES - (GDN_CONV - 1):], s_p))


        def attend_s(o, l=l, w=w):
            q128, k128, c_new, qabs, qrope = o[3], o[4], o[5], o[7], o[8]
            by_seq = lambda a: a[0].reshape(heads, bs, ts, -1).transpose(1, 2, 0, 3)
            flat = lambda a: by_seq(a).reshape(bs, ts * heads, -1)
            ctx = _attn_sample(page_table, flat(qabs), flat(qrope)[..., :dr], flat(q128), by_seq(k128),
                               c_new.reshape(bs, ts, -1), w['wukt'], cache_ckv, jnp.swapaxes(cache_krope, 2, 3),
                               layer=l, heads=heads, dn=dn)
            return ctx.reshape(1, bs * ts, heads * rank)

        ys, c_s, kr_s, cv_s, s_s = _layer_sample(ys, mods_s, tab_s, state_conv[l], state_gdn[l], lp, w, lay, dims,
                                                 bs, ts, attend_s)
        outs_s.append((c_s, kr_s, cv_s, s_s))

    stack = lambda outs, k: jnp.stack([o[k] for o in outs])
    ys = ys.reshape(bs, ts, d)
    return (yp, ys, stack(outs_p, 0), stack(outs_p, 1), stack(outs_p, 2), stack(outs_p, 3),
            stack(outs_s, 0), stack(outs_s, 1), stack(outs_s, 2), stack(outs_s, 3))


def _layer_sample(x, mods, tab, conv_prev, s0, lp, w, lay, dims, bs, ts, attend):
    heads, dn, dr, g_heads = dims[0], dims[1], dims[2], dims[4]
    x = _ffn(x, mods, lp['norm_ffn1'], *w['f1'], mod0=0, per_token=True)
    outs = _inproj(x, mods, lp['norm_mix'], w['w_all'], tab, w['gq'], w['mq'], w['gc'], w['wuk'], w['mk'],
                   w['gk'], w['gkr'], w['wqa'], lay=lay, heads=heads, per_token=True)
    qkv, z, ba, c, krot = outs[0], outs[1], outs[2], outs[5], outs[6]
    seq = lambda a: a.reshape(bs, ts, a.shape[-1])
    minor = lambda a: seq(a).transpose(1, 2, 0)
    o_t, s_new = _gdn_sample(minor(qkv), conv_prev.transpose(1, 2, 0), minor(z), minor(ba[..., :g_heads]),
                             minor(ba[..., g_heads:2 * g_heads]), s0.transpose(1, 2, 3, 0), lp['gdn_conv_w'].T,
                             w['aparams'][:, g_heads:2 * g_heads], lp['gdn_norm'])
    gdn_out = o_t.transpose(2, 0, 1).reshape(1, bs * ts, -1)
    ctx = attend(outs)
    x = _outproj(x, mods, gdn_out, ctx, w['wuv_bd'], w['wo_g'], w['wo_m'], per_token=True)
    x = _ffn(x, mods, lp['norm_ffn2'], *w['f2'], mod0=6, per_token=True)
    nconv = jnp.concatenate([conv_prev, seq(qkv)], axis=1)[:, -(GDN_CONV - 1):]
    return x, seq(c), seq(krot)[..., dn:dn + dr], nconv, s_new.transpose(3, 0, 1, 2)
```

```python
import functools
import math

import jax
import jax.numpy as jnp
from jax import lax
from jax.experimental import pallas as pl
from jax.experimental.pallas import tpu as pltpu

F32 = jnp.float32
BF16 = jnp.bfloat16

NORM_EPS = 1e-6
ROPE_THETA = 10000.0
PAGE_SIZE = 128
GDN_CONV = 4
GDN_CHUNK = 128
N_MOD = 9
LANES = 128
SUBLANES = 8
NEG_BIG = -1e30
VMEM_LIMIT = 56 * 1024 * 1024

TOKEN_TILE = 512
ATTN_TILE = 512
GDN_TILE = 256
SUB_PAGES = 16


def _dot(a, b):
    return jnp.dot(a, b, preferred_element_type=F32)


def _dot_t(a, b):
    return lax.dot_general(a, b, (((1,), (1,)), ((), ())), preferred_element_type=F32)


def _dot_ta(a, b):
    return lax.dot_general(a, b, (((0,), (0,)), ((), ())), preferred_element_type=F32)


def _bmm(a, b):
    return lax.dot_general(a, b, (((2,), (1,)), ((0,), (0,))), preferred_element_type=F32)


def _bmm_t(a, b):
    return lax.dot_general(a, b, (((2,), (2,)), ((0,), (0,))), preferred_element_type=F32)


def _sigmoid(x):
    return 1.0 / (1.0 + jnp.exp(-x))


def _silu(x):
    return x * _sigmoid(x)


def _params(*sem):
    return pltpu.CompilerParams(dimension_semantics=sem, vmem_limit_bytes=VMEM_LIMIT)


def _resident(shape):
    nd = len(shape)
    return pl.BlockSpec(shape, lambda *_: (0,) * nd, pipeline_mode=pl.Buffered(1))


MODS_PER_SUBLAYER = 3


def _dot_split(x, e, terms, left=False):
    out = None
    for _ in range(terms):
        xb = x.astype(BF16)
        d = _dot(e, xb) if left else _dot(xb, e)
        out = d if out is None else out + d
        x = x - xb.astype(F32)
    return out


def _mod_rows(m_ref, ks, tokens_per_seq, rows):
    if tokens_per_seq is None:
        return [m_ref[k:k + 1, :] for k in ks]
    nseq = m_ref.shape[1]
    row = lax.broadcasted_iota(jnp.int32, (rows, nseq), 0)
    first = lax.broadcasted_iota(jnp.int32, (rows, nseq), 1) * tokens_per_seq
    spread = jnp.where((row >= first) & (row < first + tokens_per_seq), 1.0, 0.0).astype(BF16)
    return [_dot_split(m_ref[k % MODS_PER_SUBLAYER], spread, 3, left=True) for k in ks]


def _mod_norm(x, gain, shift, scale):
    y = x * lax.rsqrt(jnp.mean(x * x, axis=-1, keepdims=True) + NORM_EPS)
    return (y * gain) * (1.0 + scale) + shift


def _mods_spec(tokens_per_seq, tm, d, sublayer):
    if tokens_per_seq is not None:
        assert tm % tokens_per_seq == 0
        return pl.BlockSpec((MODS_PER_SUBLAYER, tm // tokens_per_seq, d), lambda b, i: (sublayer, i, 0))
    return pl.BlockSpec((None, N_MOD, d), lambda b, i: (b, 0, 0))


def _mods_body(c_ref, w_ref, b_ref, o_ref):
    c = c_ref[...]
    o_ref[...] = _dot(_silu(c).astype(BF16), w_ref[...].astype(BF16)) + b_ref[...]


def _mods(cond, ada_w, ada_b):
    rows, d = cond.shape
    n = ada_w.shape[1]
    tn = d
    return pl.pallas_call(
        _mods_body,
        grid=(n // tn,),
        in_specs=[pl.BlockSpec((rows, d), lambda j: (0, 0)),
                  pl.BlockSpec((d, tn), lambda j: (0, j)),
                  pl.BlockSpec((1, tn), lambda j: (0, j))],
        out_specs=pl.BlockSpec((rows, tn), lambda j: (0, j)),
        out_shape=jax.ShapeDtypeStruct((rows, n), F32),
        compiler_params=_params("arbitrary"),
        name="mods",
    )(cond, ada_w, ada_b.reshape(1, n))


def _ffn_body(x_ref, m_ref, g_ref, wi_ref, wo_ref, o_ref, *, mod0, tokens_per_seq, fc):
    x = x_ref[...]
    shift, scale, gate = _mod_rows(m_ref, [mod0, mod0 + 1, mod0 + 2], tokens_per_seq, x.shape[0])
    h = _mod_norm(x, g_ref[...], shift, scale).astype(BF16)
    acc = jnp.zeros(x.shape, F32)
    dff = wo_ref.shape[0]
    for c in range(dff // fc):
        lo, hi = c * fc, (c + 1) * fc
        a = _silu(_dot(h, wi_ref[:, lo:hi])) * _dot(h, wi_ref[:, dff + lo:dff + hi])
        acc = acc + _dot(a.astype(BF16), wo_ref[lo:hi, :])
    o_ref[...] = x + 0.5 * gate * acc


def _ffn(x, mods, gain, wi, wo, *, mod0, tokens_per_seq):
    nb, t, d = x.shape
    tm = min(TOKEN_TILE, t)
    dff = wo.shape[0]
    fc = 2 * LANES
    assert t % tm == 0 and dff % fc == 0
    body = functools.partial(_ffn_body, mod0=mod0, tokens_per_seq=tokens_per_seq, fc=fc)
    return pl.pallas_call(
        body,
        grid=(nb, t // tm),
        in_specs=[pl.BlockSpec((None, tm, d), lambda b, i: (b, i, 0)),
                  _mods_spec(tokens_per_seq, tm, d, mod0 // MODS_PER_SUBLAYER),
                  _resident((1, d)), _resident((d, 2 * dff)), _resident((dff, d))],
        out_specs=pl.BlockSpec((None, tm, d), lambda b, i: (b, i, 0)),
        out_shape=jax.ShapeDtypeStruct(x.shape, F32),
        compiler_params=_params("arbitrary", "arbitrary"),
        name="ffn",
    )(x, mods, gain.reshape(1, d), wi, wo)


class _InLayout:
    def __init__(self, conv_dim, v_dim, heads):
        self.qkv = (0, conv_dim)
        self.z = (conv_dim, conv_dim + v_dim)
        self.ba = (self.z[1], self.z[1] + LANES)
        self.q = (self.ba[1], self.ba[1] + heads * LANES)
        self.ckv = (self.q[1], self.q[1] + LANES)
        self.kr = (self.ckv[1], self.ckv[1] + LANES)
        self.krs = (self.kr[1], self.kr[1] + LANES)
        self.total = self.krs[1]


def _inproj_body(x_ref, m_ref, g_ref, w_ref, cs_ref, place_ref, tbias_ref, gq_ref, mq_ref, gc_ref, wuk_ref, mk_ref,
                 gk_ref, gkr_ref, *rest, lay, heads, tokens_per_seq, sample):
    if sample:
        wqa_ref, rest = rest[0], rest[1:]
    qkv_ref, z_ref, ba_ref, q128_ref, k128_ref, c_ref, krot_ref = rest[:7]
    x = x_ref[...]
    shift, scale = _mod_rows(m_ref, [3, 4], tokens_per_seq, x.shape[0])
    h = _mod_norm(x, g_ref[...], shift, scale).astype(BF16)
    p = _dot(h, w_ref[...])
    qkv_ref[...] = p[:, lay.qkv[0]:lay.qkv[1]]
    z_ref[...] = p[:, lay.z[0]:lay.z[1]]
    ba_ref[...] = p[:, lay.ba[0]:lay.ba[1]]

    tab = _dot_split(cs_ref[...], place_ref[...], 3) + tbias_ref[...]
    tab_q, tab_c, tab_s = tab[:, :LANES], tab[:, LANES:2 * LANES], tab[:, 2 * LANES:]
    for hh in range(heads):
        qh = p[:, lay.q[0] + hh * LANES: lay.q[0] + (hh + 1) * LANES]
        msq = _dot((qh * qh).astype(BF16), mq_ref[...])
        qn = qh * lax.rsqrt(msq + NORM_EPS) * gq_ref[...] * tab_q
        q128_ref[hh] = qn.astype(BF16)
        if sample:
            qabs_ref, qrope_ref = rest[7], rest[8]
            qa = _dot(qn.astype(BF16), wqa_ref[hh])
            qabs_ref[hh] = qa[:, :LANES].astype(BF16)
            qrope_ref[hh] = qa[:, LANES:2 * LANES].astype(BF16)

    ckv = p[:, lay.ckv[0]:lay.ckv[1]]
    c = ckv * lax.rsqrt(jnp.mean(ckv * ckv, axis=-1, keepdims=True) + NORM_EPS) * gc_ref[...]
    c_ref[...] = c
    cb = c.astype(BF16)
    if not sample:
        ct_ref = rest[7]
        ct_ref[...] = jnp.concatenate([c.T, jnp.ones((SUBLANES, c.shape[0]), F32)], axis=0).astype(BF16)
    kr = p[:, lay.kr[0]:lay.kr[1]]
    krs = p[:, lay.krs[0]:lay.krs[1]]
    inv = lax.rsqrt(jnp.sum(kr * kr, axis=-1, keepdims=True) * (2.0 / LANES) + NORM_EPS)
    krot = kr * inv * gkr_ref[0:1, :] * tab_c + krs * inv * gkr_ref[1:2, :] * tab_s
    krot_ref[...] = krot
    knr = _dot(cb, wuk_ref[...])
    for hh in range(heads):
        kh = knr[:, hh * LANES:(hh + 1) * LANES]
        msq = _dot((kh * kh).astype(BF16), mk_ref[...])
        k128_ref[hh] = (kh * lax.rsqrt(msq + NORM_EPS) * gk_ref[...] + krot).astype(BF16)


def _inproj(x, mods, gain, w_all, cs, place, tab_bias, gq, mq, gc, wuk, mk, gk, gkr, wqa, *, lay, heads,
            tokens_per_seq):
    nb, t, d = x.shape
    tm = min(TOKEN_TILE, t)
    assert t % tm == 0
    sample = wqa is not None
    body = functools.partial(_inproj_body, lay=lay, heads=heads, tokens_per_seq=tokens_per_seq, sample=sample)
    tok = lambda w: pl.BlockSpec((None, tm, w), lambda b, i: (b, i, 0))
    hd = lambda w: pl.BlockSpec((None, heads, tm, w), lambda b, i: (b, 0, i, 0))
    conv_dim, v_dim = lay.qkv[1], lay.z[1] - lay.z[0]
    in_specs = [tok(d), _mods_spec(tokens_per_seq, tm, d, 1), _resident((1, d)), _resident(w_all.shape),
                pl.BlockSpec((tm, cs.shape[1]), lambda b, i: (i, 0)), _resident(place.shape),
                _resident(tab_bias.shape),
                _resident((1, LANES)), _resident((LANES, LANES)), _resident((1, LANES)),
                _resident(wuk.shape), _resident((LANES, LANES)), _resident((1, LANES)), _resident((2, LANES))]
    args = [x, mods, gain.reshape(1, d), w_all, cs, place, tab_bias, gq, mq, gc, wuk, mk, gk, gkr]
    out_specs = [tok(conv_dim), tok(v_dim), tok(LANES), hd(LANES), hd(LANES), tok(LANES), tok(LANES)]
    out_shape = [jax.ShapeDtypeStruct((nb, t, conv_dim), F32), jax.ShapeDtypeStruct((nb, t, v_dim), F32),
                 jax.ShapeDtypeStruct((nb, t, LANES), F32),
                 jax.ShapeDtypeStruct((nb, heads, t, LANES), BF16), jax.ShapeDtypeStruct((nb, heads, t, LANES), BF16),
                 jax.ShapeDtypeStruct((nb, t, LANES), F32), jax.ShapeDtypeStruct((nb, t, LANES), F32)]
    if sample:
        in_specs.append(_resident(wqa.shape))
        args.append(wqa)
        out_specs += [hd(LANES), hd(LANES)]
        out_shape += [jax.ShapeDtypeStruct((nb, heads, t, LANES), BF16)] * 2
    else:
        out_specs.append(pl.BlockSpec((None, LANES + SUBLANES, tm), lambda b, i: (b, 0, i)))
        out_shape.append(jax.ShapeDtypeStruct((nb, LANES + SUBLANES, t), BF16))
    return pl.pallas_call(
        body, grid=(nb, t // tm), in_specs=in_specs, out_specs=out_specs, out_shape=out_shape,
        compiler_params=_params("arbitrary", "arbitrary"), name="inproj",
    )(*args)


def _unit_lower_inverse(a):
    n = a.shape[-1]
    assert n & (n - 1) == 0
    row = lax.broadcasted_iota(jnp.int32, (n, n), 0)
    col = lax.broadcasted_iota(jnp.int32, (n, n), 1)
    x = jnp.broadcast_to(jnp.where(row == col, 1.0, 0.0), a.shape)
    b = 1
    while b < n:
        lo_mask = (jnp.bitwise_xor(row, col) < 2 * b) & (jnp.bitwise_and(row, b) != 0) & (jnp.bitwise_and(col, b) == 0)
        lo = jnp.where(lo_mask, a, 0.0)
        if b == 1:
            x = x - lo
        else:
            xb = x.astype(BF16)
            x = x - _bmm(xb, _bmm(lo.astype(BF16), xb).astype(BF16))
        b *= 2
    return x


def _gdn_body(qkv_ref, z_ref, ba_ref, cprev_ref, s0_ref, cw_ref, ap_ref, ng_ref, ms_ref,
              o_ref, nconv_ref, sout_ref, xbuf, s_scr,
              *, tt, t_valid, chunk, heads, dk, dv):
    j = pl.program_id(1)
    hist = SUBLANES

    @pl.when(j == 0)
    def _():
        xbuf[0:hist, :] = cprev_ref[...]
        s_scr[...] = s0_ref[...]

    xbuf[hist:hist + tt, :] = qkv_ref[...]
    conv = xbuf[hist - 3:hist - 3 + tt, :] * cw_ref[0:1, :]
    for jj in range(1, GDN_CONV):
        conv = conv + xbuf[hist - 3 + jj:hist - 3 + jj + tt, :] * cw_ref[jj:jj + 1, :]
    y = _silu(conv)
    tail = xbuf[t_valid:t_valid + hist, :]
    nconv_ref[...] = tail
    xbuf[0:hist, :] = tail

    qk_dim = heads * dk

    def l2(v):
        parts = [_dot((v[:, i * LANES:(i + 1) * LANES] ** 2).astype(BF16), ms_ref[...])
                 for i in range(v.shape[1] // LANES)]
        return v * lax.rsqrt(jnp.concatenate(parts, axis=1) + NORM_EPS)

    q = l2(y[:, :qk_dim]) * (dk ** -0.5)
    k = l2(y[:, qk_dim:2 * qk_dim])
    v = y[:, 2 * qk_dim:]
    ba = ba_ref[...]
    beta = _sigmoid(ba)
    xg = ba + ap_ref[1:2, :]
    g = -ap_ref[0:1, :] * (jnp.maximum(xg, 0.0) + jnp.log1p(jnp.exp(-jnp.abs(xg))))
    if t_valid < tt:
        keep = lax.broadcasted_iota(jnp.int32, (tt, 1), 0) < t_valid
        q, k, v = (jnp.where(keep, a, 0.0) for a in (q, k, v))
        beta, g = jnp.where(keep, beta, 0.0), jnp.where(keep, g, 0.0)
    row = lax.broadcasted_iota(jnp.int32, (tt, tt), 0)
    col = lax.broadcasted_iota(jnp.int32, (tt, tt), 1)
    assert chunk & (chunk - 1) == 0
    tri = jnp.where((row >= col) & (jnp.bitwise_xor(row, col) < chunk), 1.0, 0.0).astype(BF16)
    g_hi = g.astype(BF16)
    g_lo = (g - g_hi.astype(F32)).astype(BF16)
    gc = _dot(tri, g_hi) + _dot(tri, g_lo)
    gct = gc.T
    z = z_ref[...]

    crow = lax.broadcasted_iota(jnp.int32, (chunk, chunk), 0)
    ccol = lax.broadcasted_iota(jnp.int32, (chunk, chunk), 1)
    causal = crow >= ccol
    strict = crow > ccol
    stack = lambda xs: jnp.stack(xs, axis=0)

    s3 = s_scr[...]
    for c in range(tt // chunk):
        r0, r1 = c * chunk, (c + 1) * chunk
        qkb, kbf, vbeta, kbeg, qdec, kdec, decay, gl, zc = ([] for _ in range(9))
        for h in range(heads):
            qh = q[r0:r1, h * dk:(h + 1) * dk]
            kh = k[r0:r1, h * dk:(h + 1) * dk]
            vh = v[r0:r1, h * dv:(h + 1) * dv]
            bh = beta[r0:r1, h:h + 1]
            gcol = gc[r0:r1, heads + h:heads + h + 1]
            grow = gct[heads + h:heads + h + 1, r0:r1]
            glast = grow[:, chunk - 1:chunk]
            eg = jnp.exp(gcol)
            kb = kh * bh
            qkb.append(jnp.concatenate([qh, kb], axis=0).astype(BF16))
            kbf.append(kh.astype(BF16))
            vbeta.append((vh * bh).astype(BF16))
            kbeg.append((kb * eg).astype(BF16))
            qdec.append((qh * eg).astype(BF16))
            kdec.append((kh * jnp.exp(glast - gcol)).astype(BF16))
            decay.append(jnp.exp(jnp.where(causal, jnp.broadcast_to(gcol, (chunk, chunk)) - grow, NEG_BIG)))
            gl.append(jnp.exp(glast))
            zc.append(z[r0:r1, h * dv:(h + 1) * dv])
        decay3 = stack(decay)
        sc = _bmm_t(stack(qkb), stack(kbf))
        qkm = (sc[:, :chunk] * decay3).astype(BF16)
        m = jnp.where(strict, sc[:, chunk:] * decay3, 0.0)
        tinv = _unit_lower_inverse(m).astype(BF16)
        u = _bmm(tinv, stack(vbeta))
        w = _bmm(tinv, stack(kbeg))
        sb = s3.astype(BF16)
        ws_qs = _bmm(jnp.concatenate([w.astype(BF16), stack(qdec)], axis=1), sb)
        vnb = (u - ws_qs[:, :chunk]).astype(BF16)
        o = ws_qs[:, chunk:] + _bmm(qkm, vnb)
        kd3 = stack(kdec)
        upd = stack([_dot_ta(kd3[h], vnb[h]) for h in range(heads)])
        s3 = s3 * stack(gl) + upd
        on = o * lax.rsqrt(jnp.mean(o * o, axis=-1, keepdims=True) + NORM_EPS) * ng_ref[...] * _silu(stack(zc))
        for h in range(heads):
            o_ref[r0:r1, h * dv:(h + 1) * dv] = on[h]
    s_scr[...] = s3
    sout_ref[...] = s3


def _gdn(qkv, z, ba, conv_prev, s0, conv_w, aparams, norm_g, mseg, *, t_valid, chunk):
    nb, t, conv_dim = qkv.shape
    heads, dk, dv = s0.shape[1:]
    tt = min(GDN_TILE, t)
    assert t % tt == 0 and tt % chunk == 0 and (t_valid == tt or t == tt)
    body = functools.partial(_gdn_body, tt=tt, t_valid=t_valid, chunk=chunk, heads=heads, dk=dk, dv=dv)
    v_dim = heads * dv
    tok = lambda w: pl.BlockSpec((None, tt, w), lambda b, i: (b, i, 0))
    return pl.pallas_call(
        body,
        grid=(nb, t // tt),
        in_specs=[tok(conv_dim), tok(v_dim), tok(LANES),
                  pl.BlockSpec((None, SUBLANES, conv_dim), lambda b, i: (b, 0, 0)),
                  pl.BlockSpec((None, heads, dk, dv), lambda b, i: (b, 0, 0, 0)),
                  _resident((GDN_CONV, conv_dim)), _resident((2, LANES)), _resident((1, dv)),
                  _resident((LANES, LANES))],
        out_specs=[tok(v_dim),
                   pl.BlockSpec((None, SUBLANES, conv_dim), lambda b, i: (b, 0, 0)),
                   pl.BlockSpec((None, heads, dk, dv), lambda b, i: (b, 0, 0, 0))],
        out_shape=[jax.ShapeDtypeStruct((nb, t, v_dim), F32),
                   jax.ShapeDtypeStruct((nb, SUBLANES, conv_dim), F32),
                   jax.ShapeDtypeStruct((nb, heads, dk, dv), F32)],
        scratch_shapes=[pltpu.VMEM((tt + SUBLANES, conv_dim), F32), pltpu.VMEM((heads, dk, dv), F32)],
        compiler_params=_params("arbitrary", "arbitrary"),
        name="gdn",
    )(qkv, z, ba, conv_prev, s0, conv_w, aparams, norm_g.reshape(1, dv), mseg)


def _gdn_sample_body(xq_ref, xk_ref, xv_ref, pq_ref, pk_ref, pv_ref, wq_ref, wk_ref, wv_ref, z_ref, b_ref, a_ref,
                     ap_ref, ng_ref, s_ref, o_ref, sout_ref, kq_scr, *, ts, heads, dk, dv):
    h = pl.program_id(0)
    nb = xq_ref.shape[-1]

    def conv(x_ref, p_ref, w_ref):
        taps = GDN_CONV - 1
        xin = [p_ref[i] for i in range(taps)] + [x_ref[t] for t in range(ts)]
        w = [jnp.broadcast_to(w_ref[:, jj:jj + 1], xin[0].shape) for jj in range(GDN_CONV)]
        out = []
        for t in range(ts):
            acc = xin[t] * w[0]
            for jj in range(1, GDN_CONV):
                acc = acc + xin[t + jj] * w[jj]
            out.append(_silu(acc))
        return out

    def l2(x):
        return x * lax.rsqrt(jnp.sum(x * x, axis=0, keepdims=True) + NORM_EPS)

    q = [l2(x) * (dk ** -0.5) for x in conv(xq_ref, pq_ref, wq_ref)]
    k = [l2(x) for x in conv(xk_ref, pk_ref, wk_ref)]
    v = conv(xv_ref, pv_ref, wv_ref)
    for t in range(ts):
        kq_scr[t] = k[t]
        kq_scr[ts + t] = q[t]
    a_h = ap_ref[0, h]
    dt_h = ap_ref[1, h]
    beta, decay = [], []
    for t in range(ts):
        beta.append(_sigmoid(b_ref[t, pl.ds(h, 1), :]))
        xg = a_ref[t, pl.ds(h, 1), :] + dt_h
        decay.append(jnp.exp(-a_h * (jnp.maximum(xg, 0.0) + jnp.log1p(jnp.exp(-jnp.abs(xg))))))

    def row(i, kk):
        return jnp.broadcast_to(kq_scr[i, pl.ds(kk, 1), :], (dv, nb))

    def first(kk, ks):
        return ks + row(0, kk) * s_ref[kk]

    ks = lax.fori_loop(0, dk, first, jnp.zeros((dv, nb), F32), unroll=8)
    ng = jnp.broadcast_to(ng_ref[...], (dv, nb))
    for t in range(ts):
        d = beta[t] * (v[t] - decay[t] * ks)
        src = s_ref if t == 0 else sout_ref
        last = t == ts - 1

        def step(kk, carry, t=t, d=d, src=src, last=last):
            o_acc, ks_next = carry
            s_new = decay[t] * src[kk] + row(t, kk) * d
            sout_ref[kk] = s_new
            o_acc = o_acc + row(ts + t, kk) * s_new
            if not last:
                ks_next = ks_next + row(t + 1, kk) * s_new
            return o_acc, ks_next

        zero = jnp.zeros((dv, nb), F32)
        o, ks = lax.fori_loop(0, dk, step, (zero, zero), unroll=8)
        on = o * lax.rsqrt(jnp.mean(o * o, axis=0, keepdims=True) + NORM_EPS) * ng
        o_ref[t] = on * _silu(z_ref[t])


def _gdn_sample(x_t, prev_t, z_t, b_t, a_t, s_t, conv_w_t, aparams, norm_g):
    ts, conv_dim, nb = x_t.shape
    heads, dk, dv, _ = s_t.shape
    assert dk == dv and conv_dim == 3 * heads * dk
    taps = GDN_CONV - 1
    body = functools.partial(_gdn_sample_body, ts=ts, heads=heads, dk=dk, dv=dv)
    part = lambda rows, off: pl.BlockSpec((rows, dk, nb), lambda h: (0, off + h, 0))
    wpart = lambda off: pl.BlockSpec((dk, GDN_CONV), lambda h: (off + h, 0))
    whole = lambda a: pl.BlockSpec(a.shape, lambda h: (0,) * a.ndim)
    state = pl.BlockSpec((None, dk, dv, nb), lambda h: (h, 0, 0, 0))
    return pl.pallas_call(
        body,
        grid=(heads,),
        in_specs=[part(ts, 0), part(ts, heads), part(ts, 2 * heads),
                  part(taps, 0), part(taps, heads), part(taps, 2 * heads),
                  wpart(0), wpart(heads), wpart(2 * heads),
                  part(ts, 0), whole(b_t), whole(a_t),
                  pl.BlockSpec(memory_space=pltpu.SMEM), pl.BlockSpec((dv, 1), lambda h: (0, 0)), state],
        out_specs=[part(ts, 0), state],
        out_shape=[jax.ShapeDtypeStruct((ts, heads * dv, nb), F32), jax.ShapeDtypeStruct(s_t.shape, F32)],
        scratch_shapes=[pltpu.VMEM((2 * ts, dk, nb), F32)],
        compiler_params=_params("arbitrary"),
        name="gdn_sample",
    )(x_t, x_t, x_t, prev_t, prev_t, prev_t, conv_w_t, conv_w_t, conv_w_t, z_t, b_t, a_t, aparams,
      norm_g.reshape(dv, 1), s_t)


def _attn_body(q_ref, k_ref, ct_ref, o_ref, m_scr, acc_scr, *, heads, tq, rank):
    i = pl.program_id(1)
    j = pl.program_id(2)

    @pl.when(j == 0)
    def _():
        m_scr[...] = jnp.full(m_scr.shape, NEG_BIG, F32)
        acc_scr[...] = jnp.zeros(acc_scr.shape, F32)

    def step(masked):
        ct = ct_ref[...]
        m_old = [m_scr[h] for h in range(heads)]
        acc_old = [acc_scr[h] for h in range(heads)]
        if masked:
            key = lax.broadcasted_iota(jnp.int32, (tq, tq), 0)
            qry = lax.broadcasted_iota(jnp.int32, (tq, tq), 1)
            keep = key <= qry
        m_out, acc_out = [], []
        ahead = 3
        scores = [_dot_t(k_ref[h], q_ref[h]) for h in range(ahead)]
        for h in range(heads):
            if h + ahead < heads:
                scores.append(_dot_t(k_ref[h + ahead], q_ref[h + ahead]))
            st = scores[h]
            if masked:
                st = jnp.where(keep, st, NEG_BIG)
            m_new = jnp.maximum(m_old[h], jnp.max(st, axis=0, keepdims=True))
            alpha = jnp.exp(m_old[h] - m_new)
            pt = jnp.exp(st - m_new).astype(BF16)
            acc_out.append(acc_old[h] * alpha + _dot(ct, pt))
            m_out.append(m_new)
        for h in range(heads):
            m_scr[h] = m_out[h]
            acc_scr[h] = acc_out[h]
        return acc_out

    @pl.when(j < i)
    def _():
        step(False)

    @pl.when(j == i)
    def _():
        acc = step(True)
        for h in range(heads):
            ctx = acc[h][:rank, :] / acc[h][rank:rank + 1, :]
            o_ref[:, h * rank:(h + 1) * rank] = ctx.T.astype(BF16)


def _attn_prompt(q128, k128, ct_ext):
    nb, heads, t, _ = q128.shape
    rows = ct_ext.shape[1]
    rank = rows - SUBLANES
    tq = min(ATTN_TILE, t)
    assert t % tq == 0
    n = t // tq
    body = functools.partial(_attn_body, heads=heads, tq=tq, rank=rank)
    return pl.pallas_call(
        body,
        grid=(nb, n, n),
        in_specs=[pl.BlockSpec((None, heads, tq, LANES), lambda b, i, j: (b, 0, i, 0)),
                  pl.BlockSpec((None, heads, tq, LANES), lambda b, i, j: (b, 0, jnp.minimum(i, j), 0)),
                  pl.BlockSpec((None, rows, tq), lambda b, i, j: (b, 0, jnp.minimum(i, j)))],
        out_specs=pl.BlockSpec((None, tq, heads * rank), lambda b, i, j: (b, i, 0)),
        out_shape=jax.ShapeDtypeStruct((nb, t, heads * rank), BF16),
        scratch_shapes=[pltpu.VMEM((heads, 1, tq), F32), pltpu.VMEM((heads, rows, tq), F32)],
        compiler_params=_params("arbitrary", "arbitrary", "arbitrary"),
        name="attn_prompt",
    )(q128, k128, ct_ext)


def _attn_sample_body(pt_ref, qabs_ref, qrope_ref, q128_ref, k128n_ref, cn_ref, wukt_ref, cache_c, cache_kr,
                      o_ref, cbuf, kbuf, cb, krb, s_all, sem, *, layer, heads, dn, s_new, n_pages, sub_pages):
    b = pl.program_id(0)
    nseq = pl.num_programs(0)
    slot = lax.rem(b, 2)
    rows = s_new * heads
    n_sub = n_pages // sub_pages
    sub_keys = sub_pages * PAGE_SIZE

    def page_copies(seq, slot_):
        out = []
        for i in range(n_pages):
            page = pt_ref[seq * n_pages + i]
            out.append(pltpu.make_async_copy(cache_c.at[layer, page],
                                             cbuf.at[slot_, pl.ds(i * PAGE_SIZE, PAGE_SIZE)], sem.at[0, slot_]))
            out.append(pltpu.make_async_copy(cache_kr.at[layer, page], kbuf.at[slot_, i], sem.at[1, slot_]))
        return out

    @pl.when(b == 0)
    def _():
        for cp in page_copies(0, 0):
            cp.start()

    for cp in page_copies(b, slot):
        cp.wait()

    @pl.when(b + 1 < nseq)
    def _():
        for cp in page_copies(b + 1, 1 - slot):
            cp.start()

    lhs = jnp.concatenate([wukt_ref[...], qabs_ref[...]], axis=0)
    nk = heads * dn
    qrope = qrope_ref[...]
    m_run = jnp.full((rows, 1), NEG_BIG, F32)
    for sb in range(n_sub):
        for i in range(sub_pages):
            pg = sb * sub_pages + i
            cb[sb, i * PAGE_SIZE:(i + 1) * PAGE_SIZE, :] = cbuf[slot, pg * PAGE_SIZE:(pg + 1) * PAGE_SIZE, :].astype(BF16)
            krb[sb, :, i * PAGE_SIZE:(i + 1) * PAGE_SIZE] = kbuf[slot, pg].astype(BF16)
        kq = _dot_t(lhs, cb[sb])
        ssq = jnp.concatenate([jnp.sum(kq[h * dn:(h + 1) * dn, :] ** 2, axis=0, keepdims=True)
                               for h in range(heads)], axis=0)
        r = lax.rsqrt(ssq * (1.0 / dn) + NORM_EPS)
        s = kq[nk:, :] * jnp.concatenate([r] * s_new, axis=0) + _dot(qrope, krb[sb])
        s_all[sb] = s
        m_run = jnp.maximum(m_run, jnp.max(s, axis=1, keepdims=True))

    qf = q128_ref[...].astype(F32)
    assert heads & (heads - 1) == 0
    tok = lax.shift_right_logical(lax.broadcasted_iota(jnp.int32, (rows, 1), 0), heads.bit_length() - 1)
    sn = []
    for t in range(s_new):
        kt = jnp.concatenate([k128n_ref[t].astype(F32)] * s_new, axis=0)
        st = jnp.sum(qf * kt, axis=1, keepdims=True)
        sn.append(jnp.where(tok >= t, st, NEG_BIG))
        m_run = jnp.maximum(m_run, sn[-1])

    l = jnp.zeros((rows, 1), F32)
    acc = jnp.zeros((rows, LANES), F32)
    for sb in range(n_sub):
        p = jnp.exp(s_all[sb] - m_run)
        l = l + jnp.sum(p, axis=1, keepdims=True)
        acc = acc + _dot(p.astype(BF16), cb[sb])
    cn = cn_ref[...].astype(BF16).astype(F32)
    for t in range(s_new):
        pt = jnp.exp(sn[t] - m_run)
        l = l + pt
        acc = acc + pt.astype(BF16).astype(F32) * cn[t:t + 1, :]
    o_ref[...] = (acc / l).astype(BF16)


def _attn_sample(page_table, qabs, qrope, q128, k128n, c_new, wukt, cache_c, cache_kr, *, layer, heads, dn):
    nseq, rows, _ = qabs.shape
    s_new = rows // heads
    n_pages = page_table.shape[1]
    rank = cache_c.shape[-1]
    rope = cache_kr.shape[-2]
    sub_pages = min(SUB_PAGES, n_pages)
    assert n_pages % sub_pages == 0
    n_sub = n_pages // sub_pages
    sub_keys = sub_pages * PAGE_SIZE
    body = functools.partial(_attn_sample_body, layer=layer, heads=heads, dn=dn, s_new=s_new, n_pages=n_pages,
                             sub_pages=sub_pages)
    per_seq = lambda shape: pl.BlockSpec((None,) + shape, lambda b, pt: (b,) + (0,) * len(shape))
    grid_spec = pltpu.PrefetchScalarGridSpec(
        num_scalar_prefetch=1,
        grid=(nseq,),
        in_specs=[per_seq((rows, LANES)), per_seq((rows, rope)), per_seq((rows, LANES)),
                  per_seq((s_new, heads, LANES)), per_seq((s_new, LANES)),
                  pl.BlockSpec(wukt.shape, lambda b, pt: (0, 0)),
                  pl.BlockSpec(memory_space=pl.ANY), pl.BlockSpec(memory_space=pl.ANY)],
        out_specs=per_seq((rows, LANES)),
        scratch_shapes=[pltpu.VMEM((2, n_pages * PAGE_SIZE, rank), F32),
                        pltpu.VMEM((2, n_pages, rope, PAGE_SIZE), F32),
                        pltpu.VMEM((n_sub, sub_keys, rank), BF16),
                        pltpu.VMEM((n_sub, rope, sub_keys), BF16),
                        pltpu.VMEM((n_sub, rows, sub_keys), F32),
                        pltpu.SemaphoreType.DMA((2, 2))],
    )
    return pl.pallas_call(
        body, grid_spec=grid_spec,
        out_shape=jax.ShapeDtypeStruct((nseq, rows, LANES), BF16),
        compiler_params=_params("arbitrary"),
        name="attn_sample",
    )(page_table.reshape(-1), qabs, qrope, q128, k128n, c_new, wukt, cache_c, cache_kr)


def _outproj_body(x_ref, m_ref, gdn_ref, ctx_ref, wuv_ref, wog_ref, wom_ref, o_ref, *, tokens_per_seq):
    (gate,) = _mod_rows(m_ref, [5], tokens_per_seq, x_ref.shape[0])
    mla = _dot(ctx_ref[...], wuv_ref[...])
    mix = _dot(gdn_ref[...].astype(BF16), wog_ref[...]) + _dot(mla.astype(BF16), wom_ref[...])
    o_ref[...] = x_ref[...] + gate * mix


def _outproj(x, mods, gdn_out, ctx, wuv_bd, wo_g, wo_m, *, tokens_per_seq):
    nb, t, d = x.shape
    tm = min(TOKEN_TILE, t)
    tok = lambda w: pl.BlockSpec((None, tm, w), lambda b, i: (b, i, 0))
    body = functools.partial(_outproj_body, tokens_per_seq=tokens_per_seq)
    return pl.pallas_call(
        body,
        grid=(nb, t // tm),
        in_specs=[tok(d), _mods_spec(tokens_per_seq, tm, d, 1), tok(gdn_out.shape[-1]), tok(ctx.shape[-1]),
                  _resident(wuv_bd.shape), _resident(wo_g.shape), _resident(wo_m.shape)],
        out_specs=tok(d),
        out_shape=jax.ShapeDtypeStruct(x.shape, F32),
        compiler_params=_params("arbitrary", "arbitrary"),
        name="outproj",
    )(x, mods, gdn_out, ctx, wuv_bd, wo_g, wo_m)


def _rope_cos_sin(pos, half):
    inv = ROPE_THETA ** (-jnp.arange(half, dtype=F32) / half)
    t = pos.shape[0]
    if (t * half) % LANES == 0:
        ang = jnp.repeat(pos, half).reshape(-1, LANES) * jnp.tile(inv, t).reshape(-1, LANES)
        return jnp.concatenate([jnp.cos(ang).reshape(t, half), jnp.sin(ang).reshape(t, half)], axis=1)
    ang = pos[:, None] * inv[None, :]
    return jnp.concatenate([jnp.cos(ang), jnp.sin(ang)], axis=1)


def _rope_placement(half, dn):
    j = jnp.arange(half)
    place = jnp.zeros((2 * half, 3 * LANES), F32)
    dr = 2 * half
    for base, kind in ((dn, 'c'), (dn + dr, 's'), (LANES + dn, 'c'), (LANES + dn + dr, 'c'),
                       (2 * LANES + dn, 's'), (2 * LANES + dn + dr, 's')):
        if kind == 'c':
            place = place.at[j, base + j].set(1.0).at[j, base + half + j].set(1.0)
        else:
            place = place.at[half + j, base + j].set(-1.0).at[half + j, base + half + j].set(1.0)
    bias = jnp.zeros((1, 3 * LANES), F32).at[0, :dn].set(1.0)
    return place.astype(BF16), bias


def _swap_halves(a, axis=-1):
    lo, hi = jnp.split(a, 2, axis=axis)
    return jnp.concatenate([hi, lo], axis=axis)


def _prep_layer(lp, dims):
    heads, dn, dr, rank, g_heads, conv_dim, v_dim = dims
    assert dn + 2 * dr == LANES and rank == LANES and 2 * g_heads <= LANES
    w_in = lp['w_in']
    d = w_in.shape[0]
    o = 0
    w_qkv = w_in[:, o:o + conv_dim]; o += conv_dim
    w_z = w_in[:, o:o + v_dim]; o += v_dim
    w_b = w_in[:, o:o + g_heads]; o += g_heads
    w_a = w_in[:, o:o + g_heads]; o += g_heads
    w_q = w_in[:, o:o + heads * (dn + dr)].reshape(d, heads, dn + dr); o += heads * (dn + dr)
    w_c = w_in[:, o:o + rank]; o += rank
    w_kr = w_in[:, o:o + dr]
    zeros = lambda n: jnp.zeros((d, n), F32)
    w_q128 = jnp.concatenate([w_q, _swap_halves(w_q[:, :, dn:])], axis=2).reshape(d, heads * LANES)
    w_krs = _swap_halves(w_kr)
    w_all = jnp.concatenate([w_qkv, w_z, w_b, w_a, zeros(LANES - 2 * g_heads), w_q128, w_c,
                             zeros(dn), w_kr, w_kr, zeros(dn), w_krs, w_krs], axis=1).astype(BF16)
    scale = (dn + dr) ** -0.5
    qr_g = lp['mla_qr_norm']
    gq = (jnp.concatenate([lp['mla_qn_norm'], qr_g, _swap_halves(qr_g)]) * scale).reshape(1, LANES)
    lane = jnp.arange(LANES)
    seg = jnp.where(lane < dn, 0, jnp.where(lane < dn + dr, 1, 2))
    seg_len = jnp.where(lane < dn, dn, dr).astype(F32)
    mq = jnp.where(seg[:, None] == seg[None, :], 1.0 / seg_len[None, :], 0.0).astype(BF16)
    mk = jnp.where((lane[:, None] < dn) & (lane[None, :] < dn), 1.0 / dn, 0.0).astype(BF16)
    gk = jnp.concatenate([lp['mla_kn_norm'], jnp.zeros((LANES - dn,), F32)]).reshape(1, LANES)
    kr_g = lp['mla_kr_norm']
    zdn = jnp.zeros((dn,), F32)
    gkr = jnp.stack([jnp.concatenate([zdn, kr_g, kr_g]),
                     jnp.concatenate([zdn, _swap_halves(kr_g), _swap_halves(kr_g)])])
    w_uk = lp['mla_w_uk']
    wuk = jnp.concatenate([w_uk, jnp.zeros((rank, heads, LANES - dn), F32)], axis=2)
    wuk = wuk.reshape(rank, heads * LANES).astype(BF16)
    wukt = w_uk.transpose(1, 2, 0).reshape(heads * dn, rank).astype(BF16)
    absorb = jnp.concatenate([w_uk.transpose(1, 2, 0) * lp['mla_kn_norm'][None, :, None],
                              jnp.zeros((heads, LANES - dn, rank), F32)], axis=1)
    fold = jnp.zeros((LANES, LANES), F32)
    fold = fold.at[dn + jnp.arange(dr), jnp.arange(dr)].set(1.0).at[dn + dr + jnp.arange(dr), jnp.arange(dr)].set(1.0)
    wqa = jnp.concatenate([absorb, jnp.broadcast_to(fold, (heads, LANES, LANES))], axis=2).astype(BF16)
    w_uv = lp['mla_w_uv']
    mv = w_uv.shape[2]
    wuv_bd = (w_uv.transpose(1, 0, 2)[:, :, None, :] * jnp.eye(heads, dtype=F32)[:, None, :, None])
    wuv_bd = wuv_bd.reshape(heads * rank, heads * mv).astype(BF16)
    lane_h = lane // (LANES // 2)
    ms = jnp.where(lane_h[:, None] == lane_h[None, :], 1.0, 0.0).astype(BF16)
    aparams = jnp.zeros((2, LANES), F32)
    aparams = aparams.at[0, g_heads:2 * g_heads].set(jnp.exp(lp['gdn_a_log']))
    aparams = aparams.at[1, g_heads:2 * g_heads].set(lp['gdn_dt_bias'])
    return dict(
        w_all=w_all, gq=gq, mq=mq, gc=lp['mla_ckv_norm'].reshape(1, LANES), wuk=wuk, mk=mk, gk=gk, gkr=gkr,
        wukt=wukt, wqa=wqa, wuv_bd=wuv_bd, ms=ms, aparams=aparams,
        wo_g=lp['w_out'][:v_dim].astype(BF16), wo_m=lp['w_out'][v_dim:].astype(BF16),
        f1=(lp['ffn1_wi'].astype(BF16), lp['ffn1_wo'].astype(BF16)),
        f2=(lp['ffn2_wi'].astype(BF16), lp['ffn2_wo'].astype(BF16)),
    )


def _layer(x, mods, rope, conv_prev, s0, lp, w, lay, dims, *, t_valid, chunk, attend):
    heads = dims[0]
    x = _ffn(x, mods, lp['norm_ffn1'], *w['f1'], mod0=0, tokens_per_seq=None)
    outs = _inproj(x, mods, lp['norm_mix'], w['w_all'], *rope, w['gq'], w['mq'], w['gc'], w['wuk'], w['mk'],
                   w['gk'], w['gkr'], None, lay=lay, heads=heads, tokens_per_seq=None)
    qkv, z, ba, c, krot = outs[0], outs[1], outs[2], outs[5], outs[6]
    gdn_out, nconv, s_new = _gdn(qkv, z, ba, conv_prev, s0, lp['gdn_conv_w'], w['aparams'], lp['gdn_norm'],
                                 w['ms'], t_valid=t_valid, chunk=chunk)
    ctx = attend(outs)
    x = _outproj(x, mods, gdn_out, ctx, w['wuv_bd'], w['wo_g'], w['wo_m'], tokens_per_seq=None)
    x = _ffn(x, mods, lp['norm_ffn2'], *w['f2'], mod0=6, tokens_per_seq=None)
    return x, c, krot, nconv, s_new


def kernel(x_prompt, x_sample, cache_ckv, cache_krope, state_conv, state_gdn, page_table, c_prompt, c_sample,
           ada_w, ada_b, norm_ffn1, ffn1_wi, ffn1_wo, norm_mix, w_in, gdn_conv_w, gdn_a_log, gdn_dt_bias, gdn_norm,
           mla_qn_norm, mla_qr_norm, mla_ckv_norm, mla_kr_norm, mla_kn_norm, mla_w_uk, mla_w_uv, w_out,
           norm_ffn2, ffn2_wi, ffn2_wo):
    depth = ada_w.shape[0]
    bp, tp, d = x_prompt.shape
    bs, ts, _ = x_sample.shape
    g_heads, dk, dv = state_gdn.shape[2:]
    conv_dim = state_conv.shape[-1]
    v_dim = g_heads * dv
    rank, heads, dn = mla_w_uk.shape[1:]
    dr = mla_qr_norm.shape[1]
    past = page_table.shape[1] * PAGE_SIZE
    dims = (heads, dn, dr, rank, g_heads, conv_dim, v_dim)
    lay = _InLayout(conv_dim, v_dim, heads)

    cs_p = _rope_cos_sin(jnp.arange(tp, dtype=F32), dr // 2)
    cs_s = jnp.tile(_rope_cos_sin(past + jnp.arange(ts, dtype=F32), dr // 2), (bs, 1))
    place, tab_bias = _rope_placement(dr // 2, dn)
    cond = jnp.concatenate([c_prompt, c_sample], axis=0)
    rows = -(-cond.shape[0] // SUBLANES) * SUBLANES
    cond = jnp.pad(cond, ((0, rows - cond.shape[0]), (0, 0)))

    yp = x_prompt
    ys = x_sample.reshape(1, bs * ts, d)
    outs_p, outs_s = [], []
    for l in range(depth):
        lp = dict(norm_ffn1=norm_ffn1[l], ffn1_wi=ffn1_wi[l], ffn1_wo=ffn1_wo[l], norm_mix=norm_mix[l], w_in=w_in[l],
                  gdn_conv_w=gdn_conv_w[l], gdn_a_log=gdn_a_log[l], gdn_dt_bias=gdn_dt_bias[l], gdn_norm=gdn_norm[l],
                  mla_qn_norm=mla_qn_norm[l], mla_qr_norm=mla_qr_norm[l], mla_ckv_norm=mla_ckv_norm[l],
                  mla_kr_norm=mla_kr_norm[l], mla_kn_norm=mla_kn_norm[l], mla_w_uk=mla_w_uk[l],
                  mla_w_uv=mla_w_uv[l], w_out=w_out[l], norm_ffn2=norm_ffn2[l], ffn2_wi=ffn2_wi[l],
                  ffn2_wo=ffn2_wo[l])
        w = _prep_layer(lp, dims)
        mods = _mods(cond, ada_w[l], ada_b[l])
        mods_p = mods[:bp].reshape(bp, N_MOD, d)
        mods_s = mods[bp:bp + bs].reshape(bs, N_MOD, d).transpose(1, 0, 2)

        conv0 = jnp.zeros((bp, SUBLANES, conv_dim), F32)
        s0 = jnp.zeros((bp, g_heads, dk, dv), F32)
        attend_p = lambda o: _attn_prompt(o[3], o[4], o[7])
        yp, c_p, kr_p, cv_p, s_p = _layer(yp, mods_p, (cs_p, place, tab_bias), conv0, s0, lp, w, lay, dims,
                                          t_valid=min(GDN_TILE, tp), chunk=min(GDN_CHUNK, tp), attend=attend_p)
        outs_p.append((c_p, kr_p[..., dn:dn + dr], cv_p[:, SUBLANES - (GDN_CONV - 1):], s_p))


        def attend_s(o, l=l, w=w):
            q128, k128, c_new, qabs, qrope = o[3], o[4], o[5], o[7], o[8]
            by_seq = lambda a: a[0].reshape(heads, bs, ts, -1).transpose(1, 2, 0, 3)
            flat = lambda a: by_seq(a).reshape(bs, ts * heads, -1)
            ctx = _attn_sample(page_table, flat(qabs), flat(qrope)[..., :dr], flat(q128), by_seq(k128),
                               c_new.reshape(bs, ts, -1), w['wukt'], cache_ckv, jnp.swapaxes(cache_krope, 2, 3),
                               layer=l, heads=heads, dn=dn)
            return ctx.reshape(1, bs * ts, heads * rank)

        ys, c_s, kr_s, cv_s, s_s = _layer_sample(ys, mods_s, (cs_s, place, tab_bias), state_conv[l], state_gdn[l], lp, w, lay, dims,
                                                 bs, ts, attend_s)
        outs_s.append((c_s, kr_s, cv_s, s_s))

    stack = lambda outs, k: jnp.stack([o[k] for o in outs])
    ys = ys.reshape(bs, ts, d)
    return (yp, ys, stack(outs_p, 0), stack(outs_p, 1), stack(outs_p, 2), stack(outs_p, 3),
            stack(outs_s, 0), stack(outs_s, 1), stack(outs_s, 2), stack(outs_s, 3))


def _layer_sample(x, mods, rope, conv_prev, s0, lp, w, lay, dims, bs, ts, attend):
    heads, dn, dr, g_heads = dims[0], dims[1], dims[2], dims[4]
    x = _ffn(x, mods, lp['norm_ffn1'], *w['f1'], mod0=0, tokens_per_seq=ts)
    outs = _inproj(x, mods, lp['norm_mix'], w['w_all'], *rope, w['gq'], w['mq'], w['gc'], w['wuk'], w['mk'],
                   w['gk'], w['gkr'], w['wqa'], lay=lay, heads=heads, tokens_per_seq=ts)
    qkv, z, ba, c, krot = outs[0], outs[1], outs[2], outs[5], outs[6]
    seq = lambda a: a.reshape(bs, ts, a.shape[-1])
    minor = lambda a: seq(a).transpose(1, 2, 0)
    o_t, s_new = _gdn_sample(minor(qkv), conv_prev.transpose(1, 2, 0), minor(z), minor(ba[..., :g_heads]),
                             minor(ba[..., g_heads:2 * g_heads]), s0.transpose(1, 2, 3, 0), lp['gdn_conv_w'].T,
                             w['aparams'][:, g_heads:2 * g_heads], lp['gdn_norm'])
    gdn_out = o_t.transpose(2, 0, 1).reshape(1, bs * ts, -1)
    ctx = attend(outs)
    x = _outproj(x, mods, gdn_out, ctx, w['wuv_bd'], w['wo_g'], w['wo_m'], tokens_per_seq=ts)
    x = _ffn(x, mods, lp['norm_ffn2'], *w['f2'], mod0=6, tokens_per_seq=ts)
    nconv = jnp.concatenate([conv_prev, seq(qkv)], axis=1)[:, -(GDN_CONV - 1):]
    return x, seq(c), seq(krot)[..., dn:dn + dr], nconv, s_new.transpose(3, 0, 1, 2)
```

```python
import functools
import math

import jax
import jax.numpy as jnp
from jax import lax
from jax.experimental import pallas as pl
from jax.experimental.pallas import tpu as pltpu

F32 = jnp.float32
BF16 = jnp.bfloat16

NORM_EPS = 1e-6
ROPE_THETA = 10000.0
PAGE_SIZE = 128
GDN_CONV = 4
GDN_CHUNK = 128
N_MOD = 9
LANES = 128
SUBLANES = 8
NEG_BIG = -1e30
VMEM_LIMIT = 56 * 1024 * 1024

TOKEN_TILE = 512
ATTN_TILE = 512
GDN_TILE = 512
SUB_PAGES = 16
AHEAD = 4


def _dot(a, b):
    return jnp.dot(a, b, preferred_element_type=F32)


def _dot_t(a, b):
    return lax.dot_general(a, b, (((1,), (1,)), ((), ())), preferred_element_type=F32)


def _dot_ta(a, b):
    return lax.dot_general(a, b, (((0,), (0,)), ((), ())), preferred_element_type=F32)


def _bmm(a, b):
    return lax.dot_general(a, b, (((2,), (1,)), ((0,), (0,))), preferred_element_type=F32)


def _bmm_t(a, b):
    return lax.dot_general(a, b, (((2,), (2,)), ((0,), (0,))), preferred_element_type=F32)


def _sigmoid(x):
    return 1.0 / (1.0 + jnp.exp(-x))


def _silu(x):
    return x * _sigmoid(x)


def _params(*sem):
    return pltpu.CompilerParams(dimension_semantics=sem, vmem_limit_bytes=VMEM_LIMIT)


def _resident(shape):
    nd = len(shape)
    return pl.BlockSpec(shape, lambda *_: (0,) * nd, pipeline_mode=pl.Buffered(1))


MODS_PER_SUBLAYER = 3


def _dot_split(x, e, terms, left=False):
    out = None
    for _ in range(terms):
        xb = x.astype(BF16)
        d = _dot(e, xb) if left else _dot(xb, e)
        out = d if out is None else out + d
        x = x - xb.astype(F32)
    return out


def _mod_rows(m_ref, ks, tokens_per_seq, rows):
    if tokens_per_seq is None:
        return [m_ref[k:k + 1, :] for k in ks]
    nseq = m_ref.shape[1]
    row = lax.broadcasted_iota(jnp.int32, (rows, nseq), 0)
    first = lax.broadcasted_iota(jnp.int32, (rows, nseq), 1) * tokens_per_seq
    spread = jnp.where((row >= first) & (row < first + tokens_per_seq), 1.0, 0.0).astype(BF16)
    return [_dot_split(m_ref[k % MODS_PER_SUBLAYER], spread, 3, left=True) for k in ks]


def _mod_norm(x, gain, shift, scale):
    y = x * lax.rsqrt(jnp.mean(x * x, axis=-1, keepdims=True) + NORM_EPS)
    return (y * gain) * (1.0 + scale) + shift


def _mods_spec(tokens_per_seq, tm, d, sublayer):
    if tokens_per_seq is not None:
        assert tm % tokens_per_seq == 0
        return pl.BlockSpec((MODS_PER_SUBLAYER, tm // tokens_per_seq, d), lambda b, i: (sublayer, i, 0))
    return pl.BlockSpec((None, N_MOD, d), lambda b, i: (b, 0, 0))


def _mods_body(c_ref, w_ref, b_ref, o_ref):
    c = c_ref[...]
    o_ref[...] = _dot(_silu(c).astype(BF16), w_ref[...].astype(BF16)) + b_ref[...]


def _mods(cond, ada_w, ada_b):
    rows, d = cond.shape
    n = ada_w.shape[1]
    tn = d
    return pl.pallas_call(
        _mods_body,
        grid=(n // tn,),
        in_specs=[pl.BlockSpec((rows, d), lambda j: (0, 0)),
                  pl.BlockSpec((d, tn), lambda j: (0, j)),
                  pl.BlockSpec((1, tn), lambda j: (0, j))],
        out_specs=pl.BlockSpec((rows, tn), lambda j: (0, j)),
        out_shape=jax.ShapeDtypeStruct((rows, n), F32),
        compiler_params=_params("arbitrary"),
        name="mods",
    )(cond, ada_w, ada_b.reshape(1, n))


def _ffn_body(x_ref, m_ref, g_ref, wi_ref, wo_ref, o_ref, *, mod0, tokens_per_seq, fc):
    x = x_ref[...]
    shift, scale, gate = _mod_rows(m_ref, [mod0, mod0 + 1, mod0 + 2], tokens_per_seq, x.shape[0])
    h = _mod_norm(x, g_ref[...], shift, scale).astype(BF16)
    acc = jnp.zeros(x.shape, F32)
    dff = wo_ref.shape[0]
    for c in range(dff // fc):
        lo, hi = c * fc, (c + 1) * fc
        a = _silu(_dot(h, wi_ref[:, lo:hi])) * _dot(h, wi_ref[:, dff + lo:dff + hi])
        acc = acc + _dot(a.astype(BF16), wo_ref[lo:hi, :])
    o_ref[...] = x + 0.5 * gate * acc


def _ffn(x, mods, gain, wi, wo, *, mod0, tokens_per_seq):
    nb, t, d = x.shape
    tm = min(TOKEN_TILE, t)
    dff = wo.shape[0]
    fc = 2 * LANES
    assert t % tm == 0 and dff % fc == 0
    body = functools.partial(_ffn_body, mod0=mod0, tokens_per_seq=tokens_per_seq, fc=fc)
    return pl.pallas_call(
        body,
        grid=(nb, t // tm),
        in_specs=[pl.BlockSpec((None, tm, d), lambda b, i: (b, i, 0)),
                  _mods_spec(tokens_per_seq, tm, d, mod0 // MODS_PER_SUBLAYER),
                  _resident((1, d)), _resident((d, 2 * dff)), _resident((dff, d))],
        out_specs=pl.BlockSpec((None, tm, d), lambda b, i: (b, i, 0)),
        out_shape=jax.ShapeDtypeStruct(x.shape, F32),
        compiler_params=_params("arbitrary", "arbitrary"),
        name="ffn",
    )(x, mods, gain.reshape(1, d), wi, wo)


class _InLayout:
    def __init__(self, conv_dim, v_dim, heads):
        self.qkv = (0, conv_dim)
        self.z = (conv_dim, conv_dim + v_dim)
        self.ba = (self.z[1], self.z[1] + LANES)
        self.q = (self.ba[1], self.ba[1] + heads * LANES)
        self.ckv = (self.q[1], self.q[1] + LANES)
        self.kr = (self.ckv[1], self.ckv[1] + LANES)
        self.krs = (self.kr[1], self.kr[1] + LANES)
        self.total = self.krs[1]


def _inproj_body(x_ref, m_ref, g_ref, w_ref, cs_ref, place_ref, tbias_ref, gq_ref, mq_ref, gc_ref, wuk_ref, mk_ref,
                 gk_ref, gkr_ref, *rest, lay, heads, tokens_per_seq, sample, qk_dim):
    if sample:
        wqa_ref, rest = rest[0], rest[1:]
    else:
        cprev_ref, cw_ref, ms_ref, rest = rest[0], rest[1], rest[2], rest[3:]
    qkv_ref, z_ref, ba_ref, q128_ref, k128_ref, c_ref, krot_ref = rest[:7]
    x = x_ref[...]
    shift, scale = _mod_rows(m_ref, [3, 4], tokens_per_seq, x.shape[0])
    h = _mod_norm(x, g_ref[...], shift, scale).astype(BF16)
    cw = 4 * LANES
    assert lay.total % cw == 0 and lay.qkv[1] % cw == 0 and (lay.z[1] - lay.z[0]) == cw
    n_chunks = lay.total // cw
    n_qkv = lay.qkv[1] // cw
    rest_chunks = list(range(n_qkv, n_chunks))
    order = []
    for c in range(max(n_qkv, len(rest_chunks))):
        order += ([c] if c < n_qkv else []) + ([rest_chunks[c]] if c < len(rest_chunks) else [])
    chunks = {}

    def cols(lo, hi):
        c = lo // cw
        assert (hi - 1) // cw == c
        return chunks[c][:, lo - c * cw:hi - c * cw]

    tm = x.shape[0]
    hist = SUBLANES
    taps = GDN_CONV - 1
    if not sample:
        ct_ref, nconv_ref, xbuf = rest[7], rest[8], rest[9]

        @pl.when(pl.program_id(1) == 0)
        def _():
            xbuf[0:hist, :] = cprev_ref[...]

    tab = _dot_split(cs_ref[...], place_ref[...], 3) + tbias_ref[...]
    tab_q, tab_c, tab_s = tab[:, :LANES], tab[:, LANES:2 * LANES], tab[:, 2 * LANES:]

    def group(lo):
        hi = lo + LANES
        if lo < lay.qkv[1]:
            if sample:
                qkv_ref[:, lo:hi] = cols(lo, hi)
                return
            xbuf[hist:hist + tm, lo:hi] = cols(lo, hi)
            conv = xbuf[hist - taps:hist - taps + tm, lo:hi] * cw_ref[0:1, lo:hi]
            for jj in range(1, GDN_CONV):
                conv = conv + xbuf[hist - taps + jj:hist - taps + jj + tm, lo:hi] * cw_ref[jj:jj + 1, lo:hi]
            yi = _silu(conv)
            if lo < 2 * qk_dim:
                ss = _dot((yi * yi).astype(BF16), ms_ref[...])
                yi = yi * (lax.rsqrt(ss + NORM_EPS) * ((LANES // 2) ** -0.5 if lo < qk_dim else 1.0))
            qkv_ref[:, lo:hi] = yi
            tail = xbuf[tm:tm + hist, lo:hi]
            nconv_ref[:, lo:hi] = tail
            xbuf[0:hist, lo:hi] = tail
        elif lo < lay.z[1]:
            z_ref[:, lo - lay.z[0]:hi - lay.z[0]] = cols(lo, hi)
        elif lo < lay.ba[1]:
            ba_ref[...] = cols(lo, hi)
        elif lo < lay.q[1]:
            hh = (lo - lay.q[0]) // LANES
            qh = cols(lo, hi)
            msq = _dot((qh * qh).astype(BF16), mq_ref[...])
            qn = qh * lax.rsqrt(msq + NORM_EPS) * gq_ref[...] * tab_q
            q128_ref[hh] = qn.astype(BF16)
            if sample:
                qabs_ref, qrope_ref = rest[7], rest[8]
                qa = _dot(qn.astype(BF16), wqa_ref[hh])
                qabs_ref[hh] = qa[:, :LANES].astype(BF16)
                qrope_ref[hh] = qa[:, LANES:2 * LANES].astype(BF16)
        elif lo == lay.ckv[0]:
            keys()

    def keys():
        ckv = cols(*lay.ckv)
        c = ckv * lax.rsqrt(jnp.mean(ckv * ckv, axis=-1, keepdims=True) + NORM_EPS) * gc_ref[...]
        c_ref[...] = c
        cb = c.astype(BF16)
        if not sample:
            ct_ref[...] = jnp.concatenate([c.T, jnp.ones((SUBLANES, c.shape[0]), F32)], axis=0).astype(BF16)
        kr = cols(*lay.kr)
        krs = cols(*lay.krs)
        inv = lax.rsqrt(jnp.sum(kr * kr, axis=-1, keepdims=True) * (2.0 / LANES) + NORM_EPS)
        krot = kr * inv * gkr_ref[0:1, :] * tab_c + krs * inv * gkr_ref[1:2, :] * tab_s
        krot_ref[...] = krot
        knr = _dot(cb, wuk_ref[...])
        for hh in range(heads):
            kh = knr[:, hh * LANES:(hh + 1) * LANES]
            msq = _dot((kh * kh).astype(BF16), mk_ref[...])
            k128_ref[hh] = (kh * lax.rsqrt(msq + NORM_EPS) * gk_ref[...] + krot).astype(BF16)

    ahead = 2
    for i in range(len(order) + ahead):
        if i < len(order):
            k = order[i]
            chunks[k] = _dot(h, w_ref[:, k * cw:(k + 1) * cw])
        if i >= ahead:
            k = order[i - ahead]
            for lo in range(k * cw, (k + 1) * cw, LANES):
                group(lo)


def _inproj(x, mods, gain, w_all, cs, place, tab_bias, gq, mq, gc, wuk, mk, gk, gkr, wqa, conv, *, lay, heads,
            tokens_per_seq):
    nb, t, d = x.shape
    tm = min(TOKEN_TILE, t)
    assert t % tm == 0
    sample = wqa is not None
    body = functools.partial(_inproj_body, lay=lay, heads=heads, tokens_per_seq=tokens_per_seq, sample=sample,
                             qk_dim=(lay.qkv[1] - (lay.z[1] - lay.z[0])) // 2)
    tok = lambda w: pl.BlockSpec((None, tm, w), lambda b, i: (b, i, 0))
    hd = lambda w: pl.BlockSpec((None, heads, tm, w), lambda b, i: (b, 0, i, 0))
    conv_dim, v_dim = lay.qkv[1], lay.z[1] - lay.z[0]
    in_specs = [tok(d), _mods_spec(tokens_per_seq, tm, d, 1), _resident((1, d)), _resident(w_all.shape),
                pl.BlockSpec((tm, cs.shape[1]), lambda b, i: (i, 0)), _resident(place.shape),
                _resident(tab_bias.shape),
                _resident((1, LANES)), _resident((LANES, LANES)), _resident((1, LANES)),
                _resident(wuk.shape), _resident((LANES, LANES)), _resident((1, LANES)), _resident((2, LANES))]
    args = [x, mods, gain.reshape(1, d), w_all, cs, place, tab_bias, gq, mq, gc, wuk, mk, gk, gkr]
    out_specs = [tok(conv_dim), tok(v_dim), tok(LANES), hd(LANES), hd(LANES), tok(LANES), tok(LANES)]
    out_shape = [jax.ShapeDtypeStruct((nb, t, conv_dim), F32), jax.ShapeDtypeStruct((nb, t, v_dim), F32),
                 jax.ShapeDtypeStruct((nb, t, LANES), F32),
                 jax.ShapeDtypeStruct((nb, heads, t, LANES), BF16), jax.ShapeDtypeStruct((nb, heads, t, LANES), BF16),
                 jax.ShapeDtypeStruct((nb, t, LANES), F32), jax.ShapeDtypeStruct((nb, t, LANES), F32)]
    if sample:
        in_specs.append(_resident(wqa.shape))
        args.append(wqa)
        out_specs += [hd(LANES), hd(LANES)]
        out_shape += [jax.ShapeDtypeStruct((nb, heads, t, LANES), BF16)] * 2
        scratch = []
    else:
        conv_prev, conv_w, ms = conv
        in_specs += [pl.BlockSpec((None, SUBLANES, conv_dim), lambda b, i: (b, 0, 0)),
                     _resident(conv_w.shape), _resident(ms.shape)]
        args += [conv_prev, conv_w, ms]
        out_specs += [pl.BlockSpec((None, LANES + SUBLANES, tm), lambda b, i: (b, 0, i)),
                      pl.BlockSpec((None, SUBLANES, conv_dim), lambda b, i: (b, 0, 0))]
        out_shape += [jax.ShapeDtypeStruct((nb, LANES + SUBLANES, t), BF16),
                      jax.ShapeDtypeStruct((nb, SUBLANES, conv_dim), F32)]
        scratch = [pltpu.VMEM((tm + SUBLANES, conv_dim), F32)]
    return pl.pallas_call(
        body, grid=(nb, t // tm), in_specs=in_specs, out_specs=out_specs, out_shape=out_shape,
        scratch_shapes=scratch, compiler_params=_params("arbitrary", "arbitrary"), name="inproj",
    )(*args)


def _unit_lower_inverse(a):
    n = a.shape[-1]
    assert n & (n - 1) == 0
    row = lax.broadcasted_iota(jnp.int32, (n, n), 0)
    col = lax.broadcasted_iota(jnp.int32, (n, n), 1)
    x = jnp.broadcast_to(jnp.where(row == col, 1.0, 0.0), a.shape)
    b = 1
    while b < n:
        lo_mask = (jnp.bitwise_xor(row, col) < 2 * b) & (jnp.bitwise_and(row, b) != 0) & (jnp.bitwise_and(col, b) == 0)
        lo = jnp.where(lo_mask, a, 0.0)
        if b == 1:
            x = x - lo
        else:
            xb = x.astype(BF16)
            x = x - _bmm(xb, _bmm(lo.astype(BF16), xb).astype(BF16))
        b *= 2
    return x


def _gdn_body(qkv_ref, z_ref, ba_ref, s0_ref, ap_ref, ng_ref, o_ref, sout_ref, s_scr,
              *, tt, t_valid, chunk, heads, dk, dv):
    j = pl.program_id(1)

    @pl.when(j == 0)
    def _():
        s_scr[...] = s0_ref[...]

    qk_dim = heads * dk
    y = qkv_ref[...]
    q = y[:, :qk_dim]
    k = y[:, qk_dim:2 * qk_dim]
    v = y[:, 2 * qk_dim:]
    ba = ba_ref[...]
    beta = _sigmoid(ba)
    xg = ba + ap_ref[1:2, :]
    g = -ap_ref[0:1, :] * (jnp.maximum(xg, 0.0) + jnp.log1p(jnp.exp(-jnp.abs(xg))))
    if t_valid < tt:
        keep = lax.broadcasted_iota(jnp.int32, (tt, 1), 0) < t_valid
        q, k, v = (jnp.where(keep, a, 0.0) for a in (q, k, v))
        beta, g = jnp.where(keep, beta, 0.0), jnp.where(keep, g, 0.0)
    row = lax.broadcasted_iota(jnp.int32, (tt, tt), 0)
    col = lax.broadcasted_iota(jnp.int32, (tt, tt), 1)
    assert chunk & (chunk - 1) == 0
    tri = jnp.where((row >= col) & (jnp.bitwise_xor(row, col) < chunk), 1.0, 0.0).astype(BF16)
    g_hi = g.astype(BF16)
    g_lo = (g - g_hi.astype(F32)).astype(BF16)
    gc = _dot(tri, g_hi) + _dot(tri, g_lo)
    gct = gc.T
    z = z_ref[...]

    crow = lax.broadcasted_iota(jnp.int32, (chunk, chunk), 0)
    ccol = lax.broadcasted_iota(jnp.int32, (chunk, chunk), 1)
    causal = crow >= ccol
    strict = crow > ccol
    stack = lambda xs: jnp.stack(xs, axis=0)

    s3 = s_scr[...]
    for c in range(tt // chunk):
        r0, r1 = c * chunk, (c + 1) * chunk
        qkb, kbf, vbeta, kbeg, qdec, kdec, decay, gl, zc = ([] for _ in range(9))
        for h in range(heads):
            qh = q[r0:r1, h * dk:(h + 1) * dk]
            kh = k[r0:r1, h * dk:(h + 1) * dk]
            vh = v[r0:r1, h * dv:(h + 1) * dv]
            bh = beta[r0:r1, h:h + 1]
            gcol = gc[r0:r1, heads + h:heads + h + 1]
            grow = gct[heads + h:heads + h + 1, r0:r1]
            glast = grow[:, chunk - 1:chunk]
            eg = jnp.exp(gcol)
            kb = kh * bh
            qkb.append(jnp.concatenate([qh, kb], axis=0).astype(BF16))
            kbf.append(kh.astype(BF16))
            vbeta.append((vh * bh).astype(BF16))
            kbeg.append((kb * eg).astype(BF16))
            qdec.append((qh * eg).astype(BF16))
            kdec.append((kh * jnp.exp(glast - gcol)).astype(BF16))
            decay.append(jnp.exp(jnp.where(causal, jnp.broadcast_to(gcol, (chunk, chunk)) - grow, NEG_BIG)))
            gl.append(jnp.exp(glast))
            zc.append(z[r0:r1, h * dv:(h + 1) * dv])
        decay3 = stack(decay)
        sc = _bmm_t(stack(qkb), stack(kbf))
        qkm = (sc[:, :chunk] * decay3).astype(BF16)
        m = jnp.where(strict, sc[:, chunk:] * decay3, 0.0)
        tinv = _unit_lower_inverse(m).astype(BF16)
        u = _bmm(tinv, stack(vbeta))
        w = _bmm(tinv, stack(kbeg))
        sb = s3.astype(BF16)
        ws_qs = _bmm(jnp.concatenate([w.astype(BF16), stack(qdec)], axis=1), sb)
        vnb = (u - ws_qs[:, :chunk]).astype(BF16)
        o = ws_qs[:, chunk:] + _bmm(qkm, vnb)
        kd3 = stack(kdec)
        upd = stack([_dot_ta(kd3[h], vnb[h]) for h in range(heads)])
        s3 = s3 * stack(gl) + upd
        on = o * lax.rsqrt(jnp.mean(o * o, axis=-1, keepdims=True) + NORM_EPS) * ng_ref[...] * _silu(stack(zc))
        for h in range(heads):
            o_ref[r0:r1, h * dv:(h + 1) * dv] = on[h]
    s_scr[...] = s3
    sout_ref[...] = s3


def _gdn(qkv, z, ba, s0, aparams, norm_g, *, t_valid, chunk):
    nb, t, conv_dim = qkv.shape
    heads, dk, dv = s0.shape[1:]
    tt = min(GDN_TILE, t)
    assert t % tt == 0 and tt % chunk == 0 and (t_valid == tt or t == tt)
    body = functools.partial(_gdn_body, tt=tt, t_valid=t_valid, chunk=chunk, heads=heads, dk=dk, dv=dv)
    v_dim = heads * dv
    tok = lambda w: pl.BlockSpec((None, tt, w), lambda b, i: (b, i, 0))
    return pl.pallas_call(
        body,
        grid=(nb, t // tt),
        in_specs=[tok(conv_dim), tok(v_dim), tok(LANES),
                  pl.BlockSpec((None, heads, dk, dv), lambda b, i: (b, 0, 0, 0)),
                  _resident((2, LANES)), _resident((1, dv))],
        out_specs=[tok(v_dim), pl.BlockSpec((None, heads, dk, dv), lambda b, i: (b, 0, 0, 0))],
        out_shape=[jax.ShapeDtypeStruct((nb, t, v_dim), F32), jax.ShapeDtypeStruct((nb, heads, dk, dv), F32)],
        scratch_shapes=[pltpu.VMEM((heads, dk, dv), F32)],
        compiler_params=_params("arbitrary", "arbitrary"),
        name="gdn",
    )(qkv, z, ba, s0, aparams, norm_g.reshape(1, dv))


def _gdn_sample_body(xq_ref, xk_ref, xv_ref, pq_ref, pk_ref, pv_ref, wq_ref, wk_ref, wv_ref, z_ref, b_ref, a_ref,
                     ap_ref, ng_ref, s_ref, o_ref, sout_ref, kq_scr, *, ts, heads, dk, dv):
    h = pl.program_id(0)
    nb = xq_ref.shape[-1]

    def conv(x_ref, p_ref, w_ref):
        taps = GDN_CONV - 1
        xin = [p_ref[i] for i in range(taps)] + [x_ref[t] for t in range(ts)]
        w = [jnp.broadcast_to(w_ref[:, jj:jj + 1], xin[0].shape) for jj in range(GDN_CONV)]
        out = []
        for t in range(ts):
            acc = xin[t] * w[0]
            for jj in range(1, GDN_CONV):
                acc = acc + xin[t + jj] * w[jj]
            out.append(_silu(acc))
        return out

    def l2(x):
        return x * lax.rsqrt(jnp.sum(x * x, axis=0, keepdims=True) + NORM_EPS)

    q = [l2(x) * (dk ** -0.5) for x in conv(xq_ref, pq_ref, wq_ref)]
    k = [l2(x) for x in conv(xk_ref, pk_ref, wk_ref)]
    v = conv(xv_ref, pv_ref, wv_ref)
    for t in range(ts):
        kq_scr[t] = k[t]
        kq_scr[ts + t] = q[t]
    a_h = ap_ref[0, h]
    dt_h = ap_ref[1, h]
    beta, decay = [], []
    for t in range(ts):
        beta.append(_sigmoid(b_ref[t, pl.ds(h, 1), :]))
        xg = a_ref[t, pl.ds(h, 1), :] + dt_h
        decay.append(jnp.exp(-a_h * (jnp.maximum(xg, 0.0) + jnp.log1p(jnp.exp(-jnp.abs(xg))))))

    def row(i, kk):
        return jnp.broadcast_to(kq_scr[i, pl.ds(kk, 1), :], (dv, nb))

    def first(kk, ks):
        return ks + row(0, kk) * s_ref[kk]

    ks = lax.fori_loop(0, dk, first, jnp.zeros((dv, nb), F32), unroll=8)
    ng = jnp.broadcast_to(ng_ref[...], (dv, nb))
    for t in range(ts):
        d = beta[t] * (v[t] - decay[t] * ks)
        src = s_ref if t == 0 else sout_ref
        last = t == ts - 1

        def step(kk, carry, t=t, d=d, src=src, last=last):
            o_acc, ks_next = carry
            s_new = decay[t] * src[kk] + row(t, kk) * d
            sout_ref[kk] = s_new
            o_acc = o_acc + row(ts + t, kk) * s_new
            if not last:
                ks_next = ks_next + row(t + 1, kk) * s_new
            return o_acc, ks_next

        zero = jnp.zeros((dv, nb), F32)
        o, ks = lax.fori_loop(0, dk, step, (zero, zero), unroll=8)
        on = o * lax.rsqrt(jnp.mean(o * o, axis=0, keepdims=True) + NORM_EPS) * ng
        o_ref[t] = on * _silu(z_ref[t])


def _gdn_sample(x_t, prev_t, z_t, b_t, a_t, s_t, conv_w_t, aparams, norm_g):
    ts, conv_dim, nb = x_t.shape
    heads, dk, dv, _ = s_t.shape
    assert dk == dv and conv_dim == 3 * heads * dk
    taps = GDN_CONV - 1
    body = functools.partial(_gdn_sample_body, ts=ts, heads=heads, dk=dk, dv=dv)
    part = lambda rows, off: pl.BlockSpec((rows, dk, nb), lambda h: (0, off + h, 0))
    wpart = lambda off: pl.BlockSpec((dk, GDN_CONV), lambda h: (off + h, 0))
    whole = lambda a: pl.BlockSpec(a.shape, lambda h: (0,) * a.ndim)
    state = pl.BlockSpec((None, dk, dv, nb), lambda h: (h, 0, 0, 0))
    return pl.pallas_call(
        body,
        grid=(heads,),
        in_specs=[part(ts, 0), part(ts, heads), part(ts, 2 * heads),
                  part(taps, 0), part(taps, heads), part(taps, 2 * heads),
                  wpart(0), wpart(heads), wpart(2 * heads),
                  part(ts, 0), whole(b_t), whole(a_t),
                  pl.BlockSpec(memory_space=pltpu.SMEM), pl.BlockSpec((dv, 1), lambda h: (0, 0)), state],
        out_specs=[part(ts, 0), state],
        out_shape=[jax.ShapeDtypeStruct((ts, heads * dv, nb), F32), jax.ShapeDtypeStruct(s_t.shape, F32)],
        scratch_shapes=[pltpu.VMEM((2 * ts, dk, nb), F32)],
        compiler_params=_params("arbitrary"),
        name="gdn_sample",
    )(x_t, x_t, x_t, prev_t, prev_t, prev_t, conv_w_t, conv_w_t, conv_w_t, z_t, b_t, a_t, aparams,
      norm_g.reshape(dv, 1), s_t)


def _attn_body(q_ref, k_ref, ct_ref, o_ref, m_scr, acc_scr, *, heads, tq, rank):
    i = pl.program_id(1)
    j = pl.program_id(2)

    @pl.when(j == 0)
    def _():
        m_scr[...] = jnp.full(m_scr.shape, NEG_BIG, F32)
        acc_scr[...] = jnp.zeros(acc_scr.shape, F32)

    def step(masked):
        ct = ct_ref[...]
        m_old = [m_scr[h] for h in range(heads)]
        acc_old = [acc_scr[h] for h in range(heads)]
        if masked:
            key = lax.broadcasted_iota(jnp.int32, (tq, tq), 0)
            qry = lax.broadcasted_iota(jnp.int32, (tq, tq), 1)
            keep = key <= qry
        m_out, acc_out = [], []
        ahead = 3
        scores = [_dot_t(k_ref[h], q_ref[h]) for h in range(ahead)]
        for h in range(heads):
            if h + ahead < heads:
                scores.append(_dot_t(k_ref[h + ahead], q_ref[h + ahead]))
            st = scores[h]
            if masked:
                st = jnp.where(keep, st, NEG_BIG)
            m_new = jnp.maximum(m_old[h], jnp.max(st, axis=0, keepdims=True))
            alpha = jnp.exp(m_old[h] - m_new)
            pt = jnp.exp(st - m_new).astype(BF16)
            acc_out.append(acc_old[h] * alpha + _dot(ct, pt))
            m_out.append(m_new)
        for h in range(heads):
            m_scr[h] = m_out[h]
            acc_scr[h] = acc_out[h]
        return acc_out

    @pl.when(j < i)
    def _():
        step(False)

    @pl.when(j == i)
    def _():
        acc = step(True)
        for h in range(heads):
            ctx = acc[h][:rank, :] / acc[h][rank:rank + 1, :]
            o_ref[:, h * rank:(h + 1) * rank] = ctx.T.astype(BF16)


def _attn_prompt(q128, k128, ct_ext):
    nb, heads, t, _ = q128.shape
    rows = ct_ext.shape[1]
    rank = rows - SUBLANES
    tq = min(ATTN_TILE, t)
    assert t % tq == 0
    n = t // tq
    body = functools.partial(_attn_body, heads=heads, tq=tq, rank=rank)
    return pl.pallas_call(
        body,
        grid=(nb, n, n),
        in_specs=[pl.BlockSpec((None, heads, tq, LANES), lambda b, i, j: (b, 0, i, 0)),
                  pl.BlockSpec((None, heads, tq, LANES), lambda b, i, j: (b, 0, jnp.minimum(i, j), 0)),
                  pl.BlockSpec((None, rows, tq), lambda b, i, j: (b, 0, jnp.minimum(i, j)))],
        out_specs=pl.BlockSpec((None, tq, heads * rank), lambda b, i, j: (b, i, 0)),
        out_shape=jax.ShapeDtypeStruct((nb, t, heads * rank), BF16),
        scratch_shapes=[pltpu.VMEM((heads, 1, tq), F32), pltpu.VMEM((heads, rows, tq), F32)],
        compiler_params=_params("arbitrary", "arbitrary", "arbitrary"),
        name="attn_prompt",
    )(q128, k128, ct_ext)


def _attn_sample_body(pt_ref, qabs_ref, qrope_ref, q128_ref, k128n_ref, cn_ref, wukt_ref, cache_c, cache_kr,
                      o_ref, cbuf, kbuf, cb, krb, s_all, sem, *, layer, heads, dn, s_new, n_pages, sub_pages):
    b = pl.program_id(0)
    nseq = pl.num_programs(0)
    slot = lax.rem(b, 2)
    rows = s_new * heads
    n_sub = n_pages // sub_pages
    sub_keys = sub_pages * PAGE_SIZE

    def page_copies(seq, slot_):
        out = []
        for i in range(n_pages):
            page = pt_ref[seq * n_pages + i]
            out.append(pltpu.make_async_copy(cache_c.at[layer, page],
                                             cbuf.at[slot_, pl.ds(i * PAGE_SIZE, PAGE_SIZE)], sem.at[0, slot_]))
            out.append(pltpu.make_async_copy(cache_kr.at[layer, page], kbuf.at[slot_, i], sem.at[1, slot_]))
        return out

    @pl.when(b == 0)
    def _():
        for cp in page_copies(0, 0):
            cp.start()

    for cp in page_copies(b, slot):
        cp.wait()

    @pl.when(b + 1 < nseq)
    def _():
        for cp in page_copies(b + 1, 1 - slot):
            cp.start()

    lhs = jnp.concatenate([wukt_ref[...], qabs_ref[...]], axis=0)
    nk = heads * dn
    qrope = qrope_ref[...]

    def scores(sb):
        for i in range(sub_pages):
            pg = sb * sub_pages + i
            cb[sb, i * PAGE_SIZE:(i + 1) * PAGE_SIZE, :] = cbuf[slot, pg * PAGE_SIZE:(pg + 1) * PAGE_SIZE, :].astype(BF16)
            krb[sb, :, i * PAGE_SIZE:(i + 1) * PAGE_SIZE] = kbuf[slot, pg].astype(BF16)
        kq = _dot_t(lhs, cb[sb])
        ssq = jnp.concatenate([jnp.sum(kq[h * dn:(h + 1) * dn, :] ** 2, axis=0, keepdims=True)
                               for h in range(heads)], axis=0)
        r = lax.rsqrt(ssq * (1.0 / dn) + NORM_EPS)
        s = kq[nk:, :] * jnp.concatenate([r] * s_new, axis=0) + _dot(qrope, krb[sb])
        s_all[sb] = s
        return jnp.max(s, axis=1, keepdims=True)

    state = (jnp.full((rows, 1), NEG_BIG, F32), jnp.zeros((rows, 1), F32), jnp.zeros((rows, LANES), F32))

    def absorb(state, sb, m_sb):
        m_run, l, acc = state
        m_new = jnp.maximum(m_run, m_sb)
        alpha = jnp.exp(m_run - m_new)
        p = jnp.exp(s_all[sb] - m_new)
        return (m_new, alpha * l + jnp.sum(p, axis=1, keepdims=True), alpha * acc + _dot(p.astype(BF16), cb[sb]))

    ahead = AHEAD
    maxes = []
    for sb in range(n_sub):
        maxes.append(scores(sb))
        if sb >= ahead:
            state = absorb(state, sb - ahead, maxes[sb - ahead])
    for sb in range(max(n_sub - ahead, 0), n_sub):
        state = absorb(state, sb, maxes[sb])
    m_run, l, acc = state

    qf = q128_ref[...].astype(F32)
    assert heads & (heads - 1) == 0
    tok = lax.shift_right_logical(lax.broadcasted_iota(jnp.int32, (rows, 1), 0), heads.bit_length() - 1)
    sn = []
    m_fin = m_run
    for t in range(s_new):
        kt = jnp.concatenate([k128n_ref[t].astype(F32)] * s_new, axis=0)
        st = jnp.sum(qf * kt, axis=1, keepdims=True)
        sn.append(jnp.where(tok >= t, st, NEG_BIG))
        m_fin = jnp.maximum(m_fin, sn[-1])
    alpha = jnp.exp(m_run - m_fin)
    l = alpha * l
    acc = alpha * acc
    cn = cn_ref[...].astype(BF16).astype(F32)
    for t in range(s_new):
        pt = jnp.exp(sn[t] - m_fin)
        l = l + pt
        acc = acc + pt.astype(BF16).astype(F32) * cn[t:t + 1, :]
    o_ref[...] = (acc / l).astype(BF16)


def _attn_sample(page_table, qabs, qrope, q128, k128n, c_new, wukt, cache_c, cache_kr, *, layer, heads, dn):
    nseq, rows, _ = qabs.shape
    s_new = rows // heads
    n_pages = page_table.shape[1]
    rank = cache_c.shape[-1]
    rope = cache_kr.shape[-2]
    sub_pages = min(SUB_PAGES, n_pages)
    assert n_pages % sub_pages == 0
    n_sub = n_pages // sub_pages
    sub_keys = sub_pages * PAGE_SIZE
    body = functools.partial(_attn_sample_body, layer=layer, heads=heads, dn=dn, s_new=s_new, n_pages=n_pages,
                             sub_pages=sub_pages)
    per_seq = lambda shape: pl.BlockSpec((None,) + shape, lambda b, pt: (b,) + (0,) * len(shape))
    grid_spec = pltpu.PrefetchScalarGridSpec(
        num_scalar_prefetch=1,
        grid=(nseq,),
        in_specs=[per_seq((rows, LANES)), per_seq((rows, rope)), per_seq((rows, LANES)),
                  per_seq((s_new, heads, LANES)), per_seq((s_new, LANES)),
                  pl.BlockSpec(wukt.shape, lambda b, pt: (0, 0)),
                  pl.BlockSpec(memory_space=pl.ANY), pl.BlockSpec(memory_space=pl.ANY)],
        out_specs=per_seq((rows, LANES)),
        scratch_shapes=[pltpu.VMEM((2, n_pages * PAGE_SIZE, rank), F32),
                        pltpu.VMEM((2, n_pages, rope, PAGE_SIZE), F32),
                        pltpu.VMEM((n_sub, sub_keys, rank), BF16),
                        pltpu.VMEM((n_sub, rope, sub_keys), BF16),
                        pltpu.VMEM((n_sub, rows, sub_keys), F32),
                        pltpu.SemaphoreType.DMA((2, 2))],
    )
    return pl.pallas_call(
        body, grid_spec=grid_spec,
        out_shape=jax.ShapeDtypeStruct((nseq, rows, LANES), BF16),
        compiler_params=_params("arbitrary"),
        name="attn_sample",
    )(page_table.reshape(-1), qabs, qrope, q128, k128n, c_new, wukt, cache_c, cache_kr)


def _outproj_body(x_ref, m_ref, gdn_ref, ctx_ref, wuv_ref, wog_ref, wom_ref, o_ref, *, tokens_per_seq):
    (gate,) = _mod_rows(m_ref, [5], tokens_per_seq, x_ref.shape[0])
    mla = _dot(ctx_ref[...], wuv_ref[...])
    mix = _dot(gdn_ref[...].astype(BF16), wog_ref[...]) + _dot(mla.astype(BF16), wom_ref[...])
    o_ref[...] = x_ref[...] + gate * mix


def _outproj(x, mods, gdn_out, ctx, wuv_bd, wo_g, wo_m, *, tokens_per_seq):
    nb, t, d = x.shape
    tm = min(TOKEN_TILE, t)
    tok = lambda w: pl.BlockSpec((None, tm, w), lambda b, i: (b, i, 0))
    body = functools.partial(_outproj_body, tokens_per_seq=tokens_per_seq)
    return pl.pallas_call(
        body,
        grid=(nb, t // tm),
        in_specs=[tok(d), _mods_spec(tokens_per_seq, tm, d, 1), tok(gdn_out.shape[-1]), tok(ctx.shape[-1]),
                  _resident(wuv_bd.shape), _resident(wo_g.shape), _resident(wo_m.shape)],
        out_specs=tok(d),
        out_shape=jax.ShapeDtypeStruct(x.shape, F32),
        compiler_params=_params("arbitrary", "arbitrary"),
        name="outproj",
    )(x, mods, gdn_out, ctx, wuv_bd, wo_g, wo_m)


def _rope_cos_sin(pos, half):
    inv = ROPE_THETA ** (-jnp.arange(half, dtype=F32) / half)
    t = pos.shape[0]
    if (t * half) % LANES == 0:
        ang = jnp.repeat(pos, half).reshape(-1, LANES) * jnp.tile(inv, t).reshape(-1, LANES)
        return jnp.concatenate([jnp.cos(ang).reshape(t, half), jnp.sin(ang).reshape(t, half)], axis=1)
    ang = pos[:, None] * inv[None, :]
    return jnp.concatenate([jnp.cos(ang), jnp.sin(ang)], axis=1)


def _rope_placement(half, dn):
    j = jnp.arange(half)
    place = jnp.zeros((2 * half, 3 * LANES), F32)
    dr = 2 * half
    for base, kind in ((dn, 'c'), (dn + dr, 's'), (LANES + dn, 'c'), (LANES + dn + dr, 'c'),
                       (2 * LANES + dn, 's'), (2 * LANES + dn + dr, 's')):
        if kind == 'c':
            place = place.at[j, base + j].set(1.0).at[j, base + half + j].set(1.0)
        else:
            place = place.at[half + j, base + j].set(-1.0).at[half + j, base + half + j].set(1.0)
    bias = jnp.zeros((1, 3 * LANES), F32).at[0, :dn].set(1.0)
    return place.astype(BF16), bias


def _swap_halves(a, axis=-1):
    lo, hi = jnp.split(a, 2, axis=axis)
    return jnp.concatenate([hi, lo], axis=axis)


def _prep_layer(lp, dims):
    heads, dn, dr, rank, g_heads, conv_dim, v_dim = dims
    assert dn + 2 * dr == LANES and rank == LANES and 2 * g_heads <= LANES
    w_in = lp['w_in']
    d = w_in.shape[0]
    o = 0
    w_qkv = w_in[:, o:o + conv_dim]; o += conv_dim
    w_z = w_in[:, o:o + v_dim]; o += v_dim
    w_b = w_in[:, o:o + g_heads]; o += g_heads
    w_a = w_in[:, o:o + g_heads]; o += g_heads
    w_q = w_in[:, o:o + heads * (dn + dr)].reshape(d, heads, dn + dr); o += heads * (dn + dr)
    w_c = w_in[:, o:o + rank]; o += rank
    w_kr = w_in[:, o:o + dr]
    zeros = lambda n: jnp.zeros((d, n), F32)
    w_q128 = jnp.concatenate([w_q, _swap_halves(w_q[:, :, dn:])], axis=2).reshape(d, heads * LANES)
    w_krs = _swap_halves(w_kr)
    w_all = jnp.concatenate([w_qkv, w_z, w_b, w_a, zeros(LANES - 2 * g_heads), w_q128, w_c,
                             zeros(dn), w_kr, w_kr, zeros(dn), w_krs, w_krs], axis=1).astype(BF16)
    scale = (dn + dr) ** -0.5
    qr_g = lp['mla_qr_norm']
    gq = (jnp.concatenate([lp['mla_qn_norm'], qr_g, _swap_halves(qr_g)]) * scale).reshape(1, LANES)
    lane = jnp.arange(LANES)
    seg = jnp.where(lane < dn, 0, jnp.where(lane < dn + dr, 1, 2))
    seg_len = jnp.where(lane < dn, dn, dr).astype(F32)
    mq = jnp.where(seg[:, None] == seg[None, :], 1.0 / seg_len[None, :], 0.0).astype(BF16)
    mk = jnp.where((lane[:, None] < dn) & (lane[None, :] < dn), 1.0 / dn, 0.0).astype(BF16)
    gk = jnp.concatenate([lp['mla_kn_norm'], jnp.zeros((LANES - dn,), F32)]).reshape(1, LANES)
    kr_g = lp['mla_kr_norm']
    zdn = jnp.zeros((dn,), F32)
    gkr = jnp.stack([jnp.concatenate([zdn, kr_g, kr_g]),
                     jnp.concatenate([zdn, _swap_halves(kr_g), _swap_halves(kr_g)])])
    w_uk = lp['mla_w_uk']
    wuk = jnp.concatenate([w_uk, jnp.zeros((rank, heads, LANES - dn), F32)], axis=2)
    wuk = wuk.reshape(rank, heads * LANES).astype(BF16)
    wukt = w_uk.transpose(1, 2, 0).reshape(heads * dn, rank).astype(BF16)
    absorb = jnp.concatenate([w_uk.transpose(1, 2, 0) * lp['mla_kn_norm'][None, :, None],
                              jnp.zeros((heads, LANES - dn, rank), F32)], axis=1)
    fold = jnp.zeros((LANES, LANES), F32)
    fold = fold.at[dn + jnp.arange(dr), jnp.arange(dr)].set(1.0).at[dn + dr + jnp.arange(dr), jnp.arange(dr)].set(1.0)
    wqa = jnp.concatenate([absorb, jnp.broadcast_to(fold, (heads, LANES, LANES))], axis=2).astype(BF16)
    w_uv = lp['mla_w_uv']
    mv = w_uv.shape[2]
    wuv_bd = (w_uv.transpose(1, 0, 2)[:, :, None, :] * jnp.eye(heads, dtype=F32)[:, None, :, None])
    wuv_bd = wuv_bd.reshape(heads * rank, heads * mv).astype(BF16)
    lane_h = lane // (LANES // 2)
    ms = jnp.where(lane_h[:, None] == lane_h[None, :], 1.0, 0.0).astype(BF16)
    aparams = jnp.zeros((2, LANES), F32)
    aparams = aparams.at[0, g_heads:2 * g_heads].set(jnp.exp(lp['gdn_a_log']))
    aparams = aparams.at[1, g_heads:2 * g_heads].set(lp['gdn_dt_bias'])
    return dict(
        w_all=w_all, gq=gq, mq=mq, gc=lp['mla_ckv_norm'].reshape(1, LANES), wuk=wuk, mk=mk, gk=gk, gkr=gkr,
        wukt=wukt, wqa=wqa, wuv_bd=wuv_bd, ms=ms, aparams=aparams,
        wo_g=lp['w_out'][:v_dim].astype(BF16), wo_m=lp['w_out'][v_dim:].astype(BF16),
        f1=(lp['ffn1_wi'].astype(BF16), lp['ffn1_wo'].astype(BF16)),
        f2=(lp['ffn2_wi'].astype(BF16), lp['ffn2_wo'].astype(BF16)),
    )


def _layer(x, mods, rope, conv_prev, s0, lp, w, lay, dims, *, t_valid, chunk, attend):
    heads = dims[0]
    x = _ffn(x, mods, lp['norm_ffn1'], *w['f1'], mod0=0, tokens_per_seq=None)
    outs = _inproj(x, mods, lp['norm_mix'], w['w_all'], *rope, w['gq'], w['mq'], w['gc'], w['wuk'], w['mk'],
                   w['gk'], w['gkr'], None, (conv_prev, lp['gdn_conv_w'], w['ms']), lay=lay, heads=heads,
                   tokens_per_seq=None)
    qkv, z, ba, c, krot, nconv = outs[0], outs[1], outs[2], outs[5], outs[6], outs[8]
    gdn_out, s_new = _gdn(qkv, z, ba, s0, w['aparams'], lp['gdn_norm'], t_valid=t_valid, chunk=chunk)
    ctx = attend(outs)
    x = _outproj(x, mods, gdn_out, ctx, w['wuv_bd'], w['wo_g'], w['wo_m'], tokens_per_seq=None)
    x = _ffn(x, mods, lp['norm_ffn2'], *w['f2'], mod0=6, tokens_per_seq=None)
    return x, c, krot, nconv, s_new


def kernel(x_prompt, x_sample, cache_ckv, cache_krope, state_conv, state_gdn, page_table, c_prompt, c_sample,
           ada_w, ada_b, norm_ffn1, ffn1_wi, ffn1_wo, norm_mix, w_in, gdn_conv_w, gdn_a_log, gdn_dt_bias, gdn_norm,
           mla_qn_norm, mla_qr_norm, mla_ckv_norm, mla_kr_norm, mla_kn_norm, mla_w_uk, mla_w_uv, w_out,
           norm_ffn2, ffn2_wi, ffn2_wo):
    depth = ada_w.shape[0]
    bp, tp, d = x_prompt.shape
    bs, ts, _ = x_sample.shape
    g_heads, dk, dv = state_gdn.shape[2:]
    conv_dim = state_conv.shape[-1]
    v_dim = g_heads * dv
    rank, heads, dn = mla_w_uk.shape[1:]
    dr = mla_qr_norm.shape[1]
    past = page_table.shape[1] * PAGE_SIZE
    dims = (heads, dn, dr, rank, g_heads, conv_dim, v_dim)
    lay = _InLayout(conv_dim, v_dim, heads)

    cs_p = _rope_cos_sin(jnp.arange(tp, dtype=F32), dr // 2)
    cs_s = jnp.tile(_rope_cos_sin(past + jnp.arange(ts, dtype=F32), dr // 2), (bs, 1))
    place, tab_bias = _rope_placement(dr // 2, dn)
    cond = jnp.concatenate([c_prompt, c_sample], axis=0)
    rows = -(-cond.shape[0] // SUBLANES) * SUBLANES
    cond = jnp.pad(cond, ((0, rows - cond.shape[0]), (0, 0)))

    yp = x_prompt
    ys = x_sample.reshape(1, bs * ts, d)
    outs_p, outs_s = [], []
    for l in range(depth):
        lp = dict(norm_ffn1=norm_ffn1[l], ffn1_wi=ffn1_wi[l], ffn1_wo=ffn1_wo[l], norm_mix=norm_mix[l], w_in=w_in[l],
                  gdn_conv_w=gdn_conv_w[l], gdn_a_log=gdn_a_log[l], gdn_dt_bias=gdn_dt_bias[l], gdn_norm=gdn_norm[l],
                  mla_qn_norm=mla_qn_norm[l], mla_qr_norm=mla_qr_norm[l], mla_ckv_norm=mla_ckv_norm[l],
                  mla_kr_norm=mla_kr_norm[l], mla_kn_norm=mla_kn_norm[l], mla_w_uk=mla_w_uk[l],
                  mla_w_uv=mla_w_uv[l], w_out=w_out[l], norm_ffn2=norm_ffn2[l], ffn2_wi=ffn2_wi[l],
                  ffn2_wo=ffn2_wo[l])
        w = _prep_layer(lp, dims)
        mods = _mods(cond, ada_w[l], ada_b[l])
        mods_p = mods[:bp].reshape(bp, N_MOD, d)
        mods_s = mods[bp:bp + bs].reshape(bs, N_MOD, d).transpose(1, 0, 2)

        conv0 = jnp.zeros((bp, SUBLANES, conv_dim), F32)
        s0 = jnp.zeros((bp, g_heads, dk, dv), F32)
        attend_p = lambda o: _attn_prompt(o[3], o[4], o[7])
        yp, c_p, kr_p, cv_p, s_p = _layer(yp, mods_p, (cs_p, place, tab_bias), conv0, s0, lp, w, lay, dims,
                                          t_valid=min(GDN_TILE, tp), chunk=min(GDN_CHUNK, tp), attend=attend_p)
        outs_p.append((c_p, kr_p[..., dn:dn + dr], cv_p[:, SUBLANES - (GDN_CONV - 1):], s_p))


        def attend_s(o, l=l, w=w):
            q128, k128, c_new, qabs, qrope = o[3], o[4], o[5], o[7], o[8]
            by_seq = lambda a: a[0].reshape(heads, bs, ts, -1).transpose(1, 2, 0, 3)
            flat = lambda a: by_seq(a).reshape(bs, ts * heads, -1)
            ctx = _attn_sample(page_table, flat(qabs), flat(qrope)[..., :dr], flat(q128), by_seq(k128),
                               c_new.reshape(bs, ts, -1), w['wukt'], cache_ckv, jnp.swapaxes(cache_krope, 2, 3),
                               layer=l, heads=heads, dn=dn)
            return ctx.reshape(1, bs * ts, heads * rank)

        ys, c_s, kr_s, cv_s, s_s = _layer_sample(ys, mods_s, (cs_s, place, tab_bias), state_conv[l], state_gdn[l], lp, w, lay, dims,
                                                 bs, ts, attend_s)
        outs_s.append((c_s, kr_s, cv_s, s_s))

    stack = lambda outs, k: jnp.stack([o[k] for o in outs])
    ys = ys.reshape(bs, ts, d)
    return (yp, ys, stack(outs_p, 0), stack(outs_p, 1), stack(outs_p, 2), stack(outs_p, 3),
            stack(outs_s, 0), stack(outs_s, 1), stack(outs_s, 2), stack(outs_s, 3))


def _layer_sample(x, mods, rope, conv_prev, s0, lp, w, lay, dims, bs, ts, attend):
    heads, dn, dr, g_heads = dims[0], dims[1], dims[2], dims[4]
    x = _ffn(x, mods, lp['norm_ffn1'], *w['f1'], mod0=0, tokens_per_seq=ts)
    outs = _inproj(x, mods, lp['norm_mix'], w['w_all'], *rope, w['gq'], w['mq'], w['gc'], w['wuk'], w['mk'],
                   w['gk'], w['gkr'], w['wqa'], None, lay=lay, heads=heads, tokens_per_seq=ts)
    qkv, z, ba, c, krot = outs[0], outs[1], outs[2], outs[5], outs[6]
    seq = lambda a: a.reshape(bs, ts, a.shape[-1])
    minor = lambda a: seq(a).transpose(1, 2, 0)
    o_t, s_new = _gdn_sample(minor(qkv), conv_prev.transpose(1, 2, 0), minor(z), minor(ba[..., :g_heads]),
                             minor(ba[..., g_heads:2 * g_heads]), s0.transpose(1, 2, 3, 0), lp['gdn_conv_w'].T,
                             w['aparams'][:, g_heads:2 * g_heads], lp['gdn_norm'])
    gdn_out = o_t.transpose(2, 0, 1).reshape(1, bs * ts, -1)
    ctx = attend(outs)
    x = _outproj(x, mods, gdn_out, ctx, w['wuv_bd'], w['wo_g'], w['wo_m'], tokens_per_seq=ts)
    x = _ffn(x, mods, lp['norm_ffn2'], *w['f2'], mod0=6, tokens_per_seq=ts)
    nconv = jnp.concatenate([conv_prev, seq(qkv)], axis=1)[:, -(GDN_CONV - 1):]
    return x, seq(c), seq(krot)[..., dn:dn + dr], nconv, s_new.transpose(3, 0, 1, 2)
```

```python
import functools
import math

import jax
import jax.numpy as jnp
from jax import lax
from jax.experimental import pallas as pl
from jax.experimental.pallas import tpu as pltpu

F32 = jnp.float32
BF16 = jnp.bfloat16

NORM_EPS = 1e-6
ROPE_THETA = 10000.0
PAGE_SIZE = 128
GDN_CONV = 4
GDN_CHUNK = 128
N_MOD = 9
LANES = 128
SUBLANES = 8
NEG_BIG = -1e30
VMEM_LIMIT = 56 * 1024 * 1024

TOKEN_TILE = 512
ATTN_TILE = 512
GDN_TILE = 512
SUB_PAGES = 16
AHEAD = 4


def _dot(a, b):
    return jnp.dot(a, b, preferred_element_type=F32)


def _dot_t(a, b):
    return lax.dot_general(a, b, (((1,), (1,)), ((), ())), preferred_element_type=F32)


def _dot_ta(a, b):
    return lax.dot_general(a, b, (((0,), (0,)), ((), ())), preferred_element_type=F32)


def _bmm(a, b):
    return lax.dot_general(a, b, (((2,), (1,)), ((0,), (0,))), preferred_element_type=F32)


def _bmm_t(a, b):
    return lax.dot_general(a, b, (((2,), (2,)), ((0,), (0,))), preferred_element_type=F32)


def _sigmoid(x):
    return 1.0 / (1.0 + jnp.exp(-x))


def _silu(x):
    return x * _sigmoid(x)


def _params(*sem):
    return pltpu.CompilerParams(dimension_semantics=sem, vmem_limit_bytes=VMEM_LIMIT)


def _resident(shape):
    nd = len(shape)
    return pl.BlockSpec(shape, lambda *_: (0,) * nd, pipeline_mode=pl.Buffered(1))


MODS_PER_SUBLAYER = 3


def _dot_split(x, e, terms, left=False):
    out = None
    for _ in range(terms):
        xb = x.astype(BF16)
        d = _dot(e, xb) if left else _dot(xb, e)
        out = d if out is None else out + d
        x = x - xb.astype(F32)
    return out


def _mod_rows(m_ref, ks, tokens_per_seq, rows):
    if tokens_per_seq is None:
        return [m_ref[k:k + 1, :] for k in ks]
    nseq = m_ref.shape[1]
    row = lax.broadcasted_iota(jnp.int32, (rows, nseq), 0)
    first = lax.broadcasted_iota(jnp.int32, (rows, nseq), 1) * tokens_per_seq
    spread = jnp.where((row >= first) & (row < first + tokens_per_seq), 1.0, 0.0).astype(BF16)
    return [_dot_split(m_ref[k % MODS_PER_SUBLAYER], spread, 3, left=True) for k in ks]


def _mod_norm(x, gain, shift, scale):
    y = x * lax.rsqrt(jnp.mean(x * x, axis=-1, keepdims=True) + NORM_EPS)
    return (y * gain) * (1.0 + scale) + shift


def _mods_spec(tokens_per_seq, tm, d, sublayer):
    if tokens_per_seq is not None:
        assert tm % tokens_per_seq == 0
        return pl.BlockSpec((MODS_PER_SUBLAYER, tm // tokens_per_seq, d), lambda b, i: (sublayer, i, 0))
    return pl.BlockSpec((None, N_MOD, d), lambda b, i: (b, 0, 0))


def _mods_body(c_ref, w_ref, b_ref, o_ref):
    c = c_ref[...]
    o_ref[...] = _dot(_silu(c).astype(BF16), w_ref[...].astype(BF16)) + b_ref[...]


def _mods(cond, ada_w, ada_b):
    rows, d = cond.shape
    n = ada_w.shape[1]
    tn = d
    return pl.pallas_call(
        _mods_body,
        grid=(n // tn,),
        in_specs=[pl.BlockSpec((rows, d), lambda j: (0, 0)),
                  pl.BlockSpec((d, tn), lambda j: (0, j)),
                  pl.BlockSpec((1, tn), lambda j: (0, j))],
        out_specs=pl.BlockSpec((rows, tn), lambda j: (0, j)),
        out_shape=jax.ShapeDtypeStruct((rows, n), F32),
        compiler_params=_params("arbitrary"),
        name="mods",
    )(cond, ada_w, ada_b.reshape(1, n))


def _ffn_body(x_ref, m_ref, g_ref, wi_ref, wo_ref, *rest, mod0, tokens_per_seq, fc, mixed):
    x = x_ref[...]
    if mixed:
        mm_ref, gdn_ref, ctx_ref, wuv_ref, wog_ref, wom_ref, o_ref = rest
        gate_k = mod0 - 1
        (gate_mix,) = _mod_rows(mm_ref if tokens_per_seq is not None else m_ref, [gate_k], tokens_per_seq, x.shape[0])
        mla = _dot(ctx_ref[...], wuv_ref[...])
        mix = _dot(gdn_ref[...].astype(BF16), wog_ref[...]) + _dot(mla.astype(BF16), wom_ref[...])
        x = x + gate_mix * mix
    else:
        (o_ref,) = rest
    shift, scale, gate = _mod_rows(m_ref, [mod0, mod0 + 1, mod0 + 2], tokens_per_seq, x.shape[0])
    h = _mod_norm(x, g_ref[...], shift, scale).astype(BF16)
    acc = jnp.zeros(x.shape, F32)
    dff = wo_ref.shape[0]
    for c in range(dff // fc):
        lo, hi = c * fc, (c + 1) * fc
        a = _silu(_dot(h, wi_ref[:, lo:hi])) * _dot(h, wi_ref[:, dff + lo:dff + hi])
        acc = acc + _dot(a.astype(BF16), wo_ref[lo:hi, :])
    o_ref[...] = x + 0.5 * gate * acc


def _ffn(x, mods, gain, wi, wo, *, mod0, tokens_per_seq, mix=None):
    nb, t, d = x.shape
    tm = min(TOKEN_TILE, t)
    dff = wo.shape[0]
    fc = 2 * LANES
    assert t % tm == 0 and dff % fc == 0
    body = functools.partial(_ffn_body, mod0=mod0, tokens_per_seq=tokens_per_seq, fc=fc, mixed=mix is not None)
    tok = lambda w: pl.BlockSpec((None, tm, w), lambda b, i: (b, i, 0))
    sub = mod0 // MODS_PER_SUBLAYER
    in_specs = [tok(d), _mods_spec(tokens_per_seq, tm, d, sub),
                _resident((1, d)), _resident((d, 2 * dff)), _resident((dff, d))]
    args = [x, mods, gain.reshape(1, d), wi, wo]
    if mix is not None:
        gdn_out, ctx, wuv_bd, wo_g, wo_m = mix
        in_specs += [_mods_spec(tokens_per_seq, tm, d, sub - 1), tok(gdn_out.shape[-1]), tok(ctx.shape[-1]),
                     _resident(wuv_bd.shape), _resident(wo_g.shape), _resident(wo_m.shape)]
        args += [mods, gdn_out, ctx, wuv_bd, wo_g, wo_m]
    return pl.pallas_call(
        body,
        grid=(nb, t // tm),
        in_specs=in_specs,
        out_specs=tok(d),
        out_shape=jax.ShapeDtypeStruct(x.shape, F32),
        compiler_params=_params("arbitrary", "arbitrary"),
        name="ffn_mix" if mix is not None else "ffn",
    )(*args)


class _InLayout:
    def __init__(self, conv_dim, v_dim, heads):
        self.qkv = (0, conv_dim)
        self.z = (conv_dim, conv_dim + v_dim)
        self.ba = (self.z[1], self.z[1] + LANES)
        self.q = (self.ba[1], self.ba[1] + heads * LANES)
        self.ckv = (self.q[1], self.q[1] + LANES)
        self.kr = (self.ckv[1], self.ckv[1] + LANES)
        self.krs = (self.kr[1], self.kr[1] + LANES)
        self.total = self.krs[1]


def _inproj_body(x_ref, m_ref, g_ref, w_ref, cs_ref, place_ref, tbias_ref, gq_ref, mq_ref, gc_ref, wuk_ref, mk_ref,
                 gk_ref, gkr_ref, *rest, lay, heads, tokens_per_seq, sample, qk_dim):
    if sample:
        wqa_ref, rest = rest[0], rest[1:]
    else:
        cprev_ref, cw_ref, ms_ref, rest = rest[0], rest[1], rest[2], rest[3:]
    qkv_ref, z_ref, ba_ref, q128_ref, k128_ref, c_ref, krot_ref = rest[:7]
    x = x_ref[...]
    shift, scale = _mod_rows(m_ref, [3, 4], tokens_per_seq, x.shape[0])
    h = _mod_norm(x, g_ref[...], shift, scale).astype(BF16)
    cw = 4 * LANES
    assert lay.total % cw == 0 and lay.qkv[1] % cw == 0 and (lay.z[1] - lay.z[0]) == cw
    n_chunks = lay.total // cw
    n_qkv = lay.qkv[1] // cw
    rest_chunks = list(range(n_qkv, n_chunks))
    order = []
    for c in range(max(n_qkv, len(rest_chunks))):
        order += ([c] if c < n_qkv else []) + ([rest_chunks[c]] if c < len(rest_chunks) else [])
    chunks = {}

    def cols(lo, hi):
        c = lo // cw
        assert (hi - 1) // cw == c
        return chunks[c][:, lo - c * cw:hi - c * cw]

    tm = x.shape[0]
    hist = SUBLANES
    taps = GDN_CONV - 1
    if not sample:
        ct_ref, nconv_ref, xbuf = rest[7], rest[8], rest[9]

        @pl.when(pl.program_id(1) == 0)
        def _():
            xbuf[0:hist, :] = cprev_ref[...]

    tab = _dot_split(cs_ref[...], place_ref[...], 3) + tbias_ref[...]
    tab_q, tab_c, tab_s = tab[:, :LANES], tab[:, LANES:2 * LANES], tab[:, 2 * LANES:]

    def group(lo):
        hi = lo + LANES
        if lo < lay.qkv[1]:
            if sample:
                qkv_ref[:, lo:hi] = cols(lo, hi)
                return
            xbuf[hist:hist + tm, lo:hi] = cols(lo, hi)
            conv = xbuf[hist - taps:hist - taps + tm, lo:hi] * cw_ref[0:1, lo:hi]
            for jj in range(1, GDN_CONV):
                conv = conv + xbuf[hist - taps + jj:hist - taps + jj + tm, lo:hi] * cw_ref[jj:jj + 1, lo:hi]
            yi = _silu(conv)
            if lo < 2 * qk_dim:
                ss = _dot((yi * yi).astype(BF16), ms_ref[...])
                yi = yi * (lax.rsqrt(ss + NORM_EPS) * ((LANES // 2) ** -0.5 if lo < qk_dim else 1.0))
            qkv_ref[:, lo:hi] = yi
            tail = xbuf[tm:tm + hist, lo:hi]
            nconv_ref[:, lo:hi] = tail
            xbuf[0:hist, lo:hi] = tail
        elif lo < lay.z[1]:
            z_ref[:, lo - lay.z[0]:hi - lay.z[0]] = cols(lo, hi)
        elif lo < lay.ba[1]:
            ba_ref[...] = cols(lo, hi)
        elif lo < lay.q[1]:
            hh = (lo - lay.q[0]) // LANES
            qh = cols(lo, hi)
            msq = _dot((qh * qh).astype(BF16), mq_ref[...])
            qn = qh * lax.rsqrt(msq + NORM_EPS) * gq_ref[...] * tab_q
            q128_ref[hh] = qn.astype(BF16)
            if sample:
                qabs_ref, qrope_ref = rest[7], rest[8]
                qa = _dot(qn.astype(BF16), wqa_ref[hh])
                qabs_ref[hh] = qa[:, :LANES].astype(BF16)
                qrope_ref[hh] = qa[:, LANES:2 * LANES].astype(BF16)
        elif lo == lay.ckv[0]:
            keys()

    def keys():
        ckv = cols(*lay.ckv)
        c = ckv * lax.rsqrt(jnp.mean(ckv * ckv, axis=-1, keepdims=True) + NORM_EPS) * gc_ref[...]
        c_ref[...] = c
        cb = c.astype(BF16)
        if not sample:
            ct_ref[...] = jnp.concatenate([c.T, jnp.ones((SUBLANES, c.shape[0]), F32)], axis=0).astype(BF16)
        kr = cols(*lay.kr)
        krs = cols(*lay.krs)
        inv = lax.rsqrt(jnp.sum(kr * kr, axis=-1, keepdims=True) * (2.0 / LANES) + NORM_EPS)
        krot = kr * inv * gkr_ref[0:1, :] * tab_c + krs * inv * gkr_ref[1:2, :] * tab_s
        krot_ref[...] = krot
        knr = _dot(cb, wuk_ref[...])
        for hh in range(heads):
            kh = knr[:, hh * LANES:(hh + 1) * LANES]
            msq = _dot((kh * kh).astype(BF16), mk_ref[...])
            k128_ref[hh] = (kh * lax.rsqrt(msq + NORM_EPS) * gk_ref[...] + krot).astype(BF16)

    ahead = 2
    for i in range(len(order) + ahead):
        if i < len(order):
            k = order[i]
            chunks[k] = _dot(h, w_ref[:, k * cw:(k + 1) * cw])
        if i >= ahead:
            k = order[i - ahead]
            for lo in range(k * cw, (k + 1) * cw, LANES):
                group(lo)


def _inproj(x, mods, gain, w_all, cs, place, tab_bias, gq, mq, gc, wuk, mk, gk, gkr, wqa, conv, *, lay, heads,
            tokens_per_seq):
    nb, t, d = x.shape
    tm = min(TOKEN_TILE, t)
    assert t % tm == 0
    sample = wqa is not None
    body = functools.partial(_inproj_body, lay=lay, heads=heads, tokens_per_seq=tokens_per_seq, sample=sample,
                             qk_dim=(lay.qkv[1] - (lay.z[1] - lay.z[0])) // 2)
    tok = lambda w: pl.BlockSpec((None, tm, w), lambda b, i: (b, i, 0))
    hd = lambda w: pl.BlockSpec((None, heads, tm, w), lambda b, i: (b, 0, i, 0))
    conv_dim, v_dim = lay.qkv[1], lay.z[1] - lay.z[0]
    in_specs = [tok(d), _mods_spec(tokens_per_seq, tm, d, 1), _resident((1, d)), _resident(w_all.shape),
                pl.BlockSpec((tm, cs.shape[1]), lambda b, i: (i, 0)), _resident(place.shape),
                _resident(tab_bias.shape),
                _resident((1, LANES)), _resident((LANES, LANES)), _resident((1, LANES)),
                _resident(wuk.shape), _resident((LANES, LANES)), _resident((1, LANES)), _resident((2, LANES))]
    args = [x, mods, gain.reshape(1, d), w_all, cs, place, tab_bias, gq, mq, gc, wuk, mk, gk, gkr]
    out_specs = [tok(conv_dim), tok(v_dim), tok(LANES), hd(LANES), hd(LANES), tok(LANES), tok(LANES)]
    out_shape = [jax.ShapeDtypeStruct((nb, t, conv_dim), F32), jax.ShapeDtypeStruct((nb, t, v_dim), F32),
                 jax.ShapeDtypeStruct((nb, t, LANES), F32),
                 jax.ShapeDtypeStruct((nb, heads, t, LANES), BF16), jax.ShapeDtypeStruct((nb, heads, t, LANES), BF16),
                 jax.ShapeDtypeStruct((nb, t, LANES), F32), jax.ShapeDtypeStruct((nb, t, LANES), F32)]
    if sample:
        in_specs.append(_resident(wqa.shape))
        args.append(wqa)
        out_specs += [hd(LANES), hd(LANES)]
        out_shape += [jax.ShapeDtypeStruct((nb, heads, t, LANES), BF16)] * 2
        scratch = []
    else:
        conv_prev, conv_w, ms = conv
        in_specs += [pl.BlockSpec((None, SUBLANES, conv_dim), lambda b, i: (b, 0, 0)),
                     _resident(conv_w.shape), _resident(ms.shape)]
        args += [conv_prev, conv_w, ms]
        out_specs += [pl.BlockSpec((None, LANES + SUBLANES, tm), lambda b, i: (b, 0, i)),
                      pl.BlockSpec((None, SUBLANES, conv_dim), lambda b, i: (b, 0, 0))]
        out_shape += [jax.ShapeDtypeStruct((nb, LANES + SUBLANES, t), BF16),
                      jax.ShapeDtypeStruct((nb, SUBLANES, conv_dim), F32)]
        scratch = [pltpu.VMEM((tm + SUBLANES, conv_dim), F32)]
    return pl.pallas_call(
        body, grid=(nb, t // tm), in_specs=in_specs, out_specs=out_specs, out_shape=out_shape,
        scratch_shapes=scratch, compiler_params=_params("arbitrary", "arbitrary"), name="inproj",
    )(*args)


def _unit_lower_inverse(a):
    n = a.shape[-1]
    assert n & (n - 1) == 0
    row = lax.broadcasted_iota(jnp.int32, (n, n), 0)
    col = lax.broadcasted_iota(jnp.int32, (n, n), 1)
    x = jnp.broadcast_to(jnp.where(row == col, 1.0, 0.0), a.shape)
    b = 1
    while b < n:
        lo_mask = (jnp.bitwise_xor(row, col) < 2 * b) & (jnp.bitwise_and(row, b) != 0) & (jnp.bitwise_and(col, b) == 0)
        lo = jnp.where(lo_mask, a, 0.0)
        if b == 1:
            x = x - lo
        else:
            xb = x.astype(BF16)
            x = x - _bmm(xb, _bmm(lo.astype(BF16), xb).astype(BF16))
        b *= 2
    return x


def _gdn_body(qkv_ref, z_ref, ba_ref, s0_ref, ap_ref, ng_ref, o_ref, sout_ref, s_scr,
              *, tt, t_valid, chunk, heads, dk, dv):
    j = pl.program_id(1)

    @pl.when(j == 0)
    def _():
        s_scr[...] = s0_ref[...]

    qk_dim = heads * dk
    y = qkv_ref[...]
    q = y[:, :qk_dim]
    k = y[:, qk_dim:2 * qk_dim]
    v = y[:, 2 * qk_dim:]
    ba = ba_ref[...]
    beta = _sigmoid(ba)
    xg = ba + ap_ref[1:2, :]
    g = -ap_ref[0:1, :] * (jnp.maximum(xg, 0.0) + jnp.log1p(jnp.exp(-jnp.abs(xg))))
    if t_valid < tt:
        keep = lax.broadcasted_iota(jnp.int32, (tt, 1), 0) < t_valid
        q, k, v = (jnp.where(keep, a, 0.0) for a in (q, k, v))
        beta, g = jnp.where(keep, beta, 0.0), jnp.where(keep, g, 0.0)
    row = lax.broadcasted_iota(jnp.int32, (tt, tt), 0)
    col = lax.broadcasted_iota(jnp.int32, (tt, tt), 1)
    assert chunk & (chunk - 1) == 0
    tri = jnp.where((row >= col) & (jnp.bitwise_xor(row, col) < chunk), 1.0, 0.0).astype(BF16)
    g_hi = g.astype(BF16)
    g_lo = (g - g_hi.astype(F32)).astype(BF16)
    gc = _dot(tri, g_hi) + _dot(tri, g_lo)
    gct = gc.T
    z = z_ref[...]

    crow = lax.broadcasted_iota(jnp.int32, (chunk, chunk), 0)
    ccol = lax.broadcasted_iota(jnp.int32, (chunk, chunk), 1)
    causal = crow >= ccol
    strict = crow > ccol
    stack = lambda xs: jnp.stack(xs, axis=0)

    s3 = s_scr[...]
    for c in range(tt // chunk):
        r0, r1 = c * chunk, (c + 1) * chunk
        qkb, kbf, vbeta, kbeg, qdec, kdec, decay, gl, zc = ([] for _ in range(9))
        for h in range(heads):
            qh = q[r0:r1, h * dk:(h + 1) * dk]
            kh = k[r0:r1, h * dk:(h + 1) * dk]
            vh = v[r0:r1, h * dv:(h + 1) * dv]
            bh = beta[r0:r1, h:h + 1]
            gcol = gc[r0:r1, heads + h:heads + h + 1]
            grow = gct[heads + h:heads + h + 1, r0:r1]
            glast = grow[:, chunk - 1:chunk]
            eg = jnp.exp(gcol)
            kb = kh * bh
            qkb.append(jnp.concatenate([qh, kb], axis=0).astype(BF16))
            kbf.append(kh.astype(BF16))
            vbeta.append((vh * bh).astype(BF16))
            kbeg.append((kb * eg).astype(BF16))
            qdec.append((qh * eg).astype(BF16))
            kdec.append((kh * jnp.exp(glast - gcol)).astype(BF16))
            decay.append(jnp.exp(jnp.where(causal, jnp.broadcast_to(gcol, (chunk, chunk)) - grow, NEG_BIG)))
            gl.append(jnp.exp(glast))
            zc.append(z[r0:r1, h * dv:(h + 1) * dv])
        decay3 = stack(decay)
        sc = _bmm_t(stack(qkb), stack(kbf))
        qkm = (sc[:, :chunk] * decay3).astype(BF16)
        m = jnp.where(strict, sc[:, chunk:] * decay3, 0.0)
        tinv = _unit_lower_inverse(m).astype(BF16)
        u = _bmm(tinv, stack(vbeta))
        w = _bmm(tinv, stack(kbeg))
        sb = s3.astype(BF16)
        ws_qs = _bmm(jnp.concatenate([w.astype(BF16), stack(qdec)], axis=1), sb)
        vnb = (u - ws_qs[:, :chunk]).astype(BF16)
        o = ws_qs[:, chunk:] + _bmm(qkm, vnb)
        kd3 = stack(kdec)
        upd = stack([_dot_ta(kd3[h], vnb[h]) for h in range(heads)])
        s3 = s3 * stack(gl) + upd
        on = o * lax.rsqrt(jnp.mean(o * o, axis=-1, keepdims=True) + NORM_EPS) * ng_ref[...] * _silu(stack(zc))
        for h in range(heads):
            o_ref[r0:r1, h * dv:(h + 1) * dv] = on[h]
    s_scr[...] = s3
    sout_ref[...] = s3


def _gdn(qkv, z, ba, s0, aparams, norm_g, *, t_valid, chunk):
    nb, t, conv_dim = qkv.shape
    heads, dk, dv = s0.shape[1:]
    tt = min(GDN_TILE, t)
    assert t % tt == 0 and tt % chunk == 0 and (t_valid == tt or t == tt)
    body = functools.partial(_gdn_body, tt=tt, t_valid=t_valid, chunk=chunk, heads=heads, dk=dk, dv=dv)
    v_dim = heads * dv
    tok = lambda w: pl.BlockSpec((None, tt, w), lambda b, i: (b, i, 0))
    return pl.pallas_call(
        body,
        grid=(nb, t // tt),
        in_specs=[tok(conv_dim), tok(v_dim), tok(LANES),
                  pl.BlockSpec((None, heads, dk, dv), lambda b, i: (b, 0, 0, 0)),
                  _resident((2, LANES)), _resident((1, dv))],
        out_specs=[tok(v_dim), pl.BlockSpec((None, heads, dk, dv), lambda b, i: (b, 0, 0, 0))],
        out_shape=[jax.ShapeDtypeStruct((nb, t, v_dim), F32), jax.ShapeDtypeStruct((nb, heads, dk, dv), F32)],
        scratch_shapes=[pltpu.VMEM((heads, dk, dv), F32)],
        compiler_params=_params("arbitrary", "arbitrary"),
        name="gdn",
    )(qkv, z, ba, s0, aparams, norm_g.reshape(1, dv))


def _gdn_sample_body(xq_ref, xk_ref, xv_ref, pq_ref, pk_ref, pv_ref, wq_ref, wk_ref, wv_ref, z_ref, b_ref, a_ref,
                     ap_ref, ng_ref, s_ref, o_ref, sout_ref, kq_scr, *, ts, heads, dk, dv):
    h = pl.program_id(0)
    nb = xq_ref.shape[-1]

    def conv(x_ref, p_ref, w_ref):
        taps = GDN_CONV - 1
        xin = [p_ref[i] for i in range(taps)] + [x_ref[t] for t in range(ts)]
        w = [jnp.broadcast_to(w_ref[:, jj:jj + 1], xin[0].shape) for jj in range(GDN_CONV)]
        out = []
        for t in range(ts):
            acc = xin[t] * w[0]
            for jj in range(1, GDN_CONV):
                acc = acc + xin[t + jj] * w[jj]
            out.append(_silu(acc))
        return out

    def l2(x):
        return x * lax.rsqrt(jnp.sum(x * x, axis=0, keepdims=True) + NORM_EPS)

    q = [l2(x) * (dk ** -0.5) for x in conv(xq_ref, pq_ref, wq_ref)]
    k = [l2(x) for x in conv(xk_ref, pk_ref, wk_ref)]
    v = conv(xv_ref, pv_ref, wv_ref)
    for t in range(ts):
        kq_scr[t] = k[t]
        kq_scr[ts + t] = q[t]
    a_h = ap_ref[0, h]
    dt_h = ap_ref[1, h]
    beta, decay = [], []
    for t in range(ts):
        beta.append(_sigmoid(b_ref[t, pl.ds(h, 1), :]))
        xg = a_ref[t, pl.ds(h, 1), :] + dt_h
        decay.append(jnp.exp(-a_h * (jnp.maximum(xg, 0.0) + jnp.log1p(jnp.exp(-jnp.abs(xg))))))

    def row(i, kk):
        return jnp.broadcast_to(kq_scr[i, pl.ds(kk, 1), :], (dv, nb))

    def first(kk, ks):
        return ks + row(0, kk) * s_ref[kk]

    ks = lax.fori_loop(0, dk, first, jnp.zeros((dv, nb), F32), unroll=8)
    ng = jnp.broadcast_to(ng_ref[...], (dv, nb))
    for t in range(ts):
        d = beta[t] * (v[t] - decay[t] * ks)
        src = s_ref if t == 0 else sout_ref
        last = t == ts - 1

        def step(kk, carry, t=t, d=d, src=src, last=last):
            o_acc, ks_next = carry
            s_new = decay[t] * src[kk] + row(t, kk) * d
            sout_ref[kk] = s_new
            o_acc = o_acc + row(ts + t, kk) * s_new
            if not last:
                ks_next = ks_next + row(t + 1, kk) * s_new
            return o_acc, ks_next

        zero = jnp.zeros((dv, nb), F32)
        o, ks = lax.fori_loop(0, dk, step, (zero, zero), unroll=8)
        on = o * lax.rsqrt(jnp.mean(o * o, axis=0, keepdims=True) + NORM_EPS) * ng
        o_ref[t] = on * _silu(z_ref[t])


def _gdn_sample(x_t, prev_t, z_t, b_t, a_t, s_t, conv_w_t, aparams, norm_g):
    ts, conv_dim, nb = x_t.shape
    heads, dk, dv, _ = s_t.shape
    assert dk == dv and conv_dim == 3 * heads * dk
    taps = GDN_CONV - 1
    body = functools.partial(_gdn_sample_body, ts=ts, heads=heads, dk=dk, dv=dv)
    part = lambda rows, off: pl.BlockSpec((rows, dk, nb), lambda h: (0, off + h, 0))
    wpart = lambda off: pl.BlockSpec((dk, GDN_CONV), lambda h: (off + h, 0))
    whole = lambda a: pl.BlockSpec(a.shape, lambda h: (0,) * a.ndim)
    state = pl.BlockSpec((None, dk, dv, nb), lambda h: (h, 0, 0, 0))
    return pl.pallas_call(
        body,
        grid=(heads,),
        in_specs=[part(ts, 0), part(ts, heads), part(ts, 2 * heads),
                  part(taps, 0), part(taps, heads), part(taps, 2 * heads),
                  wpart(0), wpart(heads), wpart(2 * heads),
                  part(ts, 0), whole(b_t), whole(a_t),
                  pl.BlockSpec(memory_space=pltpu.SMEM), pl.BlockSpec((dv, 1), lambda h: (0, 0)), state],
        out_specs=[part(ts, 0), state],
        out_shape=[jax.ShapeDtypeStruct((ts, heads * dv, nb), F32), jax.ShapeDtypeStruct(s_t.shape, F32)],
        scratch_shapes=[pltpu.VMEM((2 * ts, dk, nb), F32)],
        compiler_params=_params("arbitrary"),
        name="gdn_sample",
    )(x_t, x_t, x_t, prev_t, prev_t, prev_t, conv_w_t, conv_w_t, conv_w_t, z_t, b_t, a_t, aparams,
      norm_g.reshape(dv, 1), s_t)


def _attn_body(qi_ref, kj_ref, q_ref, k_ref, ct_ref, o_ref, m_scr, acc_scr, *, heads, tq, rank):
    i = qi_ref[pl.program_id(1)]
    j = kj_ref[pl.program_id(1)]

    @pl.when(j == 0)
    def _():
        m_scr[...] = jnp.full(m_scr.shape, NEG_BIG, F32)
        acc_scr[...] = jnp.zeros(acc_scr.shape, F32)

    def step(masked):
        ct = ct_ref[...]
        m_old = [m_scr[h] for h in range(heads)]
        acc_old = [acc_scr[h] for h in range(heads)]
        if masked:
            key = lax.broadcasted_iota(jnp.int32, (tq, tq), 0)
            qry = lax.broadcasted_iota(jnp.int32, (tq, tq), 1)
            keep = key <= qry
        m_out, acc_out = [], []
        ahead = 3
        scores = [_dot_t(k_ref[h], q_ref[h]) for h in range(ahead)]
        for h in range(heads):
            if h + ahead < heads:
                scores.append(_dot_t(k_ref[h + ahead], q_ref[h + ahead]))
            st = scores[h]
            if masked:
                st = jnp.where(keep, st, NEG_BIG)
            m_new = jnp.maximum(m_old[h], jnp.max(st, axis=0, keepdims=True))
            alpha = jnp.exp(m_old[h] - m_new)
            pt = jnp.exp(st - m_new).astype(BF16)
            acc_out.append(acc_old[h] * alpha + _dot(ct, pt))
            m_out.append(m_new)
        for h in range(heads):
            m_scr[h] = m_out[h]
            acc_scr[h] = acc_out[h]
        return acc_out

    @pl.when(j < i)
    def _():
        step(False)

    @pl.when(j == i)
    def _():
        acc = step(True)
        for h in range(heads):
            ctx = acc[h][:rank, :] / acc[h][rank:rank + 1, :]
            o_ref[:, h * rank:(h + 1) * rank] = ctx.T.astype(BF16)


def _attn_prompt(q128, k128, ct_ext):
    nb, heads, t, _ = q128.shape
    rows = ct_ext.shape[1]
    rank = rows - SUBLANES
    tq = min(ATTN_TILE, t)
    assert t % tq == 0
    n = t // tq
    body = functools.partial(_attn_body, heads=heads, tq=tq, rank=rank)
    pairs = [(i, j) for i in range(n) for j in range(i + 1)]
    qi = jnp.asarray([p[0] for p in pairs], jnp.int32)
    kj = jnp.asarray([p[1] for p in pairs], jnp.int32)
    grid_spec = pltpu.PrefetchScalarGridSpec(
        num_scalar_prefetch=2,
        grid=(nb, len(pairs)),
        in_specs=[pl.BlockSpec((None, heads, tq, LANES), lambda b, s, qi, kj: (b, 0, qi[s], 0)),
                  pl.BlockSpec((None, heads, tq, LANES), lambda b, s, qi, kj: (b, 0, kj[s], 0)),
                  pl.BlockSpec((None, rows, tq), lambda b, s, qi, kj: (b, 0, kj[s]))],
        out_specs=pl.BlockSpec((None, tq, heads * rank), lambda b, s, qi, kj: (b, qi[s], 0)),
        scratch_shapes=[pltpu.VMEM((heads, 1, tq), F32), pltpu.VMEM((heads, rows, tq), F32)],
    )
    return pl.pallas_call(
        body, grid_spec=grid_spec,
        out_shape=jax.ShapeDtypeStruct((nb, t, heads * rank), BF16),
        compiler_params=_params("arbitrary", "arbitrary"),
        name="attn_prompt",
    )(qi, kj, q128, k128, ct_ext)


def _attn_sample_body(pt_ref, qabs_ref, qrope_ref, q128_ref, k128n_ref, cn_ref, wukt_ref, cache_c, cache_kr,
                      o_ref, cbuf, kbuf, cb, krb, s_all, sem, *, layer, heads, dn, s_new, n_pages, sub_pages):
    b = pl.program_id(0)
    nseq = pl.num_programs(0)
    slot = lax.rem(b, 2)
    rows = s_new * heads
    n_sub = n_pages // sub_pages
    sub_keys = sub_pages * PAGE_SIZE

    def page_copies(seq, slot_):
        out = []
        for i in range(n_pages):
            page = pt_ref[seq * n_pages + i]
            out.append(pltpu.make_async_copy(cache_c.at[layer, page],
                                             cbuf.at[slot_, pl.ds(i * PAGE_SIZE, PAGE_SIZE)], sem.at[0, slot_]))
            out.append(pltpu.make_async_copy(cache_kr.at[layer, page], kbuf.at[slot_, i], sem.at[1, slot_]))
        return out

    @pl.when(b == 0)
    def _():
        for cp in page_copies(0, 0):
            cp.start()

    for cp in page_copies(b, slot):
        cp.wait()

    @pl.when(b + 1 < nseq)
    def _():
        for cp in page_copies(b + 1, 1 - slot):
            cp.start()

    lhs = jnp.concatenate([wukt_ref[...], qabs_ref[...]], axis=0)
    nk = heads * dn
    qrope = qrope_ref[...]

    def scores(sb):
        for i in range(sub_pages):
            pg = sb * sub_pages + i
            cb[sb, i * PAGE_SIZE:(i + 1) * PAGE_SIZE, :] = cbuf[slot, pg * PAGE_SIZE:(pg + 1) * PAGE_SIZE, :].astype(BF16)
            krb[sb, :, i * PAGE_SIZE:(i + 1) * PAGE_SIZE] = kbuf[slot, pg].astype(BF16)
        kq = _dot_t(lhs, cb[sb])
        ssq = jnp.concatenate([jnp.sum(kq[h * dn:(h + 1) * dn, :] ** 2, axis=0, keepdims=True)
                               for h in range(heads)], axis=0)
        r = lax.rsqrt(ssq * (1.0 / dn) + NORM_EPS)
        s = kq[nk:, :] * jnp.concatenate([r] * s_new, axis=0) + _dot(qrope, krb[sb])
        s_all[sb] = s
        return jnp.max(s, axis=1, keepdims=True)

    state = (jnp.full((rows, 1), NEG_BIG, F32), jnp.zeros((rows, 1), F32), jnp.zeros((rows, LANES), F32))

    def absorb(state, sb, m_sb):
        m_run, l, acc = state
        m_new = jnp.maximum(m_run, m_sb)
        alpha = jnp.exp(m_run - m_new)
        p = jnp.exp(s_all[sb] - m_new)
        return (m_new, alpha * l + jnp.sum(p, axis=1, keepdims=True), alpha * acc + _dot(p.astype(BF16), cb[sb]))

    ahead = AHEAD
    maxes = []
    for sb in range(n_sub):
        maxes.append(scores(sb))
        if sb >= ahead:
            state = absorb(state, sb - ahead, maxes[sb - ahead])
    for sb in range(max(n_sub - ahead, 0), n_sub):
        state = absorb(state, sb, maxes[sb])
    m_run, l, acc = state

    qf = q128_ref[...].astype(F32)
    assert heads & (heads - 1) == 0
    tok = lax.shift_right_logical(lax.broadcasted_iota(jnp.int32, (rows, 1), 0), heads.bit_length() - 1)
    sn = []
    m_fin = m_run
    for t in range(s_new):
        kt = jnp.concatenate([k128n_ref[t].astype(F32)] * s_new, axis=0)
        st = jnp.sum(qf * kt, axis=1, keepdims=True)
        sn.append(jnp.where(tok >= t, st, NEG_BIG))
        m_fin = jnp.maximum(m_fin, sn[-1])
    alpha = jnp.exp(m_run - m_fin)
    l = alpha * l
    acc = alpha * acc
    cn = cn_ref[...].astype(BF16).astype(F32)
    for t in range(s_new):
        pt = jnp.exp(sn[t] - m_fin)
        l = l + pt
        acc = acc + pt.astype(BF16).astype(F32) * cn[t:t + 1, :]
    o_ref[...] = (acc / l).astype(BF16)


def _attn_sample(page_table, qabs, qrope, q128, k128n, c_new, wukt, cache_c, cache_kr, *, layer, heads, dn):
    nseq, rows, _ = qabs.shape
    s_new = rows // heads
    n_pages = page_table.shape[1]
    rank = cache_c.shape[-1]
    rope = cache_kr.shape[-2]
    sub_pages = min(SUB_PAGES, n_pages)
    assert n_pages % sub_pages == 0
    n_sub = n_pages // sub_pages
    sub_keys = sub_pages * PAGE_SIZE
    body = functools.partial(_attn_sample_body, layer=layer, heads=heads, dn=dn, s_new=s_new, n_pages=n_pages,
                             sub_pages=sub_pages)
    per_seq = lambda shape: pl.BlockSpec((None,) + shape, lambda b, pt: (b,) + (0,) * len(shape))
    grid_spec = pltpu.PrefetchScalarGridSpec(
        num_scalar_prefetch=1,
        grid=(nseq,),
        in_specs=[per_seq((rows, LANES)), per_seq((rows, rope)), per_seq((rows, LANES)),
                  per_seq((s_new, heads, LANES)), per_seq((s_new, LANES)),
                  pl.BlockSpec(wukt.shape, lambda b, pt: (0, 0)),
                  pl.BlockSpec(memory_space=pl.ANY), pl.BlockSpec(memory_space=pl.ANY)],
        out_specs=per_seq((rows, LANES)),
        scratch_shapes=[pltpu.VMEM((2, n_pages * PAGE_SIZE, rank), F32),
                        pltpu.VMEM((2, n_pages, rope, PAGE_SIZE), F32),
                        pltpu.VMEM((n_sub, sub_keys, rank), BF16),
                        pltpu.VMEM((n_sub, rope, sub_keys), BF16),
                        pltpu.VMEM((n_sub, rows, sub_keys), F32),
                        pltpu.SemaphoreType.DMA((2, 2))],
    )
    return pl.pallas_call(
        body, grid_spec=grid_spec,
        out_shape=jax.ShapeDtypeStruct((nseq, rows, LANES), BF16),
        compiler_params=_params("arbitrary"),
        name="attn_sample",
    )(page_table.reshape(-1), qabs, qrope, q128, k128n, c_new, wukt, cache_c, cache_kr)


def _rope_cos_sin(pos, half):
    inv = ROPE_THETA ** (-jnp.arange(half, dtype=F32) / half)
    t = pos.shape[0]
    if (t * half) % LANES == 0:
        ang = jnp.repeat(pos, half).reshape(-1, LANES) * jnp.tile(inv, t).reshape(-1, LANES)
        return jnp.concatenate([jnp.cos(ang).reshape(t, half), jnp.sin(ang).reshape(t, half)], axis=1)
    ang = pos[:, None] * inv[None, :]
    return jnp.concatenate([jnp.cos(ang), jnp.sin(ang)], axis=1)


def _rope_placement(half, dn):
    j = jnp.arange(half)
    place = jnp.zeros((2 * half, 3 * LANES), F32)
    dr = 2 * half
    for base, kind in ((dn, 'c'), (dn + dr, 's'), (LANES + dn, 'c'), (LANES + dn + dr, 'c'),
                       (2 * LANES + dn, 's'), (2 * LANES + dn + dr, 's')):
        if kind == 'c':
            place = place.at[j, base + j].set(1.0).at[j, base + half + j].set(1.0)
        else:
            place = place.at[half + j, base + j].set(-1.0).at[half + j, base + half + j].set(1.0)
    bias = jnp.zeros((1, 3 * LANES), F32).at[0, :dn].set(1.0)
    return place.astype(BF16), bias


def _swap_halves(a, axis=-1):
    lo, hi = jnp.split(a, 2, axis=axis)
    return jnp.concatenate([hi, lo], axis=axis)


def _prep_layer(lp, dims):
    heads, dn, dr, rank, g_heads, conv_dim, v_dim = dims
    assert dn + 2 * dr == LANES and rank == LANES and 2 * g_heads <= LANES
    w_in = lp['w_in']
    d = w_in.shape[0]
    o = 0
    w_qkv = w_in[:, o:o + conv_dim]; o += conv_dim
    w_z = w_in[:, o:o + v_dim]; o += v_dim
    w_b = w_in[:, o:o + g_heads]; o += g_heads
    w_a = w_in[:, o:o + g_heads]; o += g_heads
    w_q = w_in[:, o:o + heads * (dn + dr)].reshape(d, heads, dn + dr); o += heads * (dn + dr)
    w_c = w_in[:, o:o + rank]; o += rank
    w_kr = w_in[:, o:o + dr]
    zeros = lambda n: jnp.zeros((d, n), F32)
    w_q128 = jnp.concatenate([w_q, _swap_halves(w_q[:, :, dn:])], axis=2).reshape(d, heads * LANES)
    w_krs = _swap_halves(w_kr)
    w_all = jnp.concatenate([w_qkv, w_z, w_b, w_a, zeros(LANES - 2 * g_heads), w_q128, w_c,
                             zeros(dn), w_kr, w_kr, zeros(dn), w_krs, w_krs], axis=1).astype(BF16)
    scale = (dn + dr) ** -0.5
    qr_g = lp['mla_qr_norm']
    gq = (jnp.concatenate([lp['mla_qn_norm'], qr_g, _swap_halves(qr_g)]) * scale).reshape(1, LANES)
    lane = jnp.arange(LANES)
    seg = jnp.where(lane < dn, 0, jnp.where(lane < dn + dr, 1, 2))
    seg_len = jnp.where(lane < dn, dn, dr).astype(F32)
    mq = jnp.where(seg[:, None] == seg[None, :], 1.0 / seg_len[None, :], 0.0).astype(BF16)
    mk = jnp.where((lane[:, None] < dn) & (lane[None, :] < dn), 1.0 / dn, 0.0).astype(BF16)
    gk = jnp.concatenate([lp['mla_kn_norm'], jnp.zeros((LANES - dn,), F32)]).reshape(1, LANES)
    kr_g = lp['mla_kr_norm']
    zdn = jnp.zeros((dn,), F32)
    gkr = jnp.stack([jnp.concatenate([zdn, kr_g, kr_g]),
                     jnp.concatenate([zdn, _swap_halves(kr_g), _swap_halves(kr_g)])])
    w_uk = lp['mla_w_uk']
    wuk = jnp.concatenate([w_uk, jnp.zeros((rank, heads, LANES - dn), F32)], axis=2)
    wuk = wuk.reshape(rank, heads * LANES).astype(BF16)
    wukt = w_uk.transpose(1, 2, 0).reshape(heads * dn, rank).astype(BF16)
    absorb = jnp.concatenate([w_uk.transpose(1, 2, 0) * lp['mla_kn_norm'][None, :, None],
                              jnp.zeros((heads, LANES - dn, rank), F32)], axis=1)
    fold = jnp.zeros((LANES, LANES), F32)
    fold = fold.at[dn + jnp.arange(dr), jnp.arange(dr)].set(1.0).at[dn + dr + jnp.arange(dr), jnp.arange(dr)].set(1.0)
    wqa = jnp.concatenate([absorb, jnp.broadcast_to(fold, (heads, LANES, LANES))], axis=2).astype(BF16)
    w_uv = lp['mla_w_uv']
    mv = w_uv.shape[2]
    wuv_bd = (w_uv.transpose(1, 0, 2)[:, :, None, :] * jnp.eye(heads, dtype=F32)[:, None, :, None])
    wuv_bd = wuv_bd.reshape(heads * rank, heads * mv).astype(BF16)
    lane_h = lane // (LANES // 2)
    ms = jnp.where(lane_h[:, None] == lane_h[None, :], 1.0, 0.0).astype(BF16)
    aparams = jnp.zeros((2, LANES), F32)
    aparams = aparams.at[0, g_heads:2 * g_heads].set(jnp.exp(lp['gdn_a_log']))
    aparams = aparams.at[1, g_heads:2 * g_heads].set(lp['gdn_dt_bias'])
    return dict(
        w_all=w_all, gq=gq, mq=mq, gc=lp['mla_ckv_norm'].reshape(1, LANES), wuk=wuk, mk=mk, gk=gk, gkr=gkr,
        wukt=wukt, wqa=wqa, wuv_bd=wuv_bd, ms=ms, aparams=aparams,
        wo_g=lp['w_out'][:v_dim].astype(BF16), wo_m=lp['w_out'][v_dim:].astype(BF16),
        f1=(lp['ffn1_wi'].astype(BF16), lp['ffn1_wo'].astype(BF16)),
        f2=(lp['ffn2_wi'].astype(BF16), lp['ffn2_wo'].astype(BF16)),
    )


def _layer(x, mods, rope, conv_prev, s0, lp, w, lay, dims, *, t_valid, chunk, attend):
    heads = dims[0]
    x = _ffn(x, mods, lp['norm_ffn1'], *w['f1'], mod0=0, tokens_per_seq=None)
    outs = _inproj(x, mods, lp['norm_mix'], w['w_all'], *rope, w['gq'], w['mq'], w['gc'], w['wuk'], w['mk'],
                   w['gk'], w['gkr'], None, (conv_prev, lp['gdn_conv_w'], w['ms']), lay=lay, heads=heads,
                   tokens_per_seq=None)
    qkv, z, ba, c, krot, nconv = outs[0], outs[1], outs[2], outs[5], outs[6], outs[8]
    gdn_out, s_new = _gdn(qkv, z, ba, s0, w['aparams'], lp['gdn_norm'], t_valid=t_valid, chunk=chunk)
    ctx = attend(outs)
    x = _ffn(x, mods, lp['norm_ffn2'], *w['f2'], mod0=6, tokens_per_seq=None,
             mix=(gdn_out, ctx, w['wuv_bd'], w['wo_g'], w['wo_m']))
    return x, c, krot, nconv, s_new


def kernel(x_prompt, x_sample, cache_ckv, cache_krope, state_conv, state_gdn, page_table, c_prompt, c_sample,
           ada_w, ada_b, norm_ffn1, ffn1_wi, ffn1_wo, norm_mix, w_in, gdn_conv_w, gdn_a_log, gdn_dt_bias, gdn_norm,
           mla_qn_norm, mla_qr_norm, mla_ckv_norm, mla_kr_norm, mla_kn_norm, mla_w_uk, mla_w_uv, w_out,
           norm_ffn2, ffn2_wi, ffn2_wo):
    depth = ada_w.shape[0]
    bp, tp, d = x_prompt.shape
    bs, ts, _ = x_sample.shape
    g_heads, dk, dv = state_gdn.shape[2:]
    conv_dim = state_conv.shape[-1]
    v_dim = g_heads * dv
    rank, heads, dn = mla_w_uk.shape[1:]
    dr = mla_qr_norm.shape[1]
    past = page_table.shape[1] * PAGE_SIZE
    dims = (heads, dn, dr, rank, g_heads, conv_dim, v_dim)
    lay = _InLayout(conv_dim, v_dim, heads)

    cs_p = _rope_cos_sin(jnp.arange(tp, dtype=F32), dr // 2)
    cs_s = jnp.tile(_rope_cos_sin(past + jnp.arange(ts, dtype=F32), dr // 2), (bs, 1))
    place, tab_bias = _rope_placement(dr // 2, dn)
    cond = jnp.concatenate([c_prompt, c_sample], axis=0)
    rows = -(-cond.shape[0] // SUBLANES) * SUBLANES
    cond = jnp.pad(cond, ((0, rows - cond.shape[0]), (0, 0)))

    yp = x_prompt
    ys = x_sample.reshape(1, bs * ts, d)
    outs_p, outs_s = [], []
    for l in range(depth):
        lp = dict(norm_ffn1=norm_ffn1[l], ffn1_wi=ffn1_wi[l], ffn1_wo=ffn1_wo[l], norm_mix=norm_mix[l], w_in=w_in[l],
                  gdn_conv_w=gdn_conv_w[l], gdn_a_log=gdn_a_log[l], gdn_dt_bias=gdn_dt_bias[l], gdn_norm=gdn_norm[l],
                  mla_qn_norm=mla_qn_norm[l], mla_qr_norm=mla_qr_norm[l], mla_ckv_norm=mla_ckv_norm[l],
                  mla_kr_norm=mla_kr_norm[l], mla_kn_norm=mla_kn_norm[l], mla_w_uk=mla_w_uk[l],
                  mla_w_uv=mla_w_uv[l], w_out=w_out[l], norm_ffn2=norm_ffn2[l], ffn2_wi=ffn2_wi[l],
                  ffn2_wo=ffn2_wo[l])
        w = _prep_layer(lp, dims)
        mods = _mods(cond, ada_w[l], ada_b[l])
        mods_p = mods[:bp].reshape(bp, N_MOD, d)
        mods_s = mods[bp:bp + bs].reshape(bs, N_MOD, d).transpose(1, 0, 2)

        conv0 = jnp.zeros((bp, SUBLANES, conv_dim), F32)
        s0 = jnp.zeros((bp, g_heads, dk, dv), F32)
        attend_p = lambda o: _attn_prompt(o[3], o[4], o[7])
        yp, c_p, kr_p, cv_p, s_p = _layer(yp, mods_p, (cs_p, place, tab_bias), conv0, s0, lp, w, lay, dims,
                                          t_valid=min(GDN_TILE, tp), chunk=min(GDN_CHUNK, tp), attend=attend_p)
        outs_p.append((c_p, kr_p[..., dn:dn + dr], cv_p[:, SUBLANES - (GDN_CONV - 1):], s_p))


        def attend_s(o, l=l, w=w):
            q128, k128, c_new, qabs, qrope = o[3], o[4], o[5], o[7], o[8]
            by_seq = lambda a: a[0].reshape(heads, bs, ts, -1).transpose(1, 2, 0, 3)
            flat = lambda a: by_seq(a).reshape(bs, ts * heads, -1)
            ctx = _attn_sample(page_table, flat(qabs), flat(qrope)[..., :dr], flat(q128), by_seq(k128),
                               c_new.reshape(bs, ts, -1), w['wukt'], cache_ckv, jnp.swapaxes(cache_krope, 2, 3),
                               layer=l, heads=heads, dn=dn)
            return ctx.reshape(1, bs * ts, heads * rank)

        ys, c_s, kr_s, cv_s, s_s = _layer_sample(ys, mods_s, (cs_s, place, tab_bias), state_conv[l], state_gdn[l], lp, w, lay, dims,
                                                 bs, ts, attend_s)
        outs_s.append((c_s, kr_s, cv_s, s_s))

    stack = lambda outs, k: jnp.stack([o[k] for o in outs])
    ys = ys.reshape(bs, ts, d)
    return (yp, ys, stack(outs_p, 0), stack(outs_p, 1), stack(outs_p, 2), stack(outs_p, 3),
            stack(outs_s, 0), stack(outs_s, 1), stack(outs_s, 2), stack(outs_s, 3))


def _layer_sample(x, mods, rope, conv_prev, s0, lp, w, lay, dims, bs, ts, attend):
    heads, dn, dr, g_heads = dims[0], dims[1], dims[2], dims[4]
    x = _ffn(x, mods, lp['norm_ffn1'], *w['f1'], mod0=0, tokens_per_seq=ts)
    outs = _inproj(x, mods, lp['norm_mix'], w['w_all'], *rope, w['gq'], w['mq'], w['gc'], w['wuk'], w['mk'],
                   w['gk'], w['gkr'], w['wqa'], None, lay=lay, heads=heads, tokens_per_seq=ts)
    qkv, z, ba, c, krot = outs[0], outs[1], outs[2], outs[5], outs[6]
    seq = lambda a: a.reshape(bs, ts, a.shape[-1])
    minor = lambda a: seq(a).transpose(1, 2, 0)
    o_t, s_new = _gdn_sample(minor(qkv), conv_prev.transpose(1, 2, 0), minor(z), minor(ba[..., :g_heads]),
                             minor(ba[..., g_heads:2 * g_heads]), s0.transpose(1, 2, 3, 0), lp['gdn_conv_w'].T,
                             w['aparams'][:, g_heads:2 * g_heads], lp['gdn_norm'])
    gdn_out = o_t.transpose(2, 0, 1).reshape(1, bs * ts, -1)
    ctx = attend(outs)
    x = _ffn(x, mods, lp['norm_ffn2'], *w['f2'], mod0=6, tokens_per_seq=ts,
             mix=(gdn_out, ctx, w['wuv_bd'], w['wo_g'], w['wo_m']))
    nconv = jnp.concatenate([conv_prev, seq(qkv)], axis=1)[:, -(GDN_CONV - 1):]
    return x, seq(c), seq(krot)[..., dn:dn + dr], nconv, s_new.transpose(3, 0, 1, 2)
```

```python
import functools
import math

import jax
import jax.numpy as jnp
import numpy as np
from jax import lax
from jax.experimental import pallas as pl
from jax.experimental.pallas import tpu as pltpu

F32 = jnp.float32
BF16 = jnp.bfloat16

NORM_EPS = 1e-6
ROPE_THETA = 10000.0
PAGE_SIZE = 128
GDN_CONV = 4
GDN_CHUNK = 128
N_MOD = 9
LANES = 128
SUBLANES = 8
NEG_BIG = -1e30
VMEM_LIMIT = 56 * 1024 * 1024

TOKEN_TILE = 512
ATTN_TILE = 512
GDN_TILE = 512
SUB_PAGES = 16
AHEAD = 4


def _dot(a, b):
    return jnp.dot(a, b, preferred_element_type=F32)


def _dot_t(a, b):
    return lax.dot_general(a, b, (((1,), (1,)), ((), ())), preferred_element_type=F32)


def _dot_ta(a, b):
    return lax.dot_general(a, b, (((0,), (0,)), ((), ())), preferred_element_type=F32)


def _bmm(a, b):
    return lax.dot_general(a, b, (((2,), (1,)), ((0,), (0,))), preferred_element_type=F32)


def _bmm_t(a, b):
    return lax.dot_general(a, b, (((2,), (2,)), ((0,), (0,))), preferred_element_type=F32)


def _sigmoid(x):
    return 1.0 / (1.0 + jnp.exp(-x))


def _silu(x):
    return x * _sigmoid(x)


def _params(*sem):
    return pltpu.CompilerParams(dimension_semantics=sem, vmem_limit_bytes=VMEM_LIMIT)


def _resident(shape):
    nd = len(shape)
    return pl.BlockSpec(shape, lambda *_: (0,) * nd, pipeline_mode=pl.Buffered(1))


MODS_PER_SUBLAYER = 3


def _dot_split(x, e, terms, left=False):
    out = None
    for _ in range(terms):
        xb = x.astype(BF16)
        d = _dot(e, xb) if left else _dot(xb, e)
        out = d if out is None else out + d
        x = x - xb.astype(F32)
    return out


def _mod_rows(m_ref, ks, tokens_per_seq, rows):
    if tokens_per_seq is None:
        return [m_ref[k:k + 1, :] for k in ks]
    nseq = m_ref.shape[1]
    row = lax.broadcasted_iota(jnp.int32, (rows, nseq), 0)
    first = lax.broadcasted_iota(jnp.int32, (rows, nseq), 1) * tokens_per_seq
    spread = jnp.where((row >= first) & (row < first + tokens_per_seq), 1.0, 0.0).astype(BF16)
    return [_dot_split(m_ref[k % MODS_PER_SUBLAYER], spread, 3, left=True) for k in ks]


def _mod_norm(x, gain, shift, scale):
    y = x * lax.rsqrt(jnp.mean(x * x, axis=-1, keepdims=True) + NORM_EPS)
    return (y * gain) * (1.0 + scale) + shift


def _mods_spec(tokens_per_seq, tm, d, sublayer):
    if tokens_per_seq is not None:
        assert tm % tokens_per_seq == 0
        return pl.BlockSpec((MODS_PER_SUBLAYER, tm // tokens_per_seq, d), lambda b, i: (sublayer, i, 0))
    return pl.BlockSpec((None, N_MOD, d), lambda b, i: (b, 0, 0))


def _mods_body(c_ref, w_ref, b_ref, o_ref):
    c = c_ref[...]
    o_ref[...] = _dot(_silu(c).astype(BF16), w_ref[...].astype(BF16)) + b_ref[...]


def _mods(cond, ada_w, ada_b):
    rows, d = cond.shape
    n = ada_w.shape[1]
    tn = d
    return pl.pallas_call(
        _mods_body,
        grid=(n // tn,),
        in_specs=[pl.BlockSpec((rows, d), lambda j: (0, 0)),
                  pl.BlockSpec((d, tn), lambda j: (0, j)),
                  pl.BlockSpec((1, tn), lambda j: (0, j))],
        out_specs=pl.BlockSpec((rows, tn), lambda j: (0, j)),
        out_shape=jax.ShapeDtypeStruct((rows, n), F32),
        compiler_params=_params("arbitrary"),
        name="mods",
    )(cond, ada_w, ada_b.reshape(1, n))


def _ffn_body(x_ref, m_ref, g_ref, wi_ref, wo_ref, *rest, mod0, tokens_per_seq, fc, mixed):
    x = x_ref[...]
    if mixed:
        mm_ref, gdn_ref, ctx_ref, wuv_ref, wog_ref, wom_ref, o_ref = rest
        gate_k = mod0 - 1
        (gate_mix,) = _mod_rows(mm_ref if tokens_per_seq is not None else m_ref, [gate_k], tokens_per_seq, x.shape[0])
        mla = _dot(ctx_ref[...], wuv_ref[...])
        mix = _dot(gdn_ref[...].astype(BF16), wog_ref[...]) + _dot(mla.astype(BF16), wom_ref[...])
        x = x + gate_mix * mix
    else:
        (o_ref,) = rest
    shift, scale, gate = _mod_rows(m_ref, [mod0, mod0 + 1, mod0 + 2], tokens_per_seq, x.shape[0])
    h = _mod_norm(x, g_ref[...], shift, scale).astype(BF16)
    acc = jnp.zeros(x.shape, F32)
    dff = wo_ref.shape[0]
    for c in range(dff // fc):
        lo, hi = c * fc, (c + 1) * fc
        a = _silu(_dot(h, wi_ref[:, lo:hi])) * _dot(h, wi_ref[:, dff + lo:dff + hi])
        acc = acc + _dot(a.astype(BF16), wo_ref[lo:hi, :])
    o_ref[...] = x + 0.5 * gate * acc


def _ffn(x, mods, gain, wi, wo, *, mod0, tokens_per_seq, mix=None):
    nb, t, d = x.shape
    tm = min(TOKEN_TILE, t)
    dff = wo.shape[0]
    fc = 2 * LANES
    assert t % tm == 0 and dff % fc == 0
    body = functools.partial(_ffn_body, mod0=mod0, tokens_per_seq=tokens_per_seq, fc=fc, mixed=mix is not None)
    tok = lambda w: pl.BlockSpec((None, tm, w), lambda b, i: (b, i, 0))
    sub = mod0 // MODS_PER_SUBLAYER
    in_specs = [tok(d), _mods_spec(tokens_per_seq, tm, d, sub),
                _resident((1, d)), _resident((d, 2 * dff)), _resident((dff, d))]
    args = [x, mods, gain.reshape(1, d), wi, wo]
    if mix is not None:
        gdn_out, ctx, wuv_bd, wo_g, wo_m = mix
        in_specs += [_mods_spec(tokens_per_seq, tm, d, sub - 1), tok(gdn_out.shape[-1]), tok(ctx.shape[-1]),
                     _resident(wuv_bd.shape), _resident(wo_g.shape), _resident(wo_m.shape)]
        args += [mods, gdn_out, ctx, wuv_bd, wo_g, wo_m]
    return pl.pallas_call(
        body,
        grid=(nb, t // tm),
        in_specs=in_specs,
        out_specs=tok(d),
        out_shape=jax.ShapeDtypeStruct(x.shape, F32),
        compiler_params=_params("arbitrary", "arbitrary"),
        name="ffn_mix" if mix is not None else "ffn",
    )(*args)


class _InLayout:
    def __init__(self, conv_dim, v_dim, heads):
        self.qkv = (0, conv_dim)
        self.z = (conv_dim, conv_dim + v_dim)
        self.ba = (self.z[1], self.z[1] + LANES)
        self.q = (self.ba[1], self.ba[1] + heads * LANES)
        self.ckv = (self.q[1], self.q[1] + LANES)
        self.kr = (self.ckv[1], self.ckv[1] + LANES)
        self.krs = (self.kr[1], self.kr[1] + LANES)
        self.total = self.krs[1]


def _inproj_body(x_ref, m_ref, g_ref, w_ref, cs_ref, place_ref, tbias_ref, gq_ref, mq_ref, gc_ref, wuk_ref, mk_ref,
                 gk_ref, gkr_ref, *rest, lay, heads, tokens_per_seq, sample, qk_dim, rope):
    if sample:
        wqa_ref, rest = rest[0], rest[1:]
    else:
        cprev_ref, cw_ref, ms_ref, rest = rest[0], rest[1], rest[2], rest[3:]
    qkv_ref, z_ref, ba_ref, q128_ref, k128_ref, c_ref, krot_ref = rest[:7]
    x = x_ref[...]
    shift, scale = _mod_rows(m_ref, [3, 4], tokens_per_seq, x.shape[0])
    h = _mod_norm(x, g_ref[...], shift, scale).astype(BF16)
    cw = 4 * LANES
    assert lay.total % cw == 0 and lay.qkv[1] % cw == 0 and (lay.z[1] - lay.z[0]) == cw
    n_chunks = lay.total // cw
    n_qkv = lay.qkv[1] // cw
    rest_chunks = list(range(n_qkv, n_chunks))
    order = []
    for c in range(max(n_qkv, len(rest_chunks))):
        order += ([c] if c < n_qkv else []) + ([rest_chunks[c]] if c < len(rest_chunks) else [])
    chunks = {}

    def cols(lo, hi):
        c = lo // cw
        assert (hi - 1) // cw == c
        return chunks[c][:, lo - c * cw:hi - c * cw]

    tm = x.shape[0]
    hist = SUBLANES
    taps = GDN_CONV - 1
    if not sample:
        ct_ref, nconv_ref, xbuf = rest[7], rest[8], rest[9]

        @pl.when(pl.program_id(1) == 0)
        def _():
            xbuf[0:hist, :] = cprev_ref[...]

    tab = _dot_split(cs_ref[...], place_ref[...], 3) + tbias_ref[...]
    tab_q, tab_c, tab_s = tab[:, :LANES], tab[:, LANES:2 * LANES], tab[:, 2 * LANES:]

    def group(lo):
        hi = lo + LANES
        if lo < lay.qkv[1]:
            if sample:
                qkv_ref[:, lo:hi] = cols(lo, hi)
                return
            xbuf[hist:hist + tm, lo:hi] = cols(lo, hi)
            conv = xbuf[hist - taps:hist - taps + tm, lo:hi] * cw_ref[0:1, lo:hi]
            for jj in range(1, GDN_CONV):
                conv = conv + xbuf[hist - taps + jj:hist - taps + jj + tm, lo:hi] * cw_ref[jj:jj + 1, lo:hi]
            yi = _silu(conv)
            if lo < 2 * qk_dim:
                ss = _dot((yi * yi).astype(BF16), ms_ref[...])
                yi = yi * (lax.rsqrt(ss + NORM_EPS) * ((LANES // 2) ** -0.5 if lo < qk_dim else 1.0))
            qkv_ref[:, lo:hi] = yi
            tail = xbuf[tm:tm + hist, lo:hi]
            nconv_ref[:, lo:hi] = tail
            xbuf[0:hist, lo:hi] = tail
        elif lo < lay.z[1]:
            z_ref[:, lo - lay.z[0]:hi - lay.z[0]] = cols(lo, hi)
        elif lo < lay.ba[1]:
            ba_ref[...] = cols(lo, hi)
        elif lo < lay.q[1]:
            hh = (lo - lay.q[0]) // LANES
            qh = cols(lo, hi)
            msq = _dot((qh * qh).astype(BF16), mq_ref[...])
            qn = qh * lax.rsqrt(msq + NORM_EPS) * gq_ref[...] * tab_q
            q128_ref[hh] = qn.astype(BF16)
            if sample:
                qabs_ref, qrope_ref = rest[7], rest[8]
                qa = _dot(qn.astype(BF16), wqa_ref[hh])
                qabs_ref[hh] = qa[:, :LANES].astype(BF16)
                qrope_ref[hh] = qa[:, LANES:2 * LANES].astype(BF16)
        elif lo == lay.ckv[0]:
            keys()

    def keys():
        ckv = cols(*lay.ckv)
        c = ckv * lax.rsqrt(jnp.mean(ckv * ckv, axis=-1, keepdims=True) + NORM_EPS) * gc_ref[...]
        c_ref[...] = c
        cb = c.astype(BF16)
        if not sample:
            ct_ref[...] = jnp.concatenate([c.T, jnp.ones((SUBLANES, c.shape[0]), F32)], axis=0).astype(BF16)
        kr = cols(*lay.kr)
        krs = cols(*lay.krs)
        inv = lax.rsqrt(jnp.sum(kr * kr, axis=-1, keepdims=True) * (2.0 / LANES) + NORM_EPS)
        krot = kr * inv * gkr_ref[0:1, :] * tab_c + krs * inv * gkr_ref[1:2, :] * tab_s
        if sample:
            krot_ref[...] = krot
        else:
            krot_ref[...] = krot.T[rope[0]:rope[0] + rope[1], :]
        knr = _dot(cb, wuk_ref[...])
        for hh in range(heads):
            kh = knr[:, hh * LANES:(hh + 1) * LANES]
            msq = _dot((kh * kh).astype(BF16), mk_ref[...])
            k128_ref[hh] = (kh * lax.rsqrt(msq + NORM_EPS) * gk_ref[...] + krot).astype(BF16)

    ahead = 2
    for i in range(len(order) + ahead):
        if i < len(order):
            k = order[i]
            chunks[k] = _dot(h, w_ref[:, k * cw:(k + 1) * cw])
        if i >= ahead:
            k = order[i - ahead]
            for lo in range(k * cw, (k + 1) * cw, LANES):
                group(lo)


def _inproj(x, mods, gain, w_all, cs, place, tab_bias, gq, mq, gc, wuk, mk, gk, gkr, wqa, conv, *, lay, heads,
            tokens_per_seq, rope):
    nb, t, d = x.shape
    tm = min(TOKEN_TILE, t)
    assert t % tm == 0
    sample = wqa is not None
    body = functools.partial(_inproj_body, lay=lay, heads=heads, tokens_per_seq=tokens_per_seq, sample=sample,
                             qk_dim=(lay.qkv[1] - (lay.z[1] - lay.z[0])) // 2, rope=rope)
    tok = lambda w: pl.BlockSpec((None, tm, w), lambda b, i: (b, i, 0))
    hd = lambda w: pl.BlockSpec((None, heads, tm, w), lambda b, i: (b, 0, i, 0))
    conv_dim, v_dim = lay.qkv[1], lay.z[1] - lay.z[0]
    in_specs = [tok(d), _mods_spec(tokens_per_seq, tm, d, 1), _resident((1, d)), _resident(w_all.shape),
                pl.BlockSpec((tm, cs.shape[1]), lambda b, i: (i, 0)), _resident(place.shape),
                _resident(tab_bias.shape),
                _resident((1, LANES)), _resident((LANES, LANES)), _resident((1, LANES)),
                _resident(wuk.shape), _resident((LANES, LANES)), _resident((1, LANES)), _resident((2, LANES))]
    args = [x, mods, gain.reshape(1, d), w_all, cs, place, tab_bias, gq, mq, gc, wuk, mk, gk, gkr]
    out_specs = [tok(conv_dim), tok(v_dim), tok(LANES), hd(LANES), hd(LANES), tok(LANES), tok(LANES)]
    out_shape = [jax.ShapeDtypeStruct((nb, t, conv_dim), F32), jax.ShapeDtypeStruct((nb, t, v_dim), F32),
                 jax.ShapeDtypeStruct((nb, t, LANES), F32),
                 jax.ShapeDtypeStruct((nb, heads, t, LANES), BF16), jax.ShapeDtypeStruct((nb, heads, t, LANES), BF16),
                 jax.ShapeDtypeStruct((nb, t, LANES), F32), jax.ShapeDtypeStruct((nb, t, LANES), F32)]
    if not sample:
        out_specs[6] = pl.BlockSpec((None, rope[1], tm), lambda b, i: (b, 0, i))
        out_shape[6] = jax.ShapeDtypeStruct((nb, rope[1], t), F32)
    if sample:
        in_specs.append(_resident(wqa.shape))
        args.append(wqa)
        out_specs += [hd(LANES), hd(LANES)]
        out_shape += [jax.ShapeDtypeStruct((nb, heads, t, LANES), BF16)] * 2
        scratch = []
    else:
        conv_prev, conv_w, ms = conv
        in_specs += [pl.BlockSpec((None, SUBLANES, conv_dim), lambda b, i: (b, 0, 0)),
                     _resident(conv_w.shape), _resident(ms.shape)]
        args += [conv_prev, conv_w, ms]
        out_specs += [pl.BlockSpec((None, LANES + SUBLANES, tm), lambda b, i: (b, 0, i)),
                      pl.BlockSpec((None, SUBLANES, conv_dim), lambda b, i: (b, 0, 0))]
        out_shape += [jax.ShapeDtypeStruct((nb, LANES + SUBLANES, t), BF16),
                      jax.ShapeDtypeStruct((nb, SUBLANES, conv_dim), F32)]
        scratch = [pltpu.VMEM((tm + SUBLANES, conv_dim), F32)]
    return pl.pallas_call(
        body, grid=(nb, t // tm), in_specs=in_specs, out_specs=out_specs, out_shape=out_shape,
        scratch_shapes=scratch, compiler_params=_params("arbitrary", "arbitrary"), name="inproj",
    )(*args)


def _unit_lower_inverse(a):
    n = a.shape[-1]
    assert n & (n - 1) == 0
    row = lax.broadcasted_iota(jnp.int32, (n, n), 0)
    col = lax.broadcasted_iota(jnp.int32, (n, n), 1)
    x = jnp.broadcast_to(jnp.where(row == col, 1.0, 0.0), a.shape)
    b = 1
    while b < n:
        lo_mask = (jnp.bitwise_xor(row, col) < 2 * b) & (jnp.bitwise_and(row, b) != 0) & (jnp.bitwise_and(col, b) == 0)
        lo = jnp.where(lo_mask, a, 0.0)
        if b == 1:
            x = x - lo
        else:
            xb = x.astype(BF16)
            x = x - _bmm(xb, _bmm(lo.astype(BF16), xb).astype(BF16))
        b *= 2
    return x


def _gdn_body(qkv_ref, z_ref, ba_ref, s0_ref, ap_ref, ng_ref, o_ref, sout_ref, s_scr,
              *, tt, t_valid, chunk, heads, dk, dv):
    j = pl.program_id(1)

    @pl.when(j == 0)
    def _():
        s_scr[...] = s0_ref[...]

    qk_dim = heads * dk
    y = qkv_ref[...]
    q = y[:, :qk_dim]
    k = y[:, qk_dim:2 * qk_dim]
    v = y[:, 2 * qk_dim:]
    ba = ba_ref[...]
    beta = _sigmoid(ba)
    xg = ba + ap_ref[1:2, :]
    g = -ap_ref[0:1, :] * (jnp.maximum(xg, 0.0) + jnp.log1p(jnp.exp(-jnp.abs(xg))))
    if t_valid < tt:
        keep = lax.broadcasted_iota(jnp.int32, (tt, 1), 0) < t_valid
        q, k, v = (jnp.where(keep, a, 0.0) for a in (q, k, v))
        beta, g = jnp.where(keep, beta, 0.0), jnp.where(keep, g, 0.0)
    row = lax.broadcasted_iota(jnp.int32, (tt, tt), 0)
    col = lax.broadcasted_iota(jnp.int32, (tt, tt), 1)
    assert chunk & (chunk - 1) == 0
    tri = jnp.where((row >= col) & (jnp.bitwise_xor(row, col) < chunk), 1.0, 0.0).astype(BF16)
    g_hi = g.astype(BF16)
    g_lo = (g - g_hi.astype(F32)).astype(BF16)
    gc = _dot(tri, g_hi) + _dot(tri, g_lo)
    gct = gc.T
    z = z_ref[...]

    crow = lax.broadcasted_iota(jnp.int32, (chunk, chunk), 0)
    ccol = lax.broadcasted_iota(jnp.int32, (chunk, chunk), 1)
    causal = crow >= ccol
    strict = crow > ccol
    stack = lambda xs: jnp.stack(xs, axis=0)

    s3 = s_scr[...]
    for c in range(tt // chunk):
        r0, r1 = c * chunk, (c + 1) * chunk
        qkb, kbf, vbeta, kbeg, qdec, kdec, decay, gl, zc = ([] for _ in range(9))
        for h in range(heads):
            qh = q[r0:r1, h * dk:(h + 1) * dk]
            kh = k[r0:r1, h * dk:(h + 1) * dk]
            vh = v[r0:r1, h * dv:(h + 1) * dv]
            bh = beta[r0:r1, h:h + 1]
            gcol = gc[r0:r1, heads + h:heads + h + 1]
            grow = gct[heads + h:heads + h + 1, r0:r1]
            glast = grow[:, chunk - 1:chunk]
            eg = jnp.exp(gcol)
            kb = kh * bh
            qkb.append(jnp.concatenate([qh, kb], axis=0).astype(BF16))
            kbf.append(kh.astype(BF16))
            vbeta.append((vh * bh).astype(BF16))
            kbeg.append((kb * eg).astype(BF16))
            qdec.append((qh * eg).astype(BF16))
            kdec.append((kh * jnp.exp(glast - gcol)).astype(BF16))
            decay.append(jnp.exp(jnp.where(causal, jnp.broadcast_to(gcol, (chunk, chunk)) - grow, NEG_BIG)))
            gl.append(jnp.exp(glast))
            zc.append(z[r0:r1, h * dv:(h + 1) * dv])
        decay3 = stack(decay)
        sc = _bmm_t(stack(qkb), stack(kbf))
        qkm = (sc[:, :chunk] * decay3).astype(BF16)
        m = jnp.where(strict, sc[:, chunk:] * decay3, 0.0)
        tinv = _unit_lower_inverse(m).astype(BF16)
        u = _bmm(tinv, stack(vbeta))
        w = _bmm(tinv, stack(kbeg))
        sb = s3.astype(BF16)
        ws_qs = _bmm(jnp.concatenate([w.astype(BF16), stack(qdec)], axis=1), sb)
        vnb = (u - ws_qs[:, :chunk]).astype(BF16)
        o = ws_qs[:, chunk:] + _bmm(qkm, vnb)
        kd3 = stack(kdec)
        upd = stack([_dot_ta(kd3[h], vnb[h]) for h in range(heads)])
        s3 = s3 * stack(gl) + upd
        on = o * lax.rsqrt(jnp.mean(o * o, axis=-1, keepdims=True) + NORM_EPS) * ng_ref[...] * _silu(stack(zc))
        for h in range(heads):
            o_ref[r0:r1, h * dv:(h + 1) * dv] = on[h]
    s_scr[...] = s3
    sout_ref[...] = s3


def _gdn(qkv, z, ba, s0, aparams, norm_g, *, t_valid, chunk):
    nb, t, conv_dim = qkv.shape
    heads, dk, dv = s0.shape[1:]
    tt = min(GDN_TILE, t)
    assert t % tt == 0 and tt % chunk == 0 and (t_valid == tt or t == tt)
    body = functools.partial(_gdn_body, tt=tt, t_valid=t_valid, chunk=chunk, heads=heads, dk=dk, dv=dv)
    v_dim = heads * dv
    tok = lambda w: pl.BlockSpec((None, tt, w), lambda b, i: (b, i, 0))
    return pl.pallas_call(
        body,
        grid=(nb, t // tt),
        in_specs=[tok(conv_dim), tok(v_dim), tok(LANES),
                  pl.BlockSpec((None, heads, dk, dv), lambda b, i: (b, 0, 0, 0)),
                  _resident((2, LANES)), _resident((1, dv))],
        out_specs=[tok(v_dim), pl.BlockSpec((None, heads, dk, dv), lambda b, i: (b, 0, 0, 0))],
        out_shape=[jax.ShapeDtypeStruct((nb, t, v_dim), F32), jax.ShapeDtypeStruct((nb, heads, dk, dv), F32)],
        scratch_shapes=[pltpu.VMEM((heads, dk, dv), F32)],
        compiler_params=_params("arbitrary", "arbitrary"),
        name="gdn",
    )(qkv, z, ba, s0, aparams, norm_g.reshape(1, dv))


def _gdn_sample_body(xq_ref, xk_ref, xv_ref, pq_ref, pk_ref, pv_ref, wq_ref, wk_ref, wv_ref, z_ref, b_ref, a_ref,
                     ap_ref, ng_ref, s_ref, o_ref, sout_ref, kq_scr, *, ts, heads, dk, dv):
    h = pl.program_id(0)
    nb = xq_ref.shape[-1]

    def conv(x_ref, p_ref, w_ref):
        taps = GDN_CONV - 1
        xin = [p_ref[i] for i in range(taps)] + [x_ref[t] for t in range(ts)]
        w = [jnp.broadcast_to(w_ref[:, jj:jj + 1], xin[0].shape) for jj in range(GDN_CONV)]
        out = []
        for t in range(ts):
            acc = xin[t] * w[0]
            for jj in range(1, GDN_CONV):
                acc = acc + xin[t + jj] * w[jj]
            out.append(_silu(acc))
        return out

    def l2(x):
        return x * lax.rsqrt(jnp.sum(x * x, axis=0, keepdims=True) + NORM_EPS)

    q = [l2(x) * (dk ** -0.5) for x in conv(xq_ref, pq_ref, wq_ref)]
    k = [l2(x) for x in conv(xk_ref, pk_ref, wk_ref)]
    v = conv(xv_ref, pv_ref, wv_ref)
    for t in range(ts):
        kq_scr[t] = k[t]
        kq_scr[ts + t] = q[t]
    a_h = ap_ref[0, h]
    dt_h = ap_ref[1, h]
    beta, decay = [], []
    for t in range(ts):
        beta.append(_sigmoid(b_ref[t, pl.ds(h, 1), :]))
        xg = a_ref[t, pl.ds(h, 1), :] + dt_h
        decay.append(jnp.exp(-a_h * (jnp.maximum(xg, 0.0) + jnp.log1p(jnp.exp(-jnp.abs(xg))))))

    def row(i, kk):
        return jnp.broadcast_to(kq_scr[i, pl.ds(kk, 1), :], (dv, nb))

    def first(kk, ks):
        return ks + row(0, kk) * s_ref[kk]

    ks = lax.fori_loop(0, dk, first, jnp.zeros((dv, nb), F32), unroll=8)
    ng = jnp.broadcast_to(ng_ref[...], (dv, nb))
    for t in range(ts):
        d = beta[t] * (v[t] - decay[t] * ks)
        src = s_ref if t == 0 else sout_ref
        last = t == ts - 1

        def step(kk, carry, t=t, d=d, src=src, last=last):
            o_acc, ks_next = carry
            s_new = decay[t] * src[kk] + row(t, kk) * d
            sout_ref[kk] = s_new
            o_acc = o_acc + row(ts + t, kk) * s_new
            if not last:
                ks_next = ks_next + row(t + 1, kk) * s_new
            return o_acc, ks_next

        zero = jnp.zeros((dv, nb), F32)
        o, ks = lax.fori_loop(0, dk, step, (zero, zero), unroll=8)
        on = o * lax.rsqrt(jnp.mean(o * o, axis=0, keepdims=True) + NORM_EPS) * ng
        o_ref[t] = on * _silu(z_ref[t])


def _gdn_sample(x_t, prev_t, z_t, b_t, a_t, s_t, conv_w_t, aparams, norm_g):
    ts, conv_dim, nb = x_t.shape
    heads, dk, dv, _ = s_t.shape
    assert dk == dv and conv_dim == 3 * heads * dk
    taps = GDN_CONV - 1
    body = functools.partial(_gdn_sample_body, ts=ts, heads=heads, dk=dk, dv=dv)
    part = lambda rows, off: pl.BlockSpec((rows, dk, nb), lambda h: (0, off + h, 0))
    wpart = lambda off: pl.BlockSpec((dk, GDN_CONV), lambda h: (off + h, 0))
    whole = lambda a: pl.BlockSpec(a.shape, lambda h: (0,) * a.ndim)
    state = pl.BlockSpec((None, dk, dv, nb), lambda h: (h, 0, 0, 0))
    return pl.pallas_call(
        body,
        grid=(heads,),
        in_specs=[part(ts, 0), part(ts, heads), part(ts, 2 * heads),
                  part(taps, 0), part(taps, heads), part(taps, 2 * heads),
                  wpart(0), wpart(heads), wpart(2 * heads),
                  part(ts, 0), whole(b_t), whole(a_t),
                  pl.BlockSpec(memory_space=pltpu.SMEM), pl.BlockSpec((dv, 1), lambda h: (0, 0)), state],
        out_specs=[part(ts, 0), state],
        out_shape=[jax.ShapeDtypeStruct((ts, heads * dv, nb), F32), jax.ShapeDtypeStruct(s_t.shape, F32)],
        scratch_shapes=[pltpu.VMEM((2 * ts, dk, nb), F32)],
        compiler_params=_params("arbitrary"),
        name="gdn_sample",
    )(x_t, x_t, x_t, prev_t, prev_t, prev_t, conv_w_t, conv_w_t, conv_w_t, z_t, b_t, a_t, aparams,
      norm_g.reshape(dv, 1), s_t)


def _attn_body(qi_ref, kj_ref, q_ref, k_ref, ct_ref, o_ref, m_scr, acc_scr, *, heads, tq, rank):
    i = qi_ref[pl.program_id(1)]
    j = kj_ref[pl.program_id(1)]

    @pl.when(j == 0)
    def _():
        m_scr[...] = jnp.full(m_scr.shape, NEG_BIG, F32)
        acc_scr[...] = jnp.zeros(acc_scr.shape, F32)

    def step(masked):
        ct = ct_ref[...]
        m_old = [m_scr[h] for h in range(heads)]
        acc_old = [acc_scr[h] for h in range(heads)]
        if masked:
            key = lax.broadcasted_iota(jnp.int32, (tq, tq), 0)
            qry = lax.broadcasted_iota(jnp.int32, (tq, tq), 1)
            keep = key <= qry
        m_out, acc_out = [], []
        ahead = 3
        scores = [_dot_t(k_ref[h], q_ref[h]) for h in range(ahead)]
        for h in range(heads):
            if h + ahead < heads:
                scores.append(_dot_t(k_ref[h + ahead], q_ref[h + ahead]))
            st = scores[h]
            if masked:
                st = jnp.where(keep, st, NEG_BIG)
            m_new = jnp.maximum(m_old[h], jnp.max(st, axis=0, keepdims=True))
            alpha = jnp.exp(m_old[h] - m_new)
            pt = jnp.exp(st - m_new).astype(BF16)
            acc_out.append(acc_old[h] * alpha + _dot(ct, pt))
            m_out.append(m_new)
        for h in range(heads):
            m_scr[h] = m_out[h]
            acc_scr[h] = acc_out[h]
        return acc_out

    @pl.when(j < i)
    def _():
        step(False)

    @pl.when(j == i)
    def _():
        acc = step(True)
        for h in range(heads):
            ctx = acc[h][:rank, :] / acc[h][rank:rank + 1, :]
            o_ref[:, h * rank:(h + 1) * rank] = ctx.T.astype(BF16)


def _attn_prompt(q128, k128, ct_ext):
    nb, heads, t, _ = q128.shape
    rows = ct_ext.shape[1]
    rank = rows - SUBLANES
    tq = min(ATTN_TILE, t)
    assert t % tq == 0
    n = t // tq
    body = functools.partial(_attn_body, heads=heads, tq=tq, rank=rank)
    pairs = [(i, j) for i in range(n) for j in range(i + 1)]
    qi = jnp.asarray([p[0] for p in pairs], jnp.int32)
    kj = jnp.asarray([p[1] for p in pairs], jnp.int32)
    grid_spec = pltpu.PrefetchScalarGridSpec(
        num_scalar_prefetch=2,
        grid=(nb, len(pairs)),
        in_specs=[pl.BlockSpec((None, heads, tq, LANES), lambda b, s, qi, kj: (b, 0, qi[s], 0)),
                  pl.BlockSpec((None, heads, tq, LANES), lambda b, s, qi, kj: (b, 0, kj[s], 0)),
                  pl.BlockSpec((None, rows, tq), lambda b, s, qi, kj: (b, 0, kj[s]))],
        out_specs=pl.BlockSpec((None, tq, heads * rank), lambda b, s, qi, kj: (b, qi[s], 0)),
        scratch_shapes=[pltpu.VMEM((heads, 1, tq), F32), pltpu.VMEM((heads, rows, tq), F32)],
    )
    return pl.pallas_call(
        body, grid_spec=grid_spec,
        out_shape=jax.ShapeDtypeStruct((nb, t, heads * rank), BF16),
        compiler_params=_params("arbitrary", "arbitrary"),
        name="attn_prompt",
    )(qi, kj, q128, k128, ct_ext)


def _attn_sample_body(pt_ref, qabs_ref, qrope_ref, q128_ref, k128n_ref, cn_ref, wukt_ref, cache_c, cache_kr,
                      o_ref, cbuf, kbuf, cb, krb, s_all, sem, *, layer, heads, dn, s_new, n_pages, sub_pages):
    b = pl.program_id(0)
    nseq = pl.num_programs(0)
    slot = lax.rem(b, 2)
    rows = s_new * heads
    n_sub = n_pages // sub_pages
    sub_keys = sub_pages * PAGE_SIZE

    def page_copies(seq, slot_):
        out = []
        for i in range(n_pages):
            page = pt_ref[seq * n_pages + i]
            out.append(pltpu.make_async_copy(cache_c.at[layer, page],
                                             cbuf.at[slot_, pl.ds(i * PAGE_SIZE, PAGE_SIZE)], sem.at[0, slot_]))
            out.append(pltpu.make_async_copy(cache_kr.at[layer, page], kbuf.at[slot_, i], sem.at[1, slot_]))
        return out

    @pl.when(b == 0)
    def _():
        for cp in page_copies(0, 0):
            cp.start()

    for cp in page_copies(b, slot):
        cp.wait()

    @pl.when(b + 1 < nseq)
    def _():
        for cp in page_copies(b + 1, 1 - slot):
            cp.start()

    lhs = jnp.concatenate([wukt_ref[...], qabs_ref[...]], axis=0)
    nk = heads * dn
    qrope = qrope_ref[...]

    def scores(sb):
        for i in range(sub_pages):
            pg = sb * sub_pages + i
            cb[sb, i * PAGE_SIZE:(i + 1) * PAGE_SIZE, :] = cbuf[slot, pg * PAGE_SIZE:(pg + 1) * PAGE_SIZE, :].astype(BF16)
            krb[sb, :, i * PAGE_SIZE:(i + 1) * PAGE_SIZE] = kbuf[slot, pg].astype(BF16)
        kq = _dot_t(lhs, cb[sb])
        ssq = jnp.concatenate([jnp.sum(kq[h * dn:(h + 1) * dn, :] ** 2, axis=0, keepdims=True)
                               for h in range(heads)], axis=0)
        r = lax.rsqrt(ssq * (1.0 / dn) + NORM_EPS)
        s = kq[nk:, :] * jnp.concatenate([r] * s_new, axis=0) + _dot(qrope, krb[sb])
        s_all[sb] = s
        return jnp.max(s, axis=1, keepdims=True)

    state = (jnp.full((rows, 1), NEG_BIG, F32), jnp.zeros((rows, 1), F32), jnp.zeros((rows, LANES), F32))

    def absorb(state, sb, m_sb):
        m_run, l, acc = state
        m_new = jnp.maximum(m_run, m_sb)
        alpha = jnp.exp(m_run - m_new)
        p = jnp.exp(s_all[sb] - m_new)
        return (m_new, alpha * l + jnp.sum(p, axis=1, keepdims=True), alpha * acc + _dot(p.astype(BF16), cb[sb]))

    ahead = AHEAD
    maxes = []
    for sb in range(n_sub):
        maxes.append(scores(sb))
        if sb >= ahead:
            state = absorb(state, sb - ahead, maxes[sb - ahead])
    for sb in range(max(n_sub - ahead, 0), n_sub):
        state = absorb(state, sb, maxes[sb])
    m_run, l, acc = state

    qf = q128_ref[...].astype(F32)
    assert heads & (heads - 1) == 0
    tok = lax.shift_right_logical(lax.broadcasted_iota(jnp.int32, (rows, 1), 0), heads.bit_length() - 1)
    sn = []
    m_fin = m_run
    for t in range(s_new):
        kt = jnp.concatenate([k128n_ref[t].astype(F32)] * s_new, axis=0)
        st = jnp.sum(qf * kt, axis=1, keepdims=True)
        sn.append(jnp.where(tok >= t, st, NEG_BIG))
        m_fin = jnp.maximum(m_fin, sn[-1])
    alpha = jnp.exp(m_run - m_fin)
    l = alpha * l
    acc = alpha * acc
    cn = cn_ref[...].astype(BF16).astype(F32)
    for t in range(s_new):
        pt = jnp.exp(sn[t] - m_fin)
        l = l + pt
        acc = acc + pt.astype(BF16).astype(F32) * cn[t:t + 1, :]
    o_ref[...] = (acc / l).astype(BF16)


def _attn_sample(page_table, qabs, qrope, q128, k128n, c_new, wukt, cache_c, cache_kr, *, layer, heads, dn):
    nseq, rows, _ = qabs.shape
    s_new = rows // heads
    n_pages = page_table.shape[1]
    rank = cache_c.shape[-1]
    rope = cache_kr.shape[-2]
    sub_pages = min(SUB_PAGES, n_pages)
    assert n_pages % sub_pages == 0
    n_sub = n_pages // sub_pages
    sub_keys = sub_pages * PAGE_SIZE
    body = functools.partial(_attn_sample_body, layer=layer, heads=heads, dn=dn, s_new=s_new, n_pages=n_pages,
                             sub_pages=sub_pages)
    per_seq = lambda shape: pl.BlockSpec((None,) + shape, lambda b, pt: (b,) + (0,) * len(shape))
    grid_spec = pltpu.PrefetchScalarGridSpec(
        num_scalar_prefetch=1,
        grid=(nseq,),
        in_specs=[per_seq((rows, LANES)), per_seq((rows, rope)), per_seq((rows, LANES)),
                  per_seq((s_new, heads, LANES)), per_seq((s_new, LANES)),
                  pl.BlockSpec(wukt.shape, lambda b, pt: (0, 0)),
                  pl.BlockSpec(memory_space=pl.ANY), pl.BlockSpec(memory_space=pl.ANY)],
        out_specs=per_seq((rows, LANES)),
        scratch_shapes=[pltpu.VMEM((2, n_pages * PAGE_SIZE, rank), F32),
                        pltpu.VMEM((2, n_pages, rope, PAGE_SIZE), F32),
                        pltpu.VMEM((n_sub, sub_keys, rank), BF16),
                        pltpu.VMEM((n_sub, rope, sub_keys), BF16),
                        pltpu.VMEM((n_sub, rows, sub_keys), F32),
                        pltpu.SemaphoreType.DMA((2, 2))],
    )
    return pl.pallas_call(
        body, grid_spec=grid_spec,
        out_shape=jax.ShapeDtypeStruct((nseq, rows, LANES), BF16),
        compiler_params=_params("arbitrary"),
        name="attn_sample",
    )(page_table.reshape(-1), qabs, qrope, q128, k128n, c_new, wukt, cache_c, cache_kr)


def _rope_cos_sin(pos, half):
    inv = ROPE_THETA ** (-jnp.arange(half, dtype=F32) / half)
    t = pos.shape[0]
    if (t * half) % LANES == 0:
        ang = jnp.repeat(pos, half).reshape(-1, LANES) * jnp.tile(inv, t).reshape(-1, LANES)
        return jnp.concatenate([jnp.cos(ang).reshape(t, half), jnp.sin(ang).reshape(t, half)], axis=1)
    ang = pos[:, None] * inv[None, :]
    return jnp.concatenate([jnp.cos(ang), jnp.sin(ang)], axis=1)


def _rope_placement(half, dn):
    j = np.arange(half)
    place = np.zeros((2 * half, 3 * LANES), np.float32)
    dr = 2 * half
    for base, kind in ((dn, 'c'), (dn + dr, 's'), (LANES + dn, 'c'), (LANES + dn + dr, 'c'),
                       (2 * LANES + dn, 's'), (2 * LANES + dn + dr, 's')):
        if kind == 'c':
            place[j, base + j] = 1.0
            place[j, base + half + j] = 1.0
        else:
            place[half + j, base + j] = -1.0
            place[half + j, base + half + j] = 1.0
    bias = np.zeros((1, 3 * LANES), np.float32)
    bias[0, :dn] = 1.0
    return jnp.asarray(place, BF16), jnp.asarray(bias)


def _swap_halves(a, axis=-1):
    lo, hi = jnp.split(a, 2, axis=axis)
    return jnp.concatenate([hi, lo], axis=axis)


def _prep_layer(lp, dims):
    heads, dn, dr, rank, g_heads, conv_dim, v_dim = dims
    assert dn + 2 * dr == LANES and rank == LANES and 2 * g_heads <= LANES
    w_in = lp['w_in']
    d = w_in.shape[0]
    o = 0
    w_qkv = w_in[:, o:o + conv_dim]; o += conv_dim
    w_z = w_in[:, o:o + v_dim]; o += v_dim
    w_b = w_in[:, o:o + g_heads]; o += g_heads
    w_a = w_in[:, o:o + g_heads]; o += g_heads
    w_q = w_in[:, o:o + heads * (dn + dr)].reshape(d, heads, dn + dr); o += heads * (dn + dr)
    w_c = w_in[:, o:o + rank]; o += rank
    w_kr = w_in[:, o:o + dr]
    zeros = lambda n: jnp.zeros((d, n), F32)
    w_q128 = jnp.concatenate([w_q, _swap_halves(w_q[:, :, dn:])], axis=2).reshape(d, heads * LANES)
    w_krs = _swap_halves(w_kr)
    w_all = jnp.concatenate([w_qkv, w_z, w_b, w_a, zeros(LANES - 2 * g_heads), w_q128, w_c,
                             zeros(dn), w_kr, w_kr, zeros(dn), w_krs, w_krs], axis=1).astype(BF16)
    scale = (dn + dr) ** -0.5
    qr_g = lp['mla_qr_norm']
    gq = (jnp.concatenate([lp['mla_qn_norm'], qr_g, _swap_halves(qr_g)]) * scale).reshape(1, LANES)
    lane = np.arange(LANES)
    seg = np.where(lane < dn, 0, np.where(lane < dn + dr, 1, 2))
    seg_len = np.where(lane < dn, dn, dr).astype(np.float32)
    mq = jnp.asarray(np.where(seg[:, None] == seg[None, :], 1.0 / seg_len[None, :], 0.0), BF16)
    mk = jnp.asarray(np.where((lane[:, None] < dn) & (lane[None, :] < dn), 1.0 / dn, 0.0), BF16)
    gk = jnp.concatenate([lp['mla_kn_norm'], jnp.zeros((LANES - dn,), F32)]).reshape(1, LANES)
    kr_g = lp['mla_kr_norm']
    zdn = jnp.zeros((dn,), F32)
    gkr = jnp.stack([jnp.concatenate([zdn, kr_g, kr_g]),
                     jnp.concatenate([zdn, _swap_halves(kr_g), _swap_halves(kr_g)])])
    w_uk = lp['mla_w_uk']
    wuk = jnp.concatenate([w_uk, jnp.zeros((rank, heads, LANES - dn), F32)], axis=2)
    wuk = wuk.reshape(rank, heads * LANES).astype(BF16)
    wukt = w_uk.transpose(1, 2, 0).reshape(heads * dn, rank).astype(BF16)
    absorb = jnp.concatenate([w_uk.transpose(1, 2, 0) * lp['mla_kn_norm'][None, :, None],
                              jnp.zeros((heads, LANES - dn, rank), F32)], axis=1)
    fold = np.zeros((LANES, LANES), np.float32)
    fold[dn + np.arange(dr), np.arange(dr)] = 1.0
    fold[dn + dr + np.arange(dr), np.arange(dr)] = 1.0
    wqa = jnp.concatenate([absorb, jnp.broadcast_to(jnp.asarray(fold), (heads, LANES, LANES))], axis=2).astype(BF16)
    w_uv = lp['mla_w_uv']
    mv = w_uv.shape[2]
    wuv_bd = (w_uv.transpose(1, 0, 2)[:, :, None, :] * jnp.asarray(np.eye(heads, dtype=np.float32))[:, None, :, None])
    wuv_bd = wuv_bd.reshape(heads * rank, heads * mv).astype(BF16)
    lane_h = lane // (LANES // 2)
    ms = jnp.asarray(np.where(lane_h[:, None] == lane_h[None, :], 1.0, 0.0), BF16)
    aparams = jnp.pad(jnp.stack([jnp.exp(lp['gdn_a_log']), lp['gdn_dt_bias']]),
                      ((0, 0), (g_heads, LANES - 2 * g_heads)))
    return dict(
        w_all=w_all, gq=gq, mq=mq, gc=lp['mla_ckv_norm'].reshape(1, LANES), wuk=wuk, mk=mk, gk=gk, gkr=gkr,
        wukt=wukt, wqa=wqa, wuv_bd=wuv_bd, ms=ms, aparams=aparams,
        wo_g=lp['w_out'][:v_dim].astype(BF16), wo_m=lp['w_out'][v_dim:].astype(BF16),
        f1=(lp['ffn1_wi'].astype(BF16), lp['ffn1_wo'].astype(BF16)),
        f2=(lp['ffn2_wi'].astype(BF16), lp['ffn2_wo'].astype(BF16)),
    )


def _layer(x, mods, rope, conv_prev, s0, lp, w, lay, dims, *, t_valid, chunk, attend):
    heads = dims[0]
    x = _ffn(x, mods, lp['norm_ffn1'], *w['f1'], mod0=0, tokens_per_seq=None)
    outs = _inproj(x, mods, lp['norm_mix'], w['w_all'], *rope, w['gq'], w['mq'], w['gc'], w['wuk'], w['mk'],
                   w['gk'], w['gkr'], None, (conv_prev, lp['gdn_conv_w'], w['ms']), lay=lay, heads=heads,
                   tokens_per_seq=None, rope=(dims[1], dims[2]))
    qkv, z, ba, c, krot, nconv = outs[0], outs[1], outs[2], outs[5], outs[6], outs[8]
    gdn_out, s_new = _gdn(qkv, z, ba, s0, w['aparams'], lp['gdn_norm'], t_valid=t_valid, chunk=chunk)
    ctx = attend(outs)
    x = _ffn(x, mods, lp['norm_ffn2'], *w['f2'], mod0=6, tokens_per_seq=None,
             mix=(gdn_out, ctx, w['wuv_bd'], w['wo_g'], w['wo_m']))
    return x, c, krot, nconv, s_new


def kernel(x_prompt, x_sample, cache_ckv, cache_krope, state_conv, state_gdn, page_table, c_prompt, c_sample,
           ada_w, ada_b, norm_ffn1, ffn1_wi, ffn1_wo, norm_mix, w_in, gdn_conv_w, gdn_a_log, gdn_dt_bias, gdn_norm,
           mla_qn_norm, mla_qr_norm, mla_ckv_norm, mla_kr_norm, mla_kn_norm, mla_w_uk, mla_w_uv, w_out,
           norm_ffn2, ffn2_wi, ffn2_wo):
    depth = ada_w.shape[0]
    bp, tp, d = x_prompt.shape
    bs, ts, _ = x_sample.shape
    g_heads, dk, dv = state_gdn.shape[2:]
    conv_dim = state_conv.shape[-1]
    v_dim = g_heads * dv
    rank, heads, dn = mla_w_uk.shape[1:]
    dr = mla_qr_norm.shape[1]
    past = page_table.shape[1] * PAGE_SIZE
    dims = (heads, dn, dr, rank, g_heads, conv_dim, v_dim)
    lay = _InLayout(conv_dim, v_dim, heads)

    cs_p = _rope_cos_sin(jnp.arange(tp, dtype=F32), dr // 2)
    cs_s = jnp.tile(_rope_cos_sin(past + jnp.arange(ts, dtype=F32), dr // 2), (bs, 1))
    place, tab_bias = _rope_placement(dr // 2, dn)
    cond = jnp.concatenate([c_prompt, c_sample], axis=0)
    rows = -(-cond.shape[0] // SUBLANES) * SUBLANES
    cond = jnp.pad(cond, ((0, rows - cond.shape[0]), (0, 0)))

    yp = x_prompt
    ys = x_sample.reshape(1, bs * ts, d)
    outs_p, outs_s = [], []
    for l in range(depth):
        lp = dict(norm_ffn1=norm_ffn1[l], ffn1_wi=ffn1_wi[l], ffn1_wo=ffn1_wo[l], norm_mix=norm_mix[l], w_in=w_in[l],
                  gdn_conv_w=gdn_conv_w[l], gdn_a_log=gdn_a_log[l], gdn_dt_bias=gdn_dt_bias[l], gdn_norm=gdn_norm[l],
                  mla_qn_norm=mla_qn_norm[l], mla_qr_norm=mla_qr_norm[l], mla_ckv_norm=mla_ckv_norm[l],
                  mla_kr_norm=mla_kr_norm[l], mla_kn_norm=mla_kn_norm[l], mla_w_uk=mla_w_uk[l],
                  mla_w_uv=mla_w_uv[l], w_out=w_out[l], norm_ffn2=norm_ffn2[l], ffn2_wi=ffn2_wi[l],
                  ffn2_wo=ffn2_wo[l])
        w = _prep_layer(lp, dims)
        mods = _mods(cond, ada_w[l], ada_b[l])
        mods_p = mods[:bp].reshape(bp, N_MOD, d)
        mods_s = mods[bp:bp + bs].reshape(bs, N_MOD, d).transpose(1, 0, 2)

        conv0 = jnp.zeros((bp, SUBLANES, conv_dim), F32)
        s0 = jnp.zeros((bp, g_heads, dk, dv), F32)
        attend_p = lambda o: _attn_prompt(o[3], o[4], o[7])
        yp, c_p, kr_p, cv_p, s_p = _layer(yp, mods_p, (cs_p, place, tab_bias), conv0, s0, lp, w, lay, dims,
                                          t_valid=min(GDN_TILE, tp), chunk=min(GDN_CHUNK, tp), attend=attend_p)
        outs_p.append((c_p, kr_p.transpose(0, 2, 1), cv_p[:, SUBLANES - (GDN_CONV - 1):], s_p))


        def attend_s(o, l=l, w=w):
            q128, k128, c_new, qabs, qrope = o[3], o[4], o[5], o[7], o[8]
            by_seq = lambda a: a[0].reshape(heads, bs, ts, -1).transpose(1, 2, 0, 3)
            flat = lambda a: by_seq(a).reshape(bs, ts * heads, -1)
            ctx = _attn_sample(page_table, flat(qabs), flat(qrope)[..., :dr], flat(q128), by_seq(k128),
                               c_new.reshape(bs, ts, -1), w['wukt'], cache_ckv, jnp.swapaxes(cache_krope, 2, 3),
                               layer=l, heads=heads, dn=dn)
            return ctx.reshape(1, bs * ts, heads * rank)

        ys, c_s, kr_s, cv_s, s_s = _layer_sample(ys, mods_s, (cs_s, place, tab_bias), state_conv[l], state_gdn[l], lp, w, lay, dims,
                                                 bs, ts, attend_s)
        outs_s.append((c_s, kr_s, cv_s, s_s))

    stack = lambda outs, k: jnp.stack([o[k] for o in outs])
    ys = ys.reshape(bs, ts, d)
    return (yp, ys, stack(outs_p, 0), stack(outs_p, 1), stack(outs_p, 2), stack(outs_p, 3),
            stack(outs_s, 0), stack(outs_s, 1), stack(outs_s, 2), stack(outs_s, 3))


def _layer_sample(x, mods, rope, conv_prev, s0, lp, w, lay, dims, bs, ts, attend):
    heads, dn, dr, g_heads = dims[0], dims[1], dims[2], dims[4]
    x = _ffn(x, mods, lp['norm_ffn1'], *w['f1'], mod0=0, tokens_per_seq=ts)
    outs = _inproj(x, mods, lp['norm_mix'], w['w_all'], *rope, w['gq'], w['mq'], w['gc'], w['wuk'], w['mk'],
                   w['gk'], w['gkr'], w['wqa'], None, lay=lay, heads=heads, tokens_per_seq=ts, rope=(dn, dr))
    qkv, z, ba, c, krot = outs[0], outs[1], outs[2], outs[5], outs[6]
    seq = lambda a: a.reshape(bs, ts, a.shape[-1])
    minor = lambda a: seq(a).transpose(1, 2, 0)
    o_t, s_new = _gdn_sample(minor(qkv), conv_prev.transpose(1, 2, 0), minor(z), minor(ba[..., :g_heads]),
                             minor(ba[..., g_heads:2 * g_heads]), s0.transpose(1, 2, 3, 0), lp['gdn_conv_w'].T,
                             w['aparams'][:, g_heads:2 * g_heads], lp['gdn_norm'])
    gdn_out = o_t.transpose(2, 0, 1).reshape(1, bs * ts, -1)
    ctx = attend(outs)
    x = _ffn(x, mods, lp['norm_ffn2'], *w['f2'], mod0=6, tokens_per_seq=ts,
             mix=(gdn_out, ctx, w['wuv_bd'], w['wo_g'], w['wo_m']))
    nconv = jnp.concatenate([conv_prev, seq(qkv)], axis=1)[:, -(GDN_CONV - 1):]
    return x, seq(c), seq(krot)[..., dn:dn + dr], nconv, s_new.transpose(3, 0, 1, 2)
```

```python
import functools
import math

import jax
import jax.numpy as jnp
import numpy as np
from jax import lax
from jax.experimental import pallas as pl
from jax.experimental.pallas import tpu as pltpu

F32 = jnp.float32
BF16 = jnp.bfloat16

NORM_EPS = 1e-6
ROPE_THETA = 10000.0
PAGE_SIZE = 128
GDN_CONV = 4
GDN_CHUNK = 128
N_MOD = 9
LANES = 128
SUBLANES = 8
NEG_BIG = -1e30
VMEM_LIMIT = 56 * 1024 * 1024

TOKEN_TILE = 512
ATTN_TILE = 512
GDN_TILE = 512
SUB_PAGES = 16
AHEAD = 4


def _dot(a, b):
    return jnp.dot(a, b, preferred_element_type=F32)


def _dot_t(a, b):
    return lax.dot_general(a, b, (((1,), (1,)), ((), ())), preferred_element_type=F32)


def _dot_ta(a, b):
    return lax.dot_general(a, b, (((0,), (0,)), ((), ())), preferred_element_type=F32)


def _bmm(a, b):
    return lax.dot_general(a, b, (((2,), (1,)), ((0,), (0,))), preferred_element_type=F32)


def _bmm_t(a, b):
    return lax.dot_general(a, b, (((2,), (2,)), ((0,), (0,))), preferred_element_type=F32)


def _sigmoid(x):
    return 1.0 / (1.0 + jnp.exp(-x))


def _silu(x):
    return x * _sigmoid(x)


def _params(*sem):
    return pltpu.CompilerParams(dimension_semantics=sem, vmem_limit_bytes=VMEM_LIMIT)


def _resident(shape):
    nd = len(shape)
    return pl.BlockSpec(shape, lambda *_: (0,) * nd, pipeline_mode=pl.Buffered(1))


MODS_PER_SUBLAYER = 3


def _dot_split(x, e, terms, left=False):
    out = None
    for _ in range(terms):
        xb = x.astype(BF16)
        d = _dot(e, xb) if left else _dot(xb, e)
        out = d if out is None else out + d
        x = x - xb.astype(F32)
    return out


def _mod_rows(m_ref, ks, tokens_per_seq, rows):
    if tokens_per_seq is None:
        return [m_ref[k:k + 1, :] for k in ks]
    nseq = m_ref.shape[1]
    row = lax.broadcasted_iota(jnp.int32, (rows, nseq), 0)
    first = lax.broadcasted_iota(jnp.int32, (rows, nseq), 1) * tokens_per_seq
    spread = jnp.where((row >= first) & (row < first + tokens_per_seq), 1.0, 0.0).astype(BF16)
    return [_dot_split(m_ref[k % MODS_PER_SUBLAYER], spread, 3, left=True) for k in ks]


def _mod_norm(x, gain, shift, scale):
    y = x * lax.rsqrt(jnp.mean(x * x, axis=-1, keepdims=True) + NORM_EPS)
    return (y * gain) * (1.0 + scale) + shift


def _mods_spec(tokens_per_seq, tm, d, sublayer):
    if tokens_per_seq is not None:
        assert tm % tokens_per_seq == 0
        return pl.BlockSpec((MODS_PER_SUBLAYER, tm // tokens_per_seq, d), lambda b, i: (sublayer, i, 0))
    return pl.BlockSpec((None, N_MOD, d), lambda b, i: (b, 0, 0))


def _mods_body(c_ref, w_ref, b_ref, o_ref):
    c = c_ref[...]
    o_ref[...] = _dot(_silu(c).astype(BF16), w_ref[...].astype(BF16)) + b_ref[...]


def _mods(cond, ada_w, ada_b):
    rows, d = cond.shape
    n = ada_w.shape[1]
    tn = d
    return pl.pallas_call(
        _mods_body,
        grid=(n // tn,),
        in_specs=[pl.BlockSpec((rows, d), lambda j: (0, 0)),
                  pl.BlockSpec((d, tn), lambda j: (0, j)),
                  pl.BlockSpec((1, tn), lambda j: (0, j))],
        out_specs=pl.BlockSpec((rows, tn), lambda j: (0, j)),
        out_shape=jax.ShapeDtypeStruct((rows, n), F32),
        compiler_params=_params("arbitrary"),
        name="mods",
    )(cond, ada_w, ada_b.reshape(1, n))


def _ffn_body(x_ref, m_ref, g_ref, wi_ref, wo_ref, *rest, mod0, tokens_per_seq, fc, mixed):
    x = x_ref[...]
    if mixed:
        mm_ref, gdn_ref, ctx_ref, wuv_ref, wog_ref, wom_ref, o_ref = rest
        gate_k = mod0 - 1
        (gate_mix,) = _mod_rows(mm_ref if tokens_per_seq is not None else m_ref, [gate_k], tokens_per_seq, x.shape[0])
        mla = _dot(ctx_ref[...], wuv_ref[...])
        mix = _dot(gdn_ref[...].astype(BF16), wog_ref[...]) + _dot(mla.astype(BF16), wom_ref[...])
        x = x + gate_mix * mix
    else:
        (o_ref,) = rest
    shift, scale, gate = _mod_rows(m_ref, [mod0, mod0 + 1, mod0 + 2], tokens_per_seq, x.shape[0])
    h = _mod_norm(x, g_ref[...], shift, scale).astype(BF16)
    acc = jnp.zeros(x.shape, F32)
    dff = wo_ref.shape[0]
    for c in range(dff // fc):
        lo, hi = c * fc, (c + 1) * fc
        a = _silu(_dot(h, wi_ref[:, lo:hi])) * _dot(h, wi_ref[:, dff + lo:dff + hi])
        acc = acc + _dot(a.astype(BF16), wo_ref[lo:hi, :])
    o_ref[...] = x + 0.5 * gate * acc


def _ffn(x, mods, gain, wi, wo, *, mod0, tokens_per_seq, mix=None):
    nb, t, d = x.shape
    tm = min(TOKEN_TILE, t)
    dff = wo.shape[0]
    fc = 2 * LANES
    assert t % tm == 0 and dff % fc == 0
    body = functools.partial(_ffn_body, mod0=mod0, tokens_per_seq=tokens_per_seq, fc=fc, mixed=mix is not None)
    tok = lambda w: pl.BlockSpec((None, tm, w), lambda b, i: (b, i, 0))
    sub = mod0 // MODS_PER_SUBLAYER
    in_specs = [tok(d), _mods_spec(tokens_per_seq, tm, d, sub),
                _resident((1, d)), _resident((d, 2 * dff)), _resident((dff, d))]
    args = [x, mods, gain.reshape(1, d), wi, wo]
    if mix is not None:
        gdn_out, ctx, wuv_bd, wo_g, wo_m = mix
        in_specs += [_mods_spec(tokens_per_seq, tm, d, sub - 1), tok(gdn_out.shape[-1]), tok(ctx.shape[-1]),
                     _resident(wuv_bd.shape), _resident(wo_g.shape), _resident(wo_m.shape)]
        args += [mods, gdn_out, ctx, wuv_bd, wo_g, wo_m]
    return pl.pallas_call(
        body,
        grid=(nb, t // tm),
        in_specs=in_specs,
        out_specs=tok(d),
        out_shape=jax.ShapeDtypeStruct(x.shape, F32),
        compiler_params=_params("arbitrary", "arbitrary"),
        name="ffn_mix" if mix is not None else "ffn",
    )(*args)


class _InLayout:
    def __init__(self, conv_dim, v_dim, heads):
        self.qkv = (0, conv_dim)
        self.z = (conv_dim, conv_dim + v_dim)
        self.ba = (self.z[1], self.z[1] + LANES)
        self.q = (self.ba[1], self.ba[1] + heads * LANES)
        self.ckv = (self.q[1], self.q[1] + LANES)
        self.kr = (self.ckv[1], self.ckv[1] + LANES)
        self.krs = (self.kr[1], self.kr[1] + LANES)
        self.total = self.krs[1]


def _inproj_body(x_ref, m_ref, g_ref, w_ref, cs_ref, place_ref, tbias_ref, gq_ref, mq_ref, gc_ref, wuk_ref, mk_ref,
                 gk_ref, gkr_ref, *rest, lay, heads, tokens_per_seq, sample, qk_dim, rope):
    if sample:
        wqa_ref, rest = rest[0], rest[1:]
    else:
        cprev_ref, cw_ref, ms_ref, rest = rest[0], rest[1], rest[2], rest[3:]
    qkv_ref, z_ref, ba_ref, q128_ref, k128_ref, c_ref, krot_ref = rest[:7]
    x = x_ref[...]
    shift, scale = _mod_rows(m_ref, [3, 4], tokens_per_seq, x.shape[0])
    h = _mod_norm(x, g_ref[...], shift, scale).astype(BF16)
    cw = 4 * LANES
    assert lay.total % cw == 0 and lay.qkv[1] % cw == 0 and (lay.z[1] - lay.z[0]) == cw
    n_chunks = lay.total // cw
    n_qkv = lay.qkv[1] // cw
    rest_chunks = list(range(n_qkv, n_chunks))
    order = []
    for c in range(max(n_qkv, len(rest_chunks))):
        order += ([c] if c < n_qkv else []) + ([rest_chunks[c]] if c < len(rest_chunks) else [])
    chunks = {}

    def cols(lo, hi):
        c = lo // cw
        assert (hi - 1) // cw == c
        return chunks[c][:, lo - c * cw:hi - c * cw]

    tm = x.shape[0]
    hist = SUBLANES
    taps = GDN_CONV - 1
    if not sample:
        ct_ref, nconv_ref, xbuf = rest[7], rest[8], rest[9]

        @pl.when(pl.program_id(1) == 0)
        def _():
            xbuf[0:hist, :] = cprev_ref[...]

    tab = _dot_split(cs_ref[...], place_ref[...], 3) + tbias_ref[...]
    tab_q, tab_c, tab_s = tab[:, :LANES], tab[:, LANES:2 * LANES], tab[:, 2 * LANES:]

    def group(lo):
        hi = lo + LANES
        if lo < lay.qkv[1]:
            if sample:
                qkv_ref[:, lo:hi] = cols(lo, hi)
                return
            xbuf[hist:hist + tm, lo:hi] = cols(lo, hi)
            conv = xbuf[hist - taps:hist - taps + tm, lo:hi] * cw_ref[0:1, lo:hi]
            for jj in range(1, GDN_CONV):
                conv = conv + xbuf[hist - taps + jj:hist - taps + jj + tm, lo:hi] * cw_ref[jj:jj + 1, lo:hi]
            yi = _silu(conv)
            if lo < 2 * qk_dim:
                ss = _dot((yi * yi).astype(BF16), ms_ref[...])
                yi = yi * (lax.rsqrt(ss + NORM_EPS) * ((LANES // 2) ** -0.5 if lo < qk_dim else 1.0))
            qkv_ref[:, lo:hi] = yi
            tail = xbuf[tm:tm + hist, lo:hi]
            nconv_ref[:, lo:hi] = tail
            xbuf[0:hist, lo:hi] = tail
        elif lo < lay.z[1]:
            z_ref[:, lo - lay.z[0]:hi - lay.z[0]] = cols(lo, hi)
        elif lo < lay.ba[1]:
            ba_ref[...] = cols(lo, hi)
        elif lo < lay.q[1]:
            hh = (lo - lay.q[0]) // LANES
            qh = cols(lo, hi)
            msq = _dot((qh * qh).astype(BF16), mq_ref[...])
            qn = qh * lax.rsqrt(msq + NORM_EPS) * gq_ref[...] * tab_q
            q128_ref[hh] = qn.astype(BF16)
            if sample:
                qabs_ref, qrope_ref = rest[7], rest[8]
                qa = _dot(qn.astype(BF16), wqa_ref[hh])
                qabs_ref[hh] = qa[:, :LANES].astype(BF16)
                qrope_ref[hh] = qa[:, LANES:2 * LANES].astype(BF16)
        elif lo == lay.ckv[0]:
            keys()

    def keys():
        ckv = cols(*lay.ckv)
        c = ckv * lax.rsqrt(jnp.mean(ckv * ckv, axis=-1, keepdims=True) + NORM_EPS) * gc_ref[...]
        c_ref[...] = c
        cb = c.astype(BF16)
        if not sample:
            ct_ref[...] = jnp.concatenate([c.T, jnp.ones((SUBLANES, c.shape[0]), F32)], axis=0).astype(BF16)
        kr = cols(*lay.kr)
        krs = cols(*lay.krs)
        inv = lax.rsqrt(jnp.sum(kr * kr, axis=-1, keepdims=True) * (2.0 / LANES) + NORM_EPS)
        krot = kr * inv * gkr_ref[0:1, :] * tab_c + krs * inv * gkr_ref[1:2, :] * tab_s
        if sample:
            krot_ref[...] = krot
        else:
            krot_ref[...] = krot.T[rope[0]:rope[0] + rope[1], :]
        knr = _dot(cb, wuk_ref[...])
        for hh in range(heads):
            kh = knr[:, hh * LANES:(hh + 1) * LANES]
            msq = _dot((kh * kh).astype(BF16), mk_ref[...])
            k128_ref[hh] = (kh * lax.rsqrt(msq + NORM_EPS) * gk_ref[...] + krot).astype(BF16)

    ahead = 2
    for i in range(len(order) + ahead):
        if i < len(order):
            k = order[i]
            chunks[k] = _dot(h, w_ref[:, k * cw:(k + 1) * cw])
        if i >= ahead:
            k = order[i - ahead]
            for lo in range(k * cw, (k + 1) * cw, LANES):
                group(lo)


def _inproj(x, mods, gain, w_all, cs, place, tab_bias, gq, mq, gc, wuk, mk, gk, gkr, wqa, conv, *, lay, heads,
            tokens_per_seq, rope):
    nb, t, d = x.shape
    tm = min(TOKEN_TILE, t)
    assert t % tm == 0
    sample = wqa is not None
    body = functools.partial(_inproj_body, lay=lay, heads=heads, tokens_per_seq=tokens_per_seq, sample=sample,
                             qk_dim=(lay.qkv[1] - (lay.z[1] - lay.z[0])) // 2, rope=rope)
    tok = lambda w: pl.BlockSpec((None, tm, w), lambda b, i: (b, i, 0))
    hd = lambda w: pl.BlockSpec((None, heads, tm, w), lambda b, i: (b, 0, i, 0))
    conv_dim, v_dim = lay.qkv[1], lay.z[1] - lay.z[0]
    in_specs = [tok(d), _mods_spec(tokens_per_seq, tm, d, 1), _resident((1, d)), _resident(w_all.shape),
                pl.BlockSpec((tm, cs.shape[1]), lambda b, i: (i, 0)), _resident(place.shape),
                _resident(tab_bias.shape),
                _resident((1, LANES)), _resident((LANES, LANES)), _resident((1, LANES)),
                _resident(wuk.shape), _resident((LANES, LANES)), _resident((1, LANES)), _resident((2, LANES))]
    args = [x, mods, gain.reshape(1, d), w_all, cs, place, tab_bias, gq, mq, gc, wuk, mk, gk, gkr]
    out_specs = [tok(conv_dim), tok(v_dim), tok(LANES), hd(LANES), hd(LANES), tok(LANES), tok(LANES)]
    out_shape = [jax.ShapeDtypeStruct((nb, t, conv_dim), F32), jax.ShapeDtypeStruct((nb, t, v_dim), F32),
                 jax.ShapeDtypeStruct((nb, t, LANES), F32),
                 jax.ShapeDtypeStruct((nb, heads, t, LANES), BF16), jax.ShapeDtypeStruct((nb, heads, t, LANES), BF16),
                 jax.ShapeDtypeStruct((nb, t, LANES), F32), jax.ShapeDtypeStruct((nb, t, LANES), F32)]
    if not sample:
        out_specs[6] = pl.BlockSpec((None, rope[1], tm), lambda b, i: (b, 0, i))
        out_shape[6] = jax.ShapeDtypeStruct((nb, rope[1], t), F32)
    if sample:
        in_specs.append(_resident(wqa.shape))
        args.append(wqa)
        out_specs += [hd(LANES), hd(LANES)]
        out_shape += [jax.ShapeDtypeStruct((nb, heads, t, LANES), BF16)] * 2
        scratch = []
    else:
        conv_prev, conv_w, ms = conv
        in_specs += [pl.BlockSpec((None, SUBLANES, conv_dim), lambda b, i: (b, 0, 0)),
                     _resident(conv_w.shape), _resident(ms.shape)]
        args += [conv_prev, conv_w, ms]
        out_specs += [pl.BlockSpec((None, LANES + SUBLANES, tm), lambda b, i: (b, 0, i)),
                      pl.BlockSpec((None, SUBLANES, conv_dim), lambda b, i: (b, 0, 0))]
        out_shape += [jax.ShapeDtypeStruct((nb, LANES + SUBLANES, t), BF16),
                      jax.ShapeDtypeStruct((nb, SUBLANES, conv_dim), F32)]
        scratch = [pltpu.VMEM((tm + SUBLANES, conv_dim), F32)]
    return pl.pallas_call(
        body, grid=(nb, t // tm), in_specs=in_specs, out_specs=out_specs, out_shape=out_shape,
        scratch_shapes=scratch, compiler_params=_params("arbitrary", "arbitrary"), name="inproj",
    )(*args)


def _unit_lower_inverse(a):
    n = a.shape[-1]
    assert n & (n - 1) == 0
    row = lax.broadcasted_iota(jnp.int32, (n, n), 0)
    col = lax.broadcasted_iota(jnp.int32, (n, n), 1)
    x = jnp.broadcast_to(jnp.where(row == col, 1.0, 0.0), a.shape)
    b = 1
    while b < n:
        lo_mask = (jnp.bitwise_xor(row, col) < 2 * b) & (jnp.bitwise_and(row, b) != 0) & (jnp.bitwise_and(col, b) == 0)
        lo = jnp.where(lo_mask, a, 0.0)
        if b == 1:
            x = x - lo
        else:
            xb = x.astype(BF16)
            x = x - _bmm(xb, _bmm(lo.astype(BF16), xb).astype(BF16))
        b *= 2
    return x


def _gdn_body(qkv_ref, z_ref, ba_ref, s0_ref, ap_ref, ng_ref, o_ref, sout_ref, s_scr,
              *, tt, t_valid, chunk, heads, dk, dv):
    j = pl.program_id(1)

    @pl.when(j == 0)
    def _():
        s_scr[...] = s0_ref[...]

    qk_dim = heads * dk
    y = qkv_ref[...]
    q = y[:, :qk_dim]
    k = y[:, qk_dim:2 * qk_dim]
    v = y[:, 2 * qk_dim:]
    ba = ba_ref[...]
    beta = _sigmoid(ba)
    xg = ba + ap_ref[1:2, :]
    g = -ap_ref[0:1, :] * (jnp.maximum(xg, 0.0) + jnp.log1p(jnp.exp(-jnp.abs(xg))))
    if t_valid < tt:
        keep = lax.broadcasted_iota(jnp.int32, (tt, 1), 0) < t_valid
        q, k, v = (jnp.where(keep, a, 0.0) for a in (q, k, v))
        beta, g = jnp.where(keep, beta, 0.0), jnp.where(keep, g, 0.0)
    row = lax.broadcasted_iota(jnp.int32, (tt, tt), 0)
    col = lax.broadcasted_iota(jnp.int32, (tt, tt), 1)
    assert chunk & (chunk - 1) == 0
    tri = jnp.where((row >= col) & (jnp.bitwise_xor(row, col) < chunk), 1.0, 0.0).astype(BF16)
    g_hi = g.astype(BF16)
    g_lo = (g - g_hi.astype(F32)).astype(BF16)
    gc = _dot(tri, g_hi) + _dot(tri, g_lo)
    gct = gc.T
    z = z_ref[...]

    crow = lax.broadcasted_iota(jnp.int32, (chunk, chunk), 0)
    ccol = lax.broadcasted_iota(jnp.int32, (chunk, chunk), 1)
    causal = crow >= ccol
    strict = crow > ccol
    stack = lambda xs: jnp.stack(xs, axis=0)

    s3 = s_scr[...]
    for c in range(tt // chunk):
        r0, r1 = c * chunk, (c + 1) * chunk
        qkb, kbf, vbeta, kbeg, qdec, kdec, decay, gl, zc = ([] for _ in range(9))
        for h in range(heads):
            qh = q[r0:r1, h * dk:(h + 1) * dk]
            kh = k[r0:r1, h * dk:(h + 1) * dk]
            vh = v[r0:r1, h * dv:(h + 1) * dv]
            bh = beta[r0:r1, h:h + 1]
            gcol = gc[r0:r1, heads + h:heads + h + 1]
            grow = gct[heads + h:heads + h + 1, r0:r1]
            glast = grow[:, chunk - 1:chunk]
            eg = jnp.exp(gcol)
            kb = kh * bh
            qkb.append(jnp.concatenate([qh, kb], axis=0).astype(BF16))
            kbf.append(kh.astype(BF16))
            vbeta.append((vh * bh).astype(BF16))
            kbeg.append((kb * eg).astype(BF16))
            qdec.append((qh * eg).astype(BF16))
            kdec.append((kh * jnp.exp(glast - gcol)).astype(BF16))
            decay.append(jnp.exp(jnp.where(causal, jnp.broadcast_to(gcol, (chunk, chunk)) - grow, NEG_BIG)))
            gl.append(jnp.exp(glast))
            zc.append(z[r0:r1, h * dv:(h + 1) * dv])
        decay3 = stack(decay)
        sc = _bmm_t(stack(qkb), stack(kbf))
        qkm = (sc[:, :chunk] * decay3).astype(BF16)
        m = jnp.where(strict, sc[:, chunk:] * decay3, 0.0)
        tinv = _unit_lower_inverse(m).astype(BF16)
        u = _bmm(tinv, stack(vbeta))
        w = _bmm(tinv, stack(kbeg))
        sb = s3.astype(BF16)
        ws_qs = _bmm(jnp.concatenate([w.astype(BF16), stack(qdec)], axis=1), sb)
        vnb = (u - ws_qs[:, :chunk]).astype(BF16)
        o = ws_qs[:, chunk:] + _bmm(qkm, vnb)
        kd3 = stack(kdec)
        upd = stack([_dot_ta(kd3[h], vnb[h]) for h in range(heads)])
        s3 = s3 * stack(gl) + upd
        on = o * lax.rsqrt(jnp.mean(o * o, axis=-1, keepdims=True) + NORM_EPS) * ng_ref[...] * _silu(stack(zc))
        for h in range(heads):
            o_ref[r0:r1, h * dv:(h + 1) * dv] = on[h]
    s_scr[...] = s3
    sout_ref[...] = s3


def _gdn(qkv, z, ba, s0, aparams, norm_g, *, t_valid, chunk):
    nb, t, conv_dim = qkv.shape
    heads, dk, dv = s0.shape[1:]
    tt = min(GDN_TILE, t)
    assert t % tt == 0 and tt % chunk == 0 and (t_valid == tt or t == tt)
    body = functools.partial(_gdn_body, tt=tt, t_valid=t_valid, chunk=chunk, heads=heads, dk=dk, dv=dv)
    v_dim = heads * dv
    tok = lambda w: pl.BlockSpec((None, tt, w), lambda b, i: (b, i, 0))
    return pl.pallas_call(
        body,
        grid=(nb, t // tt),
        in_specs=[tok(conv_dim), tok(v_dim), tok(LANES),
                  pl.BlockSpec((None, heads, dk, dv), lambda b, i: (b, 0, 0, 0)),
                  _resident((2, LANES)), _resident((1, dv))],
        out_specs=[tok(v_dim), pl.BlockSpec((None, heads, dk, dv), lambda b, i: (b, 0, 0, 0))],
        out_shape=[jax.ShapeDtypeStruct((nb, t, v_dim), F32), jax.ShapeDtypeStruct((nb, heads, dk, dv), F32)],
        scratch_shapes=[pltpu.VMEM((heads, dk, dv), F32)],
        compiler_params=_params("arbitrary", "arbitrary"),
        name="gdn",
    )(qkv, z, ba, s0, aparams, norm_g.reshape(1, dv))


def _gdn_sample_body(xq_ref, xk_ref, xv_ref, pq_ref, pk_ref, pv_ref, wq_ref, wk_ref, wv_ref, z_ref, b_ref, a_ref,
                     ap_ref, ng_ref, s_ref, o_ref, sout_ref, kq_scr, *, ts, heads, dk, dv):
    h = pl.program_id(0)
    nb = xq_ref.shape[-1]

    def conv(x_ref, p_ref, w_ref):
        taps = GDN_CONV - 1
        xin = [p_ref[i] for i in range(taps)] + [x_ref[t] for t in range(ts)]
        w = [jnp.broadcast_to(w_ref[:, jj:jj + 1], xin[0].shape) for jj in range(GDN_CONV)]
        out = []
        for t in range(ts):
            acc = xin[t] * w[0]
            for jj in range(1, GDN_CONV):
                acc = acc + xin[t + jj] * w[jj]
            out.append(_silu(acc))
        return out

    def l2(x):
        return x * lax.rsqrt(jnp.sum(x * x, axis=0, keepdims=True) + NORM_EPS)

    q = [l2(x) * (dk ** -0.5) for x in conv(xq_ref, pq_ref, wq_ref)]
    k = [l2(x) for x in conv(xk_ref, pk_ref, wk_ref)]
    v = conv(xv_ref, pv_ref, wv_ref)
    for t in range(ts):
        kq_scr[t] = k[t]
        kq_scr[ts + t] = q[t]
    a_h = ap_ref[0, h]
    dt_h = ap_ref[1, h]
    beta, decay = [], []
    for t in range(ts):
        beta.append(_sigmoid(b_ref[t, pl.ds(h, 1), :]))
        xg = a_ref[t, pl.ds(h, 1), :] + dt_h
        decay.append(jnp.exp(-a_h * (jnp.maximum(xg, 0.0) + jnp.log1p(jnp.exp(-jnp.abs(xg))))))

    def row(i, kk):
        return jnp.broadcast_to(kq_scr[i, pl.ds(kk, 1), :], (dv, nb))

    def first(kk, ks):
        return ks + row(0, kk) * s_ref[kk]

    ks = lax.fori_loop(0, dk, first, jnp.zeros((dv, nb), F32), unroll=8)
    ng = jnp.broadcast_to(ng_ref[...], (dv, nb))
    for t in range(ts):
        d = beta[t] * (v[t] - decay[t] * ks)
        src = s_ref if t == 0 else sout_ref
        last = t == ts - 1

        def step(kk, carry, t=t, d=d, src=src, last=last):
            o_acc, ks_next = carry
            s_new = decay[t] * src[kk] + row(t, kk) * d
            sout_ref[kk] = s_new
            o_acc = o_acc + row(ts + t, kk) * s_new
            if not last:
                ks_next = ks_next + row(t + 1, kk) * s_new
            return o_acc, ks_next

        zero = jnp.zeros((dv, nb), F32)
        o, ks = lax.fori_loop(0, dk, step, (zero, zero), unroll=8)
        on = o * lax.rsqrt(jnp.mean(o * o, axis=0, keepdims=True) + NORM_EPS) * ng
        o_ref[t] = on * _silu(z_ref[t])


def _gdn_sample(x_t, prev_t, z_t, b_t, a_t, s_t, conv_w_t, aparams, norm_g):
    ts, conv_dim, nb = x_t.shape
    heads, dk, dv, _ = s_t.shape
    assert dk == dv and conv_dim == 3 * heads * dk
    taps = GDN_CONV - 1
    body = functools.partial(_gdn_sample_body, ts=ts, heads=heads, dk=dk, dv=dv)
    part = lambda rows, off: pl.BlockSpec((rows, dk, nb), lambda h: (0, off + h, 0))
    wpart = lambda off: pl.BlockSpec((dk, GDN_CONV), lambda h: (off + h, 0))
    whole = lambda a: pl.BlockSpec(a.shape, lambda h: (0,) * a.ndim)
    state = pl.BlockSpec((None, dk, dv, nb), lambda h: (h, 0, 0, 0))
    return pl.pallas_call(
        body,
        grid=(heads,),
        in_specs=[part(ts, 0), part(ts, heads), part(ts, 2 * heads),
                  part(taps, 0), part(taps, heads), part(taps, 2 * heads),
                  wpart(0), wpart(heads), wpart(2 * heads),
                  part(ts, 0), whole(b_t), whole(a_t),
                  pl.BlockSpec(memory_space=pltpu.SMEM), pl.BlockSpec((dv, 1), lambda h: (0, 0)), state],
        out_specs=[part(ts, 0), state],
        out_shape=[jax.ShapeDtypeStruct((ts, heads * dv, nb), F32), jax.ShapeDtypeStruct(s_t.shape, F32)],
        scratch_shapes=[pltpu.VMEM((2 * ts, dk, nb), F32)],
        compiler_params=_params("arbitrary"),
        name="gdn_sample",
    )(x_t, x_t, x_t, prev_t, prev_t, prev_t, conv_w_t, conv_w_t, conv_w_t, z_t, b_t, a_t, aparams,
      norm_g.reshape(dv, 1), s_t)


def _attn_body(qi_ref, kj_ref, q_ref, k_ref, ct_ref, o_ref, m_scr, acc_scr, *, heads, tq, rank):
    i = qi_ref[pl.program_id(1)]
    j = kj_ref[pl.program_id(1)]

    @pl.when(j == 0)
    def _():
        m_scr[...] = jnp.full(m_scr.shape, NEG_BIG, F32)
        acc_scr[...] = jnp.zeros(acc_scr.shape, F32)

    def step(masked):
        ct = ct_ref[...]
        m_old = [m_scr[h] for h in range(heads)]
        acc_old = [acc_scr[h] for h in range(heads)]
        if masked:
            key = lax.broadcasted_iota(jnp.int32, (tq, tq), 0)
            qry = lax.broadcasted_iota(jnp.int32, (tq, tq), 1)
            keep = key <= qry
        m_out, acc_out = [], []
        ahead = 3
        scores = [_dot_t(k_ref[h], q_ref[h]) for h in range(ahead)]
        for h in range(heads):
            if h + ahead < heads:
                scores.append(_dot_t(k_ref[h + ahead], q_ref[h + ahead]))
            st = scores[h]
            if masked:
                st = jnp.where(keep, st, NEG_BIG)
            m_new = jnp.maximum(m_old[h], jnp.max(st, axis=0, keepdims=True))
            alpha = jnp.exp(m_old[h] - m_new)
            pt = jnp.exp(st - m_new).astype(BF16)
            acc_out.append(acc_old[h] * alpha + _dot(ct, pt))
            m_out.append(m_new)
        for h in range(heads):
            m_scr[h] = m_out[h]
            acc_scr[h] = acc_out[h]
        return acc_out

    @pl.when(j < i)
    def _():
        step(False)

    @pl.when(j == i)
    def _():
        acc = step(True)
        for h in range(heads):
            ctx = acc[h][:rank, :] / acc[h][rank:rank + 1, :]
            o_ref[:, h * rank:(h + 1) * rank] = ctx.T.astype(BF16)


def _attn_prompt(q128, k128, ct_ext):
    nb, heads, t, _ = q128.shape
    rows = ct_ext.shape[1]
    rank = rows - SUBLANES
    tq = min(ATTN_TILE, t)
    assert t % tq == 0
    n = t // tq
    body = functools.partial(_attn_body, heads=heads, tq=tq, rank=rank)
    pairs = [(i, j) for i in range(n) for j in range(i + 1)]
    qi = jnp.asarray([p[0] for p in pairs], jnp.int32)
    kj = jnp.asarray([p[1] for p in pairs], jnp.int32)
    grid_spec = pltpu.PrefetchScalarGridSpec(
        num_scalar_prefetch=2,
        grid=(nb, len(pairs)),
        in_specs=[pl.BlockSpec((None, heads, tq, LANES), lambda b, s, qi, kj: (b, 0, qi[s], 0)),
                  pl.BlockSpec((None, heads, tq, LANES), lambda b, s, qi, kj: (b, 0, kj[s], 0)),
                  pl.BlockSpec((None, rows, tq), lambda b, s, qi, kj: (b, 0, kj[s]))],
        out_specs=pl.BlockSpec((None, tq, heads * rank), lambda b, s, qi, kj: (b, qi[s], 0)),
        scratch_shapes=[pltpu.VMEM((heads, 1, tq), F32), pltpu.VMEM((heads, rows, tq), F32)],
    )
    return pl.pallas_call(
        body, grid_spec=grid_spec,
        out_shape=jax.ShapeDtypeStruct((nb, t, heads * rank), BF16),
        compiler_params=_params("arbitrary", "arbitrary"),
        name="attn_prompt",
    )(qi, kj, q128, k128, ct_ext)


def _attn_sample_body(pt_ref, qabs_ref, qrope_ref, q128_ref, k128n_ref, cn_ref, wukt_ref, cache_c, cache_kr,
                      o_ref, cbuf, kbuf, cb, krb, s_all, sem, *, layer, heads, dn, s_new, n_pages, sub_pages):
    b = pl.program_id(0)
    nseq = pl.num_programs(0)
    slot = lax.rem(b, 2)
    rows = s_new * heads
    n_sub = n_pages // sub_pages
    sub_keys = sub_pages * PAGE_SIZE

    def page_copies(seq, slot_):
        out = []
        for i in range(n_pages):
            page = pt_ref[seq * n_pages + i]
            out.append(pltpu.make_async_copy(cache_c.at[layer, page],
                                             cbuf.at[slot_, pl.ds(i * PAGE_SIZE, PAGE_SIZE)], sem.at[0, slot_]))
            out.append(pltpu.make_async_copy(cache_kr.at[layer, page], kbuf.at[slot_, i], sem.at[1, slot_]))
        return out

    @pl.when(b == 0)
    def _():
        for cp in page_copies(0, 0):
            cp.start()

    waits = page_copies(b, slot)
    for cp in waits[0::2] + waits[1::2]:
        cp.wait()

    nxt = lax.rem(b + 1, nseq)

    lhs = jnp.concatenate([wukt_ref[...], qabs_ref[...]], axis=0)
    nk = heads * dn
    qrope = qrope_ref[...]

    def scores(sb):
        for i in range(sub_pages):
            pg = sb * sub_pages + i
            cb[sb, i * PAGE_SIZE:(i + 1) * PAGE_SIZE, :] = cbuf[slot, pg * PAGE_SIZE:(pg + 1) * PAGE_SIZE, :].astype(BF16)
            krb[sb, :, i * PAGE_SIZE:(i + 1) * PAGE_SIZE] = kbuf[slot, pg].astype(BF16)
        kq = _dot_t(lhs, cb[sb])
        ssq = jnp.concatenate([jnp.sum(kq[h * dn:(h + 1) * dn, :] ** 2, axis=0, keepdims=True)
                               for h in range(heads)], axis=0)
        r = lax.rsqrt(ssq * (1.0 / dn) + NORM_EPS)
        s = kq[nk:, :] * jnp.concatenate([r] * s_new, axis=0) + _dot(qrope, krb[sb])
        s_all[sb] = s
        return jnp.max(s, axis=1, keepdims=True)

    state = (jnp.full((rows, 1), NEG_BIG, F32), jnp.zeros((rows, 1), F32), jnp.zeros((rows, LANES), F32))

    def absorb(state, sb, m_sb):
        m_run, l, acc = state
        m_new = jnp.maximum(m_run, m_sb)
        alpha = jnp.exp(m_run - m_new)
        p = jnp.exp(s_all[sb] - m_new)
        return (m_new, alpha * l + jnp.sum(p, axis=1, keepdims=True), alpha * acc + _dot(p.astype(BF16), cb[sb]))

    ahead = AHEAD
    maxes = []
    for sb in range(n_sub):
        maxes.append(scores(sb))
        if sb == 0:
            for cp in page_copies(nxt, 1 - slot):
                cp.start()
        if sb >= ahead:
            state = absorb(state, sb - ahead, maxes[sb - ahead])
    for sb in range(max(n_sub - ahead, 0), n_sub):
        state = absorb(state, sb, maxes[sb])
    m_run, l, acc = state

    qf = q128_ref[...].astype(F32)
    assert heads & (heads - 1) == 0
    tok = lax.shift_right_logical(lax.broadcasted_iota(jnp.int32, (rows, 1), 0), heads.bit_length() - 1)
    sn = []
    m_fin = m_run
    for t in range(s_new):
        kt = jnp.concatenate([k128n_ref[t].astype(F32)] * s_new, axis=0)
        st = jnp.sum(qf * kt, axis=1, keepdims=True)
        sn.append(jnp.where(tok >= t, st, NEG_BIG))
        m_fin = jnp.maximum(m_fin, sn[-1])
    alpha = jnp.exp(m_run - m_fin)
    l = alpha * l
    acc = alpha * acc
    cn = cn_ref[...].astype(BF16).astype(F32)
    for t in range(s_new):
        pt = jnp.exp(sn[t] - m_fin)
        l = l + pt
        acc = acc + pt.astype(BF16).astype(F32) * cn[t:t + 1, :]
    o_ref[...] = (acc / l).astype(BF16)

    @pl.when(b == nseq - 1)
    def _():
        tail = page_copies(nxt, 1 - slot)
        for cp in tail[0::2] + tail[1::2]:
            cp.wait()


def _attn_sample(page_table, qabs, qrope, q128, k128n, c_new, wukt, cache_c, cache_kr, *, layer, heads, dn):
    nseq, rows, _ = qabs.shape
    s_new = rows // heads
    n_pages = page_table.shape[1]
    rank = cache_c.shape[-1]
    rope = cache_kr.shape[-2]
    sub_pages = min(SUB_PAGES, n_pages)
    assert n_pages % sub_pages == 0
    n_sub = n_pages // sub_pages
    sub_keys = sub_pages * PAGE_SIZE
    body = functools.partial(_attn_sample_body, layer=layer, heads=heads, dn=dn, s_new=s_new, n_pages=n_pages,
                             sub_pages=sub_pages)
    per_seq = lambda shape: pl.BlockSpec((None,) + shape, lambda b, pt: (b,) + (0,) * len(shape))
    grid_spec = pltpu.PrefetchScalarGridSpec(
        num_scalar_prefetch=1,
        grid=(nseq,),
        in_specs=[per_seq((rows, LANES)), per_seq((rows, rope)), per_seq((rows, LANES)),
                  per_seq((s_new, heads, LANES)), per_seq((s_new, LANES)),
                  pl.BlockSpec(wukt.shape, lambda b, pt: (0, 0)),
                  pl.BlockSpec(memory_space=pl.ANY), pl.BlockSpec(memory_space=pl.ANY)],
        out_specs=per_seq((rows, LANES)),
        scratch_shapes=[pltpu.VMEM((2, n_pages * PAGE_SIZE, rank), F32),
                        pltpu.VMEM((2, n_pages, rope, PAGE_SIZE), F32),
                        pltpu.VMEM((n_sub, sub_keys, rank), BF16),
                        pltpu.VMEM((n_sub, rope, sub_keys), BF16),
                        pltpu.VMEM((n_sub, rows, sub_keys), F32),
                        pltpu.SemaphoreType.DMA((2, 2))],
    )
    return pl.pallas_call(
        body, grid_spec=grid_spec,
        out_shape=jax.ShapeDtypeStruct((nseq, rows, LANES), BF16),
        compiler_params=_params("arbitrary"),
        name="attn_sample",
    )(page_table.reshape(-1), qabs, qrope, q128, k128n, c_new, wukt, cache_c, cache_kr)


def _rope_cos_sin(pos, half):
    inv = ROPE_THETA ** (-jnp.arange(half, dtype=F32) / half)
    t = pos.shape[0]
    if (t * half) % LANES == 0:
        ang = jnp.repeat(pos, half).reshape(-1, LANES) * jnp.tile(inv, t).reshape(-1, LANES)
        return jnp.concatenate([jnp.cos(ang).reshape(t, half), jnp.sin(ang).reshape(t, half)], axis=1)
    ang = pos[:, None] * inv[None, :]
    return jnp.concatenate([jnp.cos(ang), jnp.sin(ang)], axis=1)


def _rope_placement(half, dn):
    j = np.arange(half)
    place = np.zeros((2 * half, 3 * LANES), np.float32)
    dr = 2 * half
    for base, kind in ((dn, 'c'), (dn + dr, 's'), (LANES + dn, 'c'), (LANES + dn + dr, 'c'),
                       (2 * LANES + dn, 's'), (2 * LANES + dn + dr, 's')):
        if kind == 'c':
            place[j, base + j] = 1.0
            place[j, base + half + j] = 1.0
        else:
            place[half + j, base + j] = -1.0
            place[half + j, base + half + j] = 1.0
    bias = np.zeros((1, 3 * LANES), np.float32)
    bias[0, :dn] = 1.0
    return jnp.asarray(place, BF16), jnp.asarray(bias)


def _swap_halves(a, axis=-1):
    lo, hi = jnp.split(a, 2, axis=axis)
    return jnp.concatenate([hi, lo], axis=axis)


def _prep_layer(lp, dims):
    heads, dn, dr, rank, g_heads, conv_dim, v_dim = dims
    assert dn + 2 * dr == LANES and rank == LANES and 2 * g_heads <= LANES
    w_in = lp['w_in']
    d = w_in.shape[0]
    o = 0
    w_qkv = w_in[:, o:o + conv_dim]; o += conv_dim
    w_z = w_in[:, o:o + v_dim]; o += v_dim
    w_b = w_in[:, o:o + g_heads]; o += g_heads
    w_a = w_in[:, o:o + g_heads]; o += g_heads
    w_q = w_in[:, o:o + heads * (dn + dr)].reshape(d, heads, dn + dr); o += heads * (dn + dr)
    w_c = w_in[:, o:o + rank]; o += rank
    w_kr = w_in[:, o:o + dr]
    zeros = lambda n: jnp.zeros((d, n), F32)
    w_q128 = jnp.concatenate([w_q, _swap_halves(w_q[:, :, dn:])], axis=2).reshape(d, heads * LANES)
    w_krs = _swap_halves(w_kr)
    w_all = jnp.concatenate([w_qkv, w_z, w_b, w_a, zeros(LANES - 2 * g_heads), w_q128, w_c,
                             zeros(dn), w_kr, w_kr, zeros(dn), w_krs, w_krs], axis=1).astype(BF16)
    scale = (dn + dr) ** -0.5
    qr_g = lp['mla_qr_norm']
    gq = (jnp.concatenate([lp['mla_qn_norm'], qr_g, _swap_halves(qr_g)]) * scale).reshape(1, LANES)
    lane = np.arange(LANES)
    seg = np.where(lane < dn, 0, np.where(lane < dn + dr, 1, 2))
    seg_len = np.where(lane < dn, dn, dr).astype(np.float32)
    mq = jnp.asarray(np.where(seg[:, None] == seg[None, :], 1.0 / seg_len[None, :], 0.0), BF16)
    mk = jnp.asarray(np.where((lane[:, None] < dn) & (lane[None, :] < dn), 1.0 / dn, 0.0), BF16)
    gk = jnp.concatenate([lp['mla_kn_norm'], jnp.zeros((LANES - dn,), F32)]).reshape(1, LANES)
    kr_g = lp['mla_kr_norm']
    zdn = jnp.zeros((dn,), F32)
    gkr = jnp.stack([jnp.concatenate([zdn, kr_g, kr_g]),
                     jnp.concatenate([zdn, _swap_halves(kr_g), _swap_halves(kr_g)])])
    w_uk = lp['mla_w_uk']
    wuk = jnp.concatenate([w_uk, jnp.zeros((rank, heads, LANES - dn), F32)], axis=2)
    wuk = wuk.reshape(rank, heads * LANES).astype(BF16)
    wukt = w_uk.transpose(1, 2, 0).reshape(heads * dn, rank).astype(BF16)
    absorb = jnp.concatenate([w_uk.transpose(1, 2, 0) * lp['mla_kn_norm'][None, :, None],
                              jnp.zeros((heads, LANES - dn, rank), F32)], axis=1)
    fold = np.zeros((LANES, LANES), np.float32)
    fold[dn + np.arange(dr), np.arange(dr)] = 1.0
    fold[dn + dr + np.arange(dr), np.arange(dr)] = 1.0
    wqa = jnp.concatenate([absorb, jnp.broadcast_to(jnp.asarray(fold), (heads, LANES, LANES))], axis=2).astype(BF16)
    w_uv = lp['mla_w_uv']
    mv = w_uv.shape[2]
    wuv_bd = (w_uv.transpose(1, 0, 2)[:, :, None, :] * jnp.asarray(np.eye(heads, dtype=np.float32))[:, None, :, None])
    wuv_bd = wuv_bd.reshape(heads * rank, heads * mv).astype(BF16)
    lane_h = lane // (LANES // 2)
    ms = jnp.asarray(np.where(lane_h[:, None] == lane_h[None, :], 1.0, 0.0), BF16)
    aparams = jnp.pad(jnp.stack([jnp.exp(lp['gdn_a_log']), lp['gdn_dt_bias']]),
                      ((0, 0), (g_heads, LANES - 2 * g_heads)))
    return dict(
        w_all=w_all, gq=gq, mq=mq, gc=lp['mla_ckv_norm'].reshape(1, LANES), wuk=wuk, mk=mk, gk=gk, gkr=gkr,
        wukt=wukt, wqa=wqa, wuv_bd=wuv_bd, ms=ms, aparams=aparams,
        wo_g=lp['w_out'][:v_dim].astype(BF16), wo_m=lp['w_out'][v_dim:].astype(BF16),
        f1=(lp['ffn1_wi'].astype(BF16), lp['ffn1_wo'].astype(BF16)),
        f2=(lp['ffn2_wi'].astype(BF16), lp['ffn2_wo'].astype(BF16)),
    )


def _layer(x, mods, rope, conv_prev, s0, lp, w, lay, dims, *, t_valid, chunk, attend):
    heads = dims[0]
    x = _ffn(x, mods, lp['norm_ffn1'], *w['f1'], mod0=0, tokens_per_seq=None)
    outs = _inproj(x, mods, lp['norm_mix'], w['w_all'], *rope, w['gq'], w['mq'], w['gc'], w['wuk'], w['mk'],
                   w['gk'], w['gkr'], None, (conv_prev, lp['gdn_conv_w'], w['ms']), lay=lay, heads=heads,
                   tokens_per_seq=None, rope=(dims[1], dims[2]))
    qkv, z, ba, c, krot, nconv = outs[0], outs[1], outs[2], outs[5], outs[6], outs[8]
    gdn_out, s_new = _gdn(qkv, z, ba, s0, w['aparams'], lp['gdn_norm'], t_valid=t_valid, chunk=chunk)
    ctx = attend(outs)
    x = _ffn(x, mods, lp['norm_ffn2'], *w['f2'], mod0=6, tokens_per_seq=None,
             mix=(gdn_out, ctx, w['wuv_bd'], w['wo_g'], w['wo_m']))
    return x, c, krot, nconv, s_new


def kernel(x_prompt, x_sample, cache_ckv, cache_krope, state_conv, state_gdn, page_table, c_prompt, c_sample,
           ada_w, ada_b, norm_ffn1, ffn1_wi, ffn1_wo, norm_mix, w_in, gdn_conv_w, gdn_a_log, gdn_dt_bias, gdn_norm,
           mla_qn_norm, mla_qr_norm, mla_ckv_norm, mla_kr_norm, mla_kn_norm, mla_w_uk, mla_w_uv, w_out,
           norm_ffn2, ffn2_wi, ffn2_wo):
    depth = ada_w.shape[0]
    bp, tp, d = x_prompt.shape
    bs, ts, _ = x_sample.shape
    g_heads, dk, dv = state_gdn.shape[2:]
    conv_dim = state_conv.shape[-1]
    v_dim = g_heads * dv
    rank, heads, dn = mla_w_uk.shape[1:]
    dr = mla_qr_norm.shape[1]
    past = page_table.shape[1] * PAGE_SIZE
    dims = (heads, dn, dr, rank, g_heads, conv_dim, v_dim)
    lay = _InLayout(conv_dim, v_dim, heads)

    cs_p = _rope_cos_sin(jnp.arange(tp, dtype=F32), dr // 2)
    cs_s = jnp.tile(_rope_cos_sin(past + jnp.arange(ts, dtype=F32), dr // 2), (bs, 1))
    place, tab_bias = _rope_placement(dr // 2, dn)
    cond = jnp.concatenate([c_prompt, c_sample], axis=0)
    rows = -(-cond.shape[0] // SUBLANES) * SUBLANES
    cond = jnp.pad(cond, ((0, rows - cond.shape[0]), (0, 0)))

    yp = x_prompt
    ys = x_sample.reshape(1, bs * ts, d)
    outs_p, outs_s = [], []
    for l in range(depth):
        lp = dict(norm_ffn1=norm_ffn1[l], ffn1_wi=ffn1_wi[l], ffn1_wo=ffn1_wo[l], norm_mix=norm_mix[l], w_in=w_in[l],
                  gdn_conv_w=gdn_conv_w[l], gdn_a_log=gdn_a_log[l], gdn_dt_bias=gdn_dt_bias[l], gdn_norm=gdn_norm[l],
                  mla_qn_norm=mla_qn_norm[l], mla_qr_norm=mla_qr_norm[l], mla_ckv_norm=mla_ckv_norm[l],
                  mla_kr_norm=mla_kr_norm[l], mla_kn_norm=mla_kn_norm[l], mla_w_uk=mla_w_uk[l],
                  mla_w_uv=mla_w_uv[l], w_out=w_out[l], norm_ffn2=norm_ffn2[l], ffn2_wi=ffn2_wi[l],
                  ffn2_wo=ffn2_wo[l])
        w = _prep_layer(lp, dims)
        mods = _mods(cond, ada_w[l], ada_b[l])
        mods_p = mods[:bp].reshape(bp, N_MOD, d)
        mods_s = mods[bp:bp + bs].reshape(bs, N_MOD, d).transpose(1, 0, 2)

        conv0 = jnp.zeros((bp, SUBLANES, conv_dim), F32)
        s0 = jnp.zeros((bp, g_heads, dk, dv), F32)
        attend_p = lambda o: _attn_prompt(o[3], o[4], o[7])
        yp, c_p, kr_p, cv_p, s_p = _layer(yp, mods_p, (cs_p, place, tab_bias), conv0, s0, lp, w, lay, dims,
                                          t_valid=min(GDN_TILE, tp), chunk=min(GDN_CHUNK, tp), attend=attend_p)
        outs_p.append((c_p, kr_p.transpose(0, 2, 1), cv_p[:, SUBLANES - (GDN_CONV - 1):], s_p))


        def attend_s(o, l=l, w=w):
            q128, k128, c_new, qabs, qrope = o[3], o[4], o[5], o[7], o[8]
            by_seq = lambda a: a[0].reshape(heads, bs, ts, -1).transpose(1, 2, 0, 3)
            flat = lambda a: by_seq(a).reshape(bs, ts * heads, -1)
            ctx = _attn_sample(page_table, flat(qabs), flat(qrope)[..., :dr], flat(q128), by_seq(k128),
                               c_new.reshape(bs, ts, -1), w['wukt'], cache_ckv, jnp.swapaxes(cache_krope, 2, 3),
                               layer=l, heads=heads, dn=dn)
            return ctx.reshape(1, bs * ts, heads * rank)

        ys, c_s, kr_s, cv_s, s_s = _layer_sample(ys, mods_s, (cs_s, place, tab_bias), state_conv[l], state_gdn[l], lp, w, lay, dims,
                                                 bs, ts, attend_s)
        outs_s.append((c_s, kr_s, cv_s, s_s))

    stack = lambda outs, k: jnp.stack([o[k] for o in outs])
    ys = ys.reshape(bs, ts, d)
    return (yp, ys, stack(outs_p, 0), stack(outs_p, 1), stack(outs_p, 2), stack(outs_p, 3),
            stack(outs_s, 0), stack(outs_s, 1), stack(outs_s, 2), stack(outs_s, 3))


def _layer_sample(x, mods, rope, conv_prev, s0, lp, w, lay, dims, bs, ts, attend):
    heads, dn, dr, g_heads = dims[0], dims[1], dims[2], dims[4]
    x = _ffn(x, mods, lp['norm_ffn1'], *w['f1'], mod0=0, tokens_per_seq=ts)
    outs = _inproj(x, mods, lp['norm_mix'], w['w_all'], *rope, w['gq'], w['mq'], w['gc'], w['wuk'], w['mk'],
                   w['gk'], w['gkr'], w['wqa'], None, lay=lay, heads=heads, tokens_per_seq=ts, rope=(dn, dr))
    qkv, z, ba, c, krot = outs[0], outs[1], outs[2], outs[5], outs[6]
    seq = lambda a: a.reshape(bs, ts, a.shape[-1])
    minor = lambda a: seq(a).transpose(1, 2, 0)
    o_t, s_new = _gdn_sample(minor(qkv), conv_prev.transpose(1, 2, 0), minor(z), minor(ba[..., :g_heads]),
                             minor(ba[..., g_heads:2 * g_heads]), s0.transpose(1, 2, 3, 0), lp['gdn_conv_w'].T,
                             w['aparams'][:, g_heads:2 * g_heads], lp['gdn_norm'])
    gdn_out = o_t.transpose(2, 0, 1).reshape(1, bs * ts, -1)
    ctx = attend(outs)
    x = _ffn(x, mods, lp['norm_ffn2'], *w['f2'], mod0=6, tokens_per_seq=ts,
             mix=(gdn_out, ctx, w['wuv_bd'], w['wo_g'], w['wo_m']))
    nconv = jnp.concatenate([conv_prev, seq(qkv)], axis=1)[:, -(GDN_CONV - 1):]
    return x, seq(c), seq(krot)[..., dn:dn + dr], nconv, s_new.transpose(3, 0, 1, 2)
```

```python
import functools
import math

import jax
import jax.numpy as jnp
import numpy as np
from jax import lax
from jax.experimental import pallas as pl
from jax.experimental.pallas import tpu as pltpu

F32 = jnp.float32
BF16 = jnp.bfloat16

NORM_EPS = 1e-6
ROPE_THETA = 10000.0
PAGE_SIZE = 128
GDN_CONV = 4
GDN_CHUNK = 128
N_MOD = 9
LANES = 128
SUBLANES = 8
NEG_BIG = -1e30
VMEM_LIMIT = 56 * 1024 * 1024

TOKEN_TILE = 512
ATTN_TILE = 512
GDN_TILE = 512
SUB_PAGES = 16
AHEAD = 4


def _dot(a, b):
    return jnp.dot(a, b, preferred_element_type=F32)


def _dot_t(a, b):
    return lax.dot_general(a, b, (((1,), (1,)), ((), ())), preferred_element_type=F32)


def _dot_ta(a, b):
    return lax.dot_general(a, b, (((0,), (0,)), ((), ())), preferred_element_type=F32)


def _bmm(a, b):
    return lax.dot_general(a, b, (((2,), (1,)), ((0,), (0,))), preferred_element_type=F32)


def _bmm_t(a, b):
    return lax.dot_general(a, b, (((2,), (2,)), ((0,), (0,))), preferred_element_type=F32)


def _sigmoid(x):
    return 1.0 / (1.0 + jnp.exp(-x))


def _silu(x):
    return x * _sigmoid(x)


def _params(*sem):
    return pltpu.CompilerParams(dimension_semantics=sem, vmem_limit_bytes=VMEM_LIMIT)


def _resident(shape):
    nd = len(shape)
    return pl.BlockSpec(shape, lambda *_: (0,) * nd, pipeline_mode=pl.Buffered(1))


MODS_PER_SUBLAYER = 3


def _dot_split(x, e, terms, left=False):
    out = None
    for _ in range(terms):
        xb = x.astype(BF16)
        d = _dot(e, xb) if left else _dot(xb, e)
        out = d if out is None else out + d
        x = x - xb.astype(F32)
    return out


def _mod_rows(m_ref, ks, tokens_per_seq, rows):
    if tokens_per_seq is None:
        return [m_ref[k:k + 1, :] for k in ks]
    nseq = m_ref.shape[1]
    row = lax.broadcasted_iota(jnp.int32, (rows, nseq), 0)
    first = lax.broadcasted_iota(jnp.int32, (rows, nseq), 1) * tokens_per_seq
    spread = jnp.where((row >= first) & (row < first + tokens_per_seq), 1.0, 0.0).astype(BF16)
    return [_dot_split(m_ref[k % MODS_PER_SUBLAYER], spread, 3, left=True) for k in ks]


def _mod_norm(x, gain, shift, scale):
    y = x * lax.rsqrt(jnp.mean(x * x, axis=-1, keepdims=True) + NORM_EPS)
    return (y * gain) * (1.0 + scale) + shift


def _mods_spec(tokens_per_seq, tm, d, sublayer):
    if tokens_per_seq is not None:
        assert tm % tokens_per_seq == 0
        return pl.BlockSpec((MODS_PER_SUBLAYER, tm // tokens_per_seq, d), lambda b, i: (sublayer, i, 0))
    return pl.BlockSpec((None, N_MOD, d), lambda b, i: (b, 0, 0))


def _mods_body(c_ref, w_ref, b_ref, o_ref):
    c = c_ref[...]
    o_ref[...] = _dot(_silu(c).astype(BF16), w_ref[...].astype(BF16)) + b_ref[...]


def _mods(cond, ada_w, ada_b):
    rows, d = cond.shape
    n = ada_w.shape[1]
    tn = d
    return pl.pallas_call(
        _mods_body,
        grid=(n // tn,),
        in_specs=[pl.BlockSpec((rows, d), lambda j: (0, 0)),
                  pl.BlockSpec((d, tn), lambda j: (0, j)),
                  pl.BlockSpec((1, tn), lambda j: (0, j))],
        out_specs=pl.BlockSpec((rows, tn), lambda j: (0, j)),
        out_shape=jax.ShapeDtypeStruct((rows, n), F32),
        compiler_params=_params("arbitrary"),
        name="mods",
    )(cond, ada_w, ada_b.reshape(1, n))


def _ffn_body(x_ref, m_ref, g_ref, wi_ref, wo_ref, *rest, mod0, tokens_per_seq, fc, mixed):
    x = x_ref[...]
    if mixed:
        mm_ref, gdn_ref, ctx_ref, wuv_ref, wog_ref, wom_ref, o_ref = rest
        gate_k = mod0 - 1
        (gate_mix,) = _mod_rows(mm_ref if tokens_per_seq is not None else m_ref, [gate_k], tokens_per_seq, x.shape[0])
        mla = _dot(ctx_ref[...], wuv_ref[...])
        mix = _dot(gdn_ref[...].astype(BF16), wog_ref[...]) + _dot(mla.astype(BF16), wom_ref[...])
        x = x + gate_mix * mix
    else:
        (o_ref,) = rest
    shift, scale, gate = _mod_rows(m_ref, [mod0, mod0 + 1, mod0 + 2], tokens_per_seq, x.shape[0])
    h = _mod_norm(x, g_ref[...], shift, scale).astype(BF16)
    acc = jnp.zeros(x.shape, F32)
    dff = wo_ref.shape[0]
    for c in range(dff // fc):
        lo, hi = c * fc, (c + 1) * fc
        a = _silu(_dot(h, wi_ref[:, lo:hi])) * _dot(h, wi_ref[:, dff + lo:dff + hi])
        acc = acc + _dot(a.astype(BF16), wo_ref[lo:hi, :])
    o_ref[...] = x + 0.5 * gate * acc


def _ffn(x, mods, gain, wi, wo, *, mod0, tokens_per_seq, mix=None):
    nb, t, d = x.shape
    tm = min(TOKEN_TILE, t)
    dff = wo.shape[0]
    fc = 2 * LANES
    assert t % tm == 0 and dff % fc == 0
    body = functools.partial(_ffn_body, mod0=mod0, tokens_per_seq=tokens_per_seq, fc=fc, mixed=mix is not None)
    tok = lambda w: pl.BlockSpec((None, tm, w), lambda b, i: (b, i, 0))
    sub = mod0 // MODS_PER_SUBLAYER
    in_specs = [tok(d), _mods_spec(tokens_per_seq, tm, d, sub),
                _resident((1, d)), _resident((d, 2 * dff)), _resident((dff, d))]
    args = [x, mods, gain.reshape(1, d), wi, wo]
    if mix is not None:
        gdn_out, ctx, wuv_bd, wo_g, wo_m = mix
        in_specs += [_mods_spec(tokens_per_seq, tm, d, sub - 1), tok(gdn_out.shape[-1]), tok(ctx.shape[-1]),
                     _resident(wuv_bd.shape), _resident(wo_g.shape), _resident(wo_m.shape)]
        args += [mods, gdn_out, ctx, wuv_bd, wo_g, wo_m]
    return pl.pallas_call(
        body,
        grid=(nb, t // tm),
        in_specs=in_specs,
        out_specs=tok(d),
        out_shape=jax.ShapeDtypeStruct(x.shape, F32),
        compiler_params=_params("arbitrary", "arbitrary"),
        name="ffn_mix" if mix is not None else "ffn",
    )(*args)


class _InLayout:
    def __init__(self, conv_dim, v_dim, heads):
        self.qkv = (0, conv_dim)
        self.z = (conv_dim, conv_dim + v_dim)
        self.ba = (self.z[1], self.z[1] + LANES)
        self.q = (self.ba[1], self.ba[1] + heads * LANES)
        self.ckv = (self.q[1], self.q[1] + LANES)
        self.kr = (self.ckv[1], self.ckv[1] + LANES)
        self.krs = (self.kr[1], self.kr[1] + LANES)
        self.total = self.krs[1]


def _inproj_body(x_ref, m_ref, g_ref, w_ref, cs_ref, place_ref, tbias_ref, gq_ref, mq_ref, gc_ref, wuk_ref, mk_ref,
                 gk_ref, gkr_ref, *rest, lay, heads, tokens_per_seq, sample, qk_dim, rope):
    if sample:
        wqa_ref, rest = rest[0], rest[1:]
    else:
        cprev_ref, cw_ref, ms_ref, rest = rest[0], rest[1], rest[2], rest[3:]
    qkv_ref, z_ref, ba_ref, q128_ref, k128_ref, c_ref, krot_ref = rest[:7]
    x = x_ref[...]
    shift, scale = _mod_rows(m_ref, [3, 4], tokens_per_seq, x.shape[0])
    h = _mod_norm(x, g_ref[...], shift, scale).astype(BF16)
    cw = 4 * LANES
    assert lay.total % cw == 0 and lay.qkv[1] % cw == 0 and (lay.z[1] - lay.z[0]) == cw
    n_chunks = lay.total // cw
    n_qkv = lay.qkv[1] // cw
    rest_chunks = list(range(n_qkv, n_chunks))
    order = []
    for c in range(max(n_qkv, len(rest_chunks))):
        order += ([c] if c < n_qkv else []) + ([rest_chunks[c]] if c < len(rest_chunks) else [])
    chunks = {}

    def cols(lo, hi):
        c = lo // cw
        assert (hi - 1) // cw == c
        return chunks[c][:, lo - c * cw:hi - c * cw]

    tm = x.shape[0]
    hist = SUBLANES
    taps = GDN_CONV - 1
    if not sample:
        ct_ref, nconv_ref, xbuf = rest[7], rest[8], rest[9]

        @pl.when(pl.program_id(1) == 0)
        def _():
            xbuf[0:hist, :] = cprev_ref[...]

    tab = _dot_split(cs_ref[...], place_ref[...], 3) + tbias_ref[...]
    tab_q, tab_c, tab_s = tab[:, :LANES], tab[:, LANES:2 * LANES], tab[:, 2 * LANES:]

    def group(lo):
        hi = lo + LANES
        if lo < lay.qkv[1]:
            if sample:
                qkv_ref[:, lo:hi] = cols(lo, hi)
                return
            xbuf[hist:hist + tm, lo:hi] = cols(lo, hi)
            conv = xbuf[hist - taps:hist - taps + tm, lo:hi] * cw_ref[0:1, lo:hi]
            for jj in range(1, GDN_CONV):
                conv = conv + xbuf[hist - taps + jj:hist - taps + jj + tm, lo:hi] * cw_ref[jj:jj + 1, lo:hi]
            yi = _silu(conv)
            if lo < 2 * qk_dim:
                ss = _dot((yi * yi).astype(BF16), ms_ref[...])
                yi = yi * (lax.rsqrt(ss + NORM_EPS) * ((LANES // 2) ** -0.5 if lo < qk_dim else 1.0))
            qkv_ref[:, lo:hi] = yi
            tail = xbuf[tm:tm + hist, lo:hi]
            nconv_ref[:, lo:hi] = tail
            xbuf[0:hist, lo:hi] = tail
        elif lo < lay.z[1]:
            z_ref[:, lo - lay.z[0]:hi - lay.z[0]] = cols(lo, hi)
        elif lo < lay.ba[1]:
            ba_ref[...] = cols(lo, hi)
        elif lo < lay.q[1]:
            hh = (lo - lay.q[0]) // LANES
            qh = cols(lo, hi)
            msq = _dot((qh * qh).astype(BF16), mq_ref[...])
            qn = qh * lax.rsqrt(msq + NORM_EPS) * gq_ref[...] * tab_q
            q128_ref[hh] = qn.astype(BF16)
            if sample:
                qabs_ref, qrope_ref = rest[7], rest[8]
                qa = _dot(qn.astype(BF16), wqa_ref[hh])
                qabs_ref[hh] = qa[:, :LANES].astype(BF16)
                qrope_ref[hh] = qa[:, LANES:2 * LANES].astype(BF16)
        elif lo == lay.ckv[0]:
            keys()

    def keys():
        ckv = cols(*lay.ckv)
        c = ckv * lax.rsqrt(jnp.mean(ckv * ckv, axis=-1, keepdims=True) + NORM_EPS) * gc_ref[...]
        c_ref[...] = c
        cb = c.astype(BF16)
        if not sample:
            ct_ref[...] = jnp.concatenate([c.T, jnp.ones((SUBLANES, c.shape[0]), F32)], axis=0).astype(BF16)
        kr = cols(*lay.kr)
        krs = cols(*lay.krs)
        inv = lax.rsqrt(jnp.sum(kr * kr, axis=-1, keepdims=True) * (2.0 / LANES) + NORM_EPS)
        krot = kr * inv * gkr_ref[0:1, :] * tab_c + krs * inv * gkr_ref[1:2, :] * tab_s
        if sample:
            krot_ref[...] = krot
        else:
            krot_ref[...] = krot.T[rope[0]:rope[0] + rope[1], :]
        knr = _dot(cb, wuk_ref[...])
        for hh in range(heads):
            kh = knr[:, hh * LANES:(hh + 1) * LANES]
            msq = _dot((kh * kh).astype(BF16), mk_ref[...])
            k128_ref[hh] = (kh * lax.rsqrt(msq + NORM_EPS) * gk_ref[...] + krot).astype(BF16)

    ahead = 2
    for i in range(len(order) + ahead):
        if i < len(order):
            k = order[i]
            chunks[k] = _dot(h, w_ref[:, k * cw:(k + 1) * cw])
        if i >= ahead:
            k = order[i - ahead]
            for lo in range(k * cw, (k + 1) * cw, LANES):
                group(lo)


def _inproj(x, mods, gain, w_all, cs, place, tab_bias, gq, mq, gc, wuk, mk, gk, gkr, wqa, conv, *, lay, heads,
            tokens_per_seq, rope):
    nb, t, d = x.shape
    tm = min(TOKEN_TILE, t)
    assert t % tm == 0
    sample = wqa is not None
    body = functools.partial(_inproj_body, lay=lay, heads=heads, tokens_per_seq=tokens_per_seq, sample=sample,
                             qk_dim=(lay.qkv[1] - (lay.z[1] - lay.z[0])) // 2, rope=rope)
    tok = lambda w: pl.BlockSpec((None, tm, w), lambda b, i: (b, i, 0))
    hd = lambda w: pl.BlockSpec((None, heads, tm, w), lambda b, i: (b, 0, i, 0))
    conv_dim, v_dim = lay.qkv[1], lay.z[1] - lay.z[0]
    in_specs = [tok(d), _mods_spec(tokens_per_seq, tm, d, 1), _resident((1, d)), _resident(w_all.shape),
                pl.BlockSpec((tm, cs.shape[1]), lambda b, i: (i, 0)), _resident(place.shape),
                _resident(tab_bias.shape),
                _resident((1, LANES)), _resident((LANES, LANES)), _resident((1, LANES)),
                _resident(wuk.shape), _resident((LANES, LANES)), _resident((1, LANES)), _resident((2, LANES))]
    args = [x, mods, gain.reshape(1, d), w_all, cs, place, tab_bias, gq, mq, gc, wuk, mk, gk, gkr]
    out_specs = [tok(conv_dim), tok(v_dim), tok(LANES), hd(LANES), hd(LANES), tok(LANES), tok(LANES)]
    out_shape = [jax.ShapeDtypeStruct((nb, t, conv_dim), F32), jax.ShapeDtypeStruct((nb, t, v_dim), F32),
                 jax.ShapeDtypeStruct((nb, t, LANES), F32),
                 jax.ShapeDtypeStruct((nb, heads, t, LANES), BF16), jax.ShapeDtypeStruct((nb, heads, t, LANES), BF16),
                 jax.ShapeDtypeStruct((nb, t, LANES), F32), jax.ShapeDtypeStruct((nb, t, LANES), F32)]
    if not sample:
        out_specs[6] = pl.BlockSpec((None, rope[1], tm), lambda b, i: (b, 0, i))
        out_shape[6] = jax.ShapeDtypeStruct((nb, rope[1], t), F32)
    if sample:
        in_specs.append(_resident(wqa.shape))
        args.append(wqa)
        out_specs += [hd(LANES), hd(LANES)]
        out_shape += [jax.ShapeDtypeStruct((nb, heads, t, LANES), BF16)] * 2
        scratch = []
    else:
        conv_prev, conv_w, ms = conv
        in_specs += [pl.BlockSpec((None, SUBLANES, conv_dim), lambda b, i: (b, 0, 0)),
                     _resident(conv_w.shape), _resident(ms.shape)]
        args += [conv_prev, conv_w, ms]
        out_specs += [pl.BlockSpec((None, LANES + SUBLANES, tm), lambda b, i: (b, 0, i)),
                      pl.BlockSpec((None, SUBLANES, conv_dim), lambda b, i: (b, 0, 0))]
        out_shape += [jax.ShapeDtypeStruct((nb, LANES + SUBLANES, t), BF16),
                      jax.ShapeDtypeStruct((nb, SUBLANES, conv_dim), F32)]
        scratch = [pltpu.VMEM((tm + SUBLANES, conv_dim), F32)]
    return pl.pallas_call(
        body, grid=(nb, t // tm), in_specs=in_specs, out_specs=out_specs, out_shape=out_shape,
        scratch_shapes=scratch, compiler_params=_params("arbitrary", "arbitrary"), name="inproj",
    )(*args)


def _unit_lower_inverse(a):
    n = a.shape[-1]
    assert n & (n - 1) == 0
    row = lax.broadcasted_iota(jnp.int32, (n, n), 0)
    col = lax.broadcasted_iota(jnp.int32, (n, n), 1)
    x = jnp.broadcast_to(jnp.where(row == col, 1.0, 0.0), a.shape)
    b = 1
    while b < n:
        lo_mask = (jnp.bitwise_xor(row, col) < 2 * b) & (jnp.bitwise_and(row, b) != 0) & (jnp.bitwise_and(col, b) == 0)
        lo = jnp.where(lo_mask, a, 0.0)
        if b == 1:
            x = x - lo
        else:
            xb = x.astype(BF16)
            x = x - _bmm(xb, _bmm(lo.astype(BF16), xb).astype(BF16))
        b *= 2
    return x


def _gdn_body(qkv_ref, z_ref, ba_ref, s0_ref, ap_ref, ng_ref, o_ref, sout_ref, s_scr,
              *, tt, t_valid, chunk, heads, dk, dv):
    j = pl.program_id(1)

    @pl.when(j == 0)
    def _():
        s_scr[...] = s0_ref[...]

    qk_dim = heads * dk
    y = qkv_ref[...]
    q = y[:, :qk_dim]
    k = y[:, qk_dim:2 * qk_dim]
    v = y[:, 2 * qk_dim:]
    ba = ba_ref[...]
    beta = _sigmoid(ba)
    xg = ba + ap_ref[1:2, :]
    g = -ap_ref[0:1, :] * (jnp.maximum(xg, 0.0) + jnp.log1p(jnp.exp(-jnp.abs(xg))))
    if t_valid < tt:
        keep = lax.broadcasted_iota(jnp.int32, (tt, 1), 0) < t_valid
        q, k, v = (jnp.where(keep, a, 0.0) for a in (q, k, v))
        beta, g = jnp.where(keep, beta, 0.0), jnp.where(keep, g, 0.0)
    row = lax.broadcasted_iota(jnp.int32, (tt, tt), 0)
    col = lax.broadcasted_iota(jnp.int32, (tt, tt), 1)
    assert chunk & (chunk - 1) == 0
    tri = jnp.where((row >= col) & (jnp.bitwise_xor(row, col) < chunk), 1.0, 0.0).astype(BF16)
    g_hi = g.astype(BF16)
    g_lo = (g - g_hi.astype(F32)).astype(BF16)
    gc = _dot(tri, g_hi) + _dot(tri, g_lo)
    gct = gc.T
    z = z_ref[...]

    crow = lax.broadcasted_iota(jnp.int32, (chunk, chunk), 0)
    ccol = lax.broadcasted_iota(jnp.int32, (chunk, chunk), 1)
    causal = crow >= ccol
    strict = crow > ccol
    stack = lambda xs: jnp.stack(xs, axis=0)

    n_chunks = tt // chunk
    parts = []

    def prepare(c):
        r0, r1 = c * chunk, (c + 1) * chunk
        qkb, kbf, vbeta, kbeg, qdec, kdec, decay, gl, zc = ([] for _ in range(9))
        for h in range(heads):
            qh = q[r0:r1, h * dk:(h + 1) * dk]
            kh = k[r0:r1, h * dk:(h + 1) * dk]
            vh = v[r0:r1, h * dv:(h + 1) * dv]
            bh = beta[r0:r1, h:h + 1]
            gcol = gc[r0:r1, heads + h:heads + h + 1]
            grow = gct[heads + h:heads + h + 1, r0:r1]
            glast = grow[:, chunk - 1:chunk]
            eg = jnp.exp(gcol)
            kb = kh * bh
            qkb.append(jnp.concatenate([qh, kb], axis=0).astype(BF16))
            kbf.append(kh.astype(BF16))
            vbeta.append((vh * bh).astype(BF16))
            kbeg.append((kb * eg).astype(BF16))
            qdec.append((qh * eg).astype(BF16))
            kdec.append((kh * jnp.exp(glast - gcol)).astype(BF16))
            decay.append(jnp.exp(jnp.where(causal, jnp.broadcast_to(gcol, (chunk, chunk)) - grow, NEG_BIG)))
            gl.append(jnp.exp(glast))
            zc.append(z[r0:r1, h * dv:(h + 1) * dv])
        decay3 = stack(decay)
        sc = _bmm_t(stack(qkb), stack(kbf))
        qkm = (sc[:, :chunk] * decay3).astype(BF16)
        m = jnp.where(strict, sc[:, chunk:] * decay3, 0.0)
        return (m, qkm, stack(vbeta), stack(kbeg), stack(qdec), stack(kdec), stack(gl), stack(zc))

    group = 2
    tinvs = []
    for c0 in range(0, n_chunks, group):
        cs = list(range(c0, min(c0 + group, n_chunks)))
        parts += [prepare(c) for c in cs]
        inv = _unit_lower_inverse(jnp.concatenate([parts[c][0] for c in cs], axis=0)).astype(BF16)
        tinvs += [inv[i * heads:(i + 1) * heads] for i in range(len(cs))]
    s3 = s_scr[...]
    for c in range(n_chunks):
        r0, r1 = c * chunk, (c + 1) * chunk
        _, qkm, vb3, kbeg3, qd3, kd3, gl3, z3 = parts[c]
        tinv = tinvs[c]
        u = _bmm(tinv, vb3)
        w = _bmm(tinv, kbeg3)
        sb = s3.astype(BF16)
        ws_qs = _bmm(jnp.concatenate([w.astype(BF16), qd3], axis=1), sb)
        vnb = (u - ws_qs[:, :chunk]).astype(BF16)
        o = ws_qs[:, chunk:] + _bmm(qkm, vnb)
        upd = stack([_dot_ta(kd3[h], vnb[h]) for h in range(heads)])
        s3 = s3 * gl3 + upd
        on = o * lax.rsqrt(jnp.mean(o * o, axis=-1, keepdims=True) + NORM_EPS) * ng_ref[...] * _silu(z3)
        for h in range(heads):
            o_ref[r0:r1, h * dv:(h + 1) * dv] = on[h]
    s_scr[...] = s3
    sout_ref[...] = s3


def _gdn(qkv, z, ba, s0, aparams, norm_g, *, t_valid, chunk):
    nb, t, conv_dim = qkv.shape
    heads, dk, dv = s0.shape[1:]
    tt = min(GDN_TILE, t)
    assert t % tt == 0 and tt % chunk == 0 and (t_valid == tt or t == tt)
    body = functools.partial(_gdn_body, tt=tt, t_valid=t_valid, chunk=chunk, heads=heads, dk=dk, dv=dv)
    v_dim = heads * dv
    tok = lambda w: pl.BlockSpec((None, tt, w), lambda b, i: (b, i, 0))
    return pl.pallas_call(
        body,
        grid=(nb, t // tt),
        in_specs=[tok(conv_dim), tok(v_dim), tok(LANES),
                  pl.BlockSpec((None, heads, dk, dv), lambda b, i: (b, 0, 0, 0)),
                  _resident((2, LANES)), _resident((1, dv))],
        out_specs=[tok(v_dim), pl.BlockSpec((None, heads, dk, dv), lambda b, i: (b, 0, 0, 0))],
        out_shape=[jax.ShapeDtypeStruct((nb, t, v_dim), F32), jax.ShapeDtypeStruct((nb, heads, dk, dv), F32)],
        scratch_shapes=[pltpu.VMEM((heads, dk, dv), F32)],
        compiler_params=_params("arbitrary", "arbitrary"),
        name="gdn",
    )(qkv, z, ba, s0, aparams, norm_g.reshape(1, dv))


def _gdn_sample_body(xq_ref, xk_ref, xv_ref, pq_ref, pk_ref, pv_ref, wq_ref, wk_ref, wv_ref, z_ref, b_ref, a_ref,
                     ap_ref, ng_ref, s_ref, o_ref, sout_ref, kq_scr, *, ts, heads, dk, dv):
    h = pl.program_id(0)
    nb = xq_ref.shape[-1]

    def conv(x_ref, p_ref, w_ref):
        taps = GDN_CONV - 1
        xin = [p_ref[i] for i in range(taps)] + [x_ref[t] for t in range(ts)]
        w = [jnp.broadcast_to(w_ref[:, jj:jj + 1], xin[0].shape) for jj in range(GDN_CONV)]
        out = []
        for t in range(ts):
            acc = xin[t] * w[0]
            for jj in range(1, GDN_CONV):
                acc = acc + xin[t + jj] * w[jj]
            out.append(_silu(acc))
        return out

    def l2(x):
        return x * lax.rsqrt(jnp.sum(x * x, axis=0, keepdims=True) + NORM_EPS)

    q = [l2(x) * (dk ** -0.5) for x in conv(xq_ref, pq_ref, wq_ref)]
    k = [l2(x) for x in conv(xk_ref, pk_ref, wk_ref)]
    v = conv(xv_ref, pv_ref, wv_ref)
    for t in range(ts):
        kq_scr[t] = k[t]
        kq_scr[ts + t] = q[t]
    a_h = ap_ref[0, h]
    dt_h = ap_ref[1, h]
    beta, decay = [], []
    for t in range(ts):
        beta.append(_sigmoid(b_ref[t, pl.ds(h, 1), :]))
        xg = a_ref[t, pl.ds(h, 1), :] + dt_h
        decay.append(jnp.exp(-a_h * (jnp.maximum(xg, 0.0) + jnp.log1p(jnp.exp(-jnp.abs(xg))))))

    def row(i, kk):
        return jnp.broadcast_to(kq_scr[i, pl.ds(kk, 1), :], (dv, nb))

    def first(kk, ks):
        return ks + row(0, kk) * s_ref[kk]

    ks = lax.fori_loop(0, dk, first, jnp.zeros((dv, nb), F32), unroll=8)
    ng = jnp.broadcast_to(ng_ref[...], (dv, nb))
    for t in range(ts):
        d = beta[t] * (v[t] - decay[t] * ks)
        src = s_ref if t == 0 else sout_ref
        last = t == ts - 1

        def step(kk, carry, t=t, d=d, src=src, last=last):
            o_acc, ks_next = carry
            s_new = decay[t] * src[kk] + row(t, kk) * d
            sout_ref[kk] = s_new
            o_acc = o_acc + row(ts + t, kk) * s_new
            if not last:
                ks_next = ks_next + row(t + 1, kk) * s_new
            return o_acc, ks_next

        zero = jnp.zeros((dv, nb), F32)
        o, ks = lax.fori_loop(0, dk, step, (zero, zero), unroll=8)
        on = o * lax.rsqrt(jnp.mean(o * o, axis=0, keepdims=True) + NORM_EPS) * ng
        o_ref[t] = on * _silu(z_ref[t])


def _gdn_sample(x_t, prev_t, z_t, b_t, a_t, s_t, conv_w_t, aparams, norm_g):
    ts, conv_dim, nb = x_t.shape
    heads, dk, dv, _ = s_t.shape
    assert dk == dv and conv_dim == 3 * heads * dk
    taps = GDN_CONV - 1
    body = functools.partial(_gdn_sample_body, ts=ts, heads=heads, dk=dk, dv=dv)
    part = lambda rows, off: pl.BlockSpec((rows, dk, nb), lambda h: (0, off + h, 0))
    wpart = lambda off: pl.BlockSpec((dk, GDN_CONV), lambda h: (off + h, 0))
    whole = lambda a: pl.BlockSpec(a.shape, lambda h: (0,) * a.ndim)
    state = pl.BlockSpec((None, dk, dv, nb), lambda h: (h, 0, 0, 0))
    return pl.pallas_call(
        body,
        grid=(heads,),
        in_specs=[part(ts, 0), part(ts, heads), part(ts, 2 * heads),
                  part(taps, 0), part(taps, heads), part(taps, 2 * heads),
                  wpart(0), wpart(heads), wpart(2 * heads),
                  part(ts, 0), whole(b_t), whole(a_t),
                  pl.BlockSpec(memory_space=pltpu.SMEM), pl.BlockSpec((dv, 1), lambda h: (0, 0)), state],
        out_specs=[part(ts, 0), state],
        out_shape=[jax.ShapeDtypeStruct((ts, heads * dv, nb), F32), jax.ShapeDtypeStruct(s_t.shape, F32)],
        scratch_shapes=[pltpu.VMEM((2 * ts, dk, nb), F32)],
        compiler_params=_params("arbitrary"),
        name="gdn_sample",
    )(x_t, x_t, x_t, prev_t, prev_t, prev_t, conv_w_t, conv_w_t, conv_w_t, z_t, b_t, a_t, aparams,
      norm_g.reshape(dv, 1), s_t)


def _attn_body(qi_ref, kj_ref, q_ref, k_ref, ct_ref, o_ref, m_scr, acc_scr, *, heads, tq, rank):
    i = qi_ref[pl.program_id(1)]
    j = kj_ref[pl.program_id(1)]

    @pl.when(j == 0)
    def _():
        m_scr[...] = jnp.full(m_scr.shape, NEG_BIG, F32)
        acc_scr[...] = jnp.zeros(acc_scr.shape, F32)

    def step(masked):
        ct = ct_ref[...]
        m_old = [m_scr[h] for h in range(heads)]
        acc_old = [acc_scr[h] for h in range(heads)]
        if masked:
            key = lax.broadcasted_iota(jnp.int32, (tq, tq), 0)
            qry = lax.broadcasted_iota(jnp.int32, (tq, tq), 1)
            keep = key <= qry
        m_out, acc_out = [], []
        ahead = 3
        scores = [_dot_t(k_ref[h], q_ref[h]) for h in range(ahead)]
        for h in range(heads):
            if h + ahead < heads:
                scores.append(_dot_t(k_ref[h + ahead], q_ref[h + ahead]))
            st = scores[h]
            if masked:
                st = jnp.where(keep, st, NEG_BIG)
            m_new = jnp.maximum(m_old[h], jnp.max(st, axis=0, keepdims=True))
            alpha = jnp.exp(m_old[h] - m_new)
            pt = jnp.exp(st - m_new).astype(BF16)
            acc_out.append(acc_old[h] * alpha + _dot(ct, pt))
            m_out.append(m_new)
        for h in range(heads):
            m_scr[h] = m_out[h]
            acc_scr[h] = acc_out[h]
        return acc_out

    @pl.when(j < i)
    def _():
        step(False)

    @pl.when(j == i)
    def _():
        acc = step(True)
        for h in range(heads):
            ctx = acc[h][:rank, :] / acc[h][rank:rank + 1, :]
            o_ref[:, h * rank:(h + 1) * rank] = ctx.T.astype(BF16)


def _attn_prompt(q128, k128, ct_ext):
    nb, heads, t, _ = q128.shape
    rows = ct_ext.shape[1]
    rank = rows - SUBLANES
    tq = min(ATTN_TILE, t)
    assert t % tq == 0
    n = t // tq
    body = functools.partial(_attn_body, heads=heads, tq=tq, rank=rank)
    pairs = [(i, j) for i in range(n) for j in range(i + 1)]
    qi = jnp.asarray([p[0] for p in pairs], jnp.int32)
    kj = jnp.asarray([p[1] for p in pairs], jnp.int32)
    grid_spec = pltpu.PrefetchScalarGridSpec(
        num_scalar_prefetch=2,
        grid=(nb, len(pairs)),
        in_specs=[pl.BlockSpec((None, heads, tq, LANES), lambda b, s, qi, kj: (b, 0, qi[s], 0)),
                  pl.BlockSpec((None, heads, tq, LANES), lambda b, s, qi, kj: (b, 0, kj[s], 0)),
                  pl.BlockSpec((None, rows, tq), lambda b, s, qi, kj: (b, 0, kj[s]))],
        out_specs=pl.BlockSpec((None, tq, heads * rank), lambda b, s, qi, kj: (b, qi[s], 0)),
        scratch_shapes=[pltpu.VMEM((heads, 1, tq), F32), pltpu.VMEM((heads, rows, tq), F32)],
    )
    return pl.pallas_call(
        body, grid_spec=grid_spec,
        out_shape=jax.ShapeDtypeStruct((nb, t, heads * rank), BF16),
        compiler_params=_params("arbitrary", "arbitrary"),
        name="attn_prompt",
    )(qi, kj, q128, k128, ct_ext)


def _attn_sample_body(pt_ref, qabs_ref, qrope_ref, q128_ref, k128n_ref, cn_ref, wukt_ref, cache_c, cache_kr,
                      o_ref, cbuf, kbuf, cb, krb, s_all, sem, *, layer, heads, dn, s_new, n_pages, sub_pages):
    b = pl.program_id(0)
    nseq = pl.num_programs(0)
    slot = lax.rem(b, 2)
    rows = s_new * heads
    n_sub = n_pages // sub_pages
    sub_keys = sub_pages * PAGE_SIZE

    def page_copies(seq, slot_):
        out = []
        for i in range(n_pages):
            page = pt_ref[seq * n_pages + i]
            out.append(pltpu.make_async_copy(cache_c.at[layer, page],
                                             cbuf.at[slot_, pl.ds(i * PAGE_SIZE, PAGE_SIZE)], sem.at[0, slot_]))
            out.append(pltpu.make_async_copy(cache_kr.at[layer, page], kbuf.at[slot_, i], sem.at[1, slot_]))
        return out

    @pl.when(b == 0)
    def _():
        for cp in page_copies(0, 0):
            cp.start()

    waits = page_copies(b, slot)
    for cp in waits[0::2] + waits[1::2]:
        cp.wait()

    nxt = lax.rem(b + 1, nseq)

    lhs = jnp.concatenate([wukt_ref[...], qabs_ref[...]], axis=0)
    nk = heads * dn
    qrope = qrope_ref[...]

    def scores(sb):
        for i in range(sub_pages):
            pg = sb * sub_pages + i
            cb[sb, i * PAGE_SIZE:(i + 1) * PAGE_SIZE, :] = cbuf[slot, pg * PAGE_SIZE:(pg + 1) * PAGE_SIZE, :].astype(BF16)
            krb[sb, :, i * PAGE_SIZE:(i + 1) * PAGE_SIZE] = kbuf[slot, pg].astype(BF16)
        kq = _dot_t(lhs, cb[sb])
        ssq = jnp.concatenate([jnp.sum(kq[h * dn:(h + 1) * dn, :] ** 2, axis=0, keepdims=True)
                               for h in range(heads)], axis=0)
        r = lax.rsqrt(ssq * (1.0 / dn) + NORM_EPS)
        s = kq[nk:, :] * jnp.concatenate([r] * s_new, axis=0) + _dot(qrope, krb[sb])
        s_all[sb] = s
        return jnp.max(s, axis=1, keepdims=True)

    state = (jnp.full((rows, 1), NEG_BIG, F32), jnp.zeros((rows, 1), F32), jnp.zeros((rows, LANES), F32))

    def absorb(state, sb, m_sb):
        m_run, l, acc = state
        m_new = jnp.maximum(m_run, m_sb)
        alpha = jnp.exp(m_run - m_new)
        p = jnp.exp(s_all[sb] - m_new)
        return (m_new, alpha * l + jnp.sum(p, axis=1, keepdims=True), alpha * acc + _dot(p.astype(BF16), cb[sb]))

    ahead = AHEAD
    maxes = []
    for sb in range(n_sub):
        maxes.append(scores(sb))
        if sb == 0:
            for cp in page_copies(nxt, 1 - slot):
                cp.start()
        if sb >= ahead:
            state = absorb(state, sb - ahead, maxes[sb - ahead])
    for sb in range(max(n_sub - ahead, 0), n_sub):
        state = absorb(state, sb, maxes[sb])
    m_run, l, acc = state

    qf = q128_ref[...].astype(F32)
    assert heads & (heads - 1) == 0
    tok = lax.shift_right_logical(lax.broadcasted_iota(jnp.int32, (rows, 1), 0), heads.bit_length() - 1)
    sn = []
    m_fin = m_run
    for t in range(s_new):
        kt = jnp.concatenate([k128n_ref[t].astype(F32)] * s_new, axis=0)
        st = jnp.sum(qf * kt, axis=1, keepdims=True)
        sn.append(jnp.where(tok >= t, st, NEG_BIG))
        m_fin = jnp.maximum(m_fin, sn[-1])
    alpha = jnp.exp(m_run - m_fin)
    l = alpha * l
    acc = alpha * acc
    cn = cn_ref[...].astype(BF16).astype(F32)
    for t in range(s_new):
        pt = jnp.exp(sn[t] - m_fin)
        l = l + pt
        acc = acc + pt.astype(BF16).astype(F32) * cn[t:t + 1, :]
    o_ref[...] = (acc / l).astype(BF16)

    @pl.when(b == nseq - 1)
    def _():
        tail = page_copies(nxt, 1 - slot)
        for cp in tail[0::2] + tail[1::2]:
            cp.wait()


def _attn_sample(page_table, qabs, qrope, q128, k128n, c_new, wukt, cache_c, cache_kr, *, layer, heads, dn):
    nseq, rows, _ = qabs.shape
    s_new = rows // heads
    n_pages = page_table.shape[1]
    rank = cache_c.shape[-1]
    rope = cache_kr.shape[-2]
    sub_pages = min(SUB_PAGES, n_pages)
    assert n_pages % sub_pages == 0
    n_sub = n_pages // sub_pages
    sub_keys = sub_pages * PAGE_SIZE
    body = functools.partial(_attn_sample_body, layer=layer, heads=heads, dn=dn, s_new=s_new, n_pages=n_pages,
                             sub_pages=sub_pages)
    per_seq = lambda shape: pl.BlockSpec((None,) + shape, lambda b, pt: (b,) + (0,) * len(shape))
    grid_spec = pltpu.PrefetchScalarGridSpec(
        num_scalar_prefetch=1,
        grid=(nseq,),
        in_specs=[per_seq((rows, LANES)), per_seq((rows, rope)), per_seq((rows, LANES)),
                  per_seq((s_new, heads, LANES)), per_seq((s_new, LANES)),
                  pl.BlockSpec(wukt.shape, lambda b, pt: (0, 0)),
                  pl.BlockSpec(memory_space=pl.ANY), pl.BlockSpec(memory_space=pl.ANY)],
        out_specs=per_seq((rows, LANES)),
        scratch_shapes=[pltpu.VMEM((2, n_pages * PAGE_SIZE, rank), F32),
                        pltpu.VMEM((2, n_pages, rope, PAGE_SIZE), F32),
                        pltpu.VMEM((n_sub, sub_keys, rank), BF16),
                        pltpu.VMEM((n_sub, rope, sub_keys), BF16),
                        pltpu.VMEM((n_sub, rows, sub_keys), F32),
                        pltpu.SemaphoreType.DMA((2, 2))],
    )
    return pl.pallas_call(
        body, grid_spec=grid_spec,
        out_shape=jax.ShapeDtypeStruct((nseq, rows, LANES), BF16),
        compiler_params=_params("arbitrary"),
        name="attn_sample",
    )(page_table.reshape(-1), qabs, qrope, q128, k128n, c_new, wukt, cache_c, cache_kr)


def _rope_cos_sin(pos, half):
    inv = ROPE_THETA ** (-jnp.arange(half, dtype=F32) / half)
    t = pos.shape[0]
    if (t * half) % LANES == 0:
        ang = jnp.repeat(pos, half).reshape(-1, LANES) * jnp.tile(inv, t).reshape(-1, LANES)
        return jnp.concatenate([jnp.cos(ang).reshape(t, half), jnp.sin(ang).reshape(t, half)], axis=1)
    ang = pos[:, None] * inv[None, :]
    return jnp.concatenate([jnp.cos(ang), jnp.sin(ang)], axis=1)


def _rope_placement(half, dn):
    j = np.arange(half)
    place = np.zeros((2 * half, 3 * LANES), np.float32)
    dr = 2 * half
    for base, kind in ((dn, 'c'), (dn + dr, 's'), (LANES + dn, 'c'), (LANES + dn + dr, 'c'),
                       (2 * LANES + dn, 's'), (2 * LANES + dn + dr, 's')):
        if kind == 'c':
            place[j, base + j] = 1.0
            place[j, base + half + j] = 1.0
        else:
            place[half + j, base + j] = -1.0
            place[half + j, base + half + j] = 1.0
    bias = np.zeros((1, 3 * LANES), np.float32)
    bias[0, :dn] = 1.0
    return jnp.asarray(place, BF16), jnp.asarray(bias)


def _swap_halves(a, axis=-1):
    lo, hi = jnp.split(a, 2, axis=axis)
    return jnp.concatenate([hi, lo], axis=axis)


def _prep_layer(lp, dims):
    heads, dn, dr, rank, g_heads, conv_dim, v_dim = dims
    assert dn + 2 * dr == LANES and rank == LANES and 2 * g_heads <= LANES
    w_in = lp['w_in']
    d = w_in.shape[0]
    o = 0
    w_qkv = w_in[:, o:o + conv_dim]; o += conv_dim
    w_z = w_in[:, o:o + v_dim]; o += v_dim
    w_b = w_in[:, o:o + g_heads]; o += g_heads
    w_a = w_in[:, o:o + g_heads]; o += g_heads
    w_q = w_in[:, o:o + heads * (dn + dr)].reshape(d, heads, dn + dr); o += heads * (dn + dr)
    w_c = w_in[:, o:o + rank]; o += rank
    w_kr = w_in[:, o:o + dr]
    zeros = lambda n: jnp.zeros((d, n), F32)
    w_q128 = jnp.concatenate([w_q, _swap_halves(w_q[:, :, dn:])], axis=2).reshape(d, heads * LANES)
    w_krs = _swap_halves(w_kr)
    w_all = jnp.concatenate([w_qkv, w_z, w_b, w_a, zeros(LANES - 2 * g_heads), w_q128, w_c,
                             zeros(dn), w_kr, w_kr, zeros(dn), w_krs, w_krs], axis=1).astype(BF16)
    scale = (dn + dr) ** -0.5
    qr_g = lp['mla_qr_norm']
    gq = (jnp.concatenate([lp['mla_qn_norm'], qr_g, _swap_halves(qr_g)]) * scale).reshape(1, LANES)
    lane = np.arange(LANES)
    seg = np.where(lane < dn, 0, np.where(lane < dn + dr, 1, 2))
    seg_len = np.where(lane < dn, dn, dr).astype(np.float32)
    mq = jnp.asarray(np.where(seg[:, None] == seg[None, :], 1.0 / seg_len[None, :], 0.0), BF16)
    mk = jnp.asarray(np.where((lane[:, None] < dn) & (lane[None, :] < dn), 1.0 / dn, 0.0), BF16)
    gk = jnp.concatenate([lp['mla_kn_norm'], jnp.zeros((LANES - dn,), F32)]).reshape(1, LANES)
    kr_g = lp['mla_kr_norm']
    zdn = jnp.zeros((dn,), F32)
    gkr = jnp.stack([jnp.concatenate([zdn, kr_g, kr_g]),
                     jnp.concatenate([zdn, _swap_halves(kr_g), _swap_halves(kr_g)])])
    w_uk = lp['mla_w_uk']
    wuk = jnp.concatenate([w_uk, jnp.zeros((rank, heads, LANES - dn), F32)], axis=2)
    wuk = wuk.reshape(rank, heads * LANES).astype(BF16)
    wukt = w_uk.transpose(1, 2, 0).reshape(heads * dn, rank).astype(BF16)
    absorb = jnp.concatenate([w_uk.transpose(1, 2, 0) * lp['mla_kn_norm'][None, :, None],
                              jnp.zeros((heads, LANES - dn, rank), F32)], axis=1)
    fold = np.zeros((LANES, LANES), np.float32)
    fold[dn + np.arange(dr), np.arange(dr)] = 1.0
    fold[dn + dr + np.arange(dr), np.arange(dr)] = 1.0
    wqa = jnp.concatenate([absorb, jnp.broadcast_to(jnp.asarray(fold), (heads, LANES, LANES))], axis=2).astype(BF16)
    w_uv = lp['mla_w_uv']
    mv = w_uv.shape[2]
    wuv_bd = (w_uv.transpose(1, 0, 2)[:, :, None, :] * jnp.asarray(np.eye(heads, dtype=np.float32))[:, None, :, None])
    wuv_bd = wuv_bd.reshape(heads * rank, heads * mv).astype(BF16)
    lane_h = lane // (LANES // 2)
    ms = jnp.asarray(np.where(lane_h[:, None] == lane_h[None, :], 1.0, 0.0), BF16)
    aparams = jnp.pad(jnp.stack([jnp.exp(lp['gdn_a_log']), lp['gdn_dt_bias']]),
                      ((0, 0), (g_heads, LANES - 2 * g_heads)))
    return dict(
        w_all=w_all, gq=gq, mq=mq, gc=lp['mla_ckv_norm'].reshape(1, LANES), wuk=wuk, mk=mk, gk=gk, gkr=gkr,
        wukt=wukt, wqa=wqa, wuv_bd=wuv_bd, ms=ms, aparams=aparams,
        wo_g=lp['w_out'][:v_dim].astype(BF16), wo_m=lp['w_out'][v_dim:].astype(BF16),
        f1=(lp['ffn1_wi'].astype(BF16), lp['ffn1_wo'].astype(BF16)),
        f2=(lp['ffn2_wi'].astype(BF16), lp['ffn2_wo'].astype(BF16)),
    )


def _layer(x, mods, rope, conv_prev, s0, lp, w, lay, dims, *, t_valid, chunk, attend):
    heads = dims[0]
    x = _ffn(x, mods, lp['norm_ffn1'], *w['f1'], mod0=0, tokens_per_seq=None)
    outs = _inproj(x, mods, lp['norm_mix'], w['w_all'], *rope, w['gq'], w['mq'], w['gc'], w['wuk'], w['mk'],
                   w['gk'], w['gkr'], None, (conv_prev, lp['gdn_conv_w'], w['ms']), lay=lay, heads=heads,
                   tokens_per_seq=None, rope=(dims[1], dims[2]))
    qkv, z, ba, c, krot, nconv = outs[0], outs[1], outs[2], outs[5], outs[6], outs[8]
    gdn_out, s_new = _gdn(qkv, z, ba, s0, w['aparams'], lp['gdn_norm'], t_valid=t_valid, chunk=chunk)
    ctx = attend(outs)
    x = _ffn(x, mods, lp['norm_ffn2'], *w['f2'], mod0=6, tokens_per_seq=None,
             mix=(gdn_out, ctx, w['wuv_bd'], w['wo_g'], w['wo_m']))
    return x, c, krot, nconv, s_new


def kernel(x_prompt, x_sample, cache_ckv, cache_krope, state_conv, state_gdn, page_table, c_prompt, c_sample,
           ada_w, ada_b, norm_ffn1, ffn1_wi, ffn1_wo, norm_mix, w_in, gdn_conv_w, gdn_a_log, gdn_dt_bias, gdn_norm,
           mla_qn_norm, mla_qr_norm, mla_ckv_norm, mla_kr_norm, mla_kn_norm, mla_w_uk, mla_w_uv, w_out,
           norm_ffn2, ffn2_wi, ffn2_wo):
    depth = ada_w.shape[0]
    bp, tp, d = x_prompt.shape
    bs, ts, _ = x_sample.shape
    g_heads, dk, dv = state_gdn.shape[2:]
    conv_dim = state_conv.shape[-1]
    v_dim = g_heads * dv
    rank, heads, dn = mla_w_uk.shape[1:]
    dr = mla_qr_norm.shape[1]
    past = page_table.shape[1] * PAGE_SIZE
    dims = (heads, dn, dr, rank, g_heads, conv_dim, v_dim)
    lay = _InLayout(conv_dim, v_dim, heads)

    cs_p = _rope_cos_sin(jnp.arange(tp, dtype=F32), dr // 2)
    cs_s = jnp.tile(_rope_cos_sin(past + jnp.arange(ts, dtype=F32), dr // 2), (bs, 1))
    place, tab_bias = _rope_placement(dr // 2, dn)
    cond = jnp.concatenate([c_prompt, c_sample], axis=0)
    rows = -(-cond.shape[0] // SUBLANES) * SUBLANES
    cond = jnp.pad(cond, ((0, rows - cond.shape[0]), (0, 0)))

    yp = x_prompt
    ys = x_sample.reshape(1, bs * ts, d)
    outs_p, outs_s = [], []
    for l in range(depth):
        lp = dict(norm_ffn1=norm_ffn1[l], ffn1_wi=ffn1_wi[l], ffn1_wo=ffn1_wo[l], norm_mix=norm_mix[l], w_in=w_in[l],
                  gdn_conv_w=gdn_conv_w[l], gdn_a_log=gdn_a_log[l], gdn_dt_bias=gdn_dt_bias[l], gdn_norm=gdn_norm[l],
                  mla_qn_norm=mla_qn_norm[l], mla_qr_norm=mla_qr_norm[l], mla_ckv_norm=mla_ckv_norm[l],
                  mla_kr_norm=mla_kr_norm[l], mla_kn_norm=mla_kn_norm[l], mla_w_uk=mla_w_uk[l],
                  mla_w_uv=mla_w_uv[l], w_out=w_out[l], norm_ffn2=norm_ffn2[l], ffn2_wi=ffn2_wi[l],
                  ffn2_wo=ffn2_wo[l])
        w = _prep_layer(lp, dims)
        mods = _mods(cond, ada_w[l], ada_b[l])
        mods_p = mods[:bp].reshape(bp, N_MOD, d)
        mods_s = mods[bp:bp + bs].reshape(bs, N_MOD, d).transpose(1, 0, 2)

        conv0 = jnp.zeros((bp, SUBLANES, conv_dim), F32)
        s0 = jnp.zeros((bp, g_heads, dk, dv), F32)
        attend_p = lambda o: _attn_prompt(o[3], o[4], o[7])
        yp, c_p, kr_p, cv_p, s_p = _layer(yp, mods_p, (cs_p, place, tab_bias), conv0, s0, lp, w, lay, dims,
                                          t_valid=min(GDN_TILE, tp), chunk=min(GDN_CHUNK, tp), attend=attend_p)
        outs_p.append((c_p, kr_p.transpose(0, 2, 1), cv_p[:, SUBLANES - (GDN_CONV - 1):], s_p))


        def attend_s(o, l=l, w=w):
            q128, k128, c_new, qabs, qrope = o[3], o[4], o[5], o[7], o[8]
            by_seq = lambda a: a[0].reshape(heads, bs, ts, -1).transpose(1, 2, 0, 3)
            flat = lambda a: by_seq(a).reshape(bs, ts * heads, -1)
            ctx = _attn_sample(page_table, flat(qabs), flat(qrope)[..., :dr], flat(q128), by_seq(k128),
                               c_new.reshape(bs, ts, -1), w['wukt'], cache_ckv, jnp.swapaxes(cache_krope, 2, 3),
                               layer=l, heads=heads, dn=dn)
            return ctx.reshape(1, bs * ts, heads * rank)

        ys, c_s, kr_s, cv_s, s_s = _layer_sample(ys, mods_s, (cs_s, place, tab_bias), state_conv[l], state_gdn[l], lp, w, lay, dims,
                                                 bs, ts, attend_s)
        outs_s.append((c_s, kr_s, cv_s, s_s))

    stack = lambda outs, k: jnp.stack([o[k] for o in outs])
    ys = ys.reshape(bs, ts, d)
    return (yp, ys, stack(outs_p, 0), stack(outs_p, 1), stack(outs_p, 2), stack(outs_p, 3),
            stack(outs_s, 0), stack(outs_s, 1), stack(outs_s, 2), stack(outs_s, 3))


def _layer_sample(x, mods, rope, conv_prev, s0, lp, w, lay, dims, bs, ts, attend):
    heads, dn, dr, g_heads = dims[0], dims[1], dims[2], dims[4]
    x = _ffn(x, mods, lp['norm_ffn1'], *w['f1'], mod0=0, tokens_per_seq=ts)
    outs = _inproj(x, mods, lp['norm_mix'], w['w_all'], *rope, w['gq'], w['mq'], w['gc'], w['wuk'], w['mk'],
                   w['gk'], w['gkr'], w['wqa'], None, lay=lay, heads=heads, tokens_per_seq=ts, rope=(dn, dr))
    qkv, z, ba, c, krot = outs[0], outs[1], outs[2], outs[5], outs[6]
    seq = lambda a: a.reshape(bs, ts, a.shape[-1])
    minor = lambda a: seq(a).transpose(1, 2, 0)
    o_t, s_new = _gdn_sample(minor(qkv), conv_prev.transpose(1, 2, 0), minor(z), minor(ba[..., :g_heads]),
                             minor(ba[..., g_heads:2 * g_heads]), s0.transpose(1, 2, 3, 0), lp['gdn_conv_w'].T,
                             w['aparams'][:, g_heads:2 * g_heads], lp['gdn_norm'])
    gdn_out = o_t.transpose(2, 0, 1).reshape(1, bs * ts, -1)
    ctx = attend(outs)
    x = _ffn(x, mods, lp['norm_ffn2'], *w['f2'], mod0=6, tokens_per_seq=ts,
             mix=(gdn_out, ctx, w['wuv_bd'], w['wo_g'], w['wo_m']))
    nconv = jnp.concatenate([conv_prev, seq(qkv)], axis=1)[:, -(GDN_CONV - 1):]
    return x, seq(c), seq(krot)[..., dn:dn + dr], nconv, s_new.transpose(3, 0, 1, 2)
```

```python
import functools
import math

import jax
import jax.numpy as jnp
import numpy as np
from jax import lax
from jax.experimental import pallas as pl
from jax.experimental.pallas import tpu as pltpu

F32 = jnp.float32
BF16 = jnp.bfloat16

NORM_EPS = 1e-6
ROPE_THETA = 10000.0
PAGE_SIZE = 128
GDN_CONV = 4
GDN_CHUNK = 128
N_MOD = 9
LANES = 128
SUBLANES = 8
NEG_BIG = -1e30
VMEM_LIMIT = 56 * 1024 * 1024

TOKEN_TILE = 512
ATTN_TILE = 512
GDN_TILE = 512
SUB_PAGES = 16
AHEAD = 4


def _dot(a, b):
    return jnp.dot(a, b, preferred_element_type=F32)


def _dot_t(a, b):
    return lax.dot_general(a, b, (((1,), (1,)), ((), ())), preferred_element_type=F32)


def _dot_ta(a, b):
    return lax.dot_general(a, b, (((0,), (0,)), ((), ())), preferred_element_type=F32)


def _bmm(a, b):
    return lax.dot_general(a, b, (((2,), (1,)), ((0,), (0,))), preferred_element_type=F32)


def _bmm_t(a, b):
    return lax.dot_general(a, b, (((2,), (2,)), ((0,), (0,))), preferred_element_type=F32)


def _sigmoid(x):
    return 1.0 / (1.0 + jnp.exp(-x))


def _silu(x):
    return x * _sigmoid(x)


def _params(*sem):
    return pltpu.CompilerParams(dimension_semantics=sem, vmem_limit_bytes=VMEM_LIMIT)


def _resident(shape):
    nd = len(shape)
    return pl.BlockSpec(shape, lambda *_: (0,) * nd, pipeline_mode=pl.Buffered(1))


MODS_PER_SUBLAYER = 3


def _dot_split(x, e, terms, left=False):
    out = None
    for _ in range(terms):
        xb = x.astype(BF16)
        d = _dot(e, xb) if left else _dot(xb, e)
        out = d if out is None else out + d
        x = x - xb.astype(F32)
    return out


def _mod_rows(m_ref, ks, tokens_per_seq, rows):
    if tokens_per_seq is None:
        return [m_ref[k:k + 1, :] for k in ks]
    nseq = m_ref.shape[1]
    row = lax.broadcasted_iota(jnp.int32, (rows, nseq), 0)
    first = lax.broadcasted_iota(jnp.int32, (rows, nseq), 1) * tokens_per_seq
    spread = jnp.where((row >= first) & (row < first + tokens_per_seq), 1.0, 0.0).astype(BF16)
    return [_dot_split(m_ref[k % MODS_PER_SUBLAYER], spread, 3, left=True) for k in ks]


def _mod_norm(x, gain, shift, scale):
    y = x * lax.rsqrt(jnp.mean(x * x, axis=-1, keepdims=True) + NORM_EPS)
    return (y * gain) * (1.0 + scale) + shift


def _mods_spec(tokens_per_seq, tm, d, sublayer):
    if tokens_per_seq is not None:
        assert tm % tokens_per_seq == 0
        return pl.BlockSpec((MODS_PER_SUBLAYER, tm // tokens_per_seq, d), lambda b, i: (sublayer, i, 0))
    return pl.BlockSpec((None, N_MOD, d), lambda b, i: (b, 0, 0))


def _mods_body(c_ref, w_ref, b_ref, o_ref):
    c = c_ref[...]
    o_ref[...] = _dot(_silu(c).astype(BF16), w_ref[...].astype(BF16)) + b_ref[...]


def _mods(cond, ada_w, ada_b):
    rows, d = cond.shape
    n = ada_w.shape[1]
    tn = d
    return pl.pallas_call(
        _mods_body,
        grid=(n // tn,),
        in_specs=[pl.BlockSpec((rows, d), lambda j: (0, 0)),
                  pl.BlockSpec((d, tn), lambda j: (0, j)),
                  pl.BlockSpec((1, tn), lambda j: (0, j))],
        out_specs=pl.BlockSpec((rows, tn), lambda j: (0, j)),
        out_shape=jax.ShapeDtypeStruct((rows, n), F32),
        compiler_params=_params("arbitrary"),
        name="mods",
    )(cond, ada_w, ada_b.reshape(1, n))


def _ffn_body(x_ref, m_ref, g_ref, wi_ref, wo_ref, *rest, mod0, tokens_per_seq, fc, mixed):
    x = x_ref[...]
    if mixed:
        mm_ref, gdn_ref, ctx_ref, wuv_ref, wog_ref, wom_ref, o_ref = rest
        gate_k = mod0 - 1
        (gate_mix,) = _mod_rows(mm_ref if tokens_per_seq is not None else m_ref, [gate_k], tokens_per_seq, x.shape[0])
        mla = _dot(ctx_ref[...], wuv_ref[...])
        gdn = jnp.concatenate([gdn_ref[h] for h in range(gdn_ref.shape[0])], axis=1)
        mix = _dot(gdn.astype(BF16), wog_ref[...]) + _dot(mla.astype(BF16), wom_ref[...])
        x = x + gate_mix * mix
    else:
        (o_ref,) = rest
    shift, scale, gate = _mod_rows(m_ref, [mod0, mod0 + 1, mod0 + 2], tokens_per_seq, x.shape[0])
    h = _mod_norm(x, g_ref[...], shift, scale).astype(BF16)
    acc = jnp.zeros(x.shape, F32)
    dff = wo_ref.shape[0]
    for c in range(dff // fc):
        lo, hi = c * fc, (c + 1) * fc
        a = _silu(_dot(h, wi_ref[:, lo:hi])) * _dot(h, wi_ref[:, dff + lo:dff + hi])
        acc = acc + _dot(a.astype(BF16), wo_ref[lo:hi, :])
    o_ref[...] = x + 0.5 * gate * acc


def _ffn(x, mods, gain, wi, wo, *, mod0, tokens_per_seq, mix=None):
    nb, t, d = x.shape
    tm = min(TOKEN_TILE, t)
    dff = wo.shape[0]
    fc = 2 * LANES
    assert t % tm == 0 and dff % fc == 0
    body = functools.partial(_ffn_body, mod0=mod0, tokens_per_seq=tokens_per_seq, fc=fc, mixed=mix is not None)
    tok = lambda w: pl.BlockSpec((None, tm, w), lambda b, i: (b, i, 0))
    sub = mod0 // MODS_PER_SUBLAYER
    in_specs = [tok(d), _mods_spec(tokens_per_seq, tm, d, sub),
                _resident((1, d)), _resident((d, 2 * dff)), _resident((dff, d))]
    args = [x, mods, gain.reshape(1, d), wi, wo]
    if mix is not None:
        gdn_out, ctx, wuv_bd, wo_g, wo_m = mix
        in_specs += [_mods_spec(tokens_per_seq, tm, d, sub - 1),
                     pl.BlockSpec((None, gdn_out.shape[1], tm, gdn_out.shape[3]), lambda b, i: (b, 0, i, 0)),
                     tok(ctx.shape[-1]),
                     _resident(wuv_bd.shape), _resident(wo_g.shape), _resident(wo_m.shape)]
        args += [mods, gdn_out, ctx, wuv_bd, wo_g, wo_m]
    return pl.pallas_call(
        body,
        grid=(nb, t // tm),
        in_specs=in_specs,
        out_specs=tok(d),
        out_shape=jax.ShapeDtypeStruct(x.shape, F32),
        compiler_params=_params("arbitrary", "arbitrary"),
        name="ffn_mix" if mix is not None else "ffn",
    )(*args)


class _InLayout:
    def __init__(self, conv_dim, v_dim, heads):
        self.qkv = (0, conv_dim)
        self.z = (conv_dim, conv_dim + v_dim)
        self.ba = (self.z[1], self.z[1] + LANES)
        self.q = (self.ba[1], self.ba[1] + heads * LANES)
        self.ckv = (self.q[1], self.q[1] + LANES)
        self.kr = (self.ckv[1], self.ckv[1] + LANES)
        self.krs = (self.kr[1], self.kr[1] + LANES)
        self.total = self.krs[1]


def _inproj_body(x_ref, m_ref, g_ref, w_ref, cs_ref, place_ref, tbias_ref, gq_ref, mq_ref, gc_ref, wuk_ref, mk_ref,
                 gk_ref, gkr_ref, *rest, lay, heads, tokens_per_seq, sample, qk_dim, rope):
    if sample:
        wqa_ref, rest = rest[0], rest[1:]
    else:
        cprev_ref, cw_ref, ms_ref, rest = rest[0], rest[1], rest[2], rest[3:]
    qkv_ref, z_ref, ba_ref, q128_ref, k128_ref, c_ref, krot_ref = rest[:7]
    x = x_ref[...]
    shift, scale = _mod_rows(m_ref, [3, 4], tokens_per_seq, x.shape[0])
    h = _mod_norm(x, g_ref[...], shift, scale).astype(BF16)
    cw = 4 * LANES
    assert lay.total % cw == 0 and lay.qkv[1] % cw == 0 and (lay.z[1] - lay.z[0]) == cw
    n_chunks = lay.total // cw
    n_qkv = lay.qkv[1] // cw
    rest_chunks = [n_chunks - 1] + list(range(n_qkv, n_chunks - 1))
    order = []
    for c in range(max(n_qkv, len(rest_chunks))):
        order += ([c] if c < n_qkv else []) + ([rest_chunks[c]] if c < len(rest_chunks) else [])
    chunks = {}

    def cols(lo, hi):
        c = lo // cw
        assert (hi - 1) // cw == c
        return chunks[c][:, lo - c * cw:hi - c * cw]

    tm = x.shape[0]
    hist = SUBLANES
    taps = GDN_CONV - 1
    if not sample:
        ct_ref, nconv_ref, xbuf = rest[7], rest[8], rest[9]

        @pl.when(pl.program_id(1) == 0)
        def _():
            xbuf[0:hist, :] = cprev_ref[...]

    tab = _dot_split(cs_ref[...], place_ref[...], 3) + tbias_ref[...]
    tab_q, tab_c, tab_s = tab[:, :LANES], tab[:, LANES:2 * LANES], tab[:, 2 * LANES:]

    def group(lo):
        hi = lo + LANES
        if lo < lay.qkv[1]:
            if sample:
                qkv_ref[:, lo:hi] = cols(lo, hi)
                return
            xbuf[hist:hist + tm, lo:hi] = cols(lo, hi)
            conv = xbuf[hist - taps:hist - taps + tm, lo:hi] * cw_ref[0:1, lo:hi]
            for jj in range(1, GDN_CONV):
                conv = conv + xbuf[hist - taps + jj:hist - taps + jj + tm, lo:hi] * cw_ref[jj:jj + 1, lo:hi]
            yi = _silu(conv)
            if lo < 2 * qk_dim:
                ss = _dot((yi * yi).astype(BF16), ms_ref[...])
                yi = yi * (lax.rsqrt(ss + NORM_EPS) * ((LANES // 2) ** -0.5 if lo < qk_dim else 1.0))
            qkv_ref[2 * (lo // LANES)] = yi[:, :LANES // 2]
            qkv_ref[2 * (lo // LANES) + 1] = yi[:, LANES // 2:]
            tail = xbuf[tm:tm + hist, lo:hi]
            nconv_ref[:, lo:hi] = tail
            xbuf[0:hist, lo:hi] = tail
        elif lo < lay.z[1]:
            if sample:
                z_ref[:, lo - lay.z[0]:hi - lay.z[0]] = cols(lo, hi)
            else:
                zz = cols(lo, hi)
                z_ref[2 * ((lo - lay.z[0]) // LANES)] = zz[:, :LANES // 2]
                z_ref[2 * ((lo - lay.z[0]) // LANES) + 1] = zz[:, LANES // 2:]
        elif lo < lay.ba[1]:
            ba_ref[...] = cols(lo, hi)
        elif lo < lay.q[1]:
            hh = (lo - lay.q[0]) // LANES
            qh = cols(lo, hi)
            msq = _dot((qh * qh).astype(BF16), mq_ref[...])
            qn = qh * lax.rsqrt(msq + NORM_EPS) * gq_ref[...] * tab_q
            q128_ref[hh] = qn.astype(BF16)
            if sample:
                qabs_ref, qrope_ref = rest[7], rest[8]
                qa = _dot(qn.astype(BF16), wqa_ref[hh])
                qabs_ref[hh] = qa[:, :LANES].astype(BF16)
                qrope_ref[hh] = qa[:, LANES:2 * LANES].astype(BF16)
        elif lo == lay.ckv[0]:
            keys()

    def keys():
        ckv = cols(*lay.ckv)
        c = ckv * lax.rsqrt(jnp.mean(ckv * ckv, axis=-1, keepdims=True) + NORM_EPS) * gc_ref[...]
        c_ref[...] = c
        cb = c.astype(BF16)
        if not sample:
            ct_ref[...] = jnp.concatenate([c.T, jnp.ones((SUBLANES, c.shape[0]), F32)], axis=0).astype(BF16)
        kr = cols(*lay.kr)
        krs = cols(*lay.krs)
        inv = lax.rsqrt(jnp.sum(kr * kr, axis=-1, keepdims=True) * (2.0 / LANES) + NORM_EPS)
        krot = kr * inv * gkr_ref[0:1, :] * tab_c + krs * inv * gkr_ref[1:2, :] * tab_s
        if sample:
            krot_ref[...] = krot
        else:
            krot_ref[...] = krot.T[rope[0]:rope[0] + rope[1], :]
        knr = _dot(cb, wuk_ref[...])
        for hh in range(heads):
            kh = knr[:, hh * LANES:(hh + 1) * LANES]
            msq = _dot((kh * kh).astype(BF16), mk_ref[...])
            k128_ref[hh] = (kh * lax.rsqrt(msq + NORM_EPS) * gk_ref[...] + krot).astype(BF16)

    ahead = 2
    for i in range(len(order) + ahead):
        if i < len(order):
            k = order[i]
            chunks[k] = _dot(h, w_ref[:, k * cw:(k + 1) * cw])
        if i >= ahead:
            k = order[i - ahead]
            for lo in range(k * cw, (k + 1) * cw, LANES):
                group(lo)


def _inproj(x, mods, gain, w_all, cs, place, tab_bias, gq, mq, gc, wuk, mk, gk, gkr, wqa, conv, *, lay, heads,
            tokens_per_seq, rope):
    nb, t, d = x.shape
    tm = min(TOKEN_TILE, t)
    assert t % tm == 0
    sample = wqa is not None
    body = functools.partial(_inproj_body, lay=lay, heads=heads, tokens_per_seq=tokens_per_seq, sample=sample,
                             qk_dim=(lay.qkv[1] - (lay.z[1] - lay.z[0])) // 2, rope=rope)
    tok = lambda w: pl.BlockSpec((None, tm, w), lambda b, i: (b, i, 0))
    hd = lambda w: pl.BlockSpec((None, heads, tm, w), lambda b, i: (b, 0, i, 0))
    conv_dim, v_dim = lay.qkv[1], lay.z[1] - lay.z[0]
    in_specs = [tok(d), _mods_spec(tokens_per_seq, tm, d, 1), _resident((1, d)), _resident(w_all.shape),
                pl.BlockSpec((tm, cs.shape[1]), lambda b, i: (i, 0)), _resident(place.shape),
                _resident(tab_bias.shape),
                _resident((1, LANES)), _resident((LANES, LANES)), _resident((1, LANES)),
                _resident(wuk.shape), _resident((LANES, LANES)), _resident((1, LANES)), _resident((2, LANES))]
    args = [x, mods, gain.reshape(1, d), w_all, cs, place, tab_bias, gq, mq, gc, wuk, mk, gk, gkr]
    out_specs = [tok(conv_dim), tok(v_dim), tok(LANES), hd(LANES), hd(LANES), tok(LANES), tok(LANES)]
    out_shape = [jax.ShapeDtypeStruct((nb, t, conv_dim), F32), jax.ShapeDtypeStruct((nb, t, v_dim), F32),
                 jax.ShapeDtypeStruct((nb, t, LANES), F32),
                 jax.ShapeDtypeStruct((nb, heads, t, LANES), BF16), jax.ShapeDtypeStruct((nb, heads, t, LANES), BF16),
                 jax.ShapeDtypeStruct((nb, t, LANES), F32), jax.ShapeDtypeStruct((nb, t, LANES), F32)]
    if not sample:
        out_specs[6] = pl.BlockSpec((None, rope[1], tm), lambda b, i: (b, 0, i))
        out_shape[6] = jax.ShapeDtypeStruct((nb, rope[1], t), F32)
        hw = LANES // 2
        out_specs[0] = pl.BlockSpec((None, conv_dim // hw, tm, hw), lambda b, i: (b, 0, i, 0))
        out_specs[1] = pl.BlockSpec((None, v_dim // hw, tm, hw), lambda b, i: (b, 0, i, 0))
        out_shape[0] = jax.ShapeDtypeStruct((nb, conv_dim // hw, t, hw), F32)
        out_shape[1] = jax.ShapeDtypeStruct((nb, v_dim // hw, t, hw), F32)
    if sample:
        in_specs.append(_resident(wqa.shape))
        args.append(wqa)
        out_specs += [hd(LANES), hd(LANES)]
        out_shape += [jax.ShapeDtypeStruct((nb, heads, t, LANES), BF16)] * 2
        scratch = []
    else:
        conv_prev, conv_w, ms = conv
        in_specs += [pl.BlockSpec((None, SUBLANES, conv_dim), lambda b, i: (b, 0, 0)),
                     _resident(conv_w.shape), _resident(ms.shape)]
        args += [conv_prev, conv_w, ms]
        out_specs += [pl.BlockSpec((None, LANES + SUBLANES, tm), lambda b, i: (b, 0, i)),
                      pl.BlockSpec((None, SUBLANES, conv_dim), lambda b, i: (b, 0, 0))]
        out_shape += [jax.ShapeDtypeStruct((nb, LANES + SUBLANES, t), BF16),
                      jax.ShapeDtypeStruct((nb, SUBLANES, conv_dim), F32)]
        scratch = [pltpu.VMEM((tm + SUBLANES, conv_dim), F32)]
    return pl.pallas_call(
        body, grid=(nb, t // tm), in_specs=in_specs, out_specs=out_specs, out_shape=out_shape,
        scratch_shapes=scratch, compiler_params=_params("arbitrary", "arbitrary"), name="inproj",
    )(*args)


def _unit_lower_inverse(a):
    n = a.shape[-1]
    assert n & (n - 1) == 0
    row = lax.broadcasted_iota(jnp.int32, (n, n), 0)
    col = lax.broadcasted_iota(jnp.int32, (n, n), 1)
    x = jnp.broadcast_to(jnp.where(row == col, 1.0, 0.0), a.shape)
    b = 1
    while b < n:
        lo_mask = (jnp.bitwise_xor(row, col) < 2 * b) & (jnp.bitwise_and(row, b) != 0) & (jnp.bitwise_and(col, b) == 0)
        lo = jnp.where(lo_mask, a, 0.0)
        if b == 1:
            x = x - lo
        else:
            xb = x.astype(BF16)
            x = x - _bmm(xb, _bmm(lo.astype(BF16), xb).astype(BF16))
        b *= 2
    return x


def _gdn_body(qkv_ref, z_ref, ba_ref, s0_ref, ap_ref, ng_ref, o_ref, sout_ref, s_scr,
              *, tt, t_valid, chunk, heads, dk, dv):
    j = pl.program_id(1)

    @pl.when(j == 0)
    def _():
        s_scr[...] = s0_ref[...]

    ba = ba_ref[...]
    beta = _sigmoid(ba)
    xg = ba + ap_ref[1:2, :]
    g = -ap_ref[0:1, :] * (jnp.maximum(xg, 0.0) + jnp.log1p(jnp.exp(-jnp.abs(xg))))
    assert t_valid == tt
    row = lax.broadcasted_iota(jnp.int32, (tt, tt), 0)
    col = lax.broadcasted_iota(jnp.int32, (tt, tt), 1)
    assert chunk & (chunk - 1) == 0
    tri = jnp.where((row >= col) & (jnp.bitwise_xor(row, col) < chunk), 1.0, 0.0).astype(BF16)
    g_hi = g.astype(BF16)
    g_lo = (g - g_hi.astype(F32)).astype(BF16)
    gc = _dot(tri, g_hi) + _dot(tri, g_lo)
    gct = gc.T

    crow = lax.broadcasted_iota(jnp.int32, (chunk, chunk), 0)
    ccol = lax.broadcasted_iota(jnp.int32, (chunk, chunk), 1)
    causal = crow >= ccol
    strict = crow > ccol
    stack = lambda xs: jnp.stack(xs, axis=0)

    n_chunks = tt // chunk
    parts = []

    def prepare(c):
        r0, r1 = c * chunk, (c + 1) * chunk
        q3 = qkv_ref[0:heads, r0:r1, :]
        k3 = qkv_ref[heads:2 * heads, r0:r1, :]
        v3 = qkv_ref[2 * heads:3 * heads, r0:r1, :]
        b3 = stack([beta[r0:r1, h:h + 1] for h in range(heads)])
        gcol = stack([gc[r0:r1, heads + h:heads + h + 1] for h in range(heads)])
        grow = stack([gct[heads + h:heads + h + 1, r0:r1] for h in range(heads)])
        glast = grow[:, :, chunk - 1:chunk]
        eg = jnp.exp(gcol)
        kb = k3 * b3
        decay3 = jnp.exp(jnp.where(causal, gcol - grow, NEG_BIG))
        sc = _bmm_t(jnp.concatenate([q3, kb], axis=1).astype(BF16), k3.astype(BF16))
        qkm = (sc[:, :chunk] * decay3).astype(BF16)
        m = jnp.where(strict, sc[:, chunk:] * decay3, 0.0)
        return (m, qkm, (v3 * b3).astype(BF16), (kb * eg).astype(BF16), (q3 * eg).astype(BF16),
                (k3 * jnp.exp(glast - gcol)).astype(BF16), jnp.exp(glast), None)

    group = 2
    tinvs = []
    for c0 in range(0, n_chunks, group):
        cs = list(range(c0, min(c0 + group, n_chunks)))
        parts += [prepare(c) for c in cs]
        inv = _unit_lower_inverse(jnp.concatenate([parts[c][0] for c in cs], axis=0)).astype(BF16)
        for i, c in enumerate(cs):
            tinv = inv[i * heads:(i + 1) * heads]
            tinvs.append((_bmm(tinv, parts[c][2]), _bmm(tinv, parts[c][3]).astype(BF16)))
    s3 = s_scr[...]
    for c in range(n_chunks):
        r0, r1 = c * chunk, (c + 1) * chunk
        _, qkm, vb3, kbeg3, qd3, kd3, gl3, z3 = parts[c]
        u, w = tinvs[c]
        sb = s3.astype(BF16)
        ws_qs = _bmm(jnp.concatenate([w, qd3], axis=1), sb)
        vnb = (u - ws_qs[:, :chunk]).astype(BF16)
        o = ws_qs[:, chunk:] + _bmm(qkm, vnb)
        upd = stack([_dot_ta(kd3[h], vnb[h]) for h in range(heads)])
        s3 = s3 * gl3 + upd
        on = o * lax.rsqrt(jnp.mean(o * o, axis=-1, keepdims=True) + NORM_EPS) * ng_ref[...]
        o_ref[:, r0:r1, :] = on * _silu(z_ref[:, r0:r1, :])
    s_scr[...] = s3
    sout_ref[...] = s3


def _gdn(qkv, z, ba, s0, aparams, norm_g, *, t_valid, chunk):
    nb, slabs, t, _ = qkv.shape
    heads, dk, dv = s0.shape[1:]
    assert slabs == 3 * heads and dk == dv
    tt = min(GDN_TILE, t)
    assert t % tt == 0 and tt % chunk == 0 and (t_valid == tt or t == tt)
    body = functools.partial(_gdn_body, tt=tt, t_valid=t_valid, chunk=chunk, heads=heads, dk=dk, dv=dv)
    tok = lambda w: pl.BlockSpec((None, tt, w), lambda b, i: (b, i, 0))
    slab = lambda n: pl.BlockSpec((None, n, tt, dk), lambda b, i: (b, 0, i, 0))
    return pl.pallas_call(
        body,
        grid=(nb, t // tt),
        in_specs=[slab(3 * heads), slab(heads), tok(LANES),
                  pl.BlockSpec((None, heads, dk, dv), lambda b, i: (b, 0, 0, 0)),
                  _resident((2, LANES)), _resident((1, dv))],
        out_specs=[slab(heads), pl.BlockSpec((None, heads, dk, dv), lambda b, i: (b, 0, 0, 0))],
        out_shape=[jax.ShapeDtypeStruct((nb, heads, t, dv), F32), jax.ShapeDtypeStruct((nb, heads, dk, dv), F32)],
        scratch_shapes=[pltpu.VMEM((heads, dk, dv), F32)],
        compiler_params=_params("arbitrary", "arbitrary"),
        name="gdn",
    )(qkv, z, ba, s0, aparams, norm_g.reshape(1, dv))


def _gdn_sample_body(xq_ref, xk_ref, xv_ref, pq_ref, pk_ref, pv_ref, wq_ref, wk_ref, wv_ref, z_ref, b_ref, a_ref,
                     ap_ref, ng_ref, s_ref, o_ref, sout_ref, kq_scr, *, ts, heads, dk, dv):
    h = pl.program_id(0)
    nb = xq_ref.shape[-1]

    def conv(x_ref, p_ref, w_ref):
        taps = GDN_CONV - 1
        xin = [p_ref[i] for i in range(taps)] + [x_ref[t] for t in range(ts)]
        w = [jnp.broadcast_to(w_ref[:, jj:jj + 1], xin[0].shape) for jj in range(GDN_CONV)]
        out = []
        for t in range(ts):
            acc = xin[t] * w[0]
            for jj in range(1, GDN_CONV):
                acc = acc + xin[t + jj] * w[jj]
            out.append(_silu(acc))
        return out

    def l2(x):
        return x * lax.rsqrt(jnp.sum(x * x, axis=0, keepdims=True) + NORM_EPS)

    q = [l2(x) * (dk ** -0.5) for x in conv(xq_ref, pq_ref, wq_ref)]
    k = [l2(x) for x in conv(xk_ref, pk_ref, wk_ref)]
    v = conv(xv_ref, pv_ref, wv_ref)
    for t in range(ts):
        kq_scr[t] = k[t]
        kq_scr[ts + t] = q[t]
    a_h = ap_ref[0, h]
    dt_h = ap_ref[1, h]
    beta, decay = [], []
    for t in range(ts):
        beta.append(_sigmoid(b_ref[t, pl.ds(h, 1), :]))
        xg = a_ref[t, pl.ds(h, 1), :] + dt_h
        decay.append(jnp.exp(-a_h * (jnp.maximum(xg, 0.0) + jnp.log1p(jnp.exp(-jnp.abs(xg))))))

    def row(i, kk):
        return jnp.broadcast_to(kq_scr[i, pl.ds(kk, 1), :], (dv, nb))

    def first(kk, ks):
        return ks + row(0, kk) * s_ref[kk]

    ks = lax.fori_loop(0, dk, first, jnp.zeros((dv, nb), F32), unroll=8)
    ng = jnp.broadcast_to(ng_ref[...], (dv, nb))
    for t in range(ts):
        d = beta[t] * (v[t] - decay[t] * ks)
        src = s_ref if t == 0 else sout_ref
        last = t == ts - 1

        def step(kk, carry, t=t, d=d, src=src, last=last):
            o_acc, ks_next = carry
            s_new = decay[t] * src[kk] + row(t, kk) * d
            sout_ref[kk] = s_new
            o_acc = o_acc + row(ts + t, kk) * s_new
            if not last:
                ks_next = ks_next + row(t + 1, kk) * s_new
            return o_acc, ks_next

        zero = jnp.zeros((dv, nb), F32)
        o, ks = lax.fori_loop(0, dk, step, (zero, zero), unroll=8)
        on = o * lax.rsqrt(jnp.mean(o * o, axis=0, keepdims=True) + NORM_EPS) * ng
        o_ref[t] = on * _silu(z_ref[t])


def _gdn_sample(x_t, prev_t, z_t, b_t, a_t, s_t, conv_w_t, aparams, norm_g):
    ts, conv_dim, nb = x_t.shape
    heads, dk, dv, _ = s_t.shape
    assert dk == dv and conv_dim == 3 * heads * dk
    taps = GDN_CONV - 1
    body = functools.partial(_gdn_sample_body, ts=ts, heads=heads, dk=dk, dv=dv)
    part = lambda rows, off: pl.BlockSpec((rows, dk, nb), lambda h: (0, off + h, 0))
    wpart = lambda off: pl.BlockSpec((dk, GDN_CONV), lambda h: (off + h, 0))
    whole = lambda a: pl.BlockSpec(a.shape, lambda h: (0,) * a.ndim)
    state = pl.BlockSpec((None, dk, dv, nb), lambda h: (h, 0, 0, 0))
    return pl.pallas_call(
        body,
        grid=(heads,),
        in_specs=[part(ts, 0), part(ts, heads), part(ts, 2 * heads),
                  part(taps, 0), part(taps, heads), part(taps, 2 * heads),
                  wpart(0), wpart(heads), wpart(2 * heads),
                  part(ts, 0), whole(b_t), whole(a_t),
                  pl.BlockSpec(memory_space=pltpu.SMEM), pl.BlockSpec((dv, 1), lambda h: (0, 0)), state],
        out_specs=[part(ts, 0), state],
        out_shape=[jax.ShapeDtypeStruct((ts, heads * dv, nb), F32), jax.ShapeDtypeStruct(s_t.shape, F32)],
        scratch_shapes=[pltpu.VMEM((2 * ts, dk, nb), F32)],
        compiler_params=_params("arbitrary"),
        name="gdn_sample",
    )(x_t, x_t, x_t, prev_t, prev_t, prev_t, conv_w_t, conv_w_t, conv_w_t, z_t, b_t, a_t, aparams,
      norm_g.reshape(dv, 1), s_t)


def _attn_body(qi_ref, kj_ref, q_ref, k_ref, ct_ref, o_ref, m_scr, acc_scr, *, heads, tq, rank):
    i = qi_ref[pl.program_id(1)]
    j = kj_ref[pl.program_id(1)]

    @pl.when(j == 0)
    def _():
        m_scr[...] = jnp.full(m_scr.shape, NEG_BIG, F32)
        acc_scr[...] = jnp.zeros(acc_scr.shape, F32)

    def step(masked):
        ct = ct_ref[...]
        m_old = [m_scr[h] for h in range(heads)]
        acc_old = [acc_scr[h] for h in range(heads)]
        if masked:
            key = lax.broadcasted_iota(jnp.int32, (tq, tq), 0)
            qry = lax.broadcasted_iota(jnp.int32, (tq, tq), 1)
            keep = key <= qry
        m_out, acc_out = [], []
        ahead = 3
        scores = [_dot_t(k_ref[h], q_ref[h]) for h in range(ahead)]
        for h in range(heads):
            if h + ahead < heads:
                scores.append(_dot_t(k_ref[h + ahead], q_ref[h + ahead]))
            st = scores[h]
            if masked:
                st = jnp.where(keep, st, NEG_BIG)
            m_new = jnp.maximum(m_old[h], jnp.max(st, axis=0, keepdims=True))
            alpha = jnp.exp(m_old[h] - m_new)
            pt = jnp.exp(st - m_new).astype(BF16)
            acc_out.append(acc_old[h] * alpha + _dot(ct, pt))
            m_out.append(m_new)
        for h in range(heads):
            m_scr[h] = m_out[h]
            acc_scr[h] = acc_out[h]
        return acc_out

    @pl.when(j < i)
    def _():
        step(False)

    @pl.when(j == i)
    def _():
        acc = step(True)
        for h in range(heads):
            ctx = acc[h][:rank, :] / acc[h][rank:rank + 1, :]
            o_ref[:, h * rank:(h + 1) * rank] = ctx.T.astype(BF16)


def _attn_prompt(q128, k128, ct_ext):
    nb, heads, t, _ = q128.shape
    rows = ct_ext.shape[1]
    rank = rows - SUBLANES
    tq = min(ATTN_TILE, t)
    assert t % tq == 0
    n = t // tq
    body = functools.partial(_attn_body, heads=heads, tq=tq, rank=rank)
    pairs = [(i, j) for i in range(n) for j in range(i + 1)]
    qi = jnp.asarray([p[0] for p in pairs], jnp.int32)
    kj = jnp.asarray([p[1] for p in pairs], jnp.int32)
    grid_spec = pltpu.PrefetchScalarGridSpec(
        num_scalar_prefetch=2,
        grid=(nb, len(pairs)),
        in_specs=[pl.BlockSpec((None, heads, tq, LANES), lambda b, s, qi, kj: (b, 0, qi[s], 0)),
                  pl.BlockSpec((None, heads, tq, LANES), lambda b, s, qi, kj: (b, 0, kj[s], 0)),
                  pl.BlockSpec((None, rows, tq), lambda b, s, qi, kj: (b, 0, kj[s]))],
        out_specs=pl.BlockSpec((None, tq, heads * rank), lambda b, s, qi, kj: (b, qi[s], 0)),
        scratch_shapes=[pltpu.VMEM((heads, 1, tq), F32), pltpu.VMEM((heads, rows, tq), F32)],
    )
    return pl.pallas_call(
        body, grid_spec=grid_spec,
        out_shape=jax.ShapeDtypeStruct((nb, t, heads * rank), BF16),
        compiler_params=_params("arbitrary", "arbitrary"),
        name="attn_prompt",
    )(qi, kj, q128, k128, ct_ext)


def _attn_sample_body(pt_ref, qabs_ref, qrope_ref, q128_ref, k128n_ref, cn_ref, wukt_ref, cache_c, cache_kr,
                      o_ref, cbuf, kbuf, cb, krb, s_all, sem, *, layer, heads, dn, s_new, n_pages, sub_pages):
    b = pl.program_id(0)
    nseq = pl.num_programs(0)
    slot = lax.rem(b, 2)
    rows = s_new * heads
    n_sub = n_pages // sub_pages
    sub_keys = sub_pages * PAGE_SIZE

    def page_copies(seq, slot_):
        out = []
        for i in range(n_pages):
            page = pt_ref[seq * n_pages + i]
            out.append(pltpu.make_async_copy(cache_c.at[layer, page],
                                             cbuf.at[slot_, pl.ds(i * PAGE_SIZE, PAGE_SIZE)], sem.at[0, slot_]))
            out.append(pltpu.make_async_copy(cache_kr.at[layer, page], kbuf.at[slot_, i], sem.at[1, slot_]))
        return out

    @pl.when(b == 0)
    def _():
        for cp in page_copies(0, 0):
            cp.start()

    waits = page_copies(b, slot)
    for cp in waits[0::2] + waits[1::2]:
        cp.wait()

    nxt = lax.rem(b + 1, nseq)

    lhs = jnp.concatenate([wukt_ref[...], qabs_ref[...]], axis=0)
    nk = heads * dn
    qrope = qrope_ref[...]

    def scores(sb):
        for i in range(sub_pages):
            pg = sb * sub_pages + i
            cb[sb, i * PAGE_SIZE:(i + 1) * PAGE_SIZE, :] = cbuf[slot, pg * PAGE_SIZE:(pg + 1) * PAGE_SIZE, :].astype(BF16)
            krb[sb, :, i * PAGE_SIZE:(i + 1) * PAGE_SIZE] = kbuf[slot, pg].astype(BF16)
        kq = _dot_t(lhs, cb[sb])
        ssq = jnp.concatenate([jnp.sum(kq[h * dn:(h + 1) * dn, :] ** 2, axis=0, keepdims=True)
                               for h in range(heads)], axis=0)
        r = lax.rsqrt(ssq * (1.0 / dn) + NORM_EPS)
        s = kq[nk:, :] * jnp.concatenate([r] * s_new, axis=0) + _dot(qrope, krb[sb])
        s_all[sb] = s
        return jnp.max(s, axis=1, keepdims=True)

    state = (jnp.full((rows, 1), NEG_BIG, F32), jnp.zeros((rows, 1), F32), jnp.zeros((rows, LANES), F32))

    def absorb(state, sb, m_sb):
        m_run, l, acc = state
        m_new = jnp.maximum(m_run, m_sb)
        alpha = jnp.exp(m_run - m_new)
        p = jnp.exp(s_all[sb] - m_new)
        return (m_new, alpha * l + jnp.sum(p, axis=1, keepdims=True), alpha * acc + _dot(p.astype(BF16), cb[sb]))

    ahead = AHEAD
    maxes = []
    for sb in range(n_sub):
        maxes.append(scores(sb))
        if sb == 0:
            for cp in page_copies(nxt, 1 - slot):
                cp.start()
        if sb >= ahead:
            state = absorb(state, sb - ahead, maxes[sb - ahead])
    for sb in range(max(n_sub - ahead, 0), n_sub):
        state = absorb(state, sb, maxes[sb])
    m_run, l, acc = state

    qf = q128_ref[...].astype(F32)
    assert heads & (heads - 1) == 0
    tok = lax.shift_right_logical(lax.broadcasted_iota(jnp.int32, (rows, 1), 0), heads.bit_length() - 1)
    sn = []
    m_fin = m_run
    for t in range(s_new):
        kt = jnp.concatenate([k128n_ref[t].astype(F32)] * s_new, axis=0)
        st = jnp.sum(qf * kt, axis=1, keepdims=True)
        sn.append(jnp.where(tok >= t, st, NEG_BIG))
        m_fin = jnp.maximum(m_fin, sn[-1])
    alpha = jnp.exp(m_run - m_fin)
    l = alpha * l
    acc = alpha * acc
    cn = cn_ref[...].astype(BF16).astype(F32)
    for t in range(s_new):
        pt = jnp.exp(sn[t] - m_fin)
        l = l + pt
        acc = acc + pt.astype(BF16).astype(F32) * cn[t:t + 1, :]
    o_ref[...] = (acc / l).astype(BF16)

    @pl.when(b == nseq - 1)
    def _():
        tail = page_copies(nxt, 1 - slot)
        for cp in tail[0::2] + tail[1::2]:
            cp.wait()


def _attn_sample(page_table, qabs, qrope, q128, k128n, c_new, wukt, cache_c, cache_kr, *, layer, heads, dn):
    nseq, rows, _ = qabs.shape
    s_new = rows // heads
    n_pages = page_table.shape[1]
    rank = cache_c.shape[-1]
    rope = cache_kr.shape[-2]
    sub_pages = min(SUB_PAGES, n_pages)
    assert n_pages % sub_pages == 0
    n_sub = n_pages // sub_pages
    sub_keys = sub_pages * PAGE_SIZE
    body = functools.partial(_attn_sample_body, layer=layer, heads=heads, dn=dn, s_new=s_new, n_pages=n_pages,
                             sub_pages=sub_pages)
    per_seq = lambda shape: pl.BlockSpec((None,) + shape, lambda b, pt: (b,) + (0,) * len(shape))
    grid_spec = pltpu.PrefetchScalarGridSpec(
        num_scalar_prefetch=1,
        grid=(nseq,),
        in_specs=[per_seq((rows, LANES)), per_seq((rows, rope)), per_seq((rows, LANES)),
                  per_seq((s_new, heads, LANES)), per_seq((s_new, LANES)),
                  pl.BlockSpec(wukt.shape, lambda b, pt: (0, 0)),
                  pl.BlockSpec(memory_space=pl.ANY), pl.BlockSpec(memory_space=pl.ANY)],
        out_specs=per_seq((rows, LANES)),
        scratch_shapes=[pltpu.VMEM((2, n_pages * PAGE_SIZE, rank), F32),
                        pltpu.VMEM((2, n_pages, rope, PAGE_SIZE), F32),
                        pltpu.VMEM((n_sub, sub_keys, rank), BF16),
                        pltpu.VMEM((n_sub, rope, sub_keys), BF16),
                        pltpu.VMEM((n_sub, rows, sub_keys), F32),
                        pltpu.SemaphoreType.DMA((2, 2))],
    )
    return pl.pallas_call(
        body, grid_spec=grid_spec,
        out_shape=jax.ShapeDtypeStruct((nseq, rows, LANES), BF16),
        compiler_params=_params("arbitrary"),
        name="attn_sample",
    )(page_table.reshape(-1), qabs, qrope, q128, k128n, c_new, wukt, cache_c, cache_kr)


def _rope_cos_sin(pos, half):
    inv = ROPE_THETA ** (-jnp.arange(half, dtype=F32) / half)
    t = pos.shape[0]
    if (t * half) % LANES == 0:
        ang = jnp.repeat(pos, half).reshape(-1, LANES) * jnp.tile(inv, t).reshape(-1, LANES)
        return jnp.concatenate([jnp.cos(ang).reshape(t, half), jnp.sin(ang).reshape(t, half)], axis=1)
    ang = pos[:, None] * inv[None, :]
    return jnp.concatenate([jnp.cos(ang), jnp.sin(ang)], axis=1)


def _rope_placement(half, dn):
    j = np.arange(half)
    place = np.zeros((2 * half, 3 * LANES), np.float32)
    dr = 2 * half
    for base, kind in ((dn, 'c'), (dn + dr, 's'), (LANES + dn, 'c'), (LANES + dn + dr, 'c'),
                       (2 * LANES + dn, 's'), (2 * LANES + dn + dr, 's')):
        if kind == 'c':
            place[j, base + j] = 1.0
            place[j, base + half + j] = 1.0
        else:
            place[half + j, base + j] = -1.0
            place[half + j, base + half + j] = 1.0
    bias = np.zeros((1, 3 * LANES), np.float32)
    bias[0, :dn] = 1.0
    return jnp.asarray(place, BF16), jnp.asarray(bias)


def _swap_halves(a, axis=-1):
    lo, hi = jnp.split(a, 2, axis=axis)
    return jnp.concatenate([hi, lo], axis=axis)


def _prep_layer(lp, dims):
    heads, dn, dr, rank, g_heads, conv_dim, v_dim = dims
    assert dn + 2 * dr == LANES and rank == LANES and 2 * g_heads <= LANES
    w_in = lp['w_in']
    d = w_in.shape[0]
    o = 0
    w_qkv = w_in[:, o:o + conv_dim]; o += conv_dim
    w_z = w_in[:, o:o + v_dim]; o += v_dim
    w_b = w_in[:, o:o + g_heads]; o += g_heads
    w_a = w_in[:, o:o + g_heads]; o += g_heads
    w_q = w_in[:, o:o + heads * (dn + dr)].reshape(d, heads, dn + dr); o += heads * (dn + dr)
    w_c = w_in[:, o:o + rank]; o += rank
    w_kr = w_in[:, o:o + dr]
    zeros = lambda n: jnp.zeros((d, n), F32)
    w_q128 = jnp.concatenate([w_q, _swap_halves(w_q[:, :, dn:])], axis=2).reshape(d, heads * LANES)
    w_krs = _swap_halves(w_kr)
    w_all = jnp.concatenate([w_qkv, w_z, w_b, w_a, zeros(LANES - 2 * g_heads), w_q128, w_c,
                             zeros(dn), w_kr, w_kr, zeros(dn), w_krs, w_krs], axis=1).astype(BF16)
    scale = (dn + dr) ** -0.5
    qr_g = lp['mla_qr_norm']
    gq = (jnp.concatenate([lp['mla_qn_norm'], qr_g, _swap_halves(qr_g)]) * scale).reshape(1, LANES)
    lane = np.arange(LANES)
    seg = np.where(lane < dn, 0, np.where(lane < dn + dr, 1, 2))
    seg_len = np.where(lane < dn, dn, dr).astype(np.float32)
    mq = jnp.asarray(np.where(seg[:, None] == seg[None, :], 1.0 / seg_len[None, :], 0.0), BF16)
    mk = jnp.asarray(np.where((lane[:, None] < dn) & (lane[None, :] < dn), 1.0 / dn, 0.0), BF16)
    gk = jnp.concatenate([lp['mla_kn_norm'], jnp.zeros((LANES - dn,), F32)]).reshape(1, LANES)
    kr_g = lp['mla_kr_norm']
    zdn = jnp.zeros((dn,), F32)
    gkr = jnp.stack([jnp.concatenate([zdn, kr_g, kr_g]),
                     jnp.concatenate([zdn, _swap_halves(kr_g), _swap_halves(kr_g)])])
    w_uk = lp['mla_w_uk']
    wuk = jnp.concatenate([w_uk, jnp.zeros((rank, heads, LANES - dn), F32)], axis=2)
    wuk = wuk.reshape(rank, heads * LANES).astype(BF16)
    wukt = w_uk.transpose(1, 2, 0).reshape(heads * dn, rank).astype(BF16)
    absorb = jnp.concatenate([w_uk.transpose(1, 2, 0) * lp['mla_kn_norm'][None, :, None],
                              jnp.zeros((heads, LANES - dn, rank), F32)], axis=1)
    fold = np.zeros((LANES, LANES), np.float32)
    fold[dn + np.arange(dr), np.arange(dr)] = 1.0
    fold[dn + dr + np.arange(dr), np.arange(dr)] = 1.0
    wqa = jnp.concatenate([absorb, jnp.broadcast_to(jnp.asarray(fold), (heads, LANES, LANES))], axis=2).astype(BF16)
    w_uv = lp['mla_w_uv']
    mv = w_uv.shape[2]
    wuv_bd = (w_uv.transpose(1, 0, 2)[:, :, None, :] * jnp.asarray(np.eye(heads, dtype=np.float32))[:, None, :, None])
    wuv_bd = wuv_bd.reshape(heads * rank, heads * mv).astype(BF16)
    lane_h = lane // (LANES // 2)
    ms = jnp.asarray(np.where(lane_h[:, None] == lane_h[None, :], 1.0, 0.0), BF16)
    aparams = jnp.pad(jnp.stack([jnp.exp(lp['gdn_a_log']), lp['gdn_dt_bias']]),
                      ((0, 0), (g_heads, LANES - 2 * g_heads)))
    return dict(
        w_all=w_all, gq=gq, mq=mq, gc=lp['mla_ckv_norm'].reshape(1, LANES), wuk=wuk, mk=mk, gk=gk, gkr=gkr,
        wukt=wukt, wqa=wqa, wuv_bd=wuv_bd, ms=ms, aparams=aparams,
        wo_g=lp['w_out'][:v_dim].astype(BF16), wo_m=lp['w_out'][v_dim:].astype(BF16),
        f1=(lp['ffn1_wi'].astype(BF16), lp['ffn1_wo'].astype(BF16)),
        f2=(lp['ffn2_wi'].astype(BF16), lp['ffn2_wo'].astype(BF16)),
    )


def _layer(x, mods, rope, conv_prev, s0, lp, w, lay, dims, *, t_valid, chunk, attend):
    heads = dims[0]
    x = _ffn(x, mods, lp['norm_ffn1'], *w['f1'], mod0=0, tokens_per_seq=None)
    outs = _inproj(x, mods, lp['norm_mix'], w['w_all'], *rope, w['gq'], w['mq'], w['gc'], w['wuk'], w['mk'],
                   w['gk'], w['gkr'], None, (conv_prev, lp['gdn_conv_w'], w['ms']), lay=lay, heads=heads,
                   tokens_per_seq=None, rope=(dims[1], dims[2]))
    qkv, z, ba, c, krot, nconv = outs[0], outs[1], outs[2], outs[5], outs[6], outs[8]
    gdn_out, s_new = _gdn(qkv, z, ba, s0, w['aparams'], lp['gdn_norm'], t_valid=t_valid, chunk=chunk)
    ctx = attend(outs)
    x = _ffn(x, mods, lp['norm_ffn2'], *w['f2'], mod0=6, tokens_per_seq=None,
             mix=(gdn_out, ctx, w['wuv_bd'], w['wo_g'], w['wo_m']))
    return x, c, krot, nconv, s_new


def kernel(x_prompt, x_sample, cache_ckv, cache_krope, state_conv, state_gdn, page_table, c_prompt, c_sample,
           ada_w, ada_b, norm_ffn1, ffn1_wi, ffn1_wo, norm_mix, w_in, gdn_conv_w, gdn_a_log, gdn_dt_bias, gdn_norm,
           mla_qn_norm, mla_qr_norm, mla_ckv_norm, mla_kr_norm, mla_kn_norm, mla_w_uk, mla_w_uv, w_out,
           norm_ffn2, ffn2_wi, ffn2_wo):
    depth = ada_w.shape[0]
    bp, tp, d = x_prompt.shape
    bs, ts, _ = x_sample.shape
    g_heads, dk, dv = state_gdn.shape[2:]
    conv_dim = state_conv.shape[-1]
    v_dim = g_heads * dv
    rank, heads, dn = mla_w_uk.shape[1:]
    dr = mla_qr_norm.shape[1]
    past = page_table.shape[1] * PAGE_SIZE
    dims = (heads, dn, dr, rank, g_heads, conv_dim, v_dim)
    lay = _InLayout(conv_dim, v_dim, heads)

    cs_p = _rope_cos_sin(jnp.arange(tp, dtype=F32), dr // 2)
    cs_s = jnp.tile(_rope_cos_sin(past + jnp.arange(ts, dtype=F32), dr // 2), (bs, 1))
    place, tab_bias = _rope_placement(dr // 2, dn)
    cond = jnp.concatenate([c_prompt, c_sample], axis=0)
    rows = -(-cond.shape[0] // SUBLANES) * SUBLANES
    cond = jnp.pad(cond, ((0, rows - cond.shape[0]), (0, 0)))

    yp = x_prompt
    ys = x_sample.reshape(1, bs * ts, d)
    outs_p, outs_s = [], []
    for l in range(depth):
        lp = dict(norm_ffn1=norm_ffn1[l], ffn1_wi=ffn1_wi[l], ffn1_wo=ffn1_wo[l], norm_mix=norm_mix[l], w_in=w_in[l],
                  gdn_conv_w=gdn_conv_w[l], gdn_a_log=gdn_a_log[l], gdn_dt_bias=gdn_dt_bias[l], gdn_norm=gdn_norm[l],
                  mla_qn_norm=mla_qn_norm[l], mla_qr_norm=mla_qr_norm[l], mla_ckv_norm=mla_ckv_norm[l],
                  mla_kr_norm=mla_kr_norm[l], mla_kn_norm=mla_kn_norm[l], mla_w_uk=mla_w_uk[l],
                  mla_w_uv=mla_w_uv[l], w_out=w_out[l], norm_ffn2=norm_ffn2[l], ffn2_wi=ffn2_wi[l],
                  ffn2_wo=ffn2_wo[l])
        w = _prep_layer(lp, dims)
        mods = _mods(cond, ada_w[l], ada_b[l])
        mods_p = mods[:bp].reshape(bp, N_MOD, d)
        mods_s = mods[bp:bp + bs].reshape(bs, N_MOD, d).transpose(1, 0, 2)

        conv0 = jnp.zeros((bp, SUBLANES, conv_dim), F32)
        s0 = jnp.zeros((bp, g_heads, dk, dv), F32)
        attend_p = lambda o: _attn_prompt(o[3], o[4], o[7])
        yp, c_p, kr_p, cv_p, s_p = _layer(yp, mods_p, (cs_p, place, tab_bias), conv0, s0, lp, w, lay, dims,
                                          t_valid=min(GDN_TILE, tp), chunk=min(GDN_CHUNK, tp), attend=attend_p)
        outs_p.append((c_p, kr_p.transpose(0, 2, 1), cv_p[:, SUBLANES - (GDN_CONV - 1):], s_p))


        def attend_s(o, l=l, w=w):
            q128, k128, c_new, qabs, qrope = o[3], o[4], o[5], o[7], o[8]
            by_seq = lambda a: a[0].reshape(heads, bs, ts, -1).transpose(1, 2, 0, 3)
            flat = lambda a: by_seq(a).reshape(bs, ts * heads, -1)
            ctx = _attn_sample(page_table, flat(qabs), flat(qrope)[..., :dr], flat(q128), by_seq(k128),
                               c_new.reshape(bs, ts, -1), w['wukt'], cache_ckv, jnp.swapaxes(cache_krope, 2, 3),
                               layer=l, heads=heads, dn=dn)
            return ctx.reshape(1, bs * ts, heads * rank)

        ys, c_s, kr_s, cv_s, s_s = _layer_sample(ys, mods_s, (cs_s, place, tab_bias), state_conv[l], state_gdn[l], lp, w, lay, dims,
                                                 bs, ts, attend_s)
        outs_s.append((c_s, kr_s, cv_s, s_s))

    stack = lambda outs, k: jnp.stack([o[k] for o in outs])
    ys = ys.reshape(bs, ts, d)
    return (yp, ys, stack(outs_p, 0), stack(outs_p, 1), stack(outs_p, 2), stack(outs_p, 3),
            stack(outs_s, 0), stack(outs_s, 1), stack(outs_s, 2), stack(outs_s, 3))


def _layer_sample(x, mods, rope, conv_prev, s0, lp, w, lay, dims, bs, ts, attend):
    heads, dn, dr, g_heads = dims[0], dims[1], dims[2], dims[4]
    x = _ffn(x, mods, lp['norm_ffn1'], *w['f1'], mod0=0, tokens_per_seq=ts)
    outs = _inproj(x, mods, lp['norm_mix'], w['w_all'], *rope, w['gq'], w['mq'], w['gc'], w['wuk'], w['mk'],
                   w['gk'], w['gkr'], w['wqa'], None, lay=lay, heads=heads, tokens_per_seq=ts, rope=(dn, dr))
    qkv, z, ba, c, krot = outs[0], outs[1], outs[2], outs[5], outs[6]
    seq = lambda a: a.reshape(bs, ts, a.shape[-1])
    minor = lambda a: seq(a).transpose(1, 2, 0)
    o_t, s_new = _gdn_sample(minor(qkv), conv_prev.transpose(1, 2, 0), minor(z), minor(ba[..., :g_heads]),
                             minor(ba[..., g_heads:2 * g_heads]), s0.transpose(1, 2, 3, 0), lp['gdn_conv_w'].T,
                             w['aparams'][:, g_heads:2 * g_heads], lp['gdn_norm'])
    g_heads_, dv_ = s0.shape[1], s0.shape[3]
    gdn_out = o_t.reshape(ts, g_heads_, dv_, bs).transpose(1, 3, 0, 2).reshape(1, g_heads_, bs * ts, dv_)
    ctx = attend(outs)
    x = _ffn(x, mods, lp['norm_ffn2'], *w['f2'], mod0=6, tokens_per_seq=ts,
             mix=(gdn_out, ctx, w['wuv_bd'], w['wo_g'], w['wo_m']))
    nconv = jnp.concatenate([conv_prev, seq(qkv)], axis=1)[:, -(GDN_CONV - 1):]
    return x, seq(c), seq(krot)[..., dn:dn + dr], nconv, s_new.transpose(3, 0, 1, 2)
```

```python
import functools

import jax
import jax.numpy as jnp
import numpy as np
from jax import lax
from jax.experimental import pallas as pl
from jax.experimental.pallas import tpu as pltpu

F32 = jnp.float32
BF16 = jnp.bfloat16

NORM_EPS = 1e-6
ROPE_THETA = 10000.0
PAGE_SIZE = 128
GDN_CONV = 4
GDN_CHUNK = 128
N_MOD = 9
LANES = 128
SUBLANES = 8
NEG_BIG = -1e30
VMEM_LIMIT = 56 * 1024 * 1024

TOKEN_TILE = 512
ATTN_TILE = 512
GDN_TILE = 512
SUB_PAGES = 16
AHEAD = 4


def _dot(a, b):
    return jnp.dot(a, b, preferred_element_type=F32)


def _dot_t(a, b):
    return lax.dot_general(a, b, (((1,), (1,)), ((), ())), preferred_element_type=F32)


def _dot_ta(a, b):
    return lax.dot_general(a, b, (((0,), (0,)), ((), ())), preferred_element_type=F32)


def _bmm(a, b):
    return lax.dot_general(a, b, (((2,), (1,)), ((0,), (0,))), preferred_element_type=F32)


def _bmm_t(a, b):
    return lax.dot_general(a, b, (((2,), (2,)), ((0,), (0,))), preferred_element_type=F32)


def _sigmoid(x):
    return 1.0 / (1.0 + jnp.exp(-x))


def _silu(x):
    return x * _sigmoid(x)


def _params(*sem):
    return pltpu.CompilerParams(dimension_semantics=sem, vmem_limit_bytes=VMEM_LIMIT)


def _resident(shape):
    nd = len(shape)
    return pl.BlockSpec(shape, lambda *_: (0,) * nd, pipeline_mode=pl.Buffered(1))


MODS_PER_SUBLAYER = 3


def _dot_split(x, e, terms, left=False):
    out = None
    for _ in range(terms):
        xb = x.astype(BF16)
        d = _dot(e, xb) if left else _dot(xb, e)
        out = d if out is None else out + d
        x = x - xb.astype(F32)
    return out


def _mod_rows(m_ref, ks, tokens_per_seq, rows):
    if tokens_per_seq is None:
        return [m_ref[k:k + 1, :] for k in ks]
    nseq = m_ref.shape[1]
    row = lax.broadcasted_iota(jnp.int32, (rows, nseq), 0)
    first = lax.broadcasted_iota(jnp.int32, (rows, nseq), 1) * tokens_per_seq
    spread = jnp.where((row >= first) & (row < first + tokens_per_seq), 1.0, 0.0).astype(BF16)
    return [_dot_split(m_ref[k % MODS_PER_SUBLAYER], spread, 3, left=True) for k in ks]


def _mod_norm(x, gain, shift, scale):
    y = x * lax.rsqrt(jnp.mean(x * x, axis=-1, keepdims=True) + NORM_EPS)
    return (y * gain) * (1.0 + scale) + shift


def _mods_spec(tokens_per_seq, tm, d, sublayer):
    if tokens_per_seq is not None:
        assert tm % tokens_per_seq == 0
        return pl.BlockSpec((MODS_PER_SUBLAYER, tm // tokens_per_seq, d), lambda b, i: (sublayer, i, 0))
    return pl.BlockSpec((None, N_MOD, d), lambda b, i: (b, 0, 0))


def _mods_body(c_ref, w_ref, b_ref, o_ref):
    c = c_ref[...]
    o_ref[...] = _dot(_silu(c).astype(BF16), w_ref[...].astype(BF16)) + b_ref[...]


def _mods(cond, ada_w, ada_b):
    rows, d = cond.shape
    n = ada_w.shape[1]
    tn = d
    return pl.pallas_call(
        _mods_body,
        grid=(n // tn,),
        in_specs=[pl.BlockSpec((rows, d), lambda j: (0, 0)),
                  pl.BlockSpec((d, tn), lambda j: (0, j)),
                  pl.BlockSpec((1, tn), lambda j: (0, j))],
        out_specs=pl.BlockSpec((rows, tn), lambda j: (0, j)),
        out_shape=jax.ShapeDtypeStruct((rows, n), F32),
        compiler_params=_params("arbitrary"),
        name="mods",
    )(cond, ada_w, ada_b.reshape(1, n))


def _ffn_body(x_ref, m_ref, g_ref, wi_ref, wo_ref, *rest, mod0, tokens_per_seq, fc, mixed):
    x = x_ref[...]
    if mixed:
        mm_ref, gdn_ref, ctx_ref, wuv_ref, wog_ref, wom_ref, o_ref = rest
        gate_k = mod0 - 1
        (gate_mix,) = _mod_rows(mm_ref if tokens_per_seq is not None else m_ref, [gate_k], tokens_per_seq, x.shape[0])
        mla = _dot(ctx_ref[...], wuv_ref[...])
        gdn = jnp.concatenate([gdn_ref[h] for h in range(gdn_ref.shape[0])], axis=1)
        mix = _dot(gdn.astype(BF16), wog_ref[...]) + _dot(mla.astype(BF16), wom_ref[...])
        x = x + gate_mix * mix
    else:
        (o_ref,) = rest
    shift, scale, gate = _mod_rows(m_ref, [mod0, mod0 + 1, mod0 + 2], tokens_per_seq, x.shape[0])
    h = _mod_norm(x, g_ref[...], shift, scale).astype(BF16)
    acc = jnp.zeros(x.shape, F32)
    dff = wo_ref.shape[0]
    for c in range(dff // fc):
        lo, hi = c * fc, (c + 1) * fc
        a = _silu(_dot(h, wi_ref[:, lo:hi])) * _dot(h, wi_ref[:, dff + lo:dff + hi])
        acc = acc + _dot(a.astype(BF16), wo_ref[lo:hi, :])
    o_ref[...] = x + 0.5 * gate * acc


def _ffn(x, mods, gain, wi, wo, *, mod0, tokens_per_seq, mix=None):
    nb, t, d = x.shape
    tm = min(TOKEN_TILE, t)
    dff = wo.shape[0]
    fc = 2 * LANES
    assert t % tm == 0 and dff % fc == 0
    body = functools.partial(_ffn_body, mod0=mod0, tokens_per_seq=tokens_per_seq, fc=fc, mixed=mix is not None)
    tok = lambda w: pl.BlockSpec((None, tm, w), lambda b, i: (b, i, 0))
    sub = mod0 // MODS_PER_SUBLAYER
    in_specs = [tok(d), _mods_spec(tokens_per_seq, tm, d, sub),
                _resident((1, d)), _resident((d, 2 * dff)), _resident((dff, d))]
    args = [x, mods, gain.reshape(1, d), wi, wo]
    if mix is not None:
        gdn_out, ctx, wuv_bd, wo_g, wo_m = mix
        in_specs += [_mods_spec(tokens_per_seq, tm, d, sub - 1),
                     pl.BlockSpec((None, gdn_out.shape[1], tm, gdn_out.shape[3]), lambda b, i: (b, 0, i, 0)),
                     tok(ctx.shape[-1]),
                     _resident(wuv_bd.shape), _resident(wo_g.shape), _resident(wo_m.shape)]
        args += [mods, gdn_out, ctx, wuv_bd, wo_g, wo_m]
    return pl.pallas_call(
        body,
        grid=(nb, t // tm),
        in_specs=in_specs,
        out_specs=tok(d),
        out_shape=jax.ShapeDtypeStruct(x.shape, F32),
        compiler_params=_params("arbitrary", "arbitrary"),
        name="ffn_mix" if mix is not None else "ffn",
    )(*args)


class _InLayout:
    def __init__(self, conv_dim, v_dim, heads):
        self.qkv = (0, conv_dim)
        self.z = (conv_dim, conv_dim + v_dim)
        self.ba = (self.z[1], self.z[1] + LANES)
        self.q = (self.ba[1], self.ba[1] + heads * LANES)
        self.ckv = (self.q[1], self.q[1] + LANES)
        self.kr = (self.ckv[1], self.ckv[1] + LANES)
        self.krs = (self.kr[1], self.kr[1] + LANES)
        self.total = self.krs[1]


def _inproj_body(x_ref, m_ref, g_ref, w_ref, cs_ref, place_ref, tbias_ref, gq_ref, mq_ref, gc_ref, wuk_ref, mk_ref,
                 gk_ref, gkr_ref, *rest, lay, heads, tokens_per_seq, sample, qk_dim, rope):
    if sample:
        wqa_ref, rest = rest[0], rest[1:]
    else:
        cprev_ref, cw_ref, ms_ref, rest = rest[0], rest[1], rest[2], rest[3:]
    qkv_ref, z_ref, ba_ref, q128_ref, k128_ref, c_ref, krot_ref = rest[:7]
    x = x_ref[...]
    shift, scale = _mod_rows(m_ref, [3, 4], tokens_per_seq, x.shape[0])
    h = _mod_norm(x, g_ref[...], shift, scale).astype(BF16)
    cw = 4 * LANES
    assert lay.total % cw == 0 and lay.qkv[1] % cw == 0 and (lay.z[1] - lay.z[0]) == cw
    n_chunks = lay.total // cw
    n_qkv = lay.qkv[1] // cw
    rest_chunks = [n_chunks - 1] + list(range(n_qkv, n_chunks - 1))
    order = []
    for c in range(max(n_qkv, len(rest_chunks))):
        order += ([c] if c < n_qkv else []) + ([rest_chunks[c]] if c < len(rest_chunks) else [])
    chunks = {}

    def cols(lo, hi):
        c = lo // cw
        assert (hi - 1) // cw == c
        return chunks[c][:, lo - c * cw:hi - c * cw]

    tm = x.shape[0]
    hist = SUBLANES
    taps = GDN_CONV - 1
    if not sample:
        ct_ref, nconv_ref, xbuf = rest[7], rest[8], rest[9]

        @pl.when(pl.program_id(1) == 0)
        def _():
            xbuf[0:hist, :] = cprev_ref[...]

    tab = _dot_split(cs_ref[...], place_ref[...], 3) + tbias_ref[...]
    tab_q, tab_c, tab_s = tab[:, :LANES], tab[:, LANES:2 * LANES], tab[:, 2 * LANES:]

    def group(lo):
        hi = lo + LANES
        if lo < lay.qkv[1]:
            if sample:
                qkv_ref[:, lo:hi] = cols(lo, hi)
                return
            xbuf[hist:hist + tm, lo:hi] = cols(lo, hi)
            conv = xbuf[hist - taps:hist - taps + tm, lo:hi] * cw_ref[0:1, lo:hi]
            for jj in range(1, GDN_CONV):
                conv = conv + xbuf[hist - taps + jj:hist - taps + jj + tm, lo:hi] * cw_ref[jj:jj + 1, lo:hi]
            yi = _silu(conv)
            if lo < 2 * qk_dim:
                ss = _dot((yi * yi).astype(BF16), ms_ref[...])
                yi = yi * (lax.rsqrt(ss + NORM_EPS) * ((LANES // 2) ** -0.5 if lo < qk_dim else 1.0))
            qkv_ref[2 * (lo // LANES)] = yi[:, :LANES // 2]
            qkv_ref[2 * (lo // LANES) + 1] = yi[:, LANES // 2:]
            tail = xbuf[tm:tm + hist, lo:hi]
            nconv_ref[:, lo:hi] = tail
            xbuf[0:hist, lo:hi] = tail
        elif lo < lay.z[1]:
            if sample:
                z_ref[:, lo - lay.z[0]:hi - lay.z[0]] = cols(lo, hi)
            else:
                zz = cols(lo, hi)
                z_ref[2 * ((lo - lay.z[0]) // LANES)] = zz[:, :LANES // 2]
                z_ref[2 * ((lo - lay.z[0]) // LANES) + 1] = zz[:, LANES // 2:]
        elif lo < lay.ba[1]:
            ba_ref[...] = cols(lo, hi)
        elif lo < lay.q[1]:
            hh = (lo - lay.q[0]) // LANES
            qh = cols(lo, hi)
            msq = _dot((qh * qh).astype(BF16), mq_ref[...])
            qn = qh * lax.rsqrt(msq + NORM_EPS) * gq_ref[...] * tab_q
            q128_ref[hh] = qn.astype(BF16)
            if sample:
                qabs_ref, qrope_ref = rest[7], rest[8]
                qa = _dot(qn.astype(BF16), wqa_ref[hh])
                qabs_ref[hh] = qa[:, :LANES].astype(BF16)
                qrope_ref[hh] = qa[:, LANES:2 * LANES].astype(BF16)
        elif lo == lay.ckv[0]:
            keys()

    def keys():
        ckv = cols(*lay.ckv)
        c = ckv * lax.rsqrt(jnp.mean(ckv * ckv, axis=-1, keepdims=True) + NORM_EPS) * gc_ref[...]
        c_ref[...] = c
        cb = c.astype(BF16)
        if not sample:
            ct_ref[...] = jnp.concatenate([c.T, jnp.ones((SUBLANES, c.shape[0]), F32)], axis=0).astype(BF16)
        kr = cols(*lay.kr)
        krs = cols(*lay.krs)
        inv = lax.rsqrt(jnp.sum(kr * kr, axis=-1, keepdims=True) * (2.0 / LANES) + NORM_EPS)
        krot = kr * inv * gkr_ref[0:1, :] * tab_c + krs * inv * gkr_ref[1:2, :] * tab_s
        if sample:
            krot_ref[...] = krot
        else:
            krot_ref[...] = krot.T[rope[0]:rope[0] + rope[1], :]
        knr = _dot(cb, wuk_ref[...])
        for hh in range(heads):
            kh = knr[:, hh * LANES:(hh + 1) * LANES]
            msq = _dot((kh * kh).astype(BF16), mk_ref[...])
            k128_ref[hh] = (kh * lax.rsqrt(msq + NORM_EPS) * gk_ref[...] + krot).astype(BF16)

    ahead = 2
    for i in range(len(order) + ahead):
        if i < len(order):
            k = order[i]
            chunks[k] = _dot(h, w_ref[:, k * cw:(k + 1) * cw])
        if i >= ahead:
            k = order[i - ahead]
            for lo in range(k * cw, (k + 1) * cw, LANES):
                group(lo)


def _inproj(x, mods, gain, w_all, cs, place, tab_bias, gq, mq, gc, wuk, mk, gk, gkr, wqa, conv, *, lay, heads,
            tokens_per_seq, rope):
    nb, t, d = x.shape
    tm = min(TOKEN_TILE, t)
    assert t % tm == 0
    sample = wqa is not None
    body = functools.partial(_inproj_body, lay=lay, heads=heads, tokens_per_seq=tokens_per_seq, sample=sample,
                             qk_dim=(lay.qkv[1] - (lay.z[1] - lay.z[0])) // 2, rope=rope)
    tok = lambda w: pl.BlockSpec((None, tm, w), lambda b, i: (b, i, 0))
    hd = lambda w: pl.BlockSpec((None, heads, tm, w), lambda b, i: (b, 0, i, 0))
    conv_dim, v_dim = lay.qkv[1], lay.z[1] - lay.z[0]
    in_specs = [tok(d), _mods_spec(tokens_per_seq, tm, d, 1), _resident((1, d)), _resident(w_all.shape),
                pl.BlockSpec((tm, cs.shape[1]), lambda b, i: (i, 0)), _resident(place.shape),
                _resident(tab_bias.shape),
                _resident((1, LANES)), _resident((LANES, LANES)), _resident((1, LANES)),
                _resident(wuk.shape), _resident((LANES, LANES)), _resident((1, LANES)), _resident((2, LANES))]
    args = [x, mods, gain.reshape(1, d), w_all, cs, place, tab_bias, gq, mq, gc, wuk, mk, gk, gkr]
    out_specs = [tok(conv_dim), tok(v_dim), tok(LANES), hd(LANES), hd(LANES), tok(LANES), tok(LANES)]
    out_shape = [jax.ShapeDtypeStruct((nb, t, conv_dim), F32), jax.ShapeDtypeStruct((nb, t, v_dim), F32),
                 jax.ShapeDtypeStruct((nb, t, LANES), F32),
                 jax.ShapeDtypeStruct((nb, heads, t, LANES), BF16), jax.ShapeDtypeStruct((nb, heads, t, LANES), BF16),
                 jax.ShapeDtypeStruct((nb, t, LANES), F32), jax.ShapeDtypeStruct((nb, t, LANES), F32)]
    if not sample:
        out_specs[6] = pl.BlockSpec((None, rope[1], tm), lambda b, i: (b, 0, i))
        out_shape[6] = jax.ShapeDtypeStruct((nb, rope[1], t), F32)
        hw = LANES // 2
        out_specs[0] = pl.BlockSpec((None, conv_dim // hw, tm, hw), lambda b, i: (b, 0, i, 0))
        out_specs[1] = pl.BlockSpec((None, v_dim // hw, tm, hw), lambda b, i: (b, 0, i, 0))
        out_shape[0] = jax.ShapeDtypeStruct((nb, conv_dim // hw, t, hw), F32)
        out_shape[1] = jax.ShapeDtypeStruct((nb, v_dim // hw, t, hw), F32)
    if sample:
        in_specs.append(_resident(wqa.shape))
        args.append(wqa)
        out_specs += [hd(LANES), hd(LANES)]
        out_shape += [jax.ShapeDtypeStruct((nb, heads, t, LANES), BF16)] * 2
        scratch = []
    else:
        conv_prev, conv_w, ms = conv
        in_specs += [pl.BlockSpec((None, SUBLANES, conv_dim), lambda b, i: (b, 0, 0)),
                     _resident(conv_w.shape), _resident(ms.shape)]
        args += [conv_prev, conv_w, ms]
        out_specs += [pl.BlockSpec((None, LANES + SUBLANES, tm), lambda b, i: (b, 0, i)),
                      pl.BlockSpec((None, SUBLANES, conv_dim), lambda b, i: (b, 0, 0))]
        out_shape += [jax.ShapeDtypeStruct((nb, LANES + SUBLANES, t), BF16),
                      jax.ShapeDtypeStruct((nb, SUBLANES, conv_dim), F32)]
        scratch = [pltpu.VMEM((tm + SUBLANES, conv_dim), F32)]
    return pl.pallas_call(
        body, grid=(nb, t // tm), in_specs=in_specs, out_specs=out_specs, out_shape=out_shape,
        scratch_shapes=scratch, compiler_params=_params("arbitrary", "arbitrary"), name="inproj",
    )(*args)


def _unit_lower_inverse(a):
    n = a.shape[-1]
    assert n & (n - 1) == 0
    row = lax.broadcasted_iota(jnp.int32, (n, n), 0)
    col = lax.broadcasted_iota(jnp.int32, (n, n), 1)
    x = jnp.broadcast_to(jnp.where(row == col, 1.0, 0.0), a.shape)
    b = 1
    while b < n:
        lo_mask = (jnp.bitwise_xor(row, col) < 2 * b) & (jnp.bitwise_and(row, b) != 0) & (jnp.bitwise_and(col, b) == 0)
        lo = jnp.where(lo_mask, a, 0.0)
        if b == 1:
            x = x - lo
        else:
            xb = x.astype(BF16)
            x = x - _bmm(xb, _bmm(lo.astype(BF16), xb).astype(BF16))
        b *= 2
    return x


def _gdn_body(qkv_ref, z_ref, ba_ref, s0_ref, ap_ref, ng_ref, o_ref, sout_ref, s_scr,
              *, tt, chunk, heads, dk, dv):
    j = pl.program_id(1)

    @pl.when(j == 0)
    def _():
        s_scr[...] = s0_ref[...]

    ba = ba_ref[...]
    beta = _sigmoid(ba)
    xg = ba + ap_ref[1:2, :]
    g = -ap_ref[0:1, :] * (jnp.maximum(xg, 0.0) + jnp.log1p(jnp.exp(-jnp.abs(xg))))
    row = lax.broadcasted_iota(jnp.int32, (tt, tt), 0)
    col = lax.broadcasted_iota(jnp.int32, (tt, tt), 1)
    assert chunk & (chunk - 1) == 0
    tri = jnp.where((row >= col) & (jnp.bitwise_xor(row, col) < chunk), 1.0, 0.0).astype(BF16)
    g_hi = g.astype(BF16)
    g_lo = (g - g_hi.astype(F32)).astype(BF16)
    gc = _dot(tri, g_hi) + _dot(tri, g_lo)
    gct = gc.T

    crow = lax.broadcasted_iota(jnp.int32, (chunk, chunk), 0)
    ccol = lax.broadcasted_iota(jnp.int32, (chunk, chunk), 1)
    causal = crow >= ccol
    strict = crow > ccol
    stack = lambda xs: jnp.stack(xs, axis=0)

    n_chunks = tt // chunk
    parts = []

    def prepare(c):
        r0, r1 = c * chunk, (c + 1) * chunk
        q3 = qkv_ref[0:heads, r0:r1, :]
        k3 = qkv_ref[heads:2 * heads, r0:r1, :]
        v3 = qkv_ref[2 * heads:3 * heads, r0:r1, :]
        b3 = stack([beta[r0:r1, h:h + 1] for h in range(heads)])
        gcol = stack([gc[r0:r1, heads + h:heads + h + 1] for h in range(heads)])
        grow = stack([gct[heads + h:heads + h + 1, r0:r1] for h in range(heads)])
        glast = grow[:, :, chunk - 1:chunk]
        eg = jnp.exp(gcol)
        kb = k3 * b3
        decay3 = jnp.exp(jnp.where(causal, gcol - grow, NEG_BIG))
        sc = _bmm_t(jnp.concatenate([q3, kb], axis=1).astype(BF16), k3.astype(BF16))
        qkm = (sc[:, :chunk] * decay3).astype(BF16)
        m = jnp.where(strict, sc[:, chunk:] * decay3, 0.0)
        return (m, qkm, (v3 * b3).astype(BF16), (kb * eg).astype(BF16), (q3 * eg).astype(BF16),
                (k3 * jnp.exp(glast - gcol)).astype(BF16), jnp.exp(glast))

    group = 2
    tinvs = []
    for c0 in range(0, n_chunks, group):
        cs = list(range(c0, min(c0 + group, n_chunks)))
        parts += [prepare(c) for c in cs]
        inv = _unit_lower_inverse(jnp.concatenate([parts[c][0] for c in cs], axis=0)).astype(BF16)
        for i, c in enumerate(cs):
            tinv = inv[i * heads:(i + 1) * heads]
            tinvs.append((_bmm(tinv, parts[c][2]), _bmm(tinv, parts[c][3]).astype(BF16)))
    s3 = s_scr[...]
    for c in range(n_chunks):
        r0, r1 = c * chunk, (c + 1) * chunk
        qkm, qd3, kd3, gl3 = parts[c][1], parts[c][4], parts[c][5], parts[c][6]
        u, w = tinvs[c]
        sb = s3.astype(BF16)
        ws_qs = _bmm(jnp.concatenate([w, qd3], axis=1), sb)
        vnb = (u - ws_qs[:, :chunk]).astype(BF16)
        o = ws_qs[:, chunk:] + _bmm(qkm, vnb)
        upd = stack([_dot_ta(kd3[h], vnb[h]) for h in range(heads)])
        s3 = s3 * gl3 + upd
        on = o * lax.rsqrt(jnp.mean(o * o, axis=-1, keepdims=True) + NORM_EPS) * ng_ref[...]
        o_ref[:, r0:r1, :] = on * _silu(z_ref[:, r0:r1, :])
    s_scr[...] = s3
    sout_ref[...] = s3


def _gdn(qkv, z, ba, s0, aparams, norm_g, *, chunk):
    nb, slabs, t, _ = qkv.shape
    heads, dk, dv = s0.shape[1:]
    assert slabs == 3 * heads and dk == dv
    tt = min(GDN_TILE, t)
    assert t % tt == 0 and tt % chunk == 0
    body = functools.partial(_gdn_body, tt=tt, chunk=chunk, heads=heads, dk=dk, dv=dv)
    tok = lambda w: pl.BlockSpec((None, tt, w), lambda b, i: (b, i, 0))
    slab = lambda n: pl.BlockSpec((None, n, tt, dk), lambda b, i: (b, 0, i, 0))
    return pl.pallas_call(
        body,
        grid=(nb, t // tt),
        in_specs=[slab(3 * heads), slab(heads), tok(LANES),
                  pl.BlockSpec((None, heads, dk, dv), lambda b, i: (b, 0, 0, 0)),
                  _resident((2, LANES)), _resident((1, dv))],
        out_specs=[slab(heads), pl.BlockSpec((None, heads, dk, dv), lambda b, i: (b, 0, 0, 0))],
        out_shape=[jax.ShapeDtypeStruct((nb, heads, t, dv), F32), jax.ShapeDtypeStruct((nb, heads, dk, dv), F32)],
        scratch_shapes=[pltpu.VMEM((heads, dk, dv), F32)],
        compiler_params=_params("arbitrary", "arbitrary"),
        name="gdn",
    )(qkv, z, ba, s0, aparams, norm_g.reshape(1, dv))


def _gdn_sample_body(xq_ref, xk_ref, xv_ref, pq_ref, pk_ref, pv_ref, wq_ref, wk_ref, wv_ref, z_ref, b_ref, a_ref,
                     ap_ref, ng_ref, s_ref, o_ref, sout_ref, kq_scr, *, ts, heads, dk, dv):
    h = pl.program_id(0)
    nb = xq_ref.shape[-1]

    def conv(x_ref, p_ref, w_ref):
        taps = GDN_CONV - 1
        xin = [p_ref[i] for i in range(taps)] + [x_ref[t] for t in range(ts)]
        w = [jnp.broadcast_to(w_ref[:, jj:jj + 1], xin[0].shape) for jj in range(GDN_CONV)]
        out = []
        for t in range(ts):
            acc = xin[t] * w[0]
            for jj in range(1, GDN_CONV):
                acc = acc + xin[t + jj] * w[jj]
            out.append(_silu(acc))
        return out

    def l2(x):
        return x * lax.rsqrt(jnp.sum(x * x, axis=0, keepdims=True) + NORM_EPS)

    q = [l2(x) * (dk ** -0.5) for x in conv(xq_ref, pq_ref, wq_ref)]
    k = [l2(x) for x in conv(xk_ref, pk_ref, wk_ref)]
    v = conv(xv_ref, pv_ref, wv_ref)
    for t in range(ts):
        kq_scr[t] = k[t]
        kq_scr[ts + t] = q[t]
    a_h = ap_ref[0, h]
    dt_h = ap_ref[1, h]
    beta, decay = [], []
    for t in range(ts):
        beta.append(_sigmoid(b_ref[t, pl.ds(h, 1), :]))
        xg = a_ref[t, pl.ds(h, 1), :] + dt_h
        decay.append(jnp.exp(-a_h * (jnp.maximum(xg, 0.0) + jnp.log1p(jnp.exp(-jnp.abs(xg))))))

    def row(i, kk):
        return jnp.broadcast_to(kq_scr[i, pl.ds(kk, 1), :], (dv, nb))

    def first(kk, ks):
        return ks + row(0, kk) * s_ref[kk]

    ks = lax.fori_loop(0, dk, first, jnp.zeros((dv, nb), F32), unroll=8)
    ng = jnp.broadcast_to(ng_ref[...], (dv, nb))
    for t in range(ts):
        d = beta[t] * (v[t] - decay[t] * ks)
        src = s_ref if t == 0 else sout_ref
        last = t == ts - 1

        def step(kk, carry, t=t, d=d, src=src, last=last):
            o_acc, ks_next = carry
            s_new = decay[t] * src[kk] + row(t, kk) * d
            sout_ref[kk] = s_new
            o_acc = o_acc + row(ts + t, kk) * s_new
            if not last:
                ks_next = ks_next + row(t + 1, kk) * s_new
            return o_acc, ks_next

        zero = jnp.zeros((dv, nb), F32)
        o, ks = lax.fori_loop(0, dk, step, (zero, zero), unroll=8)
        on = o * lax.rsqrt(jnp.mean(o * o, axis=0, keepdims=True) + NORM_EPS) * ng
        o_ref[t] = on * _silu(z_ref[t])


def _gdn_sample(x_t, prev_t, z_t, b_t, a_t, s_t, conv_w_t, aparams, norm_g):
    ts, conv_dim, nb = x_t.shape
    heads, dk, dv, _ = s_t.shape
    assert dk == dv and conv_dim == 3 * heads * dk
    taps = GDN_CONV - 1
    body = functools.partial(_gdn_sample_body, ts=ts, heads=heads, dk=dk, dv=dv)
    part = lambda rows, off: pl.BlockSpec((rows, dk, nb), lambda h: (0, off + h, 0))
    wpart = lambda off: pl.BlockSpec((dk, GDN_CONV), lambda h: (off + h, 0))
    whole = lambda a: pl.BlockSpec(a.shape, lambda h: (0,) * a.ndim)
    state = pl.BlockSpec((None, dk, dv, nb), lambda h: (h, 0, 0, 0))
    return pl.pallas_call(
        body,
        grid=(heads,),
        in_specs=[part(ts, 0), part(ts, heads), part(ts, 2 * heads),
                  part(taps, 0), part(taps, heads), part(taps, 2 * heads),
                  wpart(0), wpart(heads), wpart(2 * heads),
                  part(ts, 0), whole(b_t), whole(a_t),
                  pl.BlockSpec(memory_space=pltpu.SMEM), pl.BlockSpec((dv, 1), lambda h: (0, 0)), state],
        out_specs=[part(ts, 0), state],
        out_shape=[jax.ShapeDtypeStruct((ts, heads * dv, nb), F32), jax.ShapeDtypeStruct(s_t.shape, F32)],
        scratch_shapes=[pltpu.VMEM((2 * ts, dk, nb), F32)],
        compiler_params=_params("arbitrary"),
        name="gdn_sample",
    )(x_t, x_t, x_t, prev_t, prev_t, prev_t, conv_w_t, conv_w_t, conv_w_t, z_t, b_t, a_t, aparams,
      norm_g.reshape(dv, 1), s_t)


def _attn_body(qi_ref, kj_ref, q_ref, k_ref, ct_ref, o_ref, m_scr, acc_scr, *, heads, tq, rank):
    i = qi_ref[pl.program_id(1)]
    j = kj_ref[pl.program_id(1)]

    @pl.when(j == 0)
    def _():
        m_scr[...] = jnp.full(m_scr.shape, NEG_BIG, F32)
        acc_scr[...] = jnp.zeros(acc_scr.shape, F32)

    def step(masked):
        ct = ct_ref[...]
        m_old = [m_scr[h] for h in range(heads)]
        acc_old = [acc_scr[h] for h in range(heads)]
        if masked:
            key = lax.broadcasted_iota(jnp.int32, (tq, tq), 0)
            qry = lax.broadcasted_iota(jnp.int32, (tq, tq), 1)
            keep = key <= qry
        m_out, acc_out = [], []
        ahead = 3
        scores = [_dot_t(k_ref[h], q_ref[h]) for h in range(ahead)]
        for h in range(heads):
            if h + ahead < heads:
                scores.append(_dot_t(k_ref[h + ahead], q_ref[h + ahead]))
            st = scores[h]
            if masked:
                st = jnp.where(keep, st, NEG_BIG)
            m_new = jnp.maximum(m_old[h], jnp.max(st, axis=0, keepdims=True))
            alpha = jnp.exp(m_old[h] - m_new)
            pt = jnp.exp(st - m_new).astype(BF16)
            acc_out.append(acc_old[h] * alpha + _dot(ct, pt))
            m_out.append(m_new)
        for h in range(heads):
            m_scr[h] = m_out[h]
            acc_scr[h] = acc_out[h]
        return acc_out

    @pl.when(j < i)
    def _():
        step(False)

    @pl.when(j == i)
    def _():
        acc = step(True)
        for h in range(heads):
            ctx = acc[h][:rank, :] / acc[h][rank:rank + 1, :]
            o_ref[:, h * rank:(h + 1) * rank] = ctx.T.astype(BF16)


def _attn_prompt(q128, k128, ct_ext):
    nb, heads, t, _ = q128.shape
    rows = ct_ext.shape[1]
    rank = rows - SUBLANES
    tq = min(ATTN_TILE, t)
    assert t % tq == 0
    n = t // tq
    body = functools.partial(_attn_body, heads=heads, tq=tq, rank=rank)
    pairs = [(i, j) for i in range(n) for j in range(i + 1)]
    qi = jnp.asarray([p[0] for p in pairs], jnp.int32)
    kj = jnp.asarray([p[1] for p in pairs], jnp.int32)
    grid_spec = pltpu.PrefetchScalarGridSpec(
        num_scalar_prefetch=2,
        grid=(nb, len(pairs)),
        in_specs=[pl.BlockSpec((None, heads, tq, LANES), lambda b, s, qi, kj: (b, 0, qi[s], 0)),
                  pl.BlockSpec((None, heads, tq, LANES), lambda b, s, qi, kj: (b, 0, kj[s], 0)),
                  pl.BlockSpec((None, rows, tq), lambda b, s, qi, kj: (b, 0, kj[s]))],
        out_specs=pl.BlockSpec((None, tq, heads * rank), lambda b, s, qi, kj: (b, qi[s], 0)),
        scratch_shapes=[pltpu.VMEM((heads, 1, tq), F32), pltpu.VMEM((heads, rows, tq), F32)],
    )
    return pl.pallas_call(
        body, grid_spec=grid_spec,
        out_shape=jax.ShapeDtypeStruct((nb, t, heads * rank), BF16),
        compiler_params=_params("arbitrary", "arbitrary"),
        name="attn_prompt",
    )(qi, kj, q128, k128, ct_ext)


def _attn_sample_body(pt_ref, qabs_ref, qrope_ref, q128_ref, k128n_ref, cn_ref, wukt_ref, cache_c, cache_kr,
                      o_ref, cbuf, kbuf, cb, krb, s_all, sem, *, layer, heads, dn, s_new, n_pages, sub_pages):
    b = pl.program_id(0)
    nseq = pl.num_programs(0)
    slot = lax.rem(b, 2)
    rows = s_new * heads
    n_sub = n_pages // sub_pages
    sub_keys = sub_pages * PAGE_SIZE

    def page_copies(seq, slot_):
        out = []
        for i in range(n_pages):
            page = pt_ref[seq * n_pages + i]
            out.append(pltpu.make_async_copy(cache_c.at[layer, page],
                                             cbuf.at[slot_, pl.ds(i * PAGE_SIZE, PAGE_SIZE)], sem.at[0, slot_]))
            out.append(pltpu.make_async_copy(cache_kr.at[layer, page], kbuf.at[slot_, i], sem.at[1, slot_]))
        return out

    @pl.when(b == 0)
    def _():
        for cp in page_copies(0, 0):
            cp.start()

    waits = page_copies(b, slot)
    for cp in waits[0::2] + waits[1::2]:
        cp.wait()

    nxt = lax.rem(b + 1, nseq)

    lhs = jnp.concatenate([wukt_ref[...], qabs_ref[...]], axis=0)
    nk = heads * dn
    qrope = qrope_ref[...]

    def scores(sb):
        for i in range(sub_pages):
            pg = sb * sub_pages + i
            cb[sb, i * PAGE_SIZE:(i + 1) * PAGE_SIZE, :] = cbuf[slot, pg * PAGE_SIZE:(pg + 1) * PAGE_SIZE, :].astype(BF16)
            krb[sb, :, i * PAGE_SIZE:(i + 1) * PAGE_SIZE] = kbuf[slot, pg].astype(BF16)
        kq = _dot_t(lhs, cb[sb])
        ssq = jnp.concatenate([jnp.sum(kq[h * dn:(h + 1) * dn, :] ** 2, axis=0, keepdims=True)
                               for h in range(heads)], axis=0)
        r = lax.rsqrt(ssq * (1.0 / dn) + NORM_EPS)
        s = kq[nk:, :] * jnp.concatenate([r] * s_new, axis=0) + _dot(qrope, krb[sb])
        s_all[sb] = s
        return jnp.max(s, axis=1, keepdims=True)

    state = (jnp.full((rows, 1), NEG_BIG, F32), jnp.zeros((rows, 1), F32), jnp.zeros((rows, LANES), F32))

    def absorb(state, sb, m_sb):
        m_run, l, acc = state
        m_new = jnp.maximum(m_run, m_sb)
        alpha = jnp.exp(m_run - m_new)
        p = jnp.exp(s_all[sb] - m_new)
        return (m_new, alpha * l + jnp.sum(p, axis=1, keepdims=True), alpha * acc + _dot(p.astype(BF16), cb[sb]))

    ahead = AHEAD
    maxes = []
    for sb in range(n_sub):
        maxes.append(scores(sb))
        if sb == 0:
            for cp in page_copies(nxt, 1 - slot):
                cp.start()
        if sb >= ahead:
            state = absorb(state, sb - ahead, maxes[sb - ahead])
    for sb in range(max(n_sub - ahead, 0), n_sub):
        state = absorb(state, sb, maxes[sb])
    m_run, l, acc = state

    qf = q128_ref[...].astype(F32)
    assert heads & (heads - 1) == 0
    tok = lax.shift_right_logical(lax.broadcasted_iota(jnp.int32, (rows, 1), 0), heads.bit_length() - 1)
    sn = []
    m_fin = m_run
    for t in range(s_new):
        kt = jnp.concatenate([k128n_ref[t].astype(F32)] * s_new, axis=0)
        st = jnp.sum(qf * kt, axis=1, keepdims=True)
        sn.append(jnp.where(tok >= t, st, NEG_BIG))
        m_fin = jnp.maximum(m_fin, sn[-1])
    alpha = jnp.exp(m_run - m_fin)
    l = alpha * l
    acc = alpha * acc
    cn = cn_ref[...].astype(BF16).astype(F32)
    for t in range(s_new):
        pt = jnp.exp(sn[t] - m_fin)
        l = l + pt
        acc = acc + pt.astype(BF16).astype(F32) * cn[t:t + 1, :]
    o_ref[...] = (acc / l).astype(BF16)

    @pl.when(b == nseq - 1)
    def _():
        tail = page_copies(nxt, 1 - slot)
        for cp in tail[0::2] + tail[1::2]:
            cp.wait()


def _attn_sample(page_table, qabs, qrope, q128, k128n, c_new, wukt, cache_c, cache_kr, *, layer, heads, dn):
    nseq, rows, _ = qabs.shape
    s_new = rows // heads
    n_pages = page_table.shape[1]
    rank = cache_c.shape[-1]
    rope = cache_kr.shape[-2]
    sub_pages = min(SUB_PAGES, n_pages)
    assert n_pages % sub_pages == 0
    n_sub = n_pages // sub_pages
    sub_keys = sub_pages * PAGE_SIZE
    body = functools.partial(_attn_sample_body, layer=layer, heads=heads, dn=dn, s_new=s_new, n_pages=n_pages,
                             sub_pages=sub_pages)
    per_seq = lambda shape: pl.BlockSpec((None,) + shape, lambda b, pt: (b,) + (0,) * len(shape))
    grid_spec = pltpu.PrefetchScalarGridSpec(
        num_scalar_prefetch=1,
        grid=(nseq,),
        in_specs=[per_seq((rows, LANES)), per_seq((rows, rope)), per_seq((rows, LANES)),
                  per_seq((s_new, heads, LANES)), per_seq((s_new, LANES)),
                  pl.BlockSpec(wukt.shape, lambda b, pt: (0, 0)),
                  pl.BlockSpec(memory_space=pl.ANY), pl.BlockSpec(memory_space=pl.ANY)],
        out_specs=per_seq((rows, LANES)),
        scratch_shapes=[pltpu.VMEM((2, n_pages * PAGE_SIZE, rank), F32),
                        pltpu.VMEM((2, n_pages, rope, PAGE_SIZE), F32),
                        pltpu.VMEM((n_sub, sub_keys, rank), BF16),
                        pltpu.VMEM((n_sub, rope, sub_keys), BF16),
                        pltpu.VMEM((n_sub, rows, sub_keys), F32),
                        pltpu.SemaphoreType.DMA((2, 2))],
    )
    return pl.pallas_call(
        body, grid_spec=grid_spec,
        out_shape=jax.ShapeDtypeStruct((nseq, rows, LANES), BF16),
        compiler_params=_params("arbitrary"),
        name="attn_sample",
    )(page_table.reshape(-1), qabs, qrope, q128, k128n, c_new, wukt, cache_c, cache_kr)


def _rope_cos_sin(pos, half):
    inv = ROPE_THETA ** (-jnp.arange(half, dtype=F32) / half)
    t = pos.shape[0]
    if (t * half) % LANES == 0:
        ang = jnp.repeat(pos, half).reshape(-1, LANES) * jnp.tile(inv, t).reshape(-1, LANES)
        return jnp.concatenate([jnp.cos(ang).reshape(t, half), jnp.sin(ang).reshape(t, half)], axis=1)
    ang = pos[:, None] * inv[None, :]
    return jnp.concatenate([jnp.cos(ang), jnp.sin(ang)], axis=1)


def _rope_placement(half, dn):
    j = np.arange(half)
    place = np.zeros((2 * half, 3 * LANES), np.float32)
    dr = 2 * half
    for base, kind in ((dn, 'c'), (dn + dr, 's'), (LANES + dn, 'c'), (LANES + dn + dr, 'c'),
                       (2 * LANES + dn, 's'), (2 * LANES + dn + dr, 's')):
        if kind == 'c':
            place[j, base + j] = 1.0
            place[j, base + half + j] = 1.0
        else:
            place[half + j, base + j] = -1.0
            place[half + j, base + half + j] = 1.0
    bias = np.zeros((1, 3 * LANES), np.float32)
    bias[0, :dn] = 1.0
    return jnp.asarray(place, BF16), jnp.asarray(bias)


def _swap_halves(a, axis=-1):
    lo, hi = jnp.split(a, 2, axis=axis)
    return jnp.concatenate([hi, lo], axis=axis)


def _prep_layer(lp, dims):
    heads, dn, dr, rank, g_heads, conv_dim, v_dim = dims
    assert dn + 2 * dr == LANES and rank == LANES and 2 * g_heads <= LANES
    w_in = lp['w_in']
    d = w_in.shape[0]
    o = 0
    w_qkv = w_in[:, o:o + conv_dim]; o += conv_dim
    w_z = w_in[:, o:o + v_dim]; o += v_dim
    w_b = w_in[:, o:o + g_heads]; o += g_heads
    w_a = w_in[:, o:o + g_heads]; o += g_heads
    w_q = w_in[:, o:o + heads * (dn + dr)].reshape(d, heads, dn + dr); o += heads * (dn + dr)
    w_c = w_in[:, o:o + rank]; o += rank
    w_kr = w_in[:, o:o + dr]
    zeros = lambda n: jnp.zeros((d, n), F32)
    w_q128 = jnp.concatenate([w_q, _swap_halves(w_q[:, :, dn:])], axis=2).reshape(d, heads * LANES)
    w_krs = _swap_halves(w_kr)
    w_all = jnp.concatenate([w_qkv, w_z, w_b, w_a, zeros(LANES - 2 * g_heads), w_q128, w_c,
                             zeros(dn), w_kr, w_kr, zeros(dn), w_krs, w_krs], axis=1).astype(BF16)
    scale = (dn + dr) ** -0.5
    qr_g = lp['mla_qr_norm']
    gq = (jnp.concatenate([lp['mla_qn_norm'], qr_g, _swap_halves(qr_g)]) * scale).reshape(1, LANES)
    lane = np.arange(LANES)
    seg = np.where(lane < dn, 0, np.where(lane < dn + dr, 1, 2))
    seg_len = np.where(lane < dn, dn, dr).astype(np.float32)
    mq = jnp.asarray(np.where(seg[:, None] == seg[None, :], 1.0 / seg_len[None, :], 0.0), BF16)
    mk = jnp.asarray(np.where((lane[:, None] < dn) & (lane[None, :] < dn), 1.0 / dn, 0.0), BF16)
    gk = jnp.concatenate([lp['mla_kn_norm'], jnp.zeros((LANES - dn,), F32)]).reshape(1, LANES)
    kr_g = lp['mla_kr_norm']
    zdn = jnp.zeros((dn,), F32)
    gkr = jnp.stack([jnp.concatenate([zdn, kr_g, kr_g]),
                     jnp.concatenate([zdn, _swap_halves(kr_g), _swap_halves(kr_g)])])
    w_uk = lp['mla_w_uk']
    wuk = jnp.concatenate([w_uk, jnp.zeros((rank, heads, LANES - dn), F32)], axis=2)
    wuk = wuk.reshape(rank, heads * LANES).astype(BF16)
    wukt = w_uk.transpose(1, 2, 0).reshape(heads * dn, rank).astype(BF16)
    absorb = jnp.concatenate([w_uk.transpose(1, 2, 0) * lp['mla_kn_norm'][None, :, None],
                              jnp.zeros((heads, LANES - dn, rank), F32)], axis=1)
    fold = np.zeros((LANES, LANES), np.float32)
    fold[dn + np.arange(dr), np.arange(dr)] = 1.0
    fold[dn + dr + np.arange(dr), np.arange(dr)] = 1.0
    wqa = jnp.concatenate([absorb, jnp.broadcast_to(jnp.asarray(fold), (heads, LANES, LANES))], axis=2).astype(BF16)
    w_uv = lp['mla_w_uv']
    mv = w_uv.shape[2]
    wuv_bd = (w_uv.transpose(1, 0, 2)[:, :, None, :] * jnp.asarray(np.eye(heads, dtype=np.float32))[:, None, :, None])
    wuv_bd = wuv_bd.reshape(heads * rank, heads * mv).astype(BF16)
    lane_h = lane // (LANES // 2)
    ms = jnp.asarray(np.where(lane_h[:, None] == lane_h[None, :], 1.0, 0.0), BF16)
    aparams = jnp.pad(jnp.stack([jnp.exp(lp['gdn_a_log']), lp['gdn_dt_bias']]),
                      ((0, 0), (g_heads, LANES - 2 * g_heads)))
    return dict(
        w_all=w_all, gq=gq, mq=mq, gc=lp['mla_ckv_norm'].reshape(1, LANES), wuk=wuk, mk=mk, gk=gk, gkr=gkr,
        wukt=wukt, wqa=wqa, wuv_bd=wuv_bd, ms=ms, aparams=aparams,
        wo_g=lp['w_out'][:v_dim].astype(BF16), wo_m=lp['w_out'][v_dim:].astype(BF16),
        f1=(lp['ffn1_wi'].astype(BF16), lp['ffn1_wo'].astype(BF16)),
        f2=(lp['ffn2_wi'].astype(BF16), lp['ffn2_wo'].astype(BF16)),
    )


def _layer(x, mods, rope, conv_prev, s0, lp, w, lay, dims, *, chunk, attend):
    heads = dims[0]
    x = _ffn(x, mods, lp['norm_ffn1'], *w['f1'], mod0=0, tokens_per_seq=None)
    outs = _inproj(x, mods, lp['norm_mix'], w['w_all'], *rope, w['gq'], w['mq'], w['gc'], w['wuk'], w['mk'],
                   w['gk'], w['gkr'], None, (conv_prev, lp['gdn_conv_w'], w['ms']), lay=lay, heads=heads,
                   tokens_per_seq=None, rope=(dims[1], dims[2]))
    qkv, z, ba, c, krot, nconv = outs[0], outs[1], outs[2], outs[5], outs[6], outs[8]
    gdn_out, s_new = _gdn(qkv, z, ba, s0, w['aparams'], lp['gdn_norm'], chunk=chunk)
    ctx = attend(outs)
    x = _ffn(x, mods, lp['norm_ffn2'], *w['f2'], mod0=6, tokens_per_seq=None,
             mix=(gdn_out, ctx, w['wuv_bd'], w['wo_g'], w['wo_m']))
    return x, c, krot, nconv, s_new


def kernel(x_prompt, x_sample, cache_ckv, cache_krope, state_conv, state_gdn, page_table, c_prompt, c_sample,
           ada_w, ada_b, norm_ffn1, ffn1_wi, ffn1_wo, norm_mix, w_in, gdn_conv_w, gdn_a_log, gdn_dt_bias, gdn_norm,
           mla_qn_norm, mla_qr_norm, mla_ckv_norm, mla_kr_norm, mla_kn_norm, mla_w_uk, mla_w_uv, w_out,
           norm_ffn2, ffn2_wi, ffn2_wo):
    depth = ada_w.shape[0]
    bp, tp, d = x_prompt.shape
    bs, ts, _ = x_sample.shape
    g_heads, dk, dv = state_gdn.shape[2:]
    conv_dim = state_conv.shape[-1]
    v_dim = g_heads * dv
    rank, heads, dn = mla_w_uk.shape[1:]
    dr = mla_qr_norm.shape[1]
    past = page_table.shape[1] * PAGE_SIZE
    dims = (heads, dn, dr, rank, g_heads, conv_dim, v_dim)
    lay = _InLayout(conv_dim, v_dim, heads)

    cs_p = _rope_cos_sin(jnp.arange(tp, dtype=F32), dr // 2)
    cs_s = jnp.tile(_rope_cos_sin(past + jnp.arange(ts, dtype=F32), dr // 2), (bs, 1))
    place, tab_bias = _rope_placement(dr // 2, dn)
    cond = jnp.concatenate([c_prompt, c_sample], axis=0)
    rows = -(-cond.shape[0] // SUBLANES) * SUBLANES
    cond = jnp.pad(cond, ((0, rows - cond.shape[0]), (0, 0)))

    yp = x_prompt
    ys = x_sample.reshape(1, bs * ts, d)
    outs_p, outs_s = [], []
    for l in range(depth):
        lp = dict(norm_ffn1=norm_ffn1[l], ffn1_wi=ffn1_wi[l], ffn1_wo=ffn1_wo[l], norm_mix=norm_mix[l], w_in=w_in[l],
                  gdn_conv_w=gdn_conv_w[l], gdn_a_log=gdn_a_log[l], gdn_dt_bias=gdn_dt_bias[l], gdn_norm=gdn_norm[l],
                  mla_qn_norm=mla_qn_norm[l], mla_qr_norm=mla_qr_norm[l], mla_ckv_norm=mla_ckv_norm[l],
                  mla_kr_norm=mla_kr_norm[l], mla_kn_norm=mla_kn_norm[l], mla_w_uk=mla_w_uk[l],
                  mla_w_uv=mla_w_uv[l], w_out=w_out[l], norm_ffn2=norm_ffn2[l], ffn2_wi=ffn2_wi[l],
                  ffn2_wo=ffn2_wo[l])
        w = _prep_layer(lp, dims)
        mods = _mods(cond, ada_w[l], ada_b[l])
        mods_p = mods[:bp].reshape(bp, N_MOD, d)
        mods_s = mods[bp:bp + bs].reshape(bs, N_MOD, d).transpose(1, 0, 2)

        conv0 = jnp.zeros((bp, SUBLANES, conv_dim), F32)
        s0 = jnp.zeros((bp, g_heads, dk, dv), F32)
        attend_p = lambda o: _attn_prompt(o[3], o[4], o[7])
        yp, c_p, kr_p, cv_p, s_p = _layer(yp, mods_p, (cs_p, place, tab_bias), conv0, s0, lp, w, lay, dims,
                                          chunk=min(GDN_CHUNK, tp), attend=attend_p)
        outs_p.append((c_p, kr_p.transpose(0, 2, 1), cv_p[:, SUBLANES - (GDN_CONV - 1):], s_p))


        def attend_s(o, l=l, w=w):
            q128, k128, c_new, qabs, qrope = o[3], o[4], o[5], o[7], o[8]
            by_seq = lambda a: a[0].reshape(heads, bs, ts, -1).transpose(1, 2, 0, 3)
            flat = lambda a: by_seq(a).reshape(bs, ts * heads, -1)
            ctx = _attn_sample(page_table, flat(qabs), flat(qrope)[..., :dr], flat(q128), by_seq(k128),
                               c_new.reshape(bs, ts, -1), w['wukt'], cache_ckv, jnp.swapaxes(cache_krope, 2, 3),
                               layer=l, heads=heads, dn=dn)
            return ctx.reshape(1, bs * ts, heads * rank)

        ys, c_s, kr_s, cv_s, s_s = _layer_sample(ys, mods_s, (cs_s, place, tab_bias), state_conv[l], state_gdn[l], lp, w, lay, dims,
                                                 bs, ts, attend_s)
        outs_s.append((c_s, kr_s, cv_s, s_s))

    stack = lambda outs, k: jnp.stack([o[k] for o in outs])
    ys = ys.reshape(bs, ts, d)
    return (yp, ys, stack(outs_p, 0), stack(outs_p, 1), stack(outs_p, 2), stack(outs_p, 3),
            stack(outs_s, 0), stack(outs_s, 1), stack(outs_s, 2), stack(outs_s, 3))


def _layer_sample(x, mods, rope, conv_prev, s0, lp, w, lay, dims, bs, ts, attend):
    heads, dn, dr, g_heads = dims[0], dims[1], dims[2], dims[4]
    x = _ffn(x, mods, lp['norm_ffn1'], *w['f1'], mod0=0, tokens_per_seq=ts)
    outs = _inproj(x, mods, lp['norm_mix'], w['w_all'], *rope, w['gq'], w['mq'], w['gc'], w['wuk'], w['mk'],
                   w['gk'], w['gkr'], w['wqa'], None, lay=lay, heads=heads, tokens_per_seq=ts, rope=(dn, dr))
    qkv, z, ba, c, krot = outs[0], outs[1], outs[2], outs[5], outs[6]
    seq = lambda a: a.reshape(bs, ts, a.shape[-1])
    minor = lambda a: seq(a).transpose(1, 2, 0)
    o_t, s_new = _gdn_sample(minor(qkv), conv_prev.transpose(1, 2, 0), minor(z), minor(ba[..., :g_heads]),
                             minor(ba[..., g_heads:2 * g_heads]), s0.transpose(1, 2, 3, 0), lp['gdn_conv_w'].T,
                             w['aparams'][:, g_heads:2 * g_heads], lp['gdn_norm'])
    g_heads_, dv_ = s0.shape[1], s0.shape[3]
    gdn_out = o_t.reshape(ts, g_heads_, dv_, bs).transpose(1, 3, 0, 2).reshape(1, g_heads_, bs * ts, dv_)
    ctx = attend(outs)
    x = _ffn(x, mods, lp['norm_ffn2'], *w['f2'], mod0=6, tokens_per_seq=ts,
             mix=(gdn_out, ctx, w['wuv_bd'], w['wo_g'], w['wo_m']))
    nconv = jnp.concatenate([conv_prev, seq(qkv)], axis=1)[:, -(GDN_CONV - 1):]
    return x, seq(c), seq(krot)[..., dn:dn + dr], nconv, s_new.transpose(3, 0, 1, 2)
```

```python
import functools

import jax
import jax.numpy as jnp
import numpy as np
from jax import lax
from jax.experimental import pallas as pl
from jax.experimental.pallas import tpu as pltpu

F32 = jnp.float32
BF16 = jnp.bfloat16

NORM_EPS = 1e-6
ROPE_THETA = 10000.0
PAGE_SIZE = 128
GDN_CONV = 4
GDN_CHUNK = 128
N_MOD = 9
LANES = 128
SUBLANES = 8
NEG_BIG = -1e30
VMEM_LIMIT = 56 * 1024 * 1024

TOKEN_TILE = 512
ATTN_TILE = 512
GDN_TILE = 512
SUB_PAGES = 16
AHEAD = 4


def _dot(a, b):
    return jnp.dot(a, b, preferred_element_type=F32)


def _dot_t(a, b):
    return lax.dot_general(a, b, (((1,), (1,)), ((), ())), preferred_element_type=F32)


def _dot_ta(a, b):
    return lax.dot_general(a, b, (((0,), (0,)), ((), ())), preferred_element_type=F32)


def _bmm(a, b):
    return lax.dot_general(a, b, (((2,), (1,)), ((0,), (0,))), preferred_element_type=F32)


def _bmm_t(a, b):
    return lax.dot_general(a, b, (((2,), (2,)), ((0,), (0,))), preferred_element_type=F32)


def _sigmoid(x):
    return 1.0 / (1.0 + jnp.exp(-x))


def _silu(x):
    return x * _sigmoid(x)


def _params(*sem):
    return pltpu.CompilerParams(dimension_semantics=sem, vmem_limit_bytes=VMEM_LIMIT)


def _resident(shape):
    nd = len(shape)
    return pl.BlockSpec(shape, lambda *_: (0,) * nd, pipeline_mode=pl.Buffered(1))


MODS_PER_SUBLAYER = 3


def _dot_split(x, e, terms, left=False):
    out = None
    for _ in range(terms):
        xb = x.astype(BF16)
        d = _dot(e, xb) if left else _dot(xb, e)
        out = d if out is None else out + d
        x = x - xb.astype(F32)
    return out


def _mod_rows(m_ref, ks, tokens_per_seq, rows):
    if tokens_per_seq is None:
        return [m_ref[k:k + 1, :] for k in ks]
    nseq = m_ref.shape[1]
    row = lax.broadcasted_iota(jnp.int32, (rows, nseq), 0)
    first = lax.broadcasted_iota(jnp.int32, (rows, nseq), 1) * tokens_per_seq
    spread = jnp.where((row >= first) & (row < first + tokens_per_seq), 1.0, 0.0).astype(BF16)
    return [_dot_split(m_ref[k % MODS_PER_SUBLAYER], spread, 3, left=True) for k in ks]


def _mod_norm(x, gain, shift, scale):
    y = x * lax.rsqrt(jnp.mean(x * x, axis=-1, keepdims=True) + NORM_EPS)
    return (y * gain) * (1.0 + scale) + shift


def _mods_spec(tokens_per_seq, tm, d, sublayer):
    if tokens_per_seq is not None:
        assert tm % tokens_per_seq == 0
        return pl.BlockSpec((MODS_PER_SUBLAYER, tm // tokens_per_seq, d), lambda b, i: (sublayer, i, 0))
    return pl.BlockSpec((None, N_MOD, d), lambda b, i: (b, 0, 0))


def _mods_body(c_ref, w_ref, b_ref, o_ref):
    c = c_ref[...]
    o_ref[...] = _dot(_silu(c).astype(BF16), w_ref[...].astype(BF16)) + b_ref[...]


def _mods(cond, ada_w, ada_b):
    rows, d = cond.shape
    n = ada_w.shape[1]
    tn = d
    return pl.pallas_call(
        _mods_body,
        grid=(n // tn,),
        in_specs=[pl.BlockSpec((rows, d), lambda j: (0, 0)),
                  pl.BlockSpec((d, tn), lambda j: (0, j)),
                  pl.BlockSpec((1, tn), lambda j: (0, j))],
        out_specs=pl.BlockSpec((rows, tn), lambda j: (0, j)),
        out_shape=jax.ShapeDtypeStruct((rows, n), F32),
        compiler_params=_params("arbitrary"),
        name="mods",
    )(cond, ada_w, ada_b.reshape(1, n))


def _ffn_body(x_ref, m_ref, g_ref, wi_ref, wo_ref, *rest, mod0, tokens_per_seq, fc, mixed):
    x = x_ref[...]
    if mixed:
        mm_ref, gdn_ref, ctx_ref, wuv_ref, wog_ref, wom_ref, o_ref = rest
        gate_k = mod0 - 1
        (gate_mix,) = _mod_rows(mm_ref if tokens_per_seq is not None else m_ref, [gate_k], tokens_per_seq, x.shape[0])
        mla = _dot(ctx_ref[...], wuv_ref[...])
        gdn = jnp.concatenate([gdn_ref[h] for h in range(gdn_ref.shape[0])], axis=1)
        mix = _dot(gdn.astype(BF16), wog_ref[...]) + _dot(mla.astype(BF16), wom_ref[...])
        x = x + gate_mix * mix
    else:
        (o_ref,) = rest
    shift, scale, gate = _mod_rows(m_ref, [mod0, mod0 + 1, mod0 + 2], tokens_per_seq, x.shape[0])
    h = _mod_norm(x, g_ref[...], shift, scale).astype(BF16)
    acc = jnp.zeros(x.shape, F32)
    dff = wo_ref.shape[0]
    for c in range(dff // fc):
        lo, hi = c * fc, (c + 1) * fc
        a = _silu(_dot(h, wi_ref[:, lo:hi])) * _dot(h, wi_ref[:, dff + lo:dff + hi])
        acc = acc + _dot(a.astype(BF16), wo_ref[lo:hi, :])
    o_ref[...] = x + 0.5 * gate * acc


def _ffn(x, mods, gain, wi, wo, *, mod0, tokens_per_seq, mix=None):
    nb, t, d = x.shape
    tm = min(TOKEN_TILE, t)
    dff = wo.shape[0]
    fc = 2 * LANES
    assert t % tm == 0 and dff % fc == 0
    body = functools.partial(_ffn_body, mod0=mod0, tokens_per_seq=tokens_per_seq, fc=fc, mixed=mix is not None)
    tok = lambda w: pl.BlockSpec((None, tm, w), lambda b, i: (b, i, 0))
    sub = mod0 // MODS_PER_SUBLAYER
    in_specs = [tok(d), _mods_spec(tokens_per_seq, tm, d, sub),
                _resident((1, d)), _resident((d, 2 * dff)), _resident((dff, d))]
    args = [x, mods, gain.reshape(1, d), wi, wo]
    if mix is not None:
        gdn_out, ctx, wuv_bd, wo_g, wo_m = mix
        in_specs += [_mods_spec(tokens_per_seq, tm, d, sub - 1),
                     pl.BlockSpec((None, gdn_out.shape[1], tm, gdn_out.shape[3]), lambda b, i: (b, 0, i, 0)),
                     tok(ctx.shape[-1]),
                     _resident(wuv_bd.shape), _resident(wo_g.shape), _resident(wo_m.shape)]
        args += [mods, gdn_out, ctx, wuv_bd, wo_g, wo_m]
    return pl.pallas_call(
        body,
        grid=(nb, t // tm),
        in_specs=in_specs,
        out_specs=tok(d),
        out_shape=jax.ShapeDtypeStruct(x.shape, F32),
        compiler_params=_params("arbitrary", "arbitrary"),
        name="ffn_mix" if mix is not None else "ffn",
    )(*args)


class _InLayout:
    def __init__(self, conv_dim, v_dim, heads):
        self.qkv = (0, conv_dim)
        self.z = (conv_dim, conv_dim + v_dim)
        self.ba = (self.z[1], self.z[1] + LANES)
        self.q = (self.ba[1], self.ba[1] + heads * LANES)
        self.ckv = (self.q[1], self.q[1] + LANES)
        self.kr = (self.ckv[1], self.ckv[1] + LANES)
        self.krs = (self.kr[1], self.kr[1] + LANES)
        self.total = self.krs[1]


def _inproj_body(x_ref, m_ref, g_ref, w_ref, cs_ref, place_ref, tbias_ref, gq_ref, mq_ref, gc_ref, wuk_ref, mk_ref,
                 gk_ref, gkr_ref, *rest, lay, heads, tokens_per_seq, sample, qk_dim, rope):
    if sample:
        wqa_ref, rest = rest[0], rest[1:]
    else:
        cprev_ref, cw_ref, ms_ref, rest = rest[0], rest[1], rest[2], rest[3:]
    qkv_ref, z_ref, ba_ref, q128_ref, k128_ref, c_ref, krot_ref = rest[:7]
    x = x_ref[...]
    shift, scale = _mod_rows(m_ref, [3, 4], tokens_per_seq, x.shape[0])
    h = _mod_norm(x, g_ref[...], shift, scale).astype(BF16)
    cw = 4 * LANES
    assert lay.total % cw == 0 and lay.qkv[1] % cw == 0 and (lay.z[1] - lay.z[0]) == cw
    n_chunks = lay.total // cw
    n_qkv = lay.qkv[1] // cw
    rest_chunks = [n_chunks - 1] + list(range(n_qkv, n_chunks - 1))
    order = []
    for c in range(max(n_qkv, len(rest_chunks))):
        order += ([c] if c < n_qkv else []) + ([rest_chunks[c]] if c < len(rest_chunks) else [])
    chunks = {}

    def cols(lo, hi):
        c = lo // cw
        assert (hi - 1) // cw == c
        return chunks[c][:, lo - c * cw:hi - c * cw]

    tm = x.shape[0]
    hist = SUBLANES
    taps = GDN_CONV - 1
    if not sample:
        ct_ref, nconv_ref, xbuf = rest[7], rest[8], rest[9]

        @pl.when(pl.program_id(1) == 0)
        def _():
            xbuf[0:hist, :] = cprev_ref[...]

    tab = _dot_split(cs_ref[...], place_ref[...], 3) + tbias_ref[...]
    tab_q, tab_c, tab_s = tab[:, :LANES], tab[:, LANES:2 * LANES], tab[:, 2 * LANES:]

    def group(lo):
        hi = lo + LANES
        if lo < lay.qkv[1]:
            if sample:
                qkv_ref[:, lo:hi] = cols(lo, hi)
                return
            xbuf[hist:hist + tm, lo:hi] = cols(lo, hi)
            conv = xbuf[hist - taps:hist - taps + tm, lo:hi] * cw_ref[0:1, lo:hi]
            for jj in range(1, GDN_CONV):
                conv = conv + xbuf[hist - taps + jj:hist - taps + jj + tm, lo:hi] * cw_ref[jj:jj + 1, lo:hi]
            yi = _silu(conv)
            if lo < 2 * qk_dim:
                ss = _dot((yi * yi).astype(BF16), ms_ref[...])
                yi = yi * (lax.rsqrt(ss + NORM_EPS) * ((LANES // 2) ** -0.5 if lo < qk_dim else 1.0))
            qkv_ref[2 * (lo // LANES)] = yi[:, :LANES // 2]
            qkv_ref[2 * (lo // LANES) + 1] = yi[:, LANES // 2:]
            tail = xbuf[tm:tm + hist, lo:hi]
            nconv_ref[:, lo:hi] = tail
            xbuf[0:hist, lo:hi] = tail
        elif lo < lay.z[1]:
            if sample:
                z_ref[:, lo - lay.z[0]:hi - lay.z[0]] = cols(lo, hi)
            else:
                zz = cols(lo, hi)
                z_ref[2 * ((lo - lay.z[0]) // LANES)] = zz[:, :LANES // 2]
                z_ref[2 * ((lo - lay.z[0]) // LANES) + 1] = zz[:, LANES // 2:]
        elif lo < lay.ba[1]:
            ba_ref[...] = cols(lo, hi)
        elif lo < lay.q[1]:
            hh = (lo - lay.q[0]) // LANES
            qh = cols(lo, hi)
            msq = _dot((qh * qh).astype(BF16), mq_ref[...])
            qn = qh * lax.rsqrt(msq + NORM_EPS) * gq_ref[...] * tab_q
            q128_ref[hh] = qn.astype(BF16)
            if sample:
                qabs_ref, qrope_ref = rest[7], rest[8]
                qa = _dot(qn.astype(BF16), wqa_ref[hh])
                qabs_ref[hh] = qa[:, :LANES].astype(BF16)
                qrope_ref[hh] = qa[:, LANES:2 * LANES].astype(BF16)
        elif lo == lay.ckv[0]:
            keys()

    def keys():
        ckv = cols(*lay.ckv)
        c = ckv * lax.rsqrt(jnp.mean(ckv * ckv, axis=-1, keepdims=True) + NORM_EPS) * gc_ref[...]
        c_ref[...] = c
        cb = c.astype(BF16)
        if not sample:
            ct_ref[...] = jnp.concatenate([c.T, jnp.ones((SUBLANES, c.shape[0]), F32)], axis=0).astype(BF16)
        kr = cols(*lay.kr)
        krs = cols(*lay.krs)
        inv = lax.rsqrt(jnp.sum(kr * kr, axis=-1, keepdims=True) * (2.0 / LANES) + NORM_EPS)
        krot = kr * inv * gkr_ref[0:1, :] * tab_c + krs * inv * gkr_ref[1:2, :] * tab_s
        if sample:
            krot_ref[...] = krot
        else:
            krot_ref[...] = krot.T[rope[0]:rope[0] + rope[1], :]
        knr = _dot(cb, wuk_ref[...])
        for hh in range(heads):
            kh = knr[:, hh * LANES:(hh + 1) * LANES]
            msq = _dot((kh * kh).astype(BF16), mk_ref[...])
            k128_ref[hh] = (kh * lax.rsqrt(msq + NORM_EPS) * gk_ref[...] + krot).astype(BF16)

    ahead = 2
    for i in range(len(order) + ahead):
        if i < len(order):
            k = order[i]
            chunks[k] = _dot(h, w_ref[:, k * cw:(k + 1) * cw])
        if i >= ahead:
            k = order[i - ahead]
            for lo in range(k * cw, (k + 1) * cw, LANES):
                group(lo)


def _inproj(x, mods, gain, w_all, cs, place, tab_bias, gq, mq, gc, wuk, mk, gk, gkr, wqa, conv, *, lay, heads,
            tokens_per_seq, rope):
    nb, t, d = x.shape
    tm = min(TOKEN_TILE, t)
    assert t % tm == 0
    sample = wqa is not None
    body = functools.partial(_inproj_body, lay=lay, heads=heads, tokens_per_seq=tokens_per_seq, sample=sample,
                             qk_dim=(lay.qkv[1] - (lay.z[1] - lay.z[0])) // 2, rope=rope)
    tok = lambda w: pl.BlockSpec((None, tm, w), lambda b, i: (b, i, 0))
    hd = lambda w: pl.BlockSpec((None, heads, tm, w), lambda b, i: (b, 0, i, 0))
    conv_dim, v_dim = lay.qkv[1], lay.z[1] - lay.z[0]
    in_specs = [tok(d), _mods_spec(tokens_per_seq, tm, d, 1), _resident((1, d)), _resident(w_all.shape),
                pl.BlockSpec((tm, cs.shape[1]), lambda b, i: (i, 0)), _resident(place.shape),
                _resident(tab_bias.shape),
                _resident((1, LANES)), _resident((LANES, LANES)), _resident((1, LANES)),
                _resident(wuk.shape), _resident((LANES, LANES)), _resident((1, LANES)), _resident((2, LANES))]
    args = [x, mods, gain.reshape(1, d), w_all, cs, place, tab_bias, gq, mq, gc, wuk, mk, gk, gkr]
    out_specs = [tok(conv_dim), tok(v_dim), tok(LANES), hd(LANES), hd(LANES), tok(LANES), tok(LANES)]
    out_shape = [jax.ShapeDtypeStruct((nb, t, conv_dim), F32), jax.ShapeDtypeStruct((nb, t, v_dim), F32),
                 jax.ShapeDtypeStruct((nb, t, LANES), F32),
                 jax.ShapeDtypeStruct((nb, heads, t, LANES), BF16), jax.ShapeDtypeStruct((nb, heads, t, LANES), BF16),
                 jax.ShapeDtypeStruct((nb, t, LANES), F32), jax.ShapeDtypeStruct((nb, t, LANES), F32)]
    if not sample:
        out_specs[6] = pl.BlockSpec((None, rope[1], tm), lambda b, i: (b, 0, i))
        out_shape[6] = jax.ShapeDtypeStruct((nb, rope[1], t), F32)
        hw = LANES // 2
        out_specs[0] = pl.BlockSpec((None, conv_dim // hw, tm, hw), lambda b, i: (b, 0, i, 0))
        out_specs[1] = pl.BlockSpec((None, v_dim // hw, tm, hw), lambda b, i: (b, 0, i, 0))
        out_shape[0] = jax.ShapeDtypeStruct((nb, conv_dim // hw, t, hw), F32)
        out_shape[1] = jax.ShapeDtypeStruct((nb, v_dim // hw, t, hw), F32)
    if sample:
        in_specs.append(_resident(wqa.shape))
        args.append(wqa)
        out_specs += [hd(LANES), hd(LANES)]
        out_shape += [jax.ShapeDtypeStruct((nb, heads, t, LANES), BF16)] * 2
        scratch = []
    else:
        conv_prev, conv_w, ms = conv
        in_specs += [pl.BlockSpec((None, SUBLANES, conv_dim), lambda b, i: (b, 0, 0)),
                     _resident(conv_w.shape), _resident(ms.shape)]
        args += [conv_prev, conv_w, ms]
        out_specs += [pl.BlockSpec((None, LANES + SUBLANES, tm), lambda b, i: (b, 0, i)),
                      pl.BlockSpec((None, SUBLANES, conv_dim), lambda b, i: (b, 0, 0))]
        out_shape += [jax.ShapeDtypeStruct((nb, LANES + SUBLANES, t), BF16),
                      jax.ShapeDtypeStruct((nb, SUBLANES, conv_dim), F32)]
        scratch = [pltpu.VMEM((tm + SUBLANES, conv_dim), F32)]
    return pl.pallas_call(
        body, grid=(nb, t // tm), in_specs=in_specs, out_specs=out_specs, out_shape=out_shape,
        scratch_shapes=scratch, compiler_params=_params("arbitrary", "arbitrary"), name="inproj",
    )(*args)


def _unit_lower_inverse(a):
    n = a.shape[-1]
    assert n & (n - 1) == 0
    row = lax.broadcasted_iota(jnp.int32, (n, n), 0)
    col = lax.broadcasted_iota(jnp.int32, (n, n), 1)
    x = jnp.broadcast_to(jnp.where(row == col, 1.0, 0.0), a.shape)
    b = 1
    while b < n:
        lo_mask = (jnp.bitwise_xor(row, col) < 2 * b) & (jnp.bitwise_and(row, b) != 0) & (jnp.bitwise_and(col, b) == 0)
        lo = jnp.where(lo_mask, a, 0.0)
        if b == 1:
            x = x - lo
        else:
            xb = x.astype(BF16)
            x = x - _bmm(xb, _bmm(lo.astype(BF16), xb).astype(BF16))
        b *= 2
    return x


def _gdn_body(qkv_ref, z_ref, ba_ref, s0_ref, ap_ref, ng_ref, o_ref, sout_ref, s_scr,
              *, tt, chunk, heads, dk, dv):
    j = pl.program_id(1)

    @pl.when(j == 0)
    def _():
        s_scr[...] = s0_ref[...]

    ba = ba_ref[...]
    beta = _sigmoid(ba)
    xg = ba + ap_ref[1:2, :]
    g = -ap_ref[0:1, :] * (jnp.maximum(xg, 0.0) + jnp.log1p(jnp.exp(-jnp.abs(xg))))
    row = lax.broadcasted_iota(jnp.int32, (tt, tt), 0)
    col = lax.broadcasted_iota(jnp.int32, (tt, tt), 1)
    assert chunk & (chunk - 1) == 0
    tri = jnp.where((row >= col) & (jnp.bitwise_xor(row, col) < chunk), 1.0, 0.0).astype(BF16)
    g_hi = g.astype(BF16)
    g_lo = (g - g_hi.astype(F32)).astype(BF16)
    gc = _dot(tri, g_hi) + _dot(tri, g_lo)
    gct = gc.T

    crow = lax.broadcasted_iota(jnp.int32, (chunk, chunk), 0)
    ccol = lax.broadcasted_iota(jnp.int32, (chunk, chunk), 1)
    causal = crow >= ccol
    strict = crow > ccol
    stack = lambda xs: jnp.stack(xs, axis=0)

    n_chunks = tt // chunk
    parts = []

    def prepare(c):
        r0, r1 = c * chunk, (c + 1) * chunk
        q3 = qkv_ref[0:heads, r0:r1, :]
        k3 = qkv_ref[heads:2 * heads, r0:r1, :]
        v3 = qkv_ref[2 * heads:3 * heads, r0:r1, :]
        b3 = stack([beta[r0:r1, h:h + 1] for h in range(heads)])
        gcol = stack([gc[r0:r1, heads + h:heads + h + 1] for h in range(heads)])
        grow = stack([gct[heads + h:heads + h + 1, r0:r1] for h in range(heads)])
        glast = grow[:, :, chunk - 1:chunk]
        eg = jnp.exp(gcol)
        kb = k3 * b3
        decay3 = jnp.exp(jnp.where(causal, gcol - grow, NEG_BIG))
        sc = _bmm_t(jnp.concatenate([q3, kb], axis=1).astype(BF16), k3.astype(BF16))
        qkm = (sc[:, :chunk] * decay3).astype(BF16)
        m = jnp.where(strict, sc[:, chunk:] * decay3, 0.0)
        return (m, qkm, (v3 * b3).astype(BF16), (kb * eg).astype(BF16), (q3 * eg).astype(BF16),
                (k3 * jnp.exp(glast - gcol)).astype(BF16), jnp.exp(glast))

    group = 2
    tinvs = []
    for c0 in range(0, n_chunks, group):
        cs = list(range(c0, min(c0 + group, n_chunks)))
        parts += [prepare(c) for c in cs]
        inv = _unit_lower_inverse(jnp.concatenate([parts[c][0] for c in cs], axis=0)).astype(BF16)
        for i, c in enumerate(cs):
            tinv = inv[i * heads:(i + 1) * heads]
            tinvs.append((_bmm(tinv, parts[c][2]), _bmm(tinv, parts[c][3]).astype(BF16)))
    s3 = s_scr[...]
    for c in range(n_chunks):
        r0, r1 = c * chunk, (c + 1) * chunk
        qkm, qd3, kd3, gl3 = parts[c][1], parts[c][4], parts[c][5], parts[c][6]
        u, w = tinvs[c]
        sb = s3.astype(BF16)
        ws_qs = _bmm(jnp.concatenate([w, qd3], axis=1), sb)
        vnb = (u - ws_qs[:, :chunk]).astype(BF16)
        o = ws_qs[:, chunk:] + _bmm(qkm, vnb)
        upd = stack([_dot_ta(kd3[h], vnb[h]) for h in range(heads)])
        s3 = s3 * gl3 + upd
        on = o * lax.rsqrt(jnp.mean(o * o, axis=-1, keepdims=True) + NORM_EPS) * ng_ref[...]
        o_ref[:, r0:r1, :] = on * _silu(z_ref[:, r0:r1, :])
    s_scr[...] = s3
    sout_ref[...] = s3


def _gdn(qkv, z, ba, s0, aparams, norm_g, *, chunk):
    nb, slabs, t, _ = qkv.shape
    heads, dk, dv = s0.shape[1:]
    assert slabs == 3 * heads and dk == dv
    tt = min(GDN_TILE, t)
    assert t % tt == 0 and tt % chunk == 0
    body = functools.partial(_gdn_body, tt=tt, chunk=chunk, heads=heads, dk=dk, dv=dv)
    tok = lambda w: pl.BlockSpec((None, tt, w), lambda b, i: (b, i, 0))
    slab = lambda n: pl.BlockSpec((None, n, tt, dk), lambda b, i: (b, 0, i, 0))
    return pl.pallas_call(
        body,
        grid=(nb, t // tt),
        in_specs=[slab(3 * heads), slab(heads), tok(LANES),
                  pl.BlockSpec((None, heads, dk, dv), lambda b, i: (b, 0, 0, 0)),
                  _resident((2, LANES)), _resident((1, dv))],
        out_specs=[slab(heads), pl.BlockSpec((None, heads, dk, dv), lambda b, i: (b, 0, 0, 0))],
        out_shape=[jax.ShapeDtypeStruct((nb, heads, t, dv), F32), jax.ShapeDtypeStruct((nb, heads, dk, dv), F32)],
        scratch_shapes=[pltpu.VMEM((heads, dk, dv), F32)],
        compiler_params=_params("arbitrary", "arbitrary"),
        name="gdn",
    )(qkv, z, ba, s0, aparams, norm_g.reshape(1, dv))


def _gdn_sample_body(xq_ref, xk_ref, xv_ref, pq_ref, pk_ref, pv_ref, wq_ref, wk_ref, wv_ref, z_ref, b_ref, a_ref,
                     ap_ref, ng_ref, s_ref, o_ref, sout_ref, kq_scr, *, ts, heads, dk, dv):
    h = pl.program_id(0)
    nb = xq_ref.shape[-1]

    def conv(x_ref, p_ref, w_ref):
        taps = GDN_CONV - 1
        xin = [p_ref[i] for i in range(taps)] + [x_ref[t] for t in range(ts)]
        w = [jnp.broadcast_to(w_ref[:, jj:jj + 1], xin[0].shape) for jj in range(GDN_CONV)]
        out = []
        for t in range(ts):
            acc = xin[t] * w[0]
            for jj in range(1, GDN_CONV):
                acc = acc + xin[t + jj] * w[jj]
            out.append(_silu(acc))
        return out

    def l2(x):
        return x * lax.rsqrt(jnp.sum(x * x, axis=0, keepdims=True) + NORM_EPS)

    q = [l2(x) * (dk ** -0.5) for x in conv(xq_ref, pq_ref, wq_ref)]
    k = [l2(x) for x in conv(xk_ref, pk_ref, wk_ref)]
    v = conv(xv_ref, pv_ref, wv_ref)
    for t in range(ts):
        kq_scr[t] = k[t]
        kq_scr[ts + t] = q[t]
    a_h = ap_ref[0, h]
    dt_h = ap_ref[1, h]
    beta, decay = [], []
    for t in range(ts):
        beta.append(_sigmoid(b_ref[t, pl.ds(h, 1), :]))
        xg = a_ref[t, pl.ds(h, 1), :] + dt_h
        decay.append(jnp.exp(-a_h * (jnp.maximum(xg, 0.0) + jnp.log1p(jnp.exp(-jnp.abs(xg))))))

    def row(i, kk):
        return jnp.broadcast_to(kq_scr[i, pl.ds(kk, 1), :], (dv, nb))

    def first(kk, ks):
        return ks + row(0, kk) * s_ref[kk]

    ks = lax.fori_loop(0, dk, first, jnp.zeros((dv, nb), F32), unroll=8)
    ng = jnp.broadcast_to(ng_ref[...], (dv, nb))
    for t in range(ts):
        d = beta[t] * (v[t] - decay[t] * ks)
        src = s_ref if t == 0 else sout_ref
        last = t == ts - 1

        def step(kk, carry, t=t, d=d, src=src, last=last):
            o_acc, ks_next = carry
            s_new = decay[t] * src[kk] + row(t, kk) * d
            sout_ref[kk] = s_new
            o_acc = o_acc + row(ts + t, kk) * s_new
            if not last:
                ks_next = ks_next + row(t + 1, kk) * s_new
            return o_acc, ks_next

        zero = jnp.zeros((dv, nb), F32)
        o, ks = lax.fori_loop(0, dk, step, (zero, zero), unroll=8)
        on = o * lax.rsqrt(jnp.mean(o * o, axis=0, keepdims=True) + NORM_EPS) * ng
        o_ref[t] = on * _silu(z_ref[t])


def _gdn_sample(x_t, prev_t, z_t, b_t, a_t, s_t, conv_w_t, aparams, norm_g):
    ts, conv_dim, nb = x_t.shape
    heads, dk, dv, _ = s_t.shape
    assert dk == dv and conv_dim == 3 * heads * dk
    taps = GDN_CONV - 1
    body = functools.partial(_gdn_sample_body, ts=ts, heads=heads, dk=dk, dv=dv)
    part = lambda rows, off: pl.BlockSpec((rows, dk, nb), lambda h: (0, off + h, 0))
    wpart = lambda off: pl.BlockSpec((dk, GDN_CONV), lambda h: (off + h, 0))
    whole = lambda a: pl.BlockSpec(a.shape, lambda h: (0,) * a.ndim)
    state = pl.BlockSpec((None, dk, dv, nb), lambda h: (h, 0, 0, 0))
    return pl.pallas_call(
        body,
        grid=(heads,),
        in_specs=[part(ts, 0), part(ts, heads), part(ts, 2 * heads),
                  part(taps, 0), part(taps, heads), part(taps, 2 * heads),
                  wpart(0), wpart(heads), wpart(2 * heads),
                  part(ts, 0), whole(b_t), whole(a_t),
                  pl.BlockSpec(memory_space=pltpu.SMEM), pl.BlockSpec((dv, 1), lambda h: (0, 0)), state],
        out_specs=[part(ts, 0), state],
        out_shape=[jax.ShapeDtypeStruct((ts, heads * dv, nb), F32), jax.ShapeDtypeStruct(s_t.shape, F32)],
        scratch_shapes=[pltpu.VMEM((2 * ts, dk, nb), F32)],
        compiler_params=_params("arbitrary"),
        name="gdn_sample",
    )(x_t, x_t, x_t, prev_t, prev_t, prev_t, conv_w_t, conv_w_t, conv_w_t, z_t, b_t, a_t, aparams,
      norm_g.reshape(dv, 1), s_t)


def _attn_body(qi_ref, kj_ref, q_ref, k_ref, ct_ref, o_ref, m_scr, acc_scr, *, heads, tq, rank):
    i = qi_ref[pl.program_id(1)]
    j = kj_ref[pl.program_id(1)]

    @pl.when(j == 0)
    def _():
        m_scr[...] = jnp.full(m_scr.shape, NEG_BIG, F32)
        acc_scr[...] = jnp.zeros(acc_scr.shape, F32)

    def step(masked):
        ct = ct_ref[...]
        m_old = [m_scr[h] for h in range(heads)]
        acc_old = [acc_scr[h] for h in range(heads)]
        if masked:
            key = lax.broadcasted_iota(jnp.int32, (tq, tq), 0)
            qry = lax.broadcasted_iota(jnp.int32, (tq, tq), 1)
            keep = key <= qry
        m_out, acc_out = [], []
        ahead = 3
        scores = [_dot_t(k_ref[h], q_ref[h]) for h in range(ahead)]
        for h in range(heads):
            if h + ahead < heads:
                scores.append(_dot_t(k_ref[h + ahead], q_ref[h + ahead]))
            st = scores[h]
            if masked:
                st = jnp.where(keep, st, NEG_BIG)
            m_new = jnp.maximum(m_old[h], jnp.max(st, axis=0, keepdims=True))
            alpha = jnp.exp(m_old[h] - m_new)
            pt = jnp.exp(st - m_new).astype(BF16)
            acc_out.append(acc_old[h] * alpha + _dot(ct, pt))
            m_out.append(m_new)
        for h in range(heads):
            m_scr[h] = m_out[h]
            acc_scr[h] = acc_out[h]
        return acc_out

    @pl.when(j < i)
    def _():
        step(False)

    @pl.when(j == i)
    def _():
        acc = step(True)
        for h in range(heads):
            ctx = acc[h][:rank, :] / acc[h][rank:rank + 1, :]
            o_ref[:, h * rank:(h + 1) * rank] = ctx.T.astype(BF16)


def _attn_prompt(q128, k128, ct_ext):
    nb, heads, t, _ = q128.shape
    rows = ct_ext.shape[1]
    rank = rows - SUBLANES
    tq = min(ATTN_TILE, t)
    assert t % tq == 0
    n = t // tq
    body = functools.partial(_attn_body, heads=heads, tq=tq, rank=rank)
    pairs = [(i, j) for i in range(n) for j in range(i + 1)]
    qi = jnp.asarray([p[0] for p in pairs], jnp.int32)
    kj = jnp.asarray([p[1] for p in pairs], jnp.int32)
    grid_spec = pltpu.PrefetchScalarGridSpec(
        num_scalar_prefetch=2,
        grid=(nb, len(pairs)),
        in_specs=[pl.BlockSpec((None, heads, tq, LANES), lambda b, s, qi, kj: (b, 0, qi[s], 0)),
                  pl.BlockSpec((None, heads, tq, LANES), lambda b, s, qi, kj: (b, 0, kj[s], 0)),
                  pl.BlockSpec((None, rows, tq), lambda b, s, qi, kj: (b, 0, kj[s]))],
        out_specs=pl.BlockSpec((None, tq, heads * rank), lambda b, s, qi, kj: (b, qi[s], 0)),
        scratch_shapes=[pltpu.VMEM((heads, 1, tq), F32), pltpu.VMEM((heads, rows, tq), F32)],
    )
    return pl.pallas_call(
        body, grid_spec=grid_spec,
        out_shape=jax.ShapeDtypeStruct((nb, t, heads * rank), BF16),
        compiler_params=_params("arbitrary", "arbitrary"),
        name="attn_prompt",
    )(qi, kj, q128, k128, ct_ext)


def _attn_sample_body(pt_ref, qabs_ref, qrope_ref, q128_ref, k128n_ref, cn_ref, wukt_ref, cache_c, cache_kr,
                      o_ref, cbuf, kbuf, cb, krb, s_all, sem, *, layer, heads, dn, s_new, n_pages, sub_pages):
    b = pl.program_id(0)
    nseq = pl.num_programs(0)
    slot = lax.rem(b, 2)
    rows = s_new * heads
    n_sub = n_pages // sub_pages
    sub_keys = sub_pages * PAGE_SIZE

    def page_copies(seq, slot_):
        out = []
        for i in range(n_pages):
            page = pt_ref[seq * n_pages + i]
            out.append(pltpu.make_async_copy(cache_c.at[layer, page],
                                             cbuf.at[slot_, pl.ds(i * PAGE_SIZE, PAGE_SIZE)], sem.at[0, slot_]))
            out.append(pltpu.make_async_copy(cache_kr.at[layer, page], kbuf.at[slot_, i], sem.at[1, slot_]))
        return out

    @pl.when(b == 0)
    def _():
        for cp in page_copies(0, 0):
            cp.start()

    waits = page_copies(b, slot)
    for cp in waits[0::2] + waits[1::2]:
        cp.wait()

    nxt = lax.rem(b + 1, nseq)

    lhs = jnp.concatenate([wukt_ref[...], qabs_ref[...]], axis=0)
    nk = heads * dn
    qrope = qrope_ref[...]

    def scores(sb):
        for i in range(sub_pages):
            pg = sb * sub_pages + i
            cb[sb, i * PAGE_SIZE:(i + 1) * PAGE_SIZE, :] = cbuf[slot, pg * PAGE_SIZE:(pg + 1) * PAGE_SIZE, :].astype(BF16)
            krb[sb, :, i * PAGE_SIZE:(i + 1) * PAGE_SIZE] = kbuf[slot, pg].astype(BF16)
        kq = _dot_t(lhs, cb[sb])
        ssq = jnp.concatenate([jnp.sum(kq[h * dn:(h + 1) * dn, :] ** 2, axis=0, keepdims=True)
                               for h in range(heads)], axis=0)
        r = lax.rsqrt(ssq * (1.0 / dn) + NORM_EPS)
        s = kq[nk:, :] * jnp.concatenate([r] * s_new, axis=0) + _dot(qrope, krb[sb])
        s_all[sb] = s
        return jnp.max(s, axis=1, keepdims=True)

    state = (jnp.full((rows, 1), NEG_BIG, F32), jnp.zeros((rows, 1), F32), jnp.zeros((rows, LANES), F32))

    def absorb(state, sb, m_sb):
        m_run, l, acc = state
        m_new = jnp.maximum(m_run, m_sb)
        alpha = jnp.exp(m_run - m_new)
        p = jnp.exp(s_all[sb] - m_new)
        return (m_new, alpha * l + jnp.sum(p, axis=1, keepdims=True), alpha * acc + _dot(p.astype(BF16), cb[sb]))

    ahead = AHEAD
    maxes = []
    for sb in range(n_sub):
        maxes.append(scores(sb))
        if sb == 0:
            for n, cp in enumerate(page_copies(nxt, 1 - slot)):
                cp.start(priority=(n // 2) % 2)
        if sb >= ahead:
            state = absorb(state, sb - ahead, maxes[sb - ahead])
    for sb in range(max(n_sub - ahead, 0), n_sub):
        state = absorb(state, sb, maxes[sb])
    m_run, l, acc = state

    qf = q128_ref[...].astype(F32)
    assert heads & (heads - 1) == 0
    tok = lax.shift_right_logical(lax.broadcasted_iota(jnp.int32, (rows, 1), 0), heads.bit_length() - 1)
    sn = []
    m_fin = m_run
    for t in range(s_new):
        kt = jnp.concatenate([k128n_ref[t].astype(F32)] * s_new, axis=0)
        st = jnp.sum(qf * kt, axis=1, keepdims=True)
        sn.append(jnp.where(tok >= t, st, NEG_BIG))
        m_fin = jnp.maximum(m_fin, sn[-1])
    alpha = jnp.exp(m_run - m_fin)
    l = alpha * l
    acc = alpha * acc
    cn = cn_ref[...].astype(BF16).astype(F32)
    for t in range(s_new):
        pt = jnp.exp(sn[t] - m_fin)
        l = l + pt
        acc = acc + pt.astype(BF16).astype(F32) * cn[t:t + 1, :]
    o_ref[...] = (acc / l).astype(BF16)

    @pl.when(b == nseq - 1)
    def _():
        tail = page_copies(nxt, 1 - slot)
        for cp in tail[0::2] + tail[1::2]:
            cp.wait()


def _attn_sample(page_table, qabs, qrope, q128, k128n, c_new, wukt, cache_c, cache_kr, *, layer, heads, dn):
    nseq, rows, _ = qabs.shape
    s_new = rows // heads
    n_pages = page_table.shape[1]
    rank = cache_c.shape[-1]
    rope = cache_kr.shape[-2]
    sub_pages = min(SUB_PAGES, n_pages)
    assert n_pages % sub_pages == 0
    n_sub = n_pages // sub_pages
    sub_keys = sub_pages * PAGE_SIZE
    body = functools.partial(_attn_sample_body, layer=layer, heads=heads, dn=dn, s_new=s_new, n_pages=n_pages,
                             sub_pages=sub_pages)
    per_seq = lambda shape: pl.BlockSpec((None,) + shape, lambda b, pt: (b,) + (0,) * len(shape))
    grid_spec = pltpu.PrefetchScalarGridSpec(
        num_scalar_prefetch=1,
        grid=(nseq,),
        in_specs=[per_seq((rows, LANES)), per_seq((rows, rope)), per_seq((rows, LANES)),
                  per_seq((s_new, heads, LANES)), per_seq((s_new, LANES)),
                  pl.BlockSpec(wukt.shape, lambda b, pt: (0, 0)),
                  pl.BlockSpec(memory_space=pl.ANY), pl.BlockSpec(memory_space=pl.ANY)],
        out_specs=per_seq((rows, LANES)),
        scratch_shapes=[pltpu.VMEM((2, n_pages * PAGE_SIZE, rank), F32),
                        pltpu.VMEM((2, n_pages, rope, PAGE_SIZE), F32),
                        pltpu.VMEM((n_sub, sub_keys, rank), BF16),
                        pltpu.VMEM((n_sub, rope, sub_keys), BF16),
                        pltpu.VMEM((n_sub, rows, sub_keys), F32),
                        pltpu.SemaphoreType.DMA((2, 2))],
    )
    return pl.pallas_call(
        body, grid_spec=grid_spec,
        out_shape=jax.ShapeDtypeStruct((nseq, rows, LANES), BF16),
        compiler_params=_params("arbitrary"),
        name="attn_sample",
    )(page_table.reshape(-1), qabs, qrope, q128, k128n, c_new, wukt, cache_c, cache_kr)


def _rope_cos_sin(pos, half):
    inv = ROPE_THETA ** (-jnp.arange(half, dtype=F32) / half)
    t = pos.shape[0]
    if (t * half) % LANES == 0:
        ang = jnp.repeat(pos, half).reshape(-1, LANES) * jnp.tile(inv, t).reshape(-1, LANES)
        return jnp.concatenate([jnp.cos(ang).reshape(t, half), jnp.sin(ang).reshape(t, half)], axis=1)
    ang = pos[:, None] * inv[None, :]
    return jnp.concatenate([jnp.cos(ang), jnp.sin(ang)], axis=1)


def _rope_placement(half, dn):
    j = np.arange(half)
    place = np.zeros((2 * half, 3 * LANES), np.float32)
    dr = 2 * half
    for base, kind in ((dn, 'c'), (dn + dr, 's'), (LANES + dn, 'c'), (LANES + dn + dr, 'c'),
                       (2 * LANES + dn, 's'), (2 * LANES + dn + dr, 's')):
        if kind == 'c':
            place[j, base + j] = 1.0
            place[j, base + half + j] = 1.0
        else:
            place[half + j, base + j] = -1.0
            place[half + j, base + half + j] = 1.0
    bias = np.zeros((1, 3 * LANES), np.float32)
    bias[0, :dn] = 1.0
    return jnp.asarray(place, BF16), jnp.asarray(bias)


def _swap_halves(a, axis=-1):
    lo, hi = jnp.split(a, 2, axis=axis)
    return jnp.concatenate([hi, lo], axis=axis)


def _prep_layer(lp, dims):
    heads, dn, dr, rank, g_heads, conv_dim, v_dim = dims
    assert dn + 2 * dr == LANES and rank == LANES and 2 * g_heads <= LANES
    w_in = lp['w_in']
    d = w_in.shape[0]
    o = 0
    w_qkv = w_in[:, o:o + conv_dim]; o += conv_dim
    w_z = w_in[:, o:o + v_dim]; o += v_dim
    w_b = w_in[:, o:o + g_heads]; o += g_heads
    w_a = w_in[:, o:o + g_heads]; o += g_heads
    w_q = w_in[:, o:o + heads * (dn + dr)].reshape(d, heads, dn + dr); o += heads * (dn + dr)
    w_c = w_in[:, o:o + rank]; o += rank
    w_kr = w_in[:, o:o + dr]
    zeros = lambda n: jnp.zeros((d, n), F32)
    w_q128 = jnp.concatenate([w_q, _swap_halves(w_q[:, :, dn:])], axis=2).reshape(d, heads * LANES)
    w_krs = _swap_halves(w_kr)
    w_all = jnp.concatenate([w_qkv, w_z, w_b, w_a, zeros(LANES - 2 * g_heads), w_q128, w_c,
                             zeros(dn), w_kr, w_kr, zeros(dn), w_krs, w_krs], axis=1).astype(BF16)
    scale = (dn + dr) ** -0.5
    qr_g = lp['mla_qr_norm']
    gq = (jnp.concatenate([lp['mla_qn_norm'], qr_g, _swap_halves(qr_g)]) * scale).reshape(1, LANES)
    lane = np.arange(LANES)
    seg = np.where(lane < dn, 0, np.where(lane < dn + dr, 1, 2))
    seg_len = np.where(lane < dn, dn, dr).astype(np.float32)
    mq = jnp.asarray(np.where(seg[:, None] == seg[None, :], 1.0 / seg_len[None, :], 0.0), BF16)
    mk = jnp.asarray(np.where((lane[:, None] < dn) & (lane[None, :] < dn), 1.0 / dn, 0.0), BF16)
    gk = jnp.concatenate([lp['mla_kn_norm'], jnp.zeros((LANES - dn,), F32)]).reshape(1, LANES)
    kr_g = lp['mla_kr_norm']
    zdn = jnp.zeros((dn,), F32)
    gkr = jnp.stack([jnp.concatenate([zdn, kr_g, kr_g]),
                     jnp.concatenate([zdn, _swap_halves(kr_g), _swap_halves(kr_g)])])
    w_uk = lp['mla_w_uk']
    wuk = jnp.concatenate([w_uk, jnp.zeros((rank, heads, LANES - dn), F32)], axis=2)
    wuk = wuk.reshape(rank, heads * LANES).astype(BF16)
    wukt = w_uk.transpose(1, 2, 0).reshape(heads * dn, rank).astype(BF16)
    absorb = jnp.concatenate([w_uk.transpose(1, 2, 0) * lp['mla_kn_norm'][None, :, None],
                              jnp.zeros((heads, LANES - dn, rank), F32)], axis=1)
    fold = np.zeros((LANES, LANES), np.float32)
    fold[dn + np.arange(dr), np.arange(dr)] = 1.0
    fold[dn + dr + np.arange(dr), np.arange(dr)] = 1.0
    wqa = jnp.concatenate([absorb, jnp.broadcast_to(jnp.asarray(fold), (heads, LANES, LANES))], axis=2).astype(BF16)
    w_uv = lp['mla_w_uv']
    mv = w_uv.shape[2]
    wuv_bd = (w_uv.transpose(1, 0, 2)[:, :, None, :] * jnp.asarray(np.eye(heads, dtype=np.float32))[:, None, :, None])
    wuv_bd = wuv_bd.reshape(heads * rank, heads * mv).astype(BF16)
    lane_h = lane // (LANES // 2)
    ms = jnp.asarray(np.where(lane_h[:, None] == lane_h[None, :], 1.0, 0.0), BF16)
    aparams = jnp.pad(jnp.stack([jnp.exp(lp['gdn_a_log']), lp['gdn_dt_bias']]),
                      ((0, 0), (g_heads, LANES - 2 * g_heads)))
    return dict(
        w_all=w_all, gq=gq, mq=mq, gc=lp['mla_ckv_norm'].reshape(1, LANES), wuk=wuk, mk=mk, gk=gk, gkr=gkr,
        wukt=wukt, wqa=wqa, wuv_bd=wuv_bd, ms=ms, aparams=aparams,
        wo_g=lp['w_out'][:v_dim].astype(BF16), wo_m=lp['w_out'][v_dim:].astype(BF16),
        f1=(lp['ffn1_wi'].astype(BF16), lp['ffn1_wo'].astype(BF16)),
        f2=(lp['ffn2_wi'].astype(BF16), lp['ffn2_wo'].astype(BF16)),
    )


def _layer(x, mods, rope, conv_prev, s0, lp, w, lay, dims, *, chunk, attend):
    heads = dims[0]
    x = _ffn(x, mods, lp['norm_ffn1'], *w['f1'], mod0=0, tokens_per_seq=None)
    outs = _inproj(x, mods, lp['norm_mix'], w['w_all'], *rope, w['gq'], w['mq'], w['gc'], w['wuk'], w['mk'],
                   w['gk'], w['gkr'], None, (conv_prev, lp['gdn_conv_w'], w['ms']), lay=lay, heads=heads,
                   tokens_per_seq=None, rope=(dims[1], dims[2]))
    qkv, z, ba, c, krot, nconv = outs[0], outs[1], outs[2], outs[5], outs[6], outs[8]
    gdn_out, s_new = _gdn(qkv, z, ba, s0, w['aparams'], lp['gdn_norm'], chunk=chunk)
    ctx = attend(outs)
    x = _ffn(x, mods, lp['norm_ffn2'], *w['f2'], mod0=6, tokens_per_seq=None,
             mix=(gdn_out, ctx, w['wuv_bd'], w['wo_g'], w['wo_m']))
    return x, c, krot, nconv, s_new


def kernel(x_prompt, x_sample, cache_ckv, cache_krope, state_conv, state_gdn, page_table, c_prompt, c_sample,
           ada_w, ada_b, norm_ffn1, ffn1_wi, ffn1_wo, norm_mix, w_in, gdn_conv_w, gdn_a_log, gdn_dt_bias, gdn_norm,
           mla_qn_norm, mla_qr_norm, mla_ckv_norm, mla_kr_norm, mla_kn_norm, mla_w_uk, mla_w_uv, w_out,
           norm_ffn2, ffn2_wi, ffn2_wo):
    depth = ada_w.shape[0]
    bp, tp, d = x_prompt.shape
    bs, ts, _ = x_sample.shape
    g_heads, dk, dv = state_gdn.shape[2:]
    conv_dim = state_conv.shape[-1]
    v_dim = g_heads * dv
    rank, heads, dn = mla_w_uk.shape[1:]
    dr = mla_qr_norm.shape[1]
    past = page_table.shape[1] * PAGE_SIZE
    dims = (heads, dn, dr, rank, g_heads, conv_dim, v_dim)
    lay = _InLayout(conv_dim, v_dim, heads)

    cs_p = _rope_cos_sin(jnp.arange(tp, dtype=F32), dr // 2)
    cs_s = jnp.tile(_rope_cos_sin(past + jnp.arange(ts, dtype=F32), dr // 2), (bs, 1))
    place, tab_bias = _rope_placement(dr // 2, dn)
    cond = jnp.concatenate([c_prompt, c_sample], axis=0)
    rows = -(-cond.shape[0] // SUBLANES) * SUBLANES
    cond = jnp.pad(cond, ((0, rows - cond.shape[0]), (0, 0)))

    yp = x_prompt
    ys = x_sample.reshape(1, bs * ts, d)
    outs_p, outs_s = [], []
    for l in range(depth):
        lp = dict(norm_ffn1=norm_ffn1[l], ffn1_wi=ffn1_wi[l], ffn1_wo=ffn1_wo[l], norm_mix=norm_mix[l], w_in=w_in[l],
                  gdn_conv_w=gdn_conv_w[l], gdn_a_log=gdn_a_log[l], gdn_dt_bias=gdn_dt_bias[l], gdn_norm=gdn_norm[l],
                  mla_qn_norm=mla_qn_norm[l], mla_qr_norm=mla_qr_norm[l], mla_ckv_norm=mla_ckv_norm[l],
                  mla_kr_norm=mla_kr_norm[l], mla_kn_norm=mla_kn_norm[l], mla_w_uk=mla_w_uk[l],
                  mla_w_uv=mla_w_uv[l], w_out=w_out[l], norm_ffn2=norm_ffn2[l], ffn2_wi=ffn2_wi[l],
                  ffn2_wo=ffn2_wo[l])
        w = _prep_layer(lp, dims)
        mods = _mods(cond, ada_w[l], ada_b[l])
        mods_p = mods[:bp].reshape(bp, N_MOD, d)
        mods_s = mods[bp:bp + bs].reshape(bs, N_MOD, d).transpose(1, 0, 2)

        conv0 = jnp.zeros((bp, SUBLANES, conv_dim), F32)
        s0 = jnp.zeros((bp, g_heads, dk, dv), F32)
        attend_p = lambda o: _attn_prompt(o[3], o[4], o[7])
        yp, c_p, kr_p, cv_p, s_p = _layer(yp, mods_p, (cs_p, place, tab_bias), conv0, s0, lp, w, lay, dims,
                                          chunk=min(GDN_CHUNK, tp), attend=attend_p)
        outs_p.append((c_p, kr_p.transpose(0, 2, 1), cv_p[:, SUBLANES - (GDN_CONV - 1):], s_p))


        def attend_s(o, l=l, w=w):
            q128, k128, c_new, qabs, qrope = o[3], o[4], o[5], o[7], o[8]
            by_seq = lambda a: a[0].reshape(heads, bs, ts, -1).transpose(1, 2, 0, 3)
            flat = lambda a: by_seq(a).reshape(bs, ts * heads, -1)
            ctx = _attn_sample(page_table, flat(qabs), flat(qrope)[..., :dr], flat(q128), by_seq(k128),
                               c_new.reshape(bs, ts, -1), w['wukt'], cache_ckv, jnp.swapaxes(cache_krope, 2, 3),
                               layer=l, heads=heads, dn=dn)
            return ctx.reshape(1, bs * ts, heads * rank)

        ys, c_s, kr_s, cv_s, s_s = _layer_sample(ys, mods_s, (cs_s, place, tab_bias), state_conv[l], state_gdn[l], lp, w, lay, dims,
                                                 bs, ts, attend_s)
        outs_s.append((c_s, kr_s, cv_s, s_s))

    stack = lambda outs, k: jnp.stack([o[k] for o in outs])
    ys = ys.reshape(bs, ts, d)
    return (yp, ys, stack(outs_p, 0), stack(outs_p, 1), stack(outs_p, 2), stack(outs_p, 3),
            stack(outs_s, 0), stack(outs_s, 1), stack(outs_s, 2), stack(outs_s, 3))


def _layer_sample(x, mods, rope, conv_prev, s0, lp, w, lay, dims, bs, ts, attend):
    heads, dn, dr, g_heads = dims[0], dims[1], dims[2], dims[4]
    x = _ffn(x, mods, lp['norm_ffn1'], *w['f1'], mod0=0, tokens_per_seq=ts)
    outs = _inproj(x, mods, lp['norm_mix'], w['w_all'], *rope, w['gq'], w['mq'], w['gc'], w['wuk'], w['mk'],
                   w['gk'], w['gkr'], w['wqa'], None, lay=lay, heads=heads, tokens_per_seq=ts, rope=(dn, dr))
    qkv, z, ba, c, krot = outs[0], outs[1], outs[2], outs[5], outs[6]
    seq = lambda a: a.reshape(bs, ts, a.shape[-1])
    minor = lambda a: seq(a).transpose(1, 2, 0)
    o_t, s_new = _gdn_sample(minor(qkv), conv_prev.transpose(1, 2, 0), minor(z), minor(ba[..., :g_heads]),
                             minor(ba[..., g_heads:2 * g_heads]), s0.transpose(1, 2, 3, 0), lp['gdn_conv_w'].T,
                             w['aparams'][:, g_heads:2 * g_heads], lp['gdn_norm'])
    g_heads_, dv_ = s0.shape[1], s0.shape[3]
    gdn_out = o_t.reshape(ts, g_heads_, dv_, bs).transpose(1, 3, 0, 2).reshape(1, g_heads_, bs * ts, dv_)
    ctx = attend(outs)
    x = _ffn(x, mods, lp['norm_ffn2'], *w['f2'], mod0=6, tokens_per_seq=ts,
             mix=(gdn_out, ctx, w['wuv_bd'], w['wo_g'], w['wo_m']))
    nconv = jnp.concatenate([conv_prev, seq(qkv)], axis=1)[:, -(GDN_CONV - 1):]
    return x, seq(c), seq(krot)[..., dn:dn + dr], nconv, s_new.transpose(3, 0, 1, 2)
```
